```python
import math
import jax, jax.numpy as jnp
from jax import lax
import numpy as np

D_MODEL = 1024
BATCH = 16
SEQ = 2048
DEPTH = 1
DEC_BATCH = 32
DEC_SEQ = 4
PAST_LEN = 16384
PAGE_SIZE = 128

D_ATT = D_MODEL // 2
N_HEADS_ATT = 8
HEAD_DIM_ATT = D_ATT // N_HEADS_ATT
D_MLSTM = D_MODEL - D_ATT
N_HEADS_MLSTM = 4
HEAD_DIM_MLSTM = D_MLSTM // N_HEADS_MLSTM
D_MIX = D_ATT + D_MLSTM
ROT_DIM = HEAD_DIM_ATT // 4
ROPE_THETA = 500000.0
DILATED_PATTERNS = ((128, 1), (512, 4), (2048, 16))
MAX_WINDOW = 2048
ATT_BLOCK = 128
CONV_WIDTH = 4
MLSTM_CHUNK = 128
EPS = 1e-6
NEG_INF = -1e30
SPLIT_SIZES = (D_ATT, D_ATT, D_ATT, D_ATT,
               D_MLSTM, D_MLSTM, D_MLSTM, D_MLSTM, D_MLSTM, N_HEADS_MLSTM, N_HEADS_MLSTM)
N_IN = sum(SPLIT_SIZES)

kernel_name = 'hymba_dilated_attn_mlstm_decode_step'


def rmsnorm(x, g):
    xf = x.astype(jnp.float32)
    y = xf * lax.rsqrt(jnp.mean(xf * xf, axis=-1, keepdims=True) + EPS)
    return (y * g.astype(jnp.float32)).astype(x.dtype)


def rope_partial(x, pos):
    half = ROT_DIM // 2
    inv = ROPE_THETA ** (-jnp.arange(half, dtype=jnp.float32) * 2.0 / ROT_DIM)
    ang = pos.astype(jnp.float32)[:, None] * inv[None, :]
    cos = jnp.cos(ang)[None, :, None, :]
    sin = jnp.sin(ang)[None, :, None, :]
    x1 = x[..., :half]
    x2 = x[..., half:ROT_DIM]
    return jnp.concatenate([x1 * cos - x2 * sin, x2 * cos + x1 * sin, x[..., ROT_DIM:]], axis=-1)


def dilated_partial_prompt(q, k, v, dil, n_back):
    B, S, H, D = q.shape
    L = S // dil
    nb = -(-L // ATT_BLOCK)
    Lp = nb * ATT_BLOCK

    def to_sub(t):
        t = t.reshape(B, L, dil, H, D).transpose(0, 2, 1, 3, 4)
        t = jnp.pad(t, ((0, 0), (0, 0), (0, Lp - L), (0, 0), (0, 0)))
        return t.reshape(B, dil, nb, ATT_BLOCK, H, D)

    qs, ks, vs = to_sub(q), to_sub(k), to_sub(v)

    def with_prev(t):
        prev = jnp.concatenate([jnp.zeros_like(t[:, :, :1]), t[:, :, :-1]], axis=2)
        return jnp.concatenate([prev, t], axis=3)

    kk, vv = with_prev(ks), with_prev(vs)
    s = jnp.einsum('brcqhe,brckhe->brchqk', qs, kk) * (HEAD_DIM_ATT ** -0.5)
    u = jnp.arange(ATT_BLOCK)[:, None]
    wk = jnp.arange(2 * ATT_BLOCK)[None, :]
    dist = u + ATT_BLOCK - wk
    in_win = (dist >= 0) & (dist <= n_back)
    has_prev = (jnp.arange(nb) > 0)[:, None, None] | (wk >= ATT_BLOCK)[None]
    valid = in_win[None] & has_prev
    s = jnp.where(valid[None, None, :, None], s, NEG_INF)
    m = jnp.max(s, axis=-1)
    p = jnp.exp(s - m[..., None])
    den = jnp.sum(p, axis=-1)
    acc = jnp.einsum('brchqk,brckhe->brcqhe', p, vv)

    def stat_back(t):
        t = t.transpose(0, 1, 2, 4, 3).reshape(B, dil, Lp, H)[:, :, :L]
        return t.transpose(0, 2, 1, 3).reshape(B, S, H)

    acc = acc.reshape(B, dil, Lp, H, D)[:, :, :L].transpose(0, 2, 1, 3, 4).reshape(B, S, H, D)
    return stat_back(m), stat_back(den), acc


def dilated_partial_sample(q, kc, vc, dil, n_back):
    T = q.shape[1]
    WB = kc.shape[1] - T
    j = jnp.arange(n_back + 1)
    idx = WB + jnp.arange(T)[:, None] - j[None, :] * dil
    valid = idx >= 0
    idx = jnp.maximum(idx, 0)
    kg = jnp.take(kc, idx, axis=1)
    vg = jnp.take(vc, idx, axis=1)
    s = jnp.einsum('bthe,btjhe->bthj', q, kg) * (HEAD_DIM_ATT ** -0.5)
    s = jnp.where(valid[None, :, None, :], s, NEG_INF)
    m = jnp.max(s, axis=-1)
    p = jnp.exp(s - m[..., None])
    den = jnp.sum(p, axis=-1)
    acc = jnp.einsum('bthj,btjhe->bthe', p, vg)
    return m, den, acc


def combine_partials(parts):
    ms = jnp.stack([pt[0] for pt in parts])
    ss = jnp.stack([pt[1] for pt in parts])
    accs = jnp.stack([pt[2] for pt in parts])
    w = jnp.exp(ms - jnp.max(ms, axis=0, keepdims=True))
    den = jnp.sum(w * ss, axis=0)
    num = jnp.sum(w[..., None] * accs, axis=0)
    return num / den[..., None]


def causal_conv(u, buf, w, b):
    S = u.shape[1]
    xp = jnp.concatenate([buf.astype(u.dtype), u], axis=1)
    y = b + xp[:, 0:S] * w[0]
    for j in range(1, CONV_WIDTH):
        y = y + xp[:, j:j + S] * w[j]
    return jax.nn.silu(y), xp[:, -(CONV_WIDTH - 1):]


def mlstm_chunkwise(q, k, v, i_pre, log_f, C0, n0, m0):
    B, H, S, DK = q.shape
    DV = v.shape[-1]
    L = MLSTM_CHUNK if S % MLSTM_CHUNK == 0 else S
    nc = S // L

    def chunks(t):
        return jnp.moveaxis(t.reshape(t.shape[:2] + (nc, L) + t.shape[3:]), 2, 0)

    causal = jnp.tril(jnp.ones((L, L), dtype=bool))

    def step(carry, xs):
        C, n, m_prev = carry
        qc, kc, vc, ic, fc = xs
        b = jnp.cumsum(fc, axis=-1)
        m_t = jnp.maximum(m_prev[..., None] + b, b + lax.cummax(ic - b, axis=2))
        inter = jnp.exp(m_prev[..., None] + b - m_t)
        dlog = b[..., :, None] - b[..., None, :] + ic[..., None, :] - m_t[..., :, None]
        dmat = jnp.exp(jnp.where(causal, dlog, NEG_INF))
        sqk = jnp.einsum('bhtd,bhsd->bhts', qc, kc) * dmat
        num = inter[..., None] * jnp.einsum('bhtd,bhde->bhte', qc, C) + jnp.einsum('bhts,bhse->bhte', sqk, vc)
        nq = inter * jnp.einsum('bhtd,bhd->bht', qc, n) + jnp.sum(sqk, axis=-1)
        den = jnp.maximum(jnp.abs(nq), jnp.exp(-m_t))
        h = num / den[..., None]
        m_last = m_t[..., -1]
        decay = jnp.exp(m_prev + b[..., -1] - m_last)
        w = jnp.exp(b[..., -1:] - b + ic - m_last[..., None])
        C_new = decay[..., None, None] * C + jnp.einsum('bhs,bhsd,bhse->bhde', w, kc, vc)
        n_new = decay[..., None] * n + jnp.einsum('bhs,bhsd->bhd', w, kc)
        return (C_new, n_new, m_last), h

    xs = (chunks(q), chunks(k), chunks(v), chunks(i_pre), chunks(log_f))
    (C, n, m), hs = lax.scan(step, (C0, n0, m0), xs)
    h = jnp.moveaxis(hs, 0, 2).reshape(B, H, S, DV)
    return h, C, n, m


def mixer_layer(x, pos, win_k, win_v, conv_buf, C0, n0, m0,
                norm_g, w_in, conv_w, conv_b, b_i, b_f, mlstm_norm_g, w_out):
    B, S, _ = x.shape
    f32 = jnp.float32
    hn = rmsnorm(x, norm_g)
    u = hn @ w_in
    offsets = [int(o) for o in np.cumsum(SPLIT_SIZES)[:-1]]
    qa, ka, va, za, qb, kb, vb, ob, zb, ib, fb = jnp.split(u, offsets, axis=-1)

    att_heads = lambda t: t.reshape(B, S, N_HEADS_ATT, HEAD_DIM_ATT).astype(f32)
    qa = rope_partial(att_heads(qa), pos)
    ka = rope_partial(att_heads(ka), pos)
    va = att_heads(va)
    if win_k is None:
        parts = [dilated_partial_prompt(qa, ka, va, dil, win // dil) for (win, dil) in DILATED_PATTERNS]
        keep = min(MAX_WINDOW, S)
        new_k, new_v = ka[:, -keep:], va[:, -keep:]
    else:
        kc = jnp.concatenate([win_k.astype(f32), ka], axis=1)
        vc = jnp.concatenate([win_v.astype(f32), va], axis=1)
        parts = [dilated_partial_sample(qa, kc, vc, dil, win // dil) for (win, dil) in DILATED_PATTERNS]
        keep = win_k.shape[1]
        new_k, new_v = kc[:, -keep:], vc[:, -keep:]
    att = combine_partials(parts).reshape(B, S, D_ATT).astype(x.dtype)
    y_a = att * jax.nn.silu(za)

    qk, new_conv = causal_conv(jnp.concatenate([qb, kb], axis=-1), conv_buf, conv_w, conv_b)
    qm, km = jnp.split(qk, 2, axis=-1)
    m_heads = lambda t: t.reshape(B, S, N_HEADS_MLSTM, HEAD_DIM_MLSTM).transpose(0, 2, 1, 3).astype(f32)
    i_pre = (ib.astype(f32) + b_i.astype(f32)).transpose(0, 2, 1)
    log_f = jax.nn.log_sigmoid(fb.astype(f32) + b_f.astype(f32)).transpose(0, 2, 1)
    h_m, C, n, m = mlstm_chunkwise(m_heads(qm), m_heads(km) * (HEAD_DIM_MLSTM ** -0.5), m_heads(vb),
                                   i_pre, log_f, C0.astype(f32), n0.astype(f32), m0.astype(f32))
    h_m = h_m.transpose(0, 2, 1, 3)
    h_m = jax.nn.sigmoid(ob.astype(f32)).reshape(B, S, N_HEADS_MLSTM, HEAD_DIM_MLSTM) * h_m
    h_m = h_m * lax.rsqrt(jnp.mean(h_m * h_m, axis=-1, keepdims=True) + EPS)
    h_m = h_m * mlstm_norm_g.astype(f32).reshape(N_HEADS_MLSTM, HEAD_DIM_MLSTM)
    y_b = h_m.reshape(B, S, D_MLSTM).astype(x.dtype) * jax.nn.silu(zb)

    y = x + jnp.concatenate([y_a, y_b], axis=-1) @ w_out
    return y, (new_k.astype(x.dtype), new_v.astype(x.dtype), new_conv, C, n, m)


def setup_inputs(seed: int = 0) -> dict:
    key = jax.random.key(seed)
    ks = jax.random.split(key, 20)
    win_buf = min(MAX_WINDOW, PAST_LEN)
    nrm = lambda k, shp: jax.random.normal(k, shp, dtype=jnp.float32)
    return {
        'x_prompt': nrm(ks[0], (BATCH, SEQ, D_MODEL)),
        'x_sample': nrm(ks[1], (DEC_BATCH, DEC_SEQ, D_MODEL)),
        'cache_win_k': nrm(ks[2], (DEPTH, DEC_BATCH, win_buf, N_HEADS_ATT, HEAD_DIM_ATT)),
        'cache_win_v': nrm(ks[3], (DEPTH, DEC_BATCH, win_buf, N_HEADS_ATT, HEAD_DIM_ATT)),
        'state_conv': nrm(ks[4], (DEPTH, DEC_BATCH, CONV_WIDTH - 1, 2 * D_MLSTM)),
        'state_C': 0.1 * nrm(ks[5], (DEPTH, DEC_BATCH, N_HEADS_MLSTM, HEAD_DIM_MLSTM, HEAD_DIM_MLSTM)),
        'state_n': 0.1 * nrm(ks[6], (DEPTH, DEC_BATCH, N_HEADS_MLSTM, HEAD_DIM_MLSTM)),
        'state_m': 0.5 * nrm(ks[7], (DEPTH, DEC_BATCH, N_HEADS_MLSTM)),
        'norm_g': 1.0 + 0.02 * nrm(ks[8], (DEPTH, D_MODEL)),
        'w_in': nrm(ks[9], (DEPTH, D_MODEL, N_IN)) * D_MODEL ** -0.5,
        'conv_w': nrm(ks[10], (DEPTH, CONV_WIDTH, 2 * D_MLSTM)) * CONV_WIDTH ** -0.5,
        'conv_b': 0.02 * nrm(ks[11], (DEPTH, 2 * D_MLSTM)),
        'b_i': 0.1 * nrm(ks[12], (DEPTH, N_HEADS_MLSTM)),
        'b_f': jnp.linspace(3.0, 6.0, N_HEADS_MLSTM, dtype=jnp.float32)[None] + 0.1 * nrm(ks[13], (DEPTH, N_HEADS_MLSTM)),
        'mlstm_norm_g': 1.0 + 0.02 * nrm(ks[14], (DEPTH, D_MLSTM)),
        'w_out': nrm(ks[15], (DEPTH, D_MIX, D_MODEL)) * D_MIX ** -0.5,
        'final_norm_g': 1.0 + 0.02 * nrm(ks[16], (D_MODEL,)),
    }


def reference(x_prompt, x_sample, cache_win_k, cache_win_v, state_conv, state_C, state_n, state_m,
              norm_g, w_in, conv_w, conv_b, b_i, b_f, mlstm_norm_g, w_out, final_norm_g):
    B, S, _ = x_prompt.shape
    DB, T, _ = x_sample.shape
    pos_p = jnp.arange(S, dtype=jnp.int32)
    pos_s = PAST_LEN + jnp.arange(T, dtype=jnp.int32)
    xp, xs = x_prompt, x_sample
    p_states, s_states = [], []
    for l in range(DEPTH):
        params = (norm_g[l], w_in[l], conv_w[l], conv_b[l], b_i[l], b_f[l], mlstm_norm_g[l], w_out[l])
        zero_conv = jnp.zeros((B, CONV_WIDTH - 1, 2 * D_MLSTM), dtype=xp.dtype)
        zero_C = jnp.zeros((B, N_HEADS_MLSTM, HEAD_DIM_MLSTM, HEAD_DIM_MLSTM), dtype=jnp.float32)
        zero_n = jnp.zeros((B, N_HEADS_MLSTM, HEAD_DIM_MLSTM), dtype=jnp.float32)
        zero_m = jnp.zeros((B, N_HEADS_MLSTM), dtype=jnp.float32)
        xp, sp = mixer_layer(xp, pos_p, None, None, zero_conv, zero_C, zero_n, zero_m, *params)
        xs, ss = mixer_layer(xs, pos_s, cache_win_k[l], cache_win_v[l], state_conv[l],
                             state_C[l], state_n[l], state_m[l], *params)
        p_states.append(sp)
        s_states.append(ss)
    y_prompt = rmsnorm(xp, final_norm_g)
    y_sample = rmsnorm(xs, final_norm_g)
    stk = lambda sts, i: jnp.stack([st[i] for st in sts])
    p_k, p_v, p_conv, p_C, p_n, p_m = [stk(p_states, i) for i in range(6)]
    s_k, s_v, s_conv, s_C, s_n, s_m = [stk(s_states, i) for i in range(6)]
    return (y_prompt, y_sample, p_k, p_v, p_conv, p_C, p_n, p_m, s_k, s_v, s_conv, s_C, s_n, s_m)
```

```python
import functools
import math

import jax
import jax.numpy as jnp
import numpy as np
from jax import lax
from jax.experimental import pallas as pl
from jax.experimental.pallas import tpu as pltpu

F32 = jnp.float32
BF16 = jnp.bfloat16

D_MODEL = 1024
D_ATT = 512
N_HEADS_ATT = 8
HEAD_DIM_ATT = 64
D_MLSTM = 512
N_HEADS_MLSTM = 4
HEAD_DIM_MLSTM = 128
ROT_DIM = 16
ROPE_THETA = 500000.0
PAST_LEN = 16384
DILATIONS = (1, 4, 16)
N_BACK = 128
CONV_WIDTH = 4
CHUNK = 128
EPS = 1e-6
NEG_INF = -1e30

LANES = 128
SUBLANES = 8
N_IN = 4 * D_ATT + 5 * D_MLSTM + 2 * N_HEADS_MLSTM
N_IN_PAD = 4 * D_ATT + 5 * D_MLSTM + LANES
OFF_QA, OFF_KA, OFF_VA, OFF_ZA = 0, 512, 1024, 1536
OFF_QB, OFF_VB, OFF_OB, OFF_ZB, OFF_G = 2048, 3072, 3584, 4096, 4608
ROW_TILE = 512
VMEM_LIMIT = 56 * 1024 * 1024


def _silu(x):
    return x * jax.nn.sigmoid(x)


def _rmsnorm(x, g):
    return x * lax.rsqrt(jnp.mean(x * x, axis=-1, keepdims=True) + EPS) * g


def _rope(u, cos, sin_lo, sin_hi):
    outs = []
    for c in range(u.shape[1] // LANES):
        xs = u[:, c * LANES:(c + 1) * LANES]
        outs.append(xs * cos + pltpu.roll(xs, LANES - ROT_DIM // 2, 1) * sin_lo
                    + pltpu.roll(xs, ROT_DIM // 2, 1) * sin_hi)
    return jnp.concatenate(outs, axis=1)


def _gate_block(ug, bif):
    gz = ug + bif
    lane = lax.broadcasted_iota(jnp.int32, gz.shape, 1)
    logf = jnp.minimum(gz, 0.0) - jnp.log1p(jnp.exp(-jnp.abs(gz)))
    return jnp.where(lane < N_HEADS_MLSTM, gz, logf)


def _inproj_common(hn, w_ref, cos, sin_lo, sin_hi,
                   q_ref, pk_ref, pv_ref, ga_ref, vb_ref, so_ref, gb_ref):
    def seg(off, width):
        return jnp.dot(hn, w_ref[:, off:off + width], preferred_element_type=F32)

    q = _rope(seg(OFF_QA, D_ATT), cos, sin_lo, sin_hi) * (HEAD_DIM_ATT ** -0.5)
    q_ref[...] = q.astype(BF16).reshape(q_ref.shape)
    pk_ref[...] = _rope(seg(OFF_KA, D_ATT), cos, sin_lo, sin_hi).reshape(pk_ref.shape)
    pv_ref[...] = seg(OFF_VA, D_ATT).reshape(pv_ref.shape)
    ga_ref[...] = _silu(seg(OFF_ZA, D_ATT)).astype(BF16).reshape(ga_ref.shape)
    vb_ref[...] = seg(OFF_VB, D_MLSTM).astype(BF16).reshape(vb_ref.shape)
    so_ref[...] = jax.nn.sigmoid(seg(OFF_OB, D_MLSTM)).astype(BF16).reshape(so_ref.shape)
    gb_ref[...] = _silu(seg(OFF_ZB, D_MLSTM)).astype(BF16).reshape(gb_ref.shape)
    return seg


def _inproj_prompt_kernel(x_ref, g_ref, w_ref, cw_ref, cb_ref, bif_ref, cos_ref, slo_ref, shi_ref,
                          q_ref, pk_ref, pv_ref, ga_ref, qm_ref, km_ref, vb_ref, so_ref, gb_ref,
                          gt_ref, pconv_ref, xp_ref):
    j = pl.program_id(1)
    tm = x_ref.shape[1]
    hn = _rmsnorm(x_ref[0], g_ref[...]).astype(BF16)
    seg = _inproj_common(hn, w_ref, cos_ref[...], slo_ref[...], shi_ref[...],
                         q_ref, pk_ref, pv_ref, ga_ref, vb_ref, so_ref, gb_ref)

    @pl.when(j == 0)
    def _():
        xp_ref[0:SUBLANES, :] = jnp.zeros((SUBLANES, 2 * D_MLSTM), F32)

    @pl.when(j > 0)
    def _():
        xp_ref[0:SUBLANES, :] = xp_ref[tm:tm + SUBLANES, :]

    xp_ref[SUBLANES:SUBLANES + tm, :] = seg(OFF_QB, 2 * D_MLSTM)
    y = cb_ref[...] + xp_ref[SUBLANES:SUBLANES + tm, :] * cw_ref[3:4, :]
    for jj in range(CONV_WIDTH - 1):
        sh = CONV_WIDTH - 1 - jj
        y = y + xp_ref[SUBLANES - sh:SUBLANES - sh + tm, :] * cw_ref[jj:jj + 1, :]
    y = _silu(y)
    qm_ref[0] = y[:, :D_MLSTM].astype(BF16)
    km_ref[0] = (y[:, D_MLSTM:] * (HEAD_DIM_MLSTM ** -0.5)).astype(BF16)

    @pl.when(j == pl.num_programs(1) - 1)
    def _():
        pconv_ref[0] = xp_ref[tm + SUBLANES - (CONV_WIDTH - 1):tm + SUBLANES, :]

    gates = _gate_block(seg(OFF_G, LANES), bif_ref[...])
    for i in range(tm // LANES):
        gt = gates[i * LANES:(i + 1) * LANES, :].T
        gt_ref[0, :, i * LANES:(i + 1) * LANES] = gt[0:SUBLANES, :]


def _inproj_sample_kernel(x_ref, g_ref, w_ref, cw_ref, cb_ref, bif_ref, cos_ref, slo_ref, shi_ref,
                          h1_ref, h2_ref, h3_ref,
                          q_ref, pk_ref, pv_ref, ga_ref, qm_ref, km_ref, vb_ref, so_ref, gb_ref,
                          gates_ref, qk_ref, *, t_new):
    hn = _rmsnorm(x_ref[...], g_ref[...]).astype(BF16)
    seg = _inproj_common(hn, w_ref, cos_ref[...], slo_ref[...], shi_ref[...],
                         q_ref, pk_ref, pv_ref, ga_ref, vb_ref, so_ref, gb_ref)
    u = seg(OFF_QB, 2 * D_MLSTM)
    qk_ref[...] = u
    t = lax.rem(lax.broadcasted_iota(jnp.int32, u.shape, 0), t_new)
    y = cb_ref[...] + u * cw_ref[3:4, :]
    for sh, h_ref in ((1, h1_ref), (2, h2_ref), (3, h3_ref)):
        prev = jnp.where(t >= sh, pltpu.roll(u, sh, 0), h_ref[...])
        y = y + prev * cw_ref[3 - sh:4 - sh, :]
    y = _silu(y)
    qm_ref[...] = y[:, :D_MLSTM].astype(BF16)
    km_ref[...] = (y[:, D_MLSTM:] * (HEAD_DIM_MLSTM ** -0.5)).astype(BF16)
    gates_ref[...] = _gate_block(seg(OFF_G, LANES), bif_ref[...])


def _attn_prompt_kernel(q_ref, k_ref, v_ref, g_ref, o_ref, qf_ref, m_ref, d_ref, a_ref):
    seq = q_ref.shape[1]
    qf_ref[...] = q_ref[0].astype(F32)
    lane = lax.broadcasted_iota(jnp.int32, (N_BACK, LANES), 1)
    head_a = lane < HEAD_DIM_ATT

    def rows_of(start, dil):
        return pl.ds(start, N_BACK) if dil == 1 else pl.ds(start, N_BACK, stride=dil)

    def block(dil, qstart, kprev_start, first, last):
        rows = rows_of(qstart, dil)
        q = qf_ref[rows, :]
        kc = k_ref[0, rows, :]
        vc = v_ref[0, rows, :]
        if kprev_start is None:
            kk, vv = kc, vc
            u = lax.broadcasted_iota(jnp.int32, (N_BACK, N_BACK), 0)
            w = lax.broadcasted_iota(jnp.int32, (N_BACK, N_BACK), 1)
            valid = w <= u
        else:
            prow = rows_of(kprev_start, dil)
            kk = jnp.concatenate([k_ref[0, prow, :], kc], axis=0)
            vv = jnp.concatenate([v_ref[0, prow, :], vc], axis=0)
            u = lax.broadcasted_iota(jnp.int32, (N_BACK, 2 * N_BACK), 0)
            w = lax.broadcasted_iota(jnp.int32, (N_BACK, 2 * N_BACK), 1)
            valid = (w >= u) & (w <= u + N_BACK)
        kk = kk.astype(BF16)
        lane_k = lax.broadcasted_iota(jnp.int32, vv.shape, 1)
        res = []
        for is_a in (True, False):
            sel = head_a if is_a else jnp.logical_not(head_a)
            sel_k = (lane_k < HEAD_DIM_ATT) if is_a else (lane_k >= HEAD_DIM_ATT)
            qh = jnp.where(sel, q, 0.0).astype(BF16)
            s = lax.dot_general(qh, kk, (((1,), (1,)), ((), ())), preferred_element_type=F32)
            s = jnp.where(valid, s, NEG_INF)
            mh = jnp.max(s, axis=-1, keepdims=True)
            p = jnp.exp(s - mh).astype(BF16)
            vh = jnp.where(sel_k, vv, 1.0).astype(BF16)
            res.append((mh, jnp.dot(p, vh, preferred_element_type=F32)))
        (m_a, pv_a), (m_b, pv_b) = res
        acc = jnp.where(head_a, pv_a, pv_b)
        den = pltpu.roll(jnp.where(head_a, pv_b, pv_a), HEAD_DIM_ATT, 1)
        mb = jnp.where(head_a, m_a, m_b)
        if not first:
            m_old = m_ref[rows, :]
            m_new = jnp.maximum(m_old, mb)
            w_old = jnp.exp(m_old - m_new)
            w_cur = jnp.exp(mb - m_new)
            den = w_old * d_ref[rows, :] + w_cur * den
            acc = w_old * a_ref[rows, :] + w_cur * acc
            mb = m_new
        if last:
            o_ref[0, rows, :] = (acc / den * g_ref[0, rows, :].astype(F32)).astype(BF16)
        else:
            m_ref[rows, :] = mb
            d_ref[rows, :] = den
            a_ref[rows, :] = acc

    def body16(r, c):
        block(16, r, None, True, False)
        return c
    lax.fori_loop(0, 16, body16, 0)

    n4 = seq // 4 // N_BACK

    def body4r(r, c):
        block(4, r, None, False, False)

        def body4c(cc, c2):
            block(4, r + 4 * N_BACK * cc, r + 4 * N_BACK * (cc - 1), False, False)
            return c2
        lax.fori_loop(1, n4, body4c, 0)
        return c
    lax.fori_loop(0, 4, body4r, 0)

    block(1, 0, None, False, True)

    def body1(cc, c):
        start = pl.multiple_of(cc * N_BACK, N_BACK)
        block(1, start, start - N_BACK, False, True)
        return c
    lax.fori_loop(1, seq // N_BACK, body1, 0)


def _attn_sample_kernel(q_ref, kn_ref, vn_ref, kn4_ref, vn4_ref, g_ref, ck_ref, cv_ref,
                        o_ref, sk_ref, sv_ref):
    nlb = D_ATT // LANES
    wb = ck_ref.shape[1] // nlb
    t_new = q_ref.shape[1]
    kn = kn_ref[0]
    vn = vn_ref[0]
    sk_ref[0, 0:nlb * (wb - t_new), :] = ck_ref[0, nlb * t_new:nlb * wb, :]
    sk_ref[0, nlb * (wb - t_new):nlb * wb, :] = kn4_ref[0]
    sv_ref[0, 0:nlb * (wb - t_new), :] = cv_ref[0, nlb * t_new:nlb * wb, :]
    sv_ref[0, nlb * (wb - t_new):nlb * wb, :] = vn4_ref[0]

    def gather(ref, start, dil):
        return jnp.concatenate(
            [ref[0, pl.ds(nlb * start + c, N_BACK, stride=nlb * dil), :] for c in range(nlb)],
            axis=1).astype(BF16)

    lane8 = lax.broadcasted_iota(jnp.int32, (N_HEADS_ATT, D_ATT), 1)
    row8 = lax.broadcasted_iota(jnp.int32, (N_HEADS_ATT, D_ATT), 0)
    head_mask = (lane8 // HEAD_DIM_ATT) == row8
    nt = (((1,), (1,)), ((), ()))

    col = lax.broadcasted_iota(jnp.int32, (N_HEADS_ATT, N_BACK), 1)
    outs = []
    for t in range(t_new):
        qrow = q_ref[0, t:t + 1, :].astype(F32)
        qh32 = jnp.where(head_mask, jnp.broadcast_to(qrow, (N_HEADS_ATT, D_ATT)), 0.0)
        qh = qh32.astype(BF16)
        s_new = [jnp.sum(qh32 * kn[tt:tt + 1, :], axis=-1, keepdims=True) for tt in range(t + 1)]
        parts = []
        for dil in DILATIONS:
            if dil == 1:
                start = wb - N_BACK
                extra = range(t + 1)
            else:
                start = wb + t - N_BACK * dil
                extra = (t,)
            kk = gather(ck_ref, start, dil)
            vv = gather(cv_ref, start, dil)
            s = lax.dot_general(qh, kk, nt, preferred_element_type=F32)
            if dil == 1 and t > 0:
                s = jnp.where(col >= t, s, NEG_INF)
            m = jnp.max(s, axis=-1, keepdims=True)
            for tt in extra:
                m = jnp.maximum(m, s_new[tt])
            p = jnp.exp(s - m)
            den = jnp.sum(p, axis=-1, keepdims=True)
            acc = jnp.dot(p.astype(BF16), vv, preferred_element_type=F32)
            for tt in extra:
                pe = jnp.exp(s_new[tt] - m)
                den = den + pe
                acc = acc + pe * vn[tt:tt + 1, :]
            parts.append((m, den, acc))
        m_all = functools.reduce(jnp.maximum, [pt[0] for pt in parts])
        den = sum(jnp.exp(pt[0] - m_all) * pt[1] for pt in parts)
        num = sum(jnp.exp(pt[0] - m_all) * pt[2] for pt in parts)
        att = jnp.where(head_mask, num / den, 0.0)
        outs.append(jnp.sum(att, axis=0, keepdims=True))
    att = jnp.concatenate(outs, axis=0)
    o_ref[0] = (att * g_ref[0].astype(F32)).astype(BF16)


def _scan_lanes(x, op, fill):
    lane = lax.broadcasted_iota(jnp.int32, x.shape, 1)
    d = 1
    while d < x.shape[1]:
        x = op(x, jnp.where(lane >= d, pltpu.roll(x, d, 1), fill))
        d *= 2
    return x


def _mlstm_kernel(*refs, zero_init):
    if zero_init:
        (q_ref, k_ref, v_ref, gt_ref, so_ref, gb_ref, ng_ref,
         y_ref, cn_out_ref, m_out_ref, cn_ref, m_ref) = refs
    else:
        (q_ref, k_ref, v_ref, gt_ref, so_ref, gb_ref, ng_ref, cn0_ref, m0_ref,
         y_ref, cn_out_ref, m_out_ref, cn_ref, m_ref) = refs
    c_idx = pl.program_id(1)
    L = CHUNK
    dh = HEAD_DIM_MLSTM

    @pl.when(c_idx == 0)
    def _():
        if zero_init:
            cn_ref[...] = jnp.zeros(cn_ref.shape, F32)
            m_ref[...] = jnp.zeros(m_ref.shape, F32)
        else:
            cn_ref[...] = cn0_ref[0]
            m_ref[...] = m0_ref[0]

    gt = gt_ref[0]
    i_row = gt
    f_row = pltpu.roll(gt, N_HEADS_MLSTM, 0)
    m_prev = m_ref[...]
    b = _scan_lanes(f_row, jnp.add, 0.0)
    cm = _scan_lanes(i_row - b, jnp.maximum, NEG_INF)
    m_t = jnp.maximum(m_prev + b, b + cm)
    inter = jnp.exp(m_prev + b - m_t)
    a_row = b - m_t
    r_row = i_row - b
    en_row = jnp.exp(-m_t)
    m_last = jnp.broadcast_to(m_t[:, L - 1:L], m_t.shape)
    b_last = jnp.broadcast_to(b[:, L - 1:L], b.shape)
    decay = jnp.exp(m_prev + b_last - m_last)
    decay2 = jnp.concatenate([decay, decay], axis=1)
    w_row = jnp.exp(b_last - b + i_row - m_last)
    m_ref[...] = m_last

    stack = jnp.concatenate([a_row, inter, en_row, w_row,
                             jnp.zeros((L - 4 * SUBLANES, L), F32)], axis=0)
    cols = stack.T

    tri_t = lax.broadcasted_iota(jnp.int32, (L, L), 0)
    tri_s = lax.broadcasted_iota(jnp.int32, (L, L), 1)
    causal = tri_t >= tri_s
    ones_blk = jnp.ones((L, dh), BF16)

    for h in range(N_HEADS_MLSTM):
        sl = slice(h * dh, (h + 1) * dh)
        q = q_ref[0, :, sl]
        k = k_ref[0, :, sl]
        v = v_ref[0, :, sl]
        a_col = cols[:, h:h + 1]
        inter_col = cols[:, SUBLANES + h:SUBLANES + h + 1]
        en_col = cols[:, 2 * SUBLANES + h:2 * SUBLANES + h + 1]
        w_col = cols[:, 3 * SUBLANES + h:3 * SUBLANES + h + 1]
        dlog = a_col + r_row[h:h + 1, :]
        dmat = jnp.exp(jnp.where(causal, dlog, NEG_INF))
        sqk = lax.dot_general(q, k, (((1,), (1,)), ((), ())), preferred_element_type=F32) * dmat
        cn = cn_ref[h]
        tot = inter_col * jnp.dot(q, cn.astype(BF16), preferred_element_type=F32)
        tot = tot + jnp.dot(sqk.astype(BF16), jnp.concatenate([v, ones_blk], axis=1),
                            preferred_element_type=F32)
        num = tot[:, :dh]
        nq = tot[:, dh:]
        hh = num / jnp.maximum(jnp.abs(nq), en_col)
        hh = so_ref[0, :, sl].astype(F32) * hh
        hh = hh * lax.rsqrt(jnp.mean(hh * hh, axis=-1, keepdims=True) + EPS)
        hh = hh * ng_ref[:, sl]
        y_ref[0, :, sl] = (hh * gb_ref[0, :, sl].astype(F32)).astype(BF16)

        wv = jnp.concatenate([w_col * v.astype(F32), jnp.broadcast_to(w_col, (L, dh))], axis=1)
        upd = lax.dot_general(k, wv.astype(BF16), (((0,), (0,)), ((), ())),
                              preferred_element_type=F32)
        cn_ref[h] = decay2[h:h + 1, :] * cn + upd

    @pl.when(c_idx == pl.num_programs(1) - 1)
    def _():
        cn_out_ref[0] = cn_ref[...]
        m_out_ref[0] = m_ref[...]


def _outproj_kernel(x_ref, ya_ref, yb_ref, w_ref, g_ref, o_ref):
    mix = jnp.dot(ya_ref[...], w_ref[0:D_ATT, :], preferred_element_type=F32)
    mix = mix + jnp.dot(yb_ref[...], w_ref[D_ATT:, :], preferred_element_type=F32)
    o_ref[...] = _rmsnorm(x_ref[...] + mix, g_ref[...])


def _rope_tables(pos):
    half = ROT_DIM // 2
    inv = ROPE_THETA ** (-jnp.arange(half, dtype=F32) * 2.0 / ROT_DIM)
    ang = pos.astype(F32)[:, None] * inv[None, :]
    cos, sin = jnp.cos(ang), jnp.sin(ang)
    n = pos.shape[0]
    one = jnp.ones((n, HEAD_DIM_ATT - ROT_DIM), F32)
    zero = jnp.zeros((n, HEAD_DIM_ATT - ROT_DIM), F32)
    zh = jnp.zeros((n, half), F32)
    cos_t = jnp.concatenate([cos, cos, one], axis=1)
    lo_t = jnp.concatenate([-sin, zh, zero], axis=1)
    hi_t = jnp.concatenate([zh, sin, zero], axis=1)
    rep = lambda t: jnp.concatenate([t, t], axis=1)
    return rep(cos_t), rep(lo_t), rep(hi_t)


def _params(sem):
    return pltpu.CompilerParams(dimension_semantics=sem, vmem_limit_bytes=VMEM_LIMIT)


def _const_spec(shape):
    return pl.BlockSpec(shape, lambda *_: (0,) * len(shape))


def _inproj_prompt(x, norm_g, w_pad, conv_w, conv_b, bif):
    B, S, _ = x.shape
    tm = ROW_TILE
    cos, lo, hi = _rope_tables(jnp.arange(S, dtype=jnp.int32))
    tile = lambda width: pl.BlockSpec((1, tm, width), lambda b, j: (b, j, 0))
    tab = pl.BlockSpec((tm, LANES), lambda b, j: (j, 0))
    bf = lambda: jax.ShapeDtypeStruct((B, S, D_ATT), BF16)
    f3 = lambda: jax.ShapeDtypeStruct((B, S, D_ATT), F32)
    out_shape = (bf(), f3(), f3(), bf(), bf(), bf(), bf(), bf(), bf(),
                 jax.ShapeDtypeStruct((B, SUBLANES, S), F32),
                 jax.ShapeDtypeStruct((B, CONV_WIDTH - 1, 2 * D_MLSTM), F32))
    out_specs = tuple([tile(D_ATT)] * 9) + (
        pl.BlockSpec((1, SUBLANES, tm), lambda b, j: (b, 0, j)),
        pl.BlockSpec((1, CONV_WIDTH - 1, 2 * D_MLSTM), lambda b, j: (b, 0, 0)))
    return pl.pallas_call(
        _inproj_prompt_kernel,
        grid=(B, S // tm),
        in_specs=[tile(D_MODEL), _const_spec((1, D_MODEL)), _const_spec((D_MODEL, N_IN_PAD)),
                  _const_spec((CONV_WIDTH, 2 * D_MLSTM)), _const_spec((1, 2 * D_MLSTM)),
                  _const_spec((1, LANES)), tab, tab, tab],
        out_specs=out_specs,
        out_shape=out_shape,
        scratch_shapes=[pltpu.VMEM((tm + 2 * SUBLANES, 2 * D_MLSTM), F32)],
        compiler_params=_params(("arbitrary", "arbitrary")),
        name="inproj_prompt",
    )(x, norm_g, w_pad, conv_w, conv_b, bif, cos, lo, hi)


def _inproj_sample(x2, norm_g, w_pad, conv_w, conv_b, bif, hist, t_new):
    rows = x2.shape[0]
    pos = PAST_LEN + jnp.arange(t_new, dtype=jnp.int32)
    cos, lo, hi = (jnp.tile(t, (rows // t_new, 1)) for t in _rope_tables(pos))
    bf = lambda: jax.ShapeDtypeStruct((rows, D_ATT), BF16)
    f3 = lambda: jax.ShapeDtypeStruct((rows, D_ATT), F32)
    out_shape = (bf(), f3(), f3(), bf(), bf(), bf(), bf(), bf(), bf(),
                 jax.ShapeDtypeStruct((rows, LANES), F32),
                 jax.ShapeDtypeStruct((rows, 2 * D_MLSTM), F32))
    return pl.pallas_call(
        functools.partial(_inproj_sample_kernel, t_new=t_new),
        out_shape=out_shape,
        compiler_params=pltpu.CompilerParams(vmem_limit_bytes=VMEM_LIMIT),
        name="inproj_sample",
    )(x2, norm_g, w_pad, conv_w, conv_b, bif, cos, lo, hi, *hist)


def _attn_prompt(q, k, v, gate):
    B, S, _ = q.shape
    spec = pl.BlockSpec((1, S, LANES), lambda b, h: (b, 0, h))
    return pl.pallas_call(
        _attn_prompt_kernel,
        grid=(B, D_ATT // LANES),
        in_specs=[spec, spec, spec, spec],
        out_specs=spec,
        out_shape=jax.ShapeDtypeStruct((B, S, D_ATT), BF16),
        scratch_shapes=[pltpu.VMEM((S, LANES), F32)] * 4,
        compiler_params=_params(("arbitrary", "arbitrary")),
        name="attn_prompt",
    )(q, k, v, gate)


def _attn_sample(q, kn, vn, gate, ck, cv):
    B, T, _ = q.shape
    nlb = D_ATT // LANES
    rows = ck.shape[1]
    small = pl.BlockSpec((1, T, D_ATT), lambda b: (b, 0, 0))
    small4 = pl.BlockSpec((1, nlb * T, LANES), lambda b: (b, 0, 0))
    big = pl.BlockSpec((1, rows, LANES), lambda b: (b, 0, 0))
    return pl.pallas_call(
        _attn_sample_kernel,
        grid=(B,),
        in_specs=[small, small, small, small4, small4, small, big, big],
        out_specs=(small, big, big),
        out_shape=(jax.ShapeDtypeStruct((B, T, D_ATT), BF16),
                   jax.ShapeDtypeStruct((B, rows, LANES), F32),
                   jax.ShapeDtypeStruct((B, rows, LANES), F32)),
        compiler_params=_params(("arbitrary",)),
        name="attn_sample",
    )(q, kn, vn, kn.reshape(B, nlb * T, LANES), vn.reshape(B, nlb * T, LANES), gate, ck, cv)


def _mlstm(q, k, v, gates_t, sig_o, gate_b, norm_g, cn0=None, m0=None):
    B, S, _ = q.shape
    nc = S // CHUNK
    tile = pl.BlockSpec((1, CHUNK, D_MLSTM), lambda b, c: (b, c, 0))
    gspec = pl.BlockSpec((1, SUBLANES, CHUNK), lambda b, c: (b, 0, c))
    cn_spec = pl.BlockSpec((1, N_HEADS_MLSTM, HEAD_DIM_MLSTM, 2 * HEAD_DIM_MLSTM),
                           lambda b, c: (b, 0, 0, 0))
    m_spec = pl.BlockSpec((1, SUBLANES, LANES), lambda b, c: (b, 0, 0))
    zero_init = cn0 is None
    in_specs = [tile, tile, tile, gspec, tile, tile, _const_spec((1, D_MLSTM))]
    args = [q, k, v, gates_t, sig_o, gate_b, norm_g]
    if not zero_init:
        in_specs += [cn_spec, m_spec]
        args += [cn0, m0]
    return pl.pallas_call(
        functools.partial(_mlstm_kernel, zero_init=zero_init),
        grid=(B, nc),
        in_specs=in_specs,
        out_specs=(tile, cn_spec, m_spec),
        out_shape=(jax.ShapeDtypeStruct((B, S, D_MLSTM), BF16),
                   jax.ShapeDtypeStruct((B, N_HEADS_MLSTM, HEAD_DIM_MLSTM, 2 * HEAD_DIM_MLSTM), F32),
                   jax.ShapeDtypeStruct((B, SUBLANES, LANES), F32)),
        scratch_shapes=[pltpu.VMEM((N_HEADS_MLSTM, HEAD_DIM_MLSTM, 2 * HEAD_DIM_MLSTM), F32),
                        pltpu.VMEM((SUBLANES, LANES), F32)],
        compiler_params=_params(("arbitrary", "arbitrary")),
        name="mlstm_prompt" if zero_init else "mlstm_sample",
    )(*args)


def _outproj(x2, ya, yb, w_out, final_g):
    rows = x2.shape[0]
    tm = min(ROW_TILE, rows)
    tile = lambda width: pl.BlockSpec((tm, width), lambda i: (i, 0))
    return pl.pallas_call(
        _outproj_kernel,
        grid=(rows // tm,),
        in_specs=[tile(D_MODEL), tile(D_ATT), tile(D_MLSTM),
                  _const_spec((D_ATT + D_MLSTM, D_MODEL)), _const_spec((1, D_MODEL))],
        out_specs=tile(D_MODEL),
        out_shape=jax.ShapeDtypeStruct((rows, D_MODEL), F32),
        compiler_params=_params(("arbitrary",)),
        name="outproj",
    )(x2, ya, yb, w_out, final_g)


def kernel(x_prompt, x_sample, cache_win_k, cache_win_v, state_conv, state_C, state_n, state_m,
           norm_g, w_in, conv_w, conv_b, b_i, b_f, mlstm_norm_g, w_out, final_norm_g):
    assert w_in.shape[0] == 1, "single-layer model"
    B, S, D = x_prompt.shape
    DB, T, _ = x_sample.shape
    HB, DK = N_HEADS_MLSTM, HEAD_DIM_MLSTM
    wb = cache_win_k.shape[2]
    assert S % ROW_TILE == 0 and S == 16 * N_BACK and wb >= 16 * N_BACK and CONV_WIDTH - 1 <= T <= SUBLANES

    w_pad = jnp.pad(w_in[0], ((0, 0), (0, N_IN_PAD - N_IN))).astype(BF16)
    w_o = w_out[0].astype(BF16)
    g_in = norm_g[0][None, :]
    cw, cb = conv_w[0], conv_b[0][None, :]
    bif = jnp.pad(jnp.concatenate([b_i[0], b_f[0]]), (0, LANES - 2 * HB))[None, :]
    ng = mlstm_norm_g[0][None, :]
    g_fin = final_norm_g[None, :]

    (q_p, pk, pv, ga_p, qm_p, km_p, vb_p, so_p, gb_p, gt_p, p_conv) = _inproj_prompt(
        x_prompt, g_in, w_pad, cw, cb, bif)
    ya_p = _attn_prompt(q_p, pk, pv, ga_p)
    yb_p, cn_p, m_p = _mlstm(qm_p, km_p, vb_p, gt_p, so_p, gb_p, ng)
    y_prompt = _outproj(x_prompt.reshape(B * S, D), ya_p.reshape(B * S, D_ATT),
                        yb_p.reshape(B * S, D_MLSTM), w_o, g_fin).reshape(B, S, D)

    sc = state_conv[0]
    zrow = jnp.zeros((DB, 1, 2 * D_MLSTM), F32)
    hist = []
    for sh in (1, 2, 3):
        rows_ = [sc[:, CONV_WIDTH - 1 + t - sh:CONV_WIDTH + t - sh] if t < sh else zrow
                 for t in range(T)]
        hist.append(jnp.concatenate(rows_, axis=1).reshape(DB * T, 2 * D_MLSTM))
    (q_s, kn, vn, ga_s, qm_s, km_s, vb_s, so_s, gb_s, gates_s, qk_s) = _inproj_sample(
        x_sample.reshape(DB * T, D), g_in, w_pad, cw, cb, bif, hist, T)
    r3 = lambda a: a.reshape(DB, T, a.shape[-1])
    ya_s, s_k, s_v = _attn_sample(r3(q_s), r3(kn), r3(vn), r3(ga_s),
                                  cache_win_k[0].reshape(DB, wb * D_ATT // LANES, LANES),
                                  cache_win_v[0].reshape(DB, wb * D_ATT // LANES, LANES))

    pad_t = lambda a: jnp.pad(r3(a), ((0, 0), (0, CHUNK - T), (0, 0)))
    g3 = r3(gates_s)[:, :, :SUBLANES]
    null_gate = jnp.concatenate([jnp.full((HB,), NEG_INF, F32), jnp.zeros((HB,), F32)])
    gt_s = jnp.concatenate([g3, jnp.broadcast_to(null_gate, (DB, CHUNK - T, SUBLANES))], axis=1)
    gt_s = gt_s.transpose(0, 2, 1)
    cn0 = jnp.concatenate([state_C[0], jnp.broadcast_to(state_n[0][..., None], (DB, HB, DK, DK))],
                          axis=-1)
    m0 = jnp.broadcast_to(jnp.pad(state_m[0], ((0, 0), (0, SUBLANES - HB)))[..., None],
                          (DB, SUBLANES, LANES))
    yb_s, cn_s, m_s = _mlstm(pad_t(qm_s), pad_t(km_s), pad_t(vb_s), gt_s, pad_t(so_s), pad_t(gb_s),
                             ng, cn0, m0)
    y_sample = _outproj(x_sample.reshape(DB * T, D), ya_s.reshape(DB * T, D_ATT),
                        yb_s[:, :T].reshape(DB * T, D_MLSTM), w_o, g_fin).reshape(DB, T, D)

    heads = lambda a, n: a.reshape(1, n, -1, N_HEADS_ATT, HEAD_DIM_ATT)
    return (y_prompt, y_sample,
            heads(pk, B), heads(pv, B), p_conv[None],
            cn_p[None, ..., :DK], cn_p[None, ..., DK], m_p[None, :, :HB, 0],
            heads(s_k, DB), heads(s_v, DB), r3(qk_s)[None, :, T - (CONV_WIDTH - 1):],
            cn_s[None, ..., :DK], cn_s[None, ..., DK], m_s[None, :, :HB, 0])
```

```python
import functools
import math

import jax
import jax.numpy as jnp
import numpy as np
from jax import lax
from jax.experimental import pallas as pl
from jax.experimental.pallas import tpu as pltpu

F32 = jnp.float32
BF16 = jnp.bfloat16

D_MODEL = 1024
D_ATT = 512
N_HEADS_ATT = 8
HEAD_DIM_ATT = 64
D_MLSTM = 512
N_HEADS_MLSTM = 4
HEAD_DIM_MLSTM = 128
ROT_DIM = 16
ROPE_THETA = 500000.0
PAST_LEN = 16384
DILATIONS = (1, 4, 16)
N_BACK = 128
CONV_WIDTH = 4
CHUNK = 128
EPS = 1e-6
NEG_INF = -1e30

LANES = 128
SUBLANES = 8
N_IN = 4 * D_ATT + 5 * D_MLSTM + 2 * N_HEADS_MLSTM
N_IN_PAD = 4 * D_ATT + 5 * D_MLSTM + LANES
OFF_QA, OFF_KA, OFF_VA, OFF_ZA = 0, 512, 1024, 1536
OFF_QB, OFF_VB, OFF_OB, OFF_ZB, OFF_G = 2048, 3072, 3584, 4096, 4608
ROW_TILE = 512
VMEM_LIMIT = 56 * 1024 * 1024


def _silu(x):
    return x * jax.nn.sigmoid(x)


def _rmsnorm(x, g):
    return x * lax.rsqrt(jnp.mean(x * x, axis=-1, keepdims=True) + EPS) * g


def _rope(u, cos, sin_lo, sin_hi):
    outs = []
    for c in range(u.shape[1] // LANES):
        xs = u[:, c * LANES:(c + 1) * LANES]
        outs.append(xs * cos + pltpu.roll(xs, LANES - ROT_DIM // 2, 1) * sin_lo
                    + pltpu.roll(xs, ROT_DIM // 2, 1) * sin_hi)
    return jnp.concatenate(outs, axis=1)


def _gate_block(ug, bif):
    gz = ug + bif
    lane = lax.broadcasted_iota(jnp.int32, gz.shape, 1)
    logf = jnp.minimum(gz, 0.0) - jnp.log1p(jnp.exp(-jnp.abs(gz)))
    return jnp.where(lane < N_HEADS_MLSTM, gz, logf)


def _inproj_common(hn, w_ref, cos, sin_lo, sin_hi,
                   q_ref, pk_ref, pv_ref, ga_ref, vb_ref, so_ref, gb_ref):
    def seg(off, width):
        return jnp.dot(hn, w_ref[:, off:off + width], preferred_element_type=F32)

    q = _rope(seg(OFF_QA, D_ATT), cos, sin_lo, sin_hi) * (HEAD_DIM_ATT ** -0.5)
    q_ref[...] = q.astype(BF16).reshape(q_ref.shape)
    pk_ref[...] = _rope(seg(OFF_KA, D_ATT), cos, sin_lo, sin_hi).reshape(pk_ref.shape)
    pv_ref[...] = seg(OFF_VA, D_ATT).reshape(pv_ref.shape)
    ga_ref[...] = _silu(seg(OFF_ZA, D_ATT)).astype(BF16).reshape(ga_ref.shape)
    vb_ref[...] = seg(OFF_VB, D_MLSTM).astype(BF16).reshape(vb_ref.shape)
    so_ref[...] = jax.nn.sigmoid(seg(OFF_OB, D_MLSTM)).astype(BF16).reshape(so_ref.shape)
    gb_ref[...] = _silu(seg(OFF_ZB, D_MLSTM)).astype(BF16).reshape(gb_ref.shape)
    return seg


def _inproj_prompt_kernel(x_ref, g_ref, w_ref, cw_ref, cb_ref, bif_ref, cos_ref, slo_ref, shi_ref,
                          q_ref, pk_ref, pv_ref, ga_ref, qm_ref, km_ref, vb_ref, so_ref, gb_ref,
                          gt_ref, pconv_ref, xp_ref):
    j = pl.program_id(1)
    tm = x_ref.shape[1]
    hn = _rmsnorm(x_ref[0], g_ref[...]).astype(BF16)
    seg = _inproj_common(hn, w_ref, cos_ref[...], slo_ref[...], shi_ref[...],
                         q_ref, pk_ref, pv_ref, ga_ref, vb_ref, so_ref, gb_ref)

    @pl.when(j == 0)
    def _():
        xp_ref[0:SUBLANES, :] = jnp.zeros((SUBLANES, 2 * D_MLSTM), F32)

    @pl.when(j > 0)
    def _():
        xp_ref[0:SUBLANES, :] = xp_ref[tm:tm + SUBLANES, :]

    xp_ref[SUBLANES:SUBLANES + tm, :] = seg(OFF_QB, 2 * D_MLSTM)
    y = cb_ref[...] + xp_ref[SUBLANES:SUBLANES + tm, :] * cw_ref[3:4, :]
    for jj in range(CONV_WIDTH - 1):
        sh = CONV_WIDTH - 1 - jj
        y = y + xp_ref[SUBLANES - sh:SUBLANES - sh + tm, :] * cw_ref[jj:jj + 1, :]
    y = _silu(y)
    qm_ref[0] = y[:, :D_MLSTM].astype(BF16)
    km_ref[0] = (y[:, D_MLSTM:] * (HEAD_DIM_MLSTM ** -0.5)).astype(BF16)

    @pl.when(j == pl.num_programs(1) - 1)
    def _():
        pconv_ref[0] = xp_ref[tm + SUBLANES - (CONV_WIDTH - 1):tm + SUBLANES, :]

    gates = _gate_block(seg(OFF_G, LANES), bif_ref[...])
    for i in range(tm // LANES):
        gt = gates[i * LANES:(i + 1) * LANES, :].T
        gt_ref[0, :, i * LANES:(i + 1) * LANES] = gt[0:SUBLANES, :]


def _inproj_sample_kernel(x_ref, g_ref, w_ref, cw_ref, cb_ref, bif_ref, cos_ref, slo_ref, shi_ref,
                          h1_ref, h2_ref, h3_ref,
                          q_ref, pk_ref, pv_ref, ga_ref, qm_ref, km_ref, vb_ref, so_ref, gb_ref,
                          gates_ref, qk_ref, *, t_new):
    hn = _rmsnorm(x_ref[...], g_ref[...]).astype(BF16)
    seg = _inproj_common(hn, w_ref, cos_ref[...], slo_ref[...], shi_ref[...],
                         q_ref, pk_ref, pv_ref, ga_ref, vb_ref, so_ref, gb_ref)
    u = seg(OFF_QB, 2 * D_MLSTM)
    qk_ref[...] = u
    t = lax.rem(lax.broadcasted_iota(jnp.int32, u.shape, 0), t_new)
    y = cb_ref[...] + u * cw_ref[3:4, :]
    for sh, h_ref in ((1, h1_ref), (2, h2_ref), (3, h3_ref)):
        prev = jnp.where(t >= sh, pltpu.roll(u, sh, 0), h_ref[...])
        y = y + prev * cw_ref[3 - sh:4 - sh, :]
    y = _silu(y)
    qm_ref[...] = y[:, :D_MLSTM].astype(BF16)
    km_ref[...] = (y[:, D_MLSTM:] * (HEAD_DIM_MLSTM ** -0.5)).astype(BF16)
    gates_ref[...] = _gate_block(seg(OFF_G, LANES), bif_ref[...])


def _attn_prompt_kernel(q_ref, k_ref, v_ref, g_ref, o_ref,
                        src1_ref, src4_ref, src16_ref, bias_ref, st16_ref, st16p_ref, st4_ref):
    seq = q_ref.shape[1]
    n16 = seq // 16
    group = 4
    head_a_full = lax.broadcasted_iota(jnp.int32, (seq, LANES), 1) < HEAD_DIM_ATT
    by16 = lambda x: jnp.swapaxes(x.reshape(n16, 16, LANES), 0, 1)
    q = q_ref[0].astype(F32)
    k = k_ref[0]
    v = v_ref[0]
    ops = (jnp.where(head_a_full, q, 0.0), jnp.where(head_a_full, 0.0, q), k,
           jnp.where(head_a_full, v, 1.0), jnp.where(head_a_full, 1.0, v))
    for i, x in enumerate(ops):
        src4_ref[i] = x
        src1_ref[i] = x.astype(BF16)
    head_a_16 = lax.broadcasted_iota(jnp.int32, (16, n16, LANES), 2) < HEAD_DIM_ATT
    q16, k16, v16 = by16(q), by16(k), by16(v)
    ops16 = (jnp.where(head_a_16, q16, 0.0), jnp.where(head_a_16, 0.0, q16), k16,
             jnp.where(head_a_16, v16, 1.0), jnp.where(head_a_16, 1.0, v16))
    for i, x in enumerate(ops16):
        src16_ref[i] = x.astype(BF16)
    u = lax.broadcasted_iota(jnp.int32, (N_BACK, 2 * N_BACK), 0)
    w = lax.broadcasted_iota(jnp.int32, (N_BACK, 2 * N_BACK), 1)
    bias_ref[...] = jnp.where((w >= u) & (w <= u + N_BACK), jnp.finfo(F32).max, NEG_INF)
    head_a = lax.broadcasted_iota(jnp.int32, (N_BACK, LANES), 1) < HEAD_DIM_ATT

    def partials(gets):
        staged = []
        for get, has_prev in gets:
            if has_prev:
                both = lambda i, get=get: jnp.concatenate([get(i, True), get(i, False)], axis=0)
                bias = bias_ref[...]
            else:
                both = lambda i, get=get: get(i, False)
                bias = bias_ref[:, N_BACK:]
            kk = both(2)
            heads = []
            for qi in (0, 1):
                s = lax.dot_general(get(qi, False), kk, (((1,), (1,)), ((), ())),
                                    preferred_element_type=F32)
                s = jnp.minimum(s, bias)
                mh = jnp.max(s, axis=-1, keepdims=True)
                heads.append((mh, jnp.exp(s - mh).astype(BF16)))
            staged.append((both, heads))
        out = []
        for both, ((m_a, p_a), (m_b, p_b)) in staged:
            pv_a = jnp.dot(p_a, both(3), preferred_element_type=F32)
            pv_b = jnp.dot(p_b, both(4), preferred_element_type=F32)
            acc = jnp.where(head_a, pv_a, pv_b)
            den = pltpu.roll(jnp.where(head_a, pv_b, pv_a), HEAD_DIM_ATT, 1)
            out.append((jnp.where(head_a, m_a, m_b), den, acc))
        return out

    def get4(qstart, kprev_start):
        def get(i, prev):
            start = kprev_start if prev else qstart
            return src4_ref[i, pl.ds(start, N_BACK, stride=4), :].astype(BF16)
        return get, kprev_start is not None

    def get1(qstart, kprev_start):
        def get(i, prev):
            return src1_ref[i, pl.ds(kprev_start if prev else qstart, N_BACK), :]
        return get, kprev_start is not None

    def keep4(blocks):
        res = partials([get4(qs, ks) for qs, ks in blocks])
        for (qs, _), (m, den, acc) in zip(blocks, res):
            rows = pl.ds(qs, N_BACK, stride=4)
            st4_ref[0, rows, :] = m
            st4_ref[1, rows, :] = den
            st4_ref[2, rows, :] = acc

    def finish(blocks):
        res = partials([get1(qs, ks) for qs, ks in blocks])
        for (qs, _), part in zip(blocks, res):
            rows = pl.ds(qs, N_BACK)
            parts = [part] + [tuple(st[i, rows, :] for i in range(3)) for st in (st4_ref, st16p_ref)]
            m_all = functools.reduce(jnp.maximum, [pt[0] for pt in parts])
            wts = [jnp.exp(pt[0] - m_all) for pt in parts]
            den = sum(wt * pt[1] for wt, pt in zip(wts, parts))
            num = sum(wt * pt[2] for wt, pt in zip(wts, parts))
            o_ref[0, rows, :] = (num / den * g_ref[0, rows, :].astype(F32)).astype(BF16)

    def body16(g, c):
        rs = [g * group + rr for rr in range(group)]
        res = partials([(lambda i, prev, r=r: src16_ref[i, r], False) for r in rs])
        for r, (m, den, acc) in zip(rs, res):
            st16_ref[0, r] = m
            st16_ref[1, r] = den
            st16_ref[2, r] = acc
        return c
    lax.fori_loop(0, 16 // group, body16, 0)
    for i in range(3):
        st16p_ref[i] = jnp.swapaxes(st16_ref[i], 0, 1).reshape(seq, LANES)

    keep4([(r, None) for r in range(4)])

    def body4(cc, c):
        keep4([(r + 4 * N_BACK * cc, r + 4 * N_BACK * (cc - 1)) for r in range(4)])
        return c
    lax.fori_loop(1, seq // 4 // N_BACK, body4, 0)

    finish([(0, None)] + [(cc * N_BACK, (cc - 1) * N_BACK) for cc in range(1, group)])

    def body1(g, c):
        starts = [pl.multiple_of((g * group + rr) * N_BACK, N_BACK) for rr in range(group)]
        finish([(st, st - N_BACK) for st in starts])
        return c
    lax.fori_loop(1, seq // N_BACK // group, body1, 0)


def _attn_sample_kernel(q_ref, kn_ref, vn_ref, kn4_ref, vn4_ref, g_ref, ck_ref, cv_ref,
                        o_ref, sk_ref, sv_ref):
    nlb = D_ATT // LANES
    wb = ck_ref.shape[1] // nlb
    t_new = q_ref.shape[1]
    kn = kn_ref[0]
    vn = vn_ref[0]
    sk_ref[0, 0:nlb * (wb - t_new), :] = ck_ref[0, nlb * t_new:nlb * wb, :]
    sk_ref[0, nlb * (wb - t_new):nlb * wb, :] = kn4_ref[0]
    sv_ref[0, 0:nlb * (wb - t_new), :] = cv_ref[0, nlb * t_new:nlb * wb, :]
    sv_ref[0, nlb * (wb - t_new):nlb * wb, :] = vn4_ref[0]

    def gather(ref, start, dil):
        return jnp.concatenate(
            [ref[0, pl.ds(nlb * start + c, N_BACK, stride=nlb * dil), :] for c in range(nlb)],
            axis=1).astype(BF16)

    lane8 = lax.broadcasted_iota(jnp.int32, (N_HEADS_ATT, D_ATT), 1)
    row8 = lax.broadcasted_iota(jnp.int32, (N_HEADS_ATT, D_ATT), 0)
    head_mask = (lane8 // HEAD_DIM_ATT) == row8
    nt = (((1,), (1,)), ((), ()))

    col = lax.broadcasted_iota(jnp.int32, (N_HEADS_ATT, N_BACK), 1)
    outs = []
    for t in range(t_new):
        qrow = q_ref[0, t:t + 1, :].astype(F32)
        qh32 = jnp.where(head_mask, jnp.broadcast_to(qrow, (N_HEADS_ATT, D_ATT)), 0.0)
        qh = qh32.astype(BF16)
        s_new = [jnp.sum(qh32 * kn[tt:tt + 1, :], axis=-1, keepdims=True) for tt in range(t + 1)]
        parts = []
        for dil in DILATIONS:
            if dil == 1:
                start = wb - N_BACK
                extra = range(t + 1)
            else:
                start = wb + t - N_BACK * dil
                extra = (t,)
            kk = gather(ck_ref, start, dil)
            vv = gather(cv_ref, start, dil)
            s = lax.dot_general(qh, kk, nt, preferred_element_type=F32)
            if dil == 1 and t > 0:
                s = jnp.where(col >= t, s, NEG_INF)
            m = jnp.max(s, axis=-1, keepdims=True)
            for tt in extra:
                m = jnp.maximum(m, s_new[tt])
            p = jnp.exp(s - m)
            den = jnp.sum(p, axis=-1, keepdims=True)
            acc = jnp.dot(p.astype(BF16), vv, preferred_element_type=F32)
            for tt in extra:
                pe = jnp.exp(s_new[tt] - m)
                den = den + pe
                acc = acc + pe * vn[tt:tt + 1, :]
            parts.append((m, den, acc))
        m_all = functools.reduce(jnp.maximum, [pt[0] for pt in parts])
        den = sum(jnp.exp(pt[0] - m_all) * pt[1] for pt in parts)
        num = sum(jnp.exp(pt[0] - m_all) * pt[2] for pt in parts)
        att = jnp.where(head_mask, num / den, 0.0)
        outs.append(jnp.sum(att, axis=0, keepdims=True))
    att = jnp.concatenate(outs, axis=0)
    o_ref[0] = (att * g_ref[0].astype(F32)).astype(BF16)


def _scan_lanes(x, op, fill):
    lane = lax.broadcasted_iota(jnp.int32, x.shape, 1)
    d = 1
    while d < x.shape[1]:
        x = op(x, jnp.where(lane >= d, pltpu.roll(x, d, 1), fill))
        d *= 2
    return x


def _mlstm_kernel(*refs, zero_init):
    if zero_init:
        (q_ref, k_ref, v_ref, gt_ref, so_ref, gb_ref, ng_ref,
         y_ref, cn_out_ref, m_out_ref, cn_ref, m_ref) = refs
    else:
        (q_ref, k_ref, v_ref, gt_ref, so_ref, gb_ref, ng_ref, cn0_ref, m0_ref,
         y_ref, cn_out_ref, m_out_ref, cn_ref, m_ref) = refs
    c_idx = pl.program_id(1)
    L = CHUNK
    dh = HEAD_DIM_MLSTM

    @pl.when(c_idx == 0)
    def _():
        if zero_init:
            cn_ref[...] = jnp.zeros(cn_ref.shape, F32)
            m_ref[...] = jnp.zeros(m_ref.shape, F32)
        else:
            cn_ref[...] = cn0_ref[0]
            m_ref[...] = m0_ref[0]

    gt = gt_ref[0]
    i_row = gt
    f_row = pltpu.roll(gt, N_HEADS_MLSTM, 0)
    m_prev = m_ref[...]
    b = _scan_lanes(f_row, jnp.add, 0.0)
    cm = _scan_lanes(i_row - b, jnp.maximum, NEG_INF)
    m_t = jnp.maximum(m_prev + b, b + cm)
    inter = jnp.exp(m_prev + b - m_t)
    a_row = b - m_t
    r_row = i_row - b
    en_row = jnp.exp(-m_t)
    m_last = jnp.broadcast_to(m_t[:, L - 1:L], m_t.shape)
    b_last = jnp.broadcast_to(b[:, L - 1:L], b.shape)
    decay = jnp.exp(m_prev + b_last - m_last)
    decay2 = jnp.concatenate([decay, decay], axis=1)
    w_row = jnp.exp(b_last - b + i_row - m_last)
    m_ref[...] = m_last

    stack = jnp.concatenate([a_row, inter, en_row, w_row,
                             jnp.zeros((L - 4 * SUBLANES, L), F32)], axis=0)
    cols = stack.T

    tri_t = lax.broadcasted_iota(jnp.int32, (L, L), 0)
    tri_s = lax.broadcasted_iota(jnp.int32, (L, L), 1)
    causal = tri_t >= tri_s
    ones_blk = jnp.ones((L, dh), BF16)

    for h in range(N_HEADS_MLSTM):
        sl = slice(h * dh, (h + 1) * dh)
        q = q_ref[0, :, sl]
        k = k_ref[0, :, sl]
        v = v_ref[0, :, sl]
        a_col = cols[:, h:h + 1]
        inter_col = cols[:, SUBLANES + h:SUBLANES + h + 1]
        en_col = cols[:, 2 * SUBLANES + h:2 * SUBLANES + h + 1]
        w_col = cols[:, 3 * SUBLANES + h:3 * SUBLANES + h + 1]
        dlog = a_col + r_row[h:h + 1, :]
        dmat = jnp.exp(jnp.where(causal, dlog, NEG_INF))
        sqk = lax.dot_general(q, k, (((1,), (1,)), ((), ())), preferred_element_type=F32) * dmat
        cn = cn_ref[h]
        tot = inter_col * jnp.dot(q, cn.astype(BF16), preferred_element_type=F32)
        tot = tot + jnp.dot(sqk.astype(BF16), jnp.concatenate([v, ones_blk], axis=1),
                            preferred_element_type=F32)
        num = tot[:, :dh]
        nq = tot[:, dh:]
        hh = num / jnp.maximum(jnp.abs(nq), en_col)
        hh = so_ref[0, :, sl].astype(F32) * hh
        hh = hh * lax.rsqrt(jnp.mean(hh * hh, axis=-1, keepdims=True) + EPS)
        hh = hh * ng_ref[:, sl]
        y_ref[0, :, sl] = (hh * gb_ref[0, :, sl].astype(F32)).astype(BF16)

        wv = jnp.concatenate([w_col * v.astype(F32), jnp.broadcast_to(w_col, (L, dh))], axis=1)
        upd = lax.dot_general(k, wv.astype(BF16), (((0,), (0,)), ((), ())),
                              preferred_element_type=F32)
        cn_ref[h] = decay2[h:h + 1, :] * cn + upd

    @pl.when(c_idx == pl.num_programs(1) - 1)
    def _():
        cn_out_ref[0] = cn_ref[...]
        m_out_ref[0] = m_ref[...]


def _outproj_kernel(x_ref, ya_ref, yb_ref, w_ref, g_ref, o_ref):
    mix = jnp.dot(ya_ref[...], w_ref[0:D_ATT, :], preferred_element_type=F32)
    mix = mix + jnp.dot(yb_ref[...], w_ref[D_ATT:, :], preferred_element_type=F32)
    o_ref[...] = _rmsnorm(x_ref[...] + mix, g_ref[...])


def _rope_tables(pos):
    half = ROT_DIM // 2
    inv = ROPE_THETA ** (-jnp.arange(half, dtype=F32) * 2.0 / ROT_DIM)
    ang = pos.astype(F32)[:, None] * inv[None, :]
    cos, sin = jnp.cos(ang), jnp.sin(ang)
    n = pos.shape[0]
    one = jnp.ones((n, HEAD_DIM_ATT - ROT_DIM), F32)
    zero = jnp.zeros((n, HEAD_DIM_ATT - ROT_DIM), F32)
    zh = jnp.zeros((n, half), F32)
    cos_t = jnp.concatenate([cos, cos, one], axis=1)
    lo_t = jnp.concatenate([-sin, zh, zero], axis=1)
    hi_t = jnp.concatenate([zh, sin, zero], axis=1)
    rep = lambda t: jnp.concatenate([t, t], axis=1)
    return rep(cos_t), rep(lo_t), rep(hi_t)


def _params(sem):
    return pltpu.CompilerParams(dimension_semantics=sem, vmem_limit_bytes=VMEM_LIMIT)


def _const_spec(shape):
    return pl.BlockSpec(shape, lambda *_: (0,) * len(shape))


def _inproj_prompt(x, norm_g, w_pad, conv_w, conv_b, bif):
    B, S, _ = x.shape
    tm = ROW_TILE
    cos, lo, hi = _rope_tables(jnp.arange(S, dtype=jnp.int32))
    tile = lambda width: pl.BlockSpec((1, tm, width), lambda b, j: (b, j, 0))
    tab = pl.BlockSpec((tm, LANES), lambda b, j: (j, 0))
    bf = lambda: jax.ShapeDtypeStruct((B, S, D_ATT), BF16)
    f3 = lambda: jax.ShapeDtypeStruct((B, S, D_ATT), F32)
    out_shape = (bf(), f3(), f3(), bf(), bf(), bf(), bf(), bf(), bf(),
                 jax.ShapeDtypeStruct((B, SUBLANES, S), F32),
                 jax.ShapeDtypeStruct((B, CONV_WIDTH - 1, 2 * D_MLSTM), F32))
    out_specs = tuple([tile(D_ATT)] * 9) + (
        pl.BlockSpec((1, SUBLANES, tm), lambda b, j: (b, 0, j)),
        pl.BlockSpec((1, CONV_WIDTH - 1, 2 * D_MLSTM), lambda b, j: (b, 0, 0)))
    return pl.pallas_call(
        _inproj_prompt_kernel,
        grid=(B, S // tm),
        in_specs=[tile(D_MODEL), _const_spec((1, D_MODEL)), _const_spec((D_MODEL, N_IN_PAD)),
                  _const_spec((CONV_WIDTH, 2 * D_MLSTM)), _const_spec((1, 2 * D_MLSTM)),
                  _const_spec((1, LANES)), tab, tab, tab],
        out_specs=out_specs,
        out_shape=out_shape,
        scratch_shapes=[pltpu.VMEM((tm + 2 * SUBLANES, 2 * D_MLSTM), F32)],
        compiler_params=_params(("arbitrary", "arbitrary")),
        name="inproj_prompt",
    )(x, norm_g, w_pad, conv_w, conv_b, bif, cos, lo, hi)


def _inproj_sample(x2, norm_g, w_pad, conv_w, conv_b, bif, hist, t_new):
    rows = x2.shape[0]
    pos = PAST_LEN + jnp.arange(t_new, dtype=jnp.int32)
    cos, lo, hi = (jnp.tile(t, (rows // t_new, 1)) for t in _rope_tables(pos))
    bf = lambda: jax.ShapeDtypeStruct((rows, D_ATT), BF16)
    f3 = lambda: jax.ShapeDtypeStruct((rows, D_ATT), F32)
    out_shape = (bf(), f3(), f3(), bf(), bf(), bf(), bf(), bf(), bf(),
                 jax.ShapeDtypeStruct((rows, LANES), F32),
                 jax.ShapeDtypeStruct((rows, 2 * D_MLSTM), F32))
    return pl.pallas_call(
        functools.partial(_inproj_sample_kernel, t_new=t_new),
        out_shape=out_shape,
        compiler_params=pltpu.CompilerParams(vmem_limit_bytes=VMEM_LIMIT),
        name="inproj_sample",
    )(x2, norm_g, w_pad, conv_w, conv_b, bif, cos, lo, hi, *hist)


def _attn_prompt(q, k, v, gate):
    B, S, _ = q.shape
    spec = pl.BlockSpec((1, S, LANES), lambda b, h: (b, 0, h))
    return pl.pallas_call(
        _attn_prompt_kernel,
        grid=(B, D_ATT // LANES),
        in_specs=[spec, spec, spec, spec],
        out_specs=spec,
        out_shape=jax.ShapeDtypeStruct((B, S, D_ATT), BF16),
        scratch_shapes=[
            pltpu.VMEM((5, S, LANES), BF16), pltpu.VMEM((5, S, LANES), F32),
            pltpu.VMEM((5, 16, S // 16, LANES), BF16),
            pltpu.VMEM((N_BACK, 2 * N_BACK), F32),
            pltpu.VMEM((3, 16, S // 16, LANES), F32),
            pltpu.VMEM((3, S, LANES), F32), pltpu.VMEM((3, S, LANES), F32)],
        compiler_params=_params(("arbitrary", "arbitrary")),
        name="attn_prompt",
    )(q, k, v, gate)


def _attn_sample(q, kn, vn, gate, ck, cv):
    B, T, _ = q.shape
    nlb = D_ATT // LANES
    rows = ck.shape[1]
    small = pl.BlockSpec((1, T, D_ATT), lambda b: (b, 0, 0))
    small4 = pl.BlockSpec((1, nlb * T, LANES), lambda b: (b, 0, 0))
    big = pl.BlockSpec((1, rows, LANES), lambda b: (b, 0, 0))
    return pl.pallas_call(
        _attn_sample_kernel,
        grid=(B,),
        in_specs=[small, small, small, small4, small4, small, big, big],
        out_specs=(small, big, big),
        out_shape=(jax.ShapeDtypeStruct((B, T, D_ATT), BF16),
                   jax.ShapeDtypeStruct((B, rows, LANES), F32),
                   jax.ShapeDtypeStruct((B, rows, LANES), F32)),
        compiler_params=_params(("arbitrary",)),
        name="attn_sample",
    )(q, kn, vn, kn.reshape(B, nlb * T, LANES), vn.reshape(B, nlb * T, LANES), gate, ck, cv)


def _mlstm(q, k, v, gates_t, sig_o, gate_b, norm_g, cn0=None, m0=None):
    B, S, _ = q.shape
    nc = S // CHUNK
    tile = pl.BlockSpec((1, CHUNK, D_MLSTM), lambda b, c: (b, c, 0))
    gspec = pl.BlockSpec((1, SUBLANES, CHUNK), lambda b, c: (b, 0, c))
    cn_spec = pl.BlockSpec((1, N_HEADS_MLSTM, HEAD_DIM_MLSTM, 2 * HEAD_DIM_MLSTM),
                           lambda b, c: (b, 0, 0, 0))
    m_spec = pl.BlockSpec((1, SUBLANES, LANES), lambda b, c: (b, 0, 0))
    zero_init = cn0 is None
    in_specs = [tile, tile, tile, gspec, tile, tile, _const_spec((1, D_MLSTM))]
    args = [q, k, v, gates_t, sig_o, gate_b, norm_g]
    if not zero_init:
        in_specs += [cn_spec, m_spec]
        args += [cn0, m0]
    return pl.pallas_call(
        functools.partial(_mlstm_kernel, zero_init=zero_init),
        grid=(B, nc),
        in_specs=in_specs,
        out_specs=(tile, cn_spec, m_spec),
        out_shape=(jax.ShapeDtypeStruct((B, S, D_MLSTM), BF16),
                   jax.ShapeDtypeStruct((B, N_HEADS_MLSTM, HEAD_DIM_MLSTM, 2 * HEAD_DIM_MLSTM), F32),
                   jax.ShapeDtypeStruct((B, SUBLANES, LANES), F32)),
        scratch_shapes=[pltpu.VMEM((N_HEADS_MLSTM, HEAD_DIM_MLSTM, 2 * HEAD_DIM_MLSTM), F32),
                        pltpu.VMEM((SUBLANES, LANES), F32)],
        compiler_params=_params(("arbitrary", "arbitrary")),
        name="mlstm_prompt" if zero_init else "mlstm_sample",
    )(*args)


def _outproj(x2, ya, yb, w_out, final_g):
    rows = x2.shape[0]
    tm = min(ROW_TILE, rows)
    tile = lambda width: pl.BlockSpec((tm, width), lambda i: (i, 0))
    return pl.pallas_call(
        _outproj_kernel,
        grid=(rows // tm,),
        in_specs=[tile(D_MODEL), tile(D_ATT), tile(D_MLSTM),
                  _const_spec((D_ATT + D_MLSTM, D_MODEL)), _const_spec((1, D_MODEL))],
        out_specs=tile(D_MODEL),
        out_shape=jax.ShapeDtypeStruct((rows, D_MODEL), F32),
        compiler_params=_params(("arbitrary",)),
        name="outproj",
    )(x2, ya, yb, w_out, final_g)


def kernel(x_prompt, x_sample, cache_win_k, cache_win_v, state_conv, state_C, state_n, state_m,
           norm_g, w_in, conv_w, conv_b, b_i, b_f, mlstm_norm_g, w_out, final_norm_g):
    assert w_in.shape[0] == 1, "single-layer model"
    B, S, D = x_prompt.shape
    DB, T, _ = x_sample.shape
    HB, DK = N_HEADS_MLSTM, HEAD_DIM_MLSTM
    wb = cache_win_k.shape[2]
    assert S % ROW_TILE == 0 and S == 16 * N_BACK and wb >= 16 * N_BACK and CONV_WIDTH - 1 <= T <= SUBLANES

    w_pad = jnp.pad(w_in[0], ((0, 0), (0, N_IN_PAD - N_IN))).astype(BF16)
    w_o = w_out[0].astype(BF16)
    g_in = norm_g[0][None, :]
    cw, cb = conv_w[0], conv_b[0][None, :]
    bif = jnp.pad(jnp.concatenate([b_i[0], b_f[0]]), (0, LANES - 2 * HB))[None, :]
    ng = mlstm_norm_g[0][None, :]
    g_fin = final_norm_g[None, :]

    (q_p, pk, pv, ga_p, qm_p, km_p, vb_p, so_p, gb_p, gt_p, p_conv) = _inproj_prompt(
        x_prompt, g_in, w_pad, cw, cb, bif)
    ya_p = _attn_prompt(q_p, pk, pv, ga_p)
    yb_p, cn_p, m_p = _mlstm(qm_p, km_p, vb_p, gt_p, so_p, gb_p, ng)
    y_prompt = _outproj(x_prompt.reshape(B * S, D), ya_p.reshape(B * S, D_ATT),
                        yb_p.reshape(B * S, D_MLSTM), w_o, g_fin).reshape(B, S, D)

    sc = state_conv[0]
    zrow = jnp.zeros((DB, 1, 2 * D_MLSTM), F32)
    hist = []
    for sh in (1, 2, 3):
        rows_ = [sc[:, CONV_WIDTH - 1 + t - sh:CONV_WIDTH + t - sh] if t < sh else zrow
                 for t in range(T)]
        hist.append(jnp.concatenate(rows_, axis=1).reshape(DB * T, 2 * D_MLSTM))
    (q_s, kn, vn, ga_s, qm_s, km_s, vb_s, so_s, gb_s, gates_s, qk_s) = _inproj_sample(
        x_sample.reshape(DB * T, D), g_in, w_pad, cw, cb, bif, hist, T)
    r3 = lambda a: a.reshape(DB, T, a.shape[-1])
    ya_s, s_k, s_v = _attn_sample(r3(q_s), r3(kn), r3(vn), r3(ga_s),
                                  cache_win_k[0].reshape(DB, wb * D_ATT // LANES, LANES),
                                  cache_win_v[0].reshape(DB, wb * D_ATT // LANES, LANES))

    pad_t = lambda a: jnp.pad(r3(a), ((0, 0), (0, CHUNK - T), (0, 0)))
    g3 = r3(gates_s)[:, :, :SUBLANES]
    null_gate = jnp.concatenate([jnp.full((HB,), NEG_INF, F32), jnp.zeros((HB,), F32)])
    gt_s = jnp.concatenate([g3, jnp.broadcast_to(null_gate, (DB, CHUNK - T, SUBLANES))], axis=1)
    gt_s = gt_s.transpose(0, 2, 1)
    cn0 = jnp.concatenate([state_C[0], jnp.broadcast_to(state_n[0][..., None], (DB, HB, DK, DK))],
                          axis=-1)
    m0 = jnp.broadcast_to(jnp.pad(state_m[0], ((0, 0), (0, SUBLANES - HB)))[..., None],
                          (DB, SUBLANES, LANES))
    yb_s, cn_s, m_s = _mlstm(pad_t(qm_s), pad_t(km_s), pad_t(vb_s), gt_s, pad_t(so_s), pad_t(gb_s),
                             ng, cn0, m0)
    y_sample = _outproj(x_sample.reshape(DB * T, D), ya_s.reshape(DB * T, D_ATT),
                        yb_s[:, :T].reshape(DB * T, D_MLSTM), w_o, g_fin).reshape(DB, T, D)

    heads = lambda a, n: a.reshape(1, n, -1, N_HEADS_ATT, HEAD_DIM_ATT)
    return (y_prompt, y_sample,
            heads(pk, B), heads(pv, B), p_conv[None],
            cn_p[None, ..., :DK], cn_p[None, ..., DK], m_p[None, :, :HB, 0],
            heads(s_k, DB), heads(s_v, DB), r3(qk_s)[None, :, T - (CONV_WIDTH - 1):],
            cn_s[None, ..., :DK], cn_s[None, ..., DK], m_s[None, :, :HB, 0])
```

```python
import functools
import math

import jax
import jax.numpy as jnp
import numpy as np
from jax import lax
from jax.experimental import pallas as pl
from jax.experimental.pallas import tpu as pltpu

F32 = jnp.float32
BF16 = jnp.bfloat16

D_MODEL = 1024
D_ATT = 512
N_HEADS_ATT = 8
HEAD_DIM_ATT = 64
D_MLSTM = 512
N_HEADS_MLSTM = 4
HEAD_DIM_MLSTM = 128
ROT_DIM = 16
ROPE_THETA = 500000.0
PAST_LEN = 16384
DILATIONS = (1, 4, 16)
N_BACK = 128
CONV_WIDTH = 4
CHUNK = 128
EPS = 1e-6
NEG_INF = -1e30

LANES = 128
SUBLANES = 8
N_IN = 4 * D_ATT + 5 * D_MLSTM + 2 * N_HEADS_MLSTM
N_IN_PAD = 4 * D_ATT + 5 * D_MLSTM + LANES
OFF_QA, OFF_KA, OFF_VA, OFF_ZA = 0, 512, 1024, 1536
OFF_QB, OFF_VB, OFF_OB, OFF_ZB, OFF_G = 2048, 3072, 3584, 4096, 4608
ROW_TILE = 512
VMEM_LIMIT = 56 * 1024 * 1024


def _silu(x):
    return x * jax.nn.sigmoid(x)


def _rmsnorm(x, g):
    return x * lax.rsqrt(jnp.mean(x * x, axis=-1, keepdims=True) + EPS) * g


def _rope(u, cos, sin_lo, sin_hi):
    outs = []
    for c in range(u.shape[1] // LANES):
        xs = u[:, c * LANES:(c + 1) * LANES]
        outs.append(xs * cos + pltpu.roll(xs, LANES - ROT_DIM // 2, 1) * sin_lo
                    + pltpu.roll(xs, ROT_DIM // 2, 1) * sin_hi)
    return jnp.concatenate(outs, axis=1)


def _gate_block(ug, bif):
    gz = ug + bif
    lane = lax.broadcasted_iota(jnp.int32, gz.shape, 1)
    logf = jnp.minimum(gz, 0.0) - jnp.log1p(jnp.exp(-jnp.abs(gz)))
    return jnp.where(lane < N_HEADS_MLSTM, gz, logf)


def _inproj_common(hn, w_ref, cos, sin_lo, sin_hi, q_ref, ga_ref, vb_ref, so_ref, gb_ref):
    def seg(off, width):
        return jnp.dot(hn, w_ref[:, off:off + width], preferred_element_type=F32)

    q = _rope(seg(OFF_QA, D_ATT), cos, sin_lo, sin_hi) * (HEAD_DIM_ATT ** -0.5)
    q_ref[...] = q.astype(BF16).reshape(q_ref.shape)
    k = _rope(seg(OFF_KA, D_ATT), cos, sin_lo, sin_hi)
    v = seg(OFF_VA, D_ATT)
    ga_ref[...] = _silu(seg(OFF_ZA, D_ATT)).astype(BF16).reshape(ga_ref.shape)
    vb_ref[...] = seg(OFF_VB, D_MLSTM).astype(BF16).reshape(vb_ref.shape)
    so_ref[...] = jax.nn.sigmoid(seg(OFF_OB, D_MLSTM)).astype(BF16).reshape(so_ref.shape)
    gb_ref[...] = _silu(seg(OFF_ZB, D_MLSTM)).astype(BF16).reshape(gb_ref.shape)
    return seg, k, v


def _inproj_prompt_kernel(x_ref, g_ref, w_ref, cw_ref, cb_ref, bif_ref, cos_ref, slo_ref, shi_ref,
                          q_ref, kb_ref, vbf_ref, pk_ref, pv_ref, ga_ref, qm_ref, km_ref, vb_ref,
                          so_ref, gb_ref, gt_ref, pconv_ref, xp_ref):
    j = pl.program_id(1)
    tm = x_ref.shape[1]
    hn = _rmsnorm(x_ref[0], g_ref[...]).astype(BF16)
    seg, k, v = _inproj_common(hn, w_ref, cos_ref[...], slo_ref[...], shi_ref[...],
                               q_ref, ga_ref, vb_ref, so_ref, gb_ref)
    kb_ref[0] = k.astype(BF16)
    vbf_ref[0] = v.astype(BF16)
    pk_ref[0] = k.T.reshape(N_HEADS_ATT, HEAD_DIM_ATT, tm)
    pv_ref[0] = v.T.reshape(N_HEADS_ATT, HEAD_DIM_ATT, tm)

    @pl.when(j == 0)
    def _():
        xp_ref[0:SUBLANES, :] = jnp.zeros((SUBLANES, 2 * D_MLSTM), F32)

    @pl.when(j > 0)
    def _():
        xp_ref[0:SUBLANES, :] = xp_ref[tm:tm + SUBLANES, :]

    xp_ref[SUBLANES:SUBLANES + tm, :] = seg(OFF_QB, 2 * D_MLSTM)
    y = cb_ref[...] + xp_ref[SUBLANES:SUBLANES + tm, :] * cw_ref[3:4, :]
    for jj in range(CONV_WIDTH - 1):
        sh = CONV_WIDTH - 1 - jj
        y = y + xp_ref[SUBLANES - sh:SUBLANES - sh + tm, :] * cw_ref[jj:jj + 1, :]
    y = _silu(y)
    qm_ref[0] = y[:, :D_MLSTM].astype(BF16)
    km_ref[0] = (y[:, D_MLSTM:] * (HEAD_DIM_MLSTM ** -0.5)).astype(BF16)

    @pl.when(j == pl.num_programs(1) - 1)
    def _():
        pconv_ref[0] = xp_ref[tm + SUBLANES - (CONV_WIDTH - 1):tm + SUBLANES, :]

    gates = _gate_block(seg(OFF_G, LANES), bif_ref[...])
    for i in range(tm // LANES):
        gt = gates[i * LANES:(i + 1) * LANES, :].T
        gt_ref[0, :, i * LANES:(i + 1) * LANES] = gt[0:SUBLANES, :]


def _inproj_sample_kernel(x_ref, g_ref, w_ref, cw_ref, cb_ref, bif_ref, cos_ref, slo_ref, shi_ref,
                          h1_ref, h2_ref, h3_ref,
                          q_ref, kn_ref, vn_ref, knt_ref, vnt_ref, ga_ref, qm_ref, km_ref, vb_ref,
                          so_ref, gb_ref, gates_ref, qk_ref, *, t_new):
    hn = _rmsnorm(x_ref[...], g_ref[...]).astype(BF16)
    seg, k, v = _inproj_common(hn, w_ref, cos_ref[...], slo_ref[...], shi_ref[...],
                               q_ref, ga_ref, vb_ref, so_ref, gb_ref)
    kn_ref[...] = k
    vn_ref[...] = v
    knt_ref[...] = k.T
    vnt_ref[...] = v.T
    u = seg(OFF_QB, 2 * D_MLSTM)
    qk_ref[...] = u
    t = lax.rem(lax.broadcasted_iota(jnp.int32, u.shape, 0), t_new)
    y = cb_ref[...] + u * cw_ref[3:4, :]
    for sh, h_ref in ((1, h1_ref), (2, h2_ref), (3, h3_ref)):
        prev = jnp.where(t >= sh, pltpu.roll(u, sh, 0), h_ref[...])
        y = y + prev * cw_ref[3 - sh:4 - sh, :]
    y = _silu(y)
    qm_ref[...] = y[:, :D_MLSTM].astype(BF16)
    km_ref[...] = (y[:, D_MLSTM:] * (HEAD_DIM_MLSTM ** -0.5)).astype(BF16)
    gates_ref[...] = _gate_block(seg(OFF_G, LANES), bif_ref[...])


def _attn_prompt_kernel(q_ref, k_ref, v_ref, g_ref, o_ref,
                        src1_ref, src4_ref, src16_ref, bias_ref, st16_ref, st16p_ref, st4_ref):
    seq = q_ref.shape[1]
    n16 = seq // 16
    group = 4
    head_a_full = lax.broadcasted_iota(jnp.int32, (seq, LANES), 1) < HEAD_DIM_ATT
    by16 = lambda x: jnp.swapaxes(x.reshape(n16, 16, LANES), 0, 1)
    q = q_ref[0].astype(F32)
    k = k_ref[0].astype(F32)
    v = v_ref[0].astype(F32)
    ops = (jnp.where(head_a_full, q, 0.0), jnp.where(head_a_full, 0.0, q), k,
           jnp.where(head_a_full, v, 1.0), jnp.where(head_a_full, 1.0, v))
    for i, x in enumerate(ops):
        src4_ref[i] = x
        src1_ref[i] = x.astype(BF16)
    head_a_16 = lax.broadcasted_iota(jnp.int32, (16, n16, LANES), 2) < HEAD_DIM_ATT
    q16, k16, v16 = by16(q), by16(k), by16(v)
    ops16 = (jnp.where(head_a_16, q16, 0.0), jnp.where(head_a_16, 0.0, q16), k16,
             jnp.where(head_a_16, v16, 1.0), jnp.where(head_a_16, 1.0, v16))
    for i, x in enumerate(ops16):
        src16_ref[i] = x.astype(BF16)
    u = lax.broadcasted_iota(jnp.int32, (N_BACK, 2 * N_BACK), 0)
    w = lax.broadcasted_iota(jnp.int32, (N_BACK, 2 * N_BACK), 1)
    bias_ref[...] = jnp.where((w >= u) & (w <= u + N_BACK), jnp.finfo(F32).max, NEG_INF)
    head_a = lax.broadcasted_iota(jnp.int32, (N_BACK, LANES), 1) < HEAD_DIM_ATT

    def partials(gets):
        staged = []
        for get, has_prev in gets:
            if has_prev:
                both = lambda i, get=get: jnp.concatenate([get(i, True), get(i, False)], axis=0)
                bias = bias_ref[...]
            else:
                both = lambda i, get=get: get(i, False)
                bias = bias_ref[:, N_BACK:]
            kk = both(2)
            heads = []
            for qi in (0, 1):
                s = lax.dot_general(get(qi, False), kk, (((1,), (1,)), ((), ())),
                                    preferred_element_type=F32)
                s = jnp.minimum(s, bias)
                mh = jnp.max(s, axis=-1, keepdims=True)
                heads.append((mh, jnp.exp(s - mh).astype(BF16)))
            staged.append((both, heads))
        out = []
        for both, ((m_a, p_a), (m_b, p_b)) in staged:
            pv_a = jnp.dot(p_a, both(3), preferred_element_type=F32)
            pv_b = jnp.dot(p_b, both(4), preferred_element_type=F32)
            acc = jnp.where(head_a, pv_a, pv_b)
            den = pltpu.roll(jnp.where(head_a, pv_b, pv_a), HEAD_DIM_ATT, 1)
            out.append((jnp.where(head_a, m_a, m_b), den, acc))
        return out

    def get4(qstart, kprev_start):
        def get(i, prev):
            start = kprev_start if prev else qstart
            return src4_ref[i, pl.ds(start, N_BACK, stride=4), :].astype(BF16)
        return get, kprev_start is not None

    def get1(qstart, kprev_start):
        def get(i, prev):
            return src1_ref[i, pl.ds(kprev_start if prev else qstart, N_BACK), :]
        return get, kprev_start is not None

    def keep4(blocks):
        res = partials([get4(qs, ks) for qs, ks in blocks])
        for (qs, _), (m, den, acc) in zip(blocks, res):
            rows = pl.ds(qs, N_BACK, stride=4)
            st4_ref[0, rows, :] = m
            st4_ref[1, rows, :] = den
            st4_ref[2, rows, :] = acc

    def finish(blocks):
        res = partials([get1(qs, ks) for qs, ks in blocks])
        for (qs, _), part in zip(blocks, res):
            rows = pl.ds(qs, N_BACK)
            parts = [part] + [tuple(st[i, rows, :] for i in range(3)) for st in (st4_ref, st16p_ref)]
            m_all = functools.reduce(jnp.maximum, [pt[0] for pt in parts])
            wts = [jnp.exp(pt[0] - m_all) for pt in parts]
            den = sum(wt * pt[1] for wt, pt in zip(wts, parts))
            num = sum(wt * pt[2] for wt, pt in zip(wts, parts))
            o_ref[0, rows, :] = (num / den * g_ref[0, rows, :].astype(F32)).astype(BF16)

    def body16(g, c):
        rs = [g * group + rr for rr in range(group)]
        res = partials([(lambda i, prev, r=r: src16_ref[i, r], False) for r in rs])
        for r, (m, den, acc) in zip(rs, res):
            st16_ref[0, r] = m
            st16_ref[1, r] = den
            st16_ref[2, r] = acc
        return c
    lax.fori_loop(0, 16 // group, body16, 0)
    for i in range(3):
        st16p_ref[i] = jnp.swapaxes(st16_ref[i], 0, 1).reshape(seq, LANES)

    keep4([(r, None) for r in range(4)])

    def body4(cc, c):
        keep4([(r + 4 * N_BACK * cc, r + 4 * N_BACK * (cc - 1)) for r in range(4)])
        return c
    lax.fori_loop(1, seq // 4 // N_BACK, body4, 0)

    finish([(0, None)] + [(cc * N_BACK, (cc - 1) * N_BACK) for cc in range(1, group)])

    def body1(g, c):
        starts = [pl.multiple_of((g * group + rr) * N_BACK, N_BACK) for rr in range(group)]
        finish([(st, st - N_BACK) for st in starts])
        return c
    lax.fori_loop(1, seq // N_BACK // group, body1, 0)


def _attn_sample_kernel(q_ref, kn_ref, vn_ref, knt_ref, vnt_ref, g_ref, ck_ref, cv_ref,
                        o_ref, sk_ref, sv_ref, clamp_ref, *, t_new):
    b = pl.program_id(0)
    wb = ck_ref.shape[3]
    hd = HEAD_DIM_ATT
    rows = q_ref.shape[1]

    @pl.when(b == 0)
    def _():
        delta = (wb + lax.broadcasted_iota(jnp.int32, (rows, wb), 0)
                 - lax.broadcasted_iota(jnp.int32, (rows, wb), 1))
        for d, dil in enumerate(DILATIONS):
            ok = ((delta & (dil - 1)) == 0) & (delta >= dil) & (delta <= N_BACK * dil)
            clamp_ref[d] = jnp.where(ok, jnp.finfo(F32).max, NEG_INF)

    tq = lax.broadcasted_iota(jnp.int32, (rows, rows), 0)
    tk = lax.broadcasted_iota(jnp.int32, (rows, rows), 1)
    new_ok = [((tk <= tq) if dil == 1 else (tk == tq)) & (tk < t_new) for dil in DILATIONS]

    lane = lax.broadcasted_iota(jnp.int32, (hd, LANES), 1)
    shift_new = (LANES - t_new) - b * t_new
    nt = (((1,), (1,)), ((), ()))
    outs = []
    for h in range(N_HEADS_ATT):
        hs = slice(h * hd, (h + 1) * hd)
        kt = ck_ref[0, h]
        vt = cv_ref[0, h]
        for old, new_ref, out_ref in ((kt, knt_ref, sk_ref), (vt, vnt_ref, sv_ref)):
            moved = pltpu.roll(old, wb - t_new, axis=1)
            new_cols = pltpu.roll(new_ref[hs, :], shift_new, axis=1)
            out_ref[0, h, :, 0:wb - LANES] = moved[:, 0:wb - LANES]
            out_ref[0, h, :, wb - LANES:wb] = jnp.where(lane < LANES - t_new,
                                                        moved[:, wb - LANES:wb], new_cols)
        qh = q_ref[0, :, hs]
        kn_h = kn_ref[0, :, hs].astype(BF16)
        vn_h = vn_ref[0, :, hs].astype(BF16)
        s_old = jnp.dot(qh, kt.astype(BF16), preferred_element_type=F32)
        s_new = lax.dot_general(qh, kn_h, nt, preferred_element_type=F32)
        ps, pes, ms = [], [], []
        for d in range(len(DILATIONS)):
            so = jnp.minimum(s_old, clamp_ref[d])
            sn = jnp.where(new_ok[d], s_new, NEG_INF)
            m = jnp.maximum(jnp.max(so, axis=-1, keepdims=True), jnp.max(sn, axis=-1, keepdims=True))
            ps.append(jnp.exp(so - m))
            pes.append(jnp.exp(sn - m))
            ms.append(m)
        acc = lax.dot_general(jnp.concatenate(ps, axis=0).astype(BF16), vt.astype(BF16), nt,
                              preferred_element_type=F32)
        acc = acc + jnp.dot(jnp.concatenate(pes, axis=0).astype(BF16), vn_h,
                            preferred_element_type=F32)
        m_all = functools.reduce(jnp.maximum, ms)
        den = 0.0
        num = 0.0
        for d in range(len(DILATIONS)):
            wgt = jnp.exp(ms[d] - m_all)
            den = den + wgt * (jnp.sum(ps[d], axis=-1, keepdims=True)
                               + jnp.sum(pes[d], axis=-1, keepdims=True))
            num = num + wgt * acc[d * rows:(d + 1) * rows]
        outs.append(num / den)
    att = jnp.concatenate(outs, axis=1)
    o_ref[0] = (att * g_ref[0].astype(F32)).astype(BF16)


def _scan_lanes(x, op, fill):
    lane = lax.broadcasted_iota(jnp.int32, x.shape, 1)
    d = 1
    while d < x.shape[1]:
        x = op(x, jnp.where(lane >= d, pltpu.roll(x, d, 1), fill))
        d *= 2
    return x


def _mlstm_kernel(*refs, zero_init):
    if zero_init:
        (q_ref, k_ref, v_ref, gt_ref, so_ref, gb_ref, ng_ref,
         y_ref, cn_out_ref, m_out_ref, cn_ref, m_ref) = refs
    else:
        (q_ref, k_ref, v_ref, gt_ref, so_ref, gb_ref, ng_ref, cn0_ref, m0_ref,
         y_ref, cn_out_ref, m_out_ref, cn_ref, m_ref) = refs
    c_idx = pl.program_id(1)
    L = CHUNK
    dh = HEAD_DIM_MLSTM

    @pl.when(c_idx == 0)
    def _():
        if zero_init:
            cn_ref[...] = jnp.zeros(cn_ref.shape, F32)
            m_ref[...] = jnp.zeros(m_ref.shape, F32)
        else:
            cn_ref[...] = cn0_ref[0]
            m_ref[...] = m0_ref[0]

    gt = gt_ref[0]
    i_row = gt
    f_row = pltpu.roll(gt, N_HEADS_MLSTM, 0)
    m_prev = m_ref[...]
    b = _scan_lanes(f_row, jnp.add, 0.0)
    cm = _scan_lanes(i_row - b, jnp.maximum, NEG_INF)
    m_t = jnp.maximum(m_prev + b, b + cm)
    inter = jnp.exp(m_prev + b - m_t)
    a_row = b - m_t
    r_row = i_row - b
    en_row = jnp.exp(-m_t)
    m_last = jnp.broadcast_to(m_t[:, L - 1:L], m_t.shape)
    b_last = jnp.broadcast_to(b[:, L - 1:L], b.shape)
    decay = jnp.exp(m_prev + b_last - m_last)
    decay2 = jnp.concatenate([decay, decay], axis=1)
    w_row = jnp.exp(b_last - b + i_row - m_last)
    m_ref[...] = m_last

    stack = jnp.concatenate([a_row, inter, en_row, w_row,
                             jnp.zeros((L - 4 * SUBLANES, L), F32)], axis=0)
    cols = stack.T

    tri_t = lax.broadcasted_iota(jnp.int32, (L, L), 0)
    tri_s = lax.broadcasted_iota(jnp.int32, (L, L), 1)
    causal = tri_t >= tri_s
    ones_blk = jnp.ones((L, dh), BF16)

    for h in range(N_HEADS_MLSTM):
        sl = slice(h * dh, (h + 1) * dh)
        q = q_ref[0, :, sl]
        k = k_ref[0, :, sl]
        v = v_ref[0, :, sl]
        a_col = cols[:, h:h + 1]
        inter_col = cols[:, SUBLANES + h:SUBLANES + h + 1]
        en_col = cols[:, 2 * SUBLANES + h:2 * SUBLANES + h + 1]
        w_col = cols[:, 3 * SUBLANES + h:3 * SUBLANES + h + 1]
        dlog = a_col + r_row[h:h + 1, :]
        dmat = jnp.exp(jnp.where(causal, dlog, NEG_INF))
        sqk = lax.dot_general(q, k, (((1,), (1,)), ((), ())), preferred_element_type=F32) * dmat
        cn = cn_ref[h]
        tot = inter_col * jnp.dot(q, cn.astype(BF16), preferred_element_type=F32)
        tot = tot + jnp.dot(sqk.astype(BF16), jnp.concatenate([v, ones_blk], axis=1),
                            preferred_element_type=F32)
        num = tot[:, :dh]
        nq = tot[:, dh:]
        hh = num / jnp.maximum(jnp.abs(nq), en_col)
        hh = so_ref[0, :, sl].astype(F32) * hh
        hh = hh * lax.rsqrt(jnp.mean(hh * hh, axis=-1, keepdims=True) + EPS)
        hh = hh * ng_ref[:, sl]
        y_ref[0, :, sl] = (hh * gb_ref[0, :, sl].astype(F32)).astype(BF16)

        wv = jnp.concatenate([w_col * v.astype(F32), jnp.broadcast_to(w_col, (L, dh))], axis=1)
        upd = lax.dot_general(k, wv.astype(BF16), (((0,), (0,)), ((), ())),
                              preferred_element_type=F32)
        cn_ref[h] = decay2[h:h + 1, :] * cn + upd

    @pl.when(c_idx == pl.num_programs(1) - 1)
    def _():
        cn_out_ref[0] = cn_ref[...]
        m_out_ref[0] = m_ref[...]


def _outproj_kernel(x_ref, ya_ref, yb_ref, w_ref, g_ref, o_ref):
    mix = jnp.dot(ya_ref[...], w_ref[0:D_ATT, :], preferred_element_type=F32)
    mix = mix + jnp.dot(yb_ref[...], w_ref[D_ATT:, :], preferred_element_type=F32)
    o_ref[...] = _rmsnorm(x_ref[...] + mix, g_ref[...])


def _rope_tables(pos):
    half = ROT_DIM // 2
    inv = ROPE_THETA ** (-jnp.arange(half, dtype=F32) * 2.0 / ROT_DIM)
    ang = pos.astype(F32)[:, None] * inv[None, :]
    cos, sin = jnp.cos(ang), jnp.sin(ang)
    n = pos.shape[0]
    one = jnp.ones((n, HEAD_DIM_ATT - ROT_DIM), F32)
    zero = jnp.zeros((n, HEAD_DIM_ATT - ROT_DIM), F32)
    zh = jnp.zeros((n, half), F32)
    cos_t = jnp.concatenate([cos, cos, one], axis=1)
    lo_t = jnp.concatenate([-sin, zh, zero], axis=1)
    hi_t = jnp.concatenate([zh, sin, zero], axis=1)
    rep = lambda t: jnp.concatenate([t, t], axis=1)
    return rep(cos_t), rep(lo_t), rep(hi_t)


def _params(sem):
    return pltpu.CompilerParams(dimension_semantics=sem, vmem_limit_bytes=VMEM_LIMIT)


def _const_spec(shape):
    return pl.BlockSpec(shape, lambda *_: (0,) * len(shape))


def _inproj_prompt(x, norm_g, w_pad, conv_w, conv_b, bif):
    B, S, _ = x.shape
    tm = ROW_TILE
    cos, lo, hi = _rope_tables(jnp.arange(S, dtype=jnp.int32))
    tile = lambda width: pl.BlockSpec((1, tm, width), lambda b, j: (b, j, 0))
    tab = pl.BlockSpec((tm, LANES), lambda b, j: (j, 0))
    bf = lambda: jax.ShapeDtypeStruct((B, S, D_ATT), BF16)
    f5 = lambda: jax.ShapeDtypeStruct((B, N_HEADS_ATT, HEAD_DIM_ATT, S), F32)
    tile5 = pl.BlockSpec((1, N_HEADS_ATT, HEAD_DIM_ATT, tm), lambda b, j: (b, 0, 0, j))
    out_shape = (bf(), bf(), bf(), f5(), f5(), bf(), bf(), bf(), bf(), bf(), bf(),
                 jax.ShapeDtypeStruct((B, SUBLANES, S), F32),
                 jax.ShapeDtypeStruct((B, CONV_WIDTH - 1, 2 * D_MLSTM), F32))
    out_specs = tuple([tile(D_ATT)] * 3 + [tile5] * 2 + [tile(D_ATT)] * 6) + (
        pl.BlockSpec((1, SUBLANES, tm), lambda b, j: (b, 0, j)),
        pl.BlockSpec((1, CONV_WIDTH - 1, 2 * D_MLSTM), lambda b, j: (b, 0, 0)))
    return pl.pallas_call(
        _inproj_prompt_kernel,
        grid=(B, S // tm),
        in_specs=[tile(D_MODEL), _const_spec((1, D_MODEL)), _const_spec((D_MODEL, N_IN_PAD)),
                  _const_spec((CONV_WIDTH, 2 * D_MLSTM)), _const_spec((1, 2 * D_MLSTM)),
                  _const_spec((1, LANES)), tab, tab, tab],
        out_specs=out_specs,
        out_shape=out_shape,
        scratch_shapes=[pltpu.VMEM((tm + 2 * SUBLANES, 2 * D_MLSTM), F32)],
        compiler_params=_params(("arbitrary", "arbitrary")),
        name="inproj_prompt",
    )(x, norm_g, w_pad, conv_w, conv_b, bif, cos, lo, hi)


def _inproj_sample(x2, norm_g, w_pad, conv_w, conv_b, bif, hist, t_new):
    rows = x2.shape[0]
    pos = PAST_LEN + jnp.arange(t_new, dtype=jnp.int32)
    cos, lo, hi = (jnp.tile(t, (rows // t_new, 1)) for t in _rope_tables(pos))
    bf = lambda: jax.ShapeDtypeStruct((rows, D_ATT), BF16)
    f3 = lambda: jax.ShapeDtypeStruct((rows, D_ATT), F32)
    f5 = lambda: jax.ShapeDtypeStruct((D_ATT, rows), F32)
    out_shape = (bf(), f3(), f3(), f5(), f5(), bf(), bf(), bf(), bf(), bf(), bf(),
                 jax.ShapeDtypeStruct((rows, LANES), F32),
                 jax.ShapeDtypeStruct((rows, 2 * D_MLSTM), F32))
    return pl.pallas_call(
        functools.partial(_inproj_sample_kernel, t_new=t_new),
        out_shape=out_shape,
        compiler_params=pltpu.CompilerParams(vmem_limit_bytes=VMEM_LIMIT),
        name="inproj_sample",
    )(x2, norm_g, w_pad, conv_w, conv_b, bif, cos, lo, hi, *hist)


def _attn_prompt(q, k, v, gate):
    B, S, _ = q.shape
    spec = pl.BlockSpec((1, S, LANES), lambda b, h: (b, 0, h))
    return pl.pallas_call(
        _attn_prompt_kernel,
        grid=(B, D_ATT // LANES),
        in_specs=[spec, spec, spec, spec],
        out_specs=spec,
        out_shape=jax.ShapeDtypeStruct((B, S, D_ATT), BF16),
        scratch_shapes=[
            pltpu.VMEM((5, S, LANES), BF16), pltpu.VMEM((5, S, LANES), F32),
            pltpu.VMEM((5, 16, S // 16, LANES), BF16),
            pltpu.VMEM((N_BACK, 2 * N_BACK), F32),
            pltpu.VMEM((3, 16, S // 16, LANES), F32),
            pltpu.VMEM((3, S, LANES), F32), pltpu.VMEM((3, S, LANES), F32)],
        compiler_params=_params(("arbitrary", "arbitrary")),
        name="attn_prompt",
    )(q, k, v, gate)


def _attn_sample(q, kn, vn, knt, vnt, gate, ck, cv, t_new):
    B, rows, _ = q.shape
    wb = ck.shape[3]
    small = pl.BlockSpec((1, rows, D_ATT), lambda b: (b, 0, 0))
    big = pl.BlockSpec((1, N_HEADS_ATT, HEAD_DIM_ATT, wb), lambda b: (b, 0, 0, 0))
    win_shape = jax.ShapeDtypeStruct((B, N_HEADS_ATT, HEAD_DIM_ATT, wb), F32)
    return pl.pallas_call(
        functools.partial(_attn_sample_kernel, t_new=t_new),
        grid=(B,),
        in_specs=[small, small, small, _const_spec(knt.shape), _const_spec(vnt.shape), small, big, big],
        out_specs=(small, big, big),
        out_shape=(jax.ShapeDtypeStruct((B, rows, D_ATT), BF16), win_shape, win_shape),
        scratch_shapes=[pltpu.VMEM((len(DILATIONS), rows, wb), F32)],
        compiler_params=_params(("arbitrary",)),
        name="attn_sample",
    )(q, kn, vn, knt, vnt, gate, ck, cv)


def _mlstm(q, k, v, gates_t, sig_o, gate_b, norm_g, cn0=None, m0=None):
    B, S, _ = q.shape
    nc = S // CHUNK
    tile = pl.BlockSpec((1, CHUNK, D_MLSTM), lambda b, c: (b, c, 0))
    gspec = pl.BlockSpec((1, SUBLANES, CHUNK), lambda b, c: (b, 0, c))
    cn_spec = pl.BlockSpec((1, N_HEADS_MLSTM, HEAD_DIM_MLSTM, 2 * HEAD_DIM_MLSTM),
                           lambda b, c: (b, 0, 0, 0))
    m_spec = pl.BlockSpec((1, SUBLANES, LANES), lambda b, c: (b, 0, 0))
    zero_init = cn0 is None
    in_specs = [tile, tile, tile, gspec, tile, tile, _const_spec((1, D_MLSTM))]
    args = [q, k, v, gates_t, sig_o, gate_b, norm_g]
    if not zero_init:
        in_specs += [cn_spec, m_spec]
        args += [cn0, m0]
    return pl.pallas_call(
        functools.partial(_mlstm_kernel, zero_init=zero_init),
        grid=(B, nc),
        in_specs=in_specs,
        out_specs=(tile, cn_spec, m_spec),
        out_shape=(jax.ShapeDtypeStruct((B, S, D_MLSTM), BF16),
                   jax.ShapeDtypeStruct((B, N_HEADS_MLSTM, HEAD_DIM_MLSTM, 2 * HEAD_DIM_MLSTM), F32),
                   jax.ShapeDtypeStruct((B, SUBLANES, LANES), F32)),
        scratch_shapes=[pltpu.VMEM((N_HEADS_MLSTM, HEAD_DIM_MLSTM, 2 * HEAD_DIM_MLSTM), F32),
                        pltpu.VMEM((SUBLANES, LANES), F32)],
        compiler_params=_params(("arbitrary", "arbitrary")),
        name="mlstm_prompt" if zero_init else "mlstm_sample",
    )(*args)


def _outproj(x2, ya, yb, w_out, final_g):
    rows = x2.shape[0]
    tm = min(ROW_TILE, rows)
    tile = lambda width: pl.BlockSpec((tm, width), lambda i: (i, 0))
    return pl.pallas_call(
        _outproj_kernel,
        grid=(rows // tm,),
        in_specs=[tile(D_MODEL), tile(D_ATT), tile(D_MLSTM),
                  _const_spec((D_ATT + D_MLSTM, D_MODEL)), _const_spec((1, D_MODEL))],
        out_specs=tile(D_MODEL),
        out_shape=jax.ShapeDtypeStruct((rows, D_MODEL), F32),
        compiler_params=_params(("arbitrary",)),
        name="outproj",
    )(x2, ya, yb, w_out, final_g)


def kernel(x_prompt, x_sample, cache_win_k, cache_win_v, state_conv, state_C, state_n, state_m,
           norm_g, w_in, conv_w, conv_b, b_i, b_f, mlstm_norm_g, w_out, final_norm_g):
    assert w_in.shape[0] == 1, "single-layer model"
    B, S, D = x_prompt.shape
    DB, T, _ = x_sample.shape
    HB, DK = N_HEADS_MLSTM, HEAD_DIM_MLSTM
    wb = cache_win_k.shape[2]
    assert S % ROW_TILE == 0 and S == 16 * N_BACK and wb >= 16 * N_BACK and CONV_WIDTH - 1 <= T <= SUBLANES and DB * T == LANES

    w_pad = jnp.pad(w_in[0], ((0, 0), (0, N_IN_PAD - N_IN))).astype(BF16)
    w_o = w_out[0].astype(BF16)
    g_in = norm_g[0][None, :]
    cw, cb = conv_w[0], conv_b[0][None, :]
    bif = jnp.pad(jnp.concatenate([b_i[0], b_f[0]]), (0, LANES - 2 * HB))[None, :]
    ng = mlstm_norm_g[0][None, :]
    g_fin = final_norm_g[None, :]

    (q_p, k_p, v_p, pk, pv, ga_p, qm_p, km_p, vb_p, so_p, gb_p, gt_p, p_conv) = _inproj_prompt(
        x_prompt, g_in, w_pad, cw, cb, bif)
    ya_p = _attn_prompt(q_p, k_p, v_p, ga_p)
    yb_p, cn_p, m_p = _mlstm(qm_p, km_p, vb_p, gt_p, so_p, gb_p, ng)
    y_prompt = _outproj(x_prompt.reshape(B * S, D), ya_p.reshape(B * S, D_ATT),
                        yb_p.reshape(B * S, D_MLSTM), w_o, g_fin).reshape(B, S, D)

    sc = state_conv[0]
    zrow = jnp.zeros((DB, 1, 2 * D_MLSTM), F32)
    hist = []
    for sh in (1, 2, 3):
        rows_ = [sc[:, CONV_WIDTH - 1 + t - sh:CONV_WIDTH + t - sh] if t < sh else zrow
                 for t in range(T)]
        hist.append(jnp.concatenate(rows_, axis=1).reshape(DB * T, 2 * D_MLSTM))
    (q_s, kn, vn, knt, vnt, ga_s, qm_s, km_s, vb_s, so_s, gb_s, gates_s, qk_s) = _inproj_sample(
        x_sample.reshape(DB * T, D), g_in, w_pad, cw, cb, bif, hist, T)
    r3 = lambda a: a.reshape(DB, T, a.shape[-1])
    pad8 = lambda a: jnp.pad(r3(a), ((0, 0), (0, SUBLANES - T), (0, 0)))
    to_hdp = lambda c: jnp.transpose(c[0], (0, 2, 3, 1))
    from_hdp = lambda c: jnp.transpose(c, (0, 3, 1, 2))[None]
    ya_s, s_k, s_v = _attn_sample(pad8(q_s), pad8(kn), pad8(vn), knt, vnt, pad8(ga_s),
                                  to_hdp(cache_win_k), to_hdp(cache_win_v), T)
    ya_s = ya_s[:, :T]

    pad_t = lambda a: jnp.pad(r3(a), ((0, 0), (0, CHUNK - T), (0, 0)))
    g3 = r3(gates_s)[:, :, :SUBLANES]
    null_gate = jnp.concatenate([jnp.full((HB,), NEG_INF, F32), jnp.zeros((HB,), F32)])
    gt_s = jnp.concatenate([g3, jnp.broadcast_to(null_gate, (DB, CHUNK - T, SUBLANES))], axis=1)
    gt_s = gt_s.transpose(0, 2, 1)
    cn0 = jnp.concatenate([state_C[0], jnp.broadcast_to(state_n[0][..., None], (DB, HB, DK, DK))],
                          axis=-1)
    m0 = jnp.broadcast_to(jnp.pad(state_m[0], ((0, 0), (0, SUBLANES - HB)))[..., None],
                          (DB, SUBLANES, LANES))
    yb_s, cn_s, m_s = _mlstm(pad_t(qm_s), pad_t(km_s), pad_t(vb_s), gt_s, pad_t(so_s), pad_t(gb_s),
                             ng, cn0, m0)
    y_sample = _outproj(x_sample.reshape(DB * T, D), ya_s.reshape(DB * T, D_ATT),
                        yb_s[:, :T].reshape(DB * T, D_MLSTM), w_o, g_fin).reshape(DB, T, D)

    return (y_prompt, y_sample,
            from_hdp(pk), from_hdp(pv), p_conv[None],
            cn_p[None, ..., :DK], cn_p[None, ..., DK], m_p[None, :, :HB, 0],
            from_hdp(s_k), from_hdp(s_v), r3(qk_s)[None, :, T - (CONV_WIDTH - 1):],
            cn_s[None, ..., :DK], cn_s[None, ..., DK], m_s[None, :, :HB, 0])
```

```python
import functools
import math

import jax
import jax.numpy as jnp
import numpy as np
from jax import lax
from jax.experimental import pallas as pl
from jax.experimental.pallas import tpu as pltpu

F32 = jnp.float32
BF16 = jnp.bfloat16

D_MODEL = 1024
D_ATT = 512
N_HEADS_ATT = 8
HEAD_DIM_ATT = 64
D_MLSTM = 512
N_HEADS_MLSTM = 4
HEAD_DIM_MLSTM = 128
ROT_DIM = 16
ROPE_THETA = 500000.0
PAST_LEN = 16384
DILATIONS = (1, 4, 16)
N_BACK = 128
CONV_WIDTH = 4
CHUNK = 128
EPS = 1e-6
NEG_INF = -1e30

LANES = 128
SUBLANES = 8
N_IN = 4 * D_ATT + 5 * D_MLSTM + 2 * N_HEADS_MLSTM
N_IN_PAD = 4 * D_ATT + 5 * D_MLSTM + LANES
OFF_QA, OFF_KA, OFF_VA, OFF_ZA = 0, 512, 1024, 1536
OFF_QB, OFF_VB, OFF_OB, OFF_ZB, OFF_G = 2048, 3072, 3584, 4096, 4608
ROW_TILE = 512
MLSTM_BATCH = 4
VMEM_LIMIT = 56 * 1024 * 1024


def _silu(x):
    return x * jax.nn.sigmoid(x)


def _rmsnorm(x, g):
    return x * lax.rsqrt(jnp.mean(x * x, axis=-1, keepdims=True) + EPS) * g


def _rope(u, cos, sin_lo, sin_hi):
    outs = []
    for c in range(u.shape[1] // LANES):
        xs = u[:, c * LANES:(c + 1) * LANES]
        outs.append(xs * cos + pltpu.roll(xs, LANES - ROT_DIM // 2, 1) * sin_lo
                    + pltpu.roll(xs, ROT_DIM // 2, 1) * sin_hi)
    return jnp.concatenate(outs, axis=1)


def _gate_block(ug, bif):
    gz = ug + bif
    lane = lax.broadcasted_iota(jnp.int32, gz.shape, 1)
    logf = jnp.minimum(gz, 0.0) - jnp.log1p(jnp.exp(-jnp.abs(gz)))
    return jnp.where(lane < N_HEADS_MLSTM, gz, logf)


def _inproj_common(hn, w_ref, cos, sin_lo, sin_hi, q_ref, ga_ref, vb_ref, so_ref, gb_ref):
    def seg(off, width):
        return jnp.dot(hn, w_ref[:, off:off + width], preferred_element_type=F32)

    q = _rope(seg(OFF_QA, D_ATT), cos, sin_lo, sin_hi) * (HEAD_DIM_ATT ** -0.5)
    q_ref[...] = q.astype(BF16).reshape(q_ref.shape)
    k = _rope(seg(OFF_KA, D_ATT), cos, sin_lo, sin_hi)
    v = seg(OFF_VA, D_ATT)
    ga_ref[...] = _silu(seg(OFF_ZA, D_ATT)).astype(BF16).reshape(ga_ref.shape)
    vb_ref[...] = seg(OFF_VB, D_MLSTM).astype(BF16).reshape(vb_ref.shape)
    so_ref[...] = jax.nn.sigmoid(seg(OFF_OB, D_MLSTM)).astype(BF16).reshape(so_ref.shape)
    gb_ref[...] = _silu(seg(OFF_ZB, D_MLSTM)).astype(BF16).reshape(gb_ref.shape)
    return seg, k, v


def _inproj_prompt_kernel(x_ref, g_ref, w_ref, cw_ref, cb_ref, bif_ref, cos_ref, slo_ref, shi_ref,
                          q_ref, kb_ref, vbf_ref, pk_ref, pv_ref, ga_ref, qm_ref, km_ref, vb_ref,
                          so_ref, gb_ref, gt_ref, pconv_ref, xp_ref):
    j = pl.program_id(1)
    tm = x_ref.shape[1]
    hn = _rmsnorm(x_ref[0], g_ref[...]).astype(BF16)
    seg, k, v = _inproj_common(hn, w_ref, cos_ref[...], slo_ref[...], shi_ref[...],
                               q_ref, ga_ref, vb_ref, so_ref, gb_ref)
    kb_ref[0] = k.astype(BF16)
    vbf_ref[0] = v.astype(BF16)
    pk_ref[0] = k.T.reshape(N_HEADS_ATT, HEAD_DIM_ATT, tm)
    pv_ref[0] = v.T.reshape(N_HEADS_ATT, HEAD_DIM_ATT, tm)

    @pl.when(j == 0)
    def _():
        xp_ref[0:SUBLANES, :] = jnp.zeros((SUBLANES, 2 * D_MLSTM), F32)

    @pl.when(j > 0)
    def _():
        xp_ref[0:SUBLANES, :] = xp_ref[tm:tm + SUBLANES, :]

    xp_ref[SUBLANES:SUBLANES + tm, :] = seg(OFF_QB, 2 * D_MLSTM)
    y = cb_ref[...] + xp_ref[SUBLANES:SUBLANES + tm, :] * cw_ref[3:4, :]
    for jj in range(CONV_WIDTH - 1):
        sh = CONV_WIDTH - 1 - jj
        y = y + xp_ref[SUBLANES - sh:SUBLANES - sh + tm, :] * cw_ref[jj:jj + 1, :]
    y = _silu(y)
    qm_ref[0] = y[:, :D_MLSTM].astype(BF16)
    km_ref[0] = (y[:, D_MLSTM:] * (HEAD_DIM_MLSTM ** -0.5)).astype(BF16)

    @pl.when(j == pl.num_programs(1) - 1)
    def _():
        pconv_ref[0] = xp_ref[tm + SUBLANES - (CONV_WIDTH - 1):tm + SUBLANES, :]

    gates = _gate_block(seg(OFF_G, LANES), bif_ref[...])
    assert CHUNK == LANES
    for i in range(tm // CHUNK):
        gt = gates[i * CHUNK:(i + 1) * CHUNK, :].T[0:SUBLANES, :]
        gt_ref[0, 0:SUBLANES, i * CHUNK:(i + 1) * CHUNK] = gt
        gt_ref[0, SUBLANES:2 * SUBLANES, i * CHUNK:(i + 1) * CHUNK] = _gate_scans(gt)


def _inproj_sample_kernel(x_ref, g_ref, w_ref, cw_ref, cb_ref, bif_ref, cos_ref, slo_ref, shi_ref,
                          h1_ref, h2_ref, h3_ref,
                          q_ref, kn_ref, vn_ref, knt_ref, vnt_ref, ga_ref, qm_ref, km_ref, vb_ref,
                          so_ref, gb_ref, gates_ref, qk_ref, *, t_new):
    hn = _rmsnorm(x_ref[...], g_ref[...]).astype(BF16)
    seg, k, v = _inproj_common(hn, w_ref, cos_ref[...], slo_ref[...], shi_ref[...],
                               q_ref, ga_ref, vb_ref, so_ref, gb_ref)
    kn_ref[...] = k
    vn_ref[...] = v
    knt_ref[...] = k.T
    vnt_ref[...] = v.T
    u = seg(OFF_QB, 2 * D_MLSTM)
    qk_ref[...] = u
    t = lax.rem(lax.broadcasted_iota(jnp.int32, u.shape, 0), t_new)
    y = cb_ref[...] + u * cw_ref[3:4, :]
    for sh, h_ref in ((1, h1_ref), (2, h2_ref), (3, h3_ref)):
        prev = jnp.where(t >= sh, pltpu.roll(u, sh, 0), h_ref[...])
        y = y + prev * cw_ref[3 - sh:4 - sh, :]
    y = _silu(y)
    qm_ref[...] = y[:, :D_MLSTM].astype(BF16)
    km_ref[...] = (y[:, D_MLSTM:] * (HEAD_DIM_MLSTM ** -0.5)).astype(BF16)
    gates_ref[...] = _gate_block(seg(OFF_G, LANES), bif_ref[...])


def _attn_prompt_kernel(q_ref, k_ref, v_ref, g_ref, o_ref,
                        src1_ref, src4_ref, src16_ref, bias_ref, st16_ref, st16p_ref, st4_ref):
    seq = q_ref.shape[1]
    n16 = seq // 16
    group = 4
    head_a_full = lax.broadcasted_iota(jnp.int32, (seq, LANES), 1) < HEAD_DIM_ATT
    by16 = lambda x: jnp.swapaxes(x.reshape(n16, 16, LANES), 0, 1)
    q = q_ref[0].astype(F32)
    k = k_ref[0].astype(F32)
    v = v_ref[0].astype(F32)
    ops = (jnp.where(head_a_full, q, 0.0), jnp.where(head_a_full, 0.0, q), k,
           jnp.where(head_a_full, v, 1.0), jnp.where(head_a_full, 1.0, v))
    for i, x in enumerate(ops):
        src4_ref[i] = x
        src1_ref[i] = x.astype(BF16)
    head_a_16 = lax.broadcasted_iota(jnp.int32, (16, n16, LANES), 2) < HEAD_DIM_ATT
    q16, k16, v16 = by16(q), by16(k), by16(v)
    ops16 = (jnp.where(head_a_16, q16, 0.0), jnp.where(head_a_16, 0.0, q16), k16,
             jnp.where(head_a_16, v16, 1.0), jnp.where(head_a_16, 1.0, v16))
    for i, x in enumerate(ops16):
        src16_ref[i] = x.astype(BF16)
    u = lax.broadcasted_iota(jnp.int32, (N_BACK, 2 * N_BACK), 0)
    w = lax.broadcasted_iota(jnp.int32, (N_BACK, 2 * N_BACK), 1)
    bias_ref[...] = jnp.where((w >= u) & (w <= u + N_BACK), jnp.finfo(F32).max, NEG_INF)
    head_a = lax.broadcasted_iota(jnp.int32, (N_BACK, LANES), 1) < HEAD_DIM_ATT

    def partials(gets):
        staged = []
        for get, has_prev in gets:
            if has_prev:
                both = lambda i, get=get: jnp.concatenate([get(i, True), get(i, False)], axis=0)
                bias = bias_ref[...]
            else:
                both = lambda i, get=get: get(i, False)
                bias = bias_ref[:, N_BACK:]
            kk = both(2)
            heads = []
            for qi in (0, 1):
                s = lax.dot_general(get(qi, False), kk, (((1,), (1,)), ((), ())),
                                    preferred_element_type=F32)
                s = jnp.minimum(s, bias)
                mh = jnp.max(s, axis=-1, keepdims=True)
                heads.append((mh, jnp.exp(s - mh).astype(BF16)))
            staged.append((both, heads))
        out = []
        for both, ((m_a, p_a), (m_b, p_b)) in staged:
            pv_a = jnp.dot(p_a, both(3), preferred_element_type=F32)
            pv_b = jnp.dot(p_b, both(4), preferred_element_type=F32)
            acc = jnp.where(head_a, pv_a, pv_b)
            den = pltpu.roll(jnp.where(head_a, pv_b, pv_a), HEAD_DIM_ATT, 1)
            out.append((jnp.where(head_a, m_a, m_b), den, acc))
        return out

    def get4(qstart, kprev_start):
        def get(i, prev):
            start = kprev_start if prev else qstart
            return src4_ref[i, pl.ds(start, N_BACK, stride=4), :].astype(BF16)
        return get, kprev_start is not None

    def get1(qstart, kprev_start):
        def get(i, prev):
            return src1_ref[i, pl.ds(kprev_start if prev else qstart, N_BACK), :]
        return get, kprev_start is not None

    def keep4(blocks):
        res = partials([get4(qs, ks) for qs, ks in blocks])
        for (qs, _), (m, den, acc) in zip(blocks, res):
            rows = pl.ds(qs, N_BACK, stride=4)
            st4_ref[0, rows, :] = m
            st4_ref[1, rows, :] = den
            st4_ref[2, rows, :] = acc

    def finish(blocks):
        res = partials([get1(qs, ks) for qs, ks in blocks])
        for (qs, _), part in zip(blocks, res):
            rows = pl.ds(qs, N_BACK)
            parts = [part] + [tuple(st[i, rows, :] for i in range(3)) for st in (st4_ref, st16p_ref)]
            m_all = functools.reduce(jnp.maximum, [pt[0] for pt in parts])
            wts = [jnp.exp(pt[0] - m_all) for pt in parts]
            den = sum(wt * pt[1] for wt, pt in zip(wts, parts))
            num = sum(wt * pt[2] for wt, pt in zip(wts, parts))
            o_ref[0, rows, :] = (num / den * g_ref[0, rows, :].astype(F32)).astype(BF16)

    def body16(g, c):
        rs = [g * group + rr for rr in range(group)]
        res = partials([(lambda i, prev, r=r: src16_ref[i, r], False) for r in rs])
        for r, (m, den, acc) in zip(rs, res):
            st16_ref[0, r] = m
            st16_ref[1, r] = den
            st16_ref[2, r] = acc
        return c
    lax.fori_loop(0, 16 // group, body16, 0)
    for i in range(3):
        st16p_ref[i] = jnp.swapaxes(st16_ref[i], 0, 1).reshape(seq, LANES)

    keep4([(r, None) for r in range(4)])

    def body4(cc, c):
        keep4([(r + 4 * N_BACK * cc, r + 4 * N_BACK * (cc - 1)) for r in range(4)])
        return c
    lax.fori_loop(1, seq // 4 // N_BACK, body4, 0)

    finish([(0, None)] + [(cc * N_BACK, (cc - 1) * N_BACK) for cc in range(1, group)])

    def body1(g, c):
        starts = [pl.multiple_of((g * group + rr) * N_BACK, N_BACK) for rr in range(group)]
        finish([(st, st - N_BACK) for st in starts])
        return c
    lax.fori_loop(1, seq // N_BACK // group, body1, 0)


def _attn_sample_kernel(q_ref, kn_ref, vn_ref, knt_ref, vnt_ref, g_ref, ck_ref, cv_ref,
                        o_ref, sk_ref, sv_ref, clamp_ref, *, t_new):
    b = pl.program_id(0)
    wb = ck_ref.shape[3]
    hd = HEAD_DIM_ATT
    rows = q_ref.shape[1]

    @pl.when(b == 0)
    def _():
        delta = (wb + lax.broadcasted_iota(jnp.int32, (rows, wb), 0)
                 - lax.broadcasted_iota(jnp.int32, (rows, wb), 1))
        for d, dil in enumerate(DILATIONS):
            ok = ((delta & (dil - 1)) == 0) & (delta >= dil) & (delta <= N_BACK * dil)
            clamp_ref[d] = jnp.where(ok, jnp.finfo(F32).max, NEG_INF)

    tq = lax.broadcasted_iota(jnp.int32, (rows, rows), 0)
    tk = lax.broadcasted_iota(jnp.int32, (rows, rows), 1)
    new_ok = [((tk <= tq) if dil == 1 else (tk == tq)) & (tk < t_new) for dil in DILATIONS]

    lane = lax.broadcasted_iota(jnp.int32, (hd, LANES), 1)
    shift_new = (LANES - t_new) - b * t_new
    nt = (((1,), (1,)), ((), ()))
    outs = []
    for h in range(N_HEADS_ATT):
        hs = slice(h * hd, (h + 1) * hd)
        kt = ck_ref[0, h]
        vt = cv_ref[0, h]
        for old, new_ref, out_ref in ((kt, knt_ref, sk_ref), (vt, vnt_ref, sv_ref)):
            moved = pltpu.roll(old, wb - t_new, axis=1)
            new_cols = pltpu.roll(new_ref[hs, :], shift_new, axis=1)
            out_ref[0, h, :, 0:wb - LANES] = moved[:, 0:wb - LANES]
            out_ref[0, h, :, wb - LANES:wb] = jnp.where(lane < LANES - t_new,
                                                        moved[:, wb - LANES:wb], new_cols)
        qh = q_ref[0, :, hs]
        kn_h = kn_ref[0, :, hs].astype(BF16)
        vn_h = vn_ref[0, :, hs].astype(BF16)
        s_old = jnp.dot(qh, kt.astype(BF16), preferred_element_type=F32)
        s_new = lax.dot_general(qh, kn_h, nt, preferred_element_type=F32)
        ps, pes, ms = [], [], []
        for d in range(len(DILATIONS)):
            so = jnp.minimum(s_old, clamp_ref[d])
            sn = jnp.where(new_ok[d], s_new, NEG_INF)
            m = jnp.maximum(jnp.max(so, axis=-1, keepdims=True), jnp.max(sn, axis=-1, keepdims=True))
            ps.append(jnp.exp(so - m))
            pes.append(jnp.exp(sn - m))
            ms.append(m)
        acc = lax.dot_general(jnp.concatenate(ps, axis=0).astype(BF16), vt.astype(BF16), nt,
                              preferred_element_type=F32)
        acc = acc + jnp.dot(jnp.concatenate(pes, axis=0).astype(BF16), vn_h,
                            preferred_element_type=F32)
        m_all = functools.reduce(jnp.maximum, ms)
        den = 0.0
        num = 0.0
        for d in range(len(DILATIONS)):
            wgt = jnp.exp(ms[d] - m_all)
            den = den + wgt * (jnp.sum(ps[d], axis=-1, keepdims=True)
                               + jnp.sum(pes[d], axis=-1, keepdims=True))
            num = num + wgt * acc[d * rows:(d + 1) * rows]
        outs.append(num / den)
    att = jnp.concatenate(outs, axis=1)
    o_ref[0] = (att * g_ref[0].astype(F32)).astype(BF16)


def _scan_lanes(x, op, fill):
    lane = lax.broadcasted_iota(jnp.int32, x.shape, 1)
    d = 1
    while d < x.shape[1]:
        x = op(x, jnp.where(lane >= d, pltpu.roll(x, d, 1), fill))
        d *= 2
    return x


def _gate_scans(gt):
    b = _scan_lanes(pltpu.roll(gt, N_HEADS_MLSTM, 0), jnp.add, 0.0)
    cm = _scan_lanes(gt - b, jnp.maximum, NEG_INF)
    row = lax.broadcasted_iota(jnp.int32, gt.shape, 0)
    return jnp.where(row < N_HEADS_MLSTM, b, pltpu.roll(cm, N_HEADS_MLSTM, 0))


def _mlstm_kernel(*refs, zero_init, scans_given):
    if zero_init:
        (q_ref, k_ref, v_ref, gt_ref, so_ref, gb_ref, ng_ref,
         y_ref, cn_out_ref, m_out_ref, cn_ref, m_ref) = refs
    else:
        (q_ref, k_ref, v_ref, gt_ref, so_ref, gb_ref, ng_ref, cn0_ref, m0_ref,
         y_ref, cn_out_ref, m_out_ref, cn_ref, m_ref) = refs
    c_idx = pl.program_id(1)
    nb = q_ref.shape[0]
    L = CHUNK
    dh = HEAD_DIM_MLSTM
    nh = N_HEADS_MLSTM

    @pl.when(c_idx == 0)
    def _():
        if zero_init:
            cn_ref[...] = jnp.zeros(cn_ref.shape, F32)
            m_ref[...] = jnp.zeros(m_ref.shape, F32)
        else:
            cn_ref[...] = cn0_ref[...]
            m_ref[...] = m0_ref[...]

    tri_t = lax.broadcasted_iota(jnp.int32, (L, L), 0)
    tri_s = lax.broadcasted_iota(jnp.int32, (L, L), 1)
    causal = tri_t >= tri_s
    ones_blk = jnp.ones((L, dh), BF16)
    nt = (((1,), (1,)), ((), ()))
    pairs = [(bb, h) for bb in range(nb) for h in range(nh)]
    sl = lambda h: slice(h * dh, (h + 1) * dh)

    qk = {(bb, h): lax.dot_general(q_ref[bb, :, sl(h)], k_ref[bb, :, sl(h)], nt,
                                   preferred_element_type=F32) for bb, h in pairs}
    cn_old = {p: cn_ref[p[0], p[1]] for p in pairs}
    qc = {(bb, h): jnp.dot(q_ref[bb, :, sl(h)], cn_old[bb, h].astype(BF16),
                           preferred_element_type=F32) for bb, h in pairs}

    rows, cols, decays, w_rows = [], [], [], []
    for bb in range(nb):
        if scans_given:
            i_row = gt_ref[bb, 0:SUBLANES, :]
            sc = gt_ref[bb, SUBLANES:2 * SUBLANES, :]
        else:
            i_row = gt_ref[bb]
            sc = _gate_scans(i_row)
        b = sc
        cm = pltpu.roll(sc, nh, 0)
        m_prev = m_ref[bb]
        m_t = jnp.maximum(m_prev + b, b + cm)
        inter = jnp.exp(m_prev + b - m_t)
        m_last = jnp.broadcast_to(m_t[:, L - 1:L], m_t.shape)
        b_last = jnp.broadcast_to(b[:, L - 1:L], b.shape)
        decay = jnp.exp(m_prev + b_last - m_last)
        w_row = jnp.exp(b_last - b + i_row - m_last)
        m_ref[bb] = m_last
        stack = jnp.concatenate([b - m_t, inter, jnp.exp(-m_t),
                                 jnp.zeros((L - 3 * SUBLANES, L), F32)], axis=0)
        cols.append(stack.T)
        rows.append(i_row - b)
        w_rows.append(w_row)
        decays.append(jnp.concatenate([decay, decay], axis=1))
    col = lambda bb, kind, h: cols[bb][:, kind * SUBLANES + h:kind * SUBLANES + h + 1]

    sqk = {}
    for bb, h in pairs:
        dlog = col(bb, 0, h) + rows[bb][h:h + 1, :]
        sqk[bb, h] = (qk[bb, h] * jnp.exp(jnp.where(causal, dlog, NEG_INF))).astype(BF16)
    v_one = {(bb, h): jnp.concatenate([v_ref[bb, :, sl(h)], ones_blk], axis=1) for bb, h in pairs}
    pv = {p: jnp.dot(sqk[p], v_one[p], preferred_element_type=F32) for p in pairs}
    for bb, h in pairs:
        tot = col(bb, 1, h) * qc[bb, h] + pv[bb, h]
        hh = tot[:, :dh] / jnp.maximum(jnp.abs(tot[:, dh:]), col(bb, 2, h))
        hh = so_ref[bb, :, sl(h)].astype(F32) * hh
        hh = hh * lax.rsqrt(jnp.mean(hh * hh, axis=-1, keepdims=True) + EPS)
        hh = hh * ng_ref[:, sl(h)]
        y_ref[bb, :, sl(h)] = (hh * gb_ref[bb, :, sl(h)].astype(F32)).astype(BF16)
    for bb, h in pairs:
        kt_w = k_ref[bb, :, sl(h)].T.astype(F32) * w_rows[bb][h:h + 1, :]
        upd = jnp.dot(kt_w.astype(BF16), v_one[bb, h], preferred_element_type=F32)
        cn_ref[bb, h] = decays[bb][h:h + 1, :] * cn_old[bb, h] + upd

    @pl.when(c_idx == pl.num_programs(1) - 1)
    def _():
        cn_out_ref[...] = cn_ref[...]
        m_out_ref[...] = m_ref[...]


def _outproj_kernel(x_ref, ya_ref, yb_ref, w_ref, g_ref, o_ref):
    mix = jnp.dot(ya_ref[...], w_ref[0:D_ATT, :], preferred_element_type=F32)
    mix = mix + jnp.dot(yb_ref[...], w_ref[D_ATT:, :], preferred_element_type=F32)
    o_ref[...] = _rmsnorm(x_ref[...] + mix, g_ref[...])


def _rope_tables(pos):
    half = ROT_DIM // 2
    inv = ROPE_THETA ** (-jnp.arange(half, dtype=F32) * 2.0 / ROT_DIM)
    ang = pos.astype(F32)[:, None] * inv[None, :]
    cos, sin = jnp.cos(ang), jnp.sin(ang)
    n = pos.shape[0]
    one = jnp.ones((n, HEAD_DIM_ATT - ROT_DIM), F32)
    zero = jnp.zeros((n, HEAD_DIM_ATT - ROT_DIM), F32)
    zh = jnp.zeros((n, half), F32)
    cos_t = jnp.concatenate([cos, cos, one], axis=1)
    lo_t = jnp.concatenate([-sin, zh, zero], axis=1)
    hi_t = jnp.concatenate([zh, sin, zero], axis=1)
    rep = lambda t: jnp.concatenate([t, t], axis=1)
    return rep(cos_t), rep(lo_t), rep(hi_t)


def _params(sem):
    return pltpu.CompilerParams(dimension_semantics=sem, vmem_limit_bytes=VMEM_LIMIT)


def _const_spec(shape):
    return pl.BlockSpec(shape, lambda *_: (0,) * len(shape))


def _inproj_prompt(x, norm_g, w_pad, conv_w, conv_b, bif):
    B, S, _ = x.shape
    tm = ROW_TILE
    cos, lo, hi = _rope_tables(jnp.arange(S, dtype=jnp.int32))
    tile = lambda width: pl.BlockSpec((1, tm, width), lambda b, j: (b, j, 0))
    tab = pl.BlockSpec((tm, LANES), lambda b, j: (j, 0))
    bf = lambda: jax.ShapeDtypeStruct((B, S, D_ATT), BF16)
    f5 = lambda: jax.ShapeDtypeStruct((B, N_HEADS_ATT, HEAD_DIM_ATT, S), F32)
    tile5 = pl.BlockSpec((1, N_HEADS_ATT, HEAD_DIM_ATT, tm), lambda b, j: (b, 0, 0, j))
    out_shape = (bf(), bf(), bf(), f5(), f5(), bf(), bf(), bf(), bf(), bf(), bf(),
                 jax.ShapeDtypeStruct((B, 2 * SUBLANES, S), F32),
                 jax.ShapeDtypeStruct((B, CONV_WIDTH - 1, 2 * D_MLSTM), F32))
    out_specs = tuple([tile(D_ATT)] * 3 + [tile5] * 2 + [tile(D_ATT)] * 6) + (
        pl.BlockSpec((1, 2 * SUBLANES, tm), lambda b, j: (b, 0, j)),
        pl.BlockSpec((1, CONV_WIDTH - 1, 2 * D_MLSTM), lambda b, j: (b, 0, 0)))
    return pl.pallas_call(
        _inproj_prompt_kernel,
        grid=(B, S // tm),
        in_specs=[tile(D_MODEL), _const_spec((1, D_MODEL)), _const_spec((D_MODEL, N_IN_PAD)),
                  _const_spec((CONV_WIDTH, 2 * D_MLSTM)), _const_spec((1, 2 * D_MLSTM)),
                  _const_spec((1, LANES)), tab, tab, tab],
        out_specs=out_specs,
        out_shape=out_shape,
        scratch_shapes=[pltpu.VMEM((tm + 2 * SUBLANES, 2 * D_MLSTM), F32)],
        compiler_params=_params(("arbitrary", "arbitrary")),
        name="inproj_prompt",
    )(x, norm_g, w_pad, conv_w, conv_b, bif, cos, lo, hi)


def _inproj_sample(x2, norm_g, w_pad, conv_w, conv_b, bif, hist, t_new):
    rows = x2.shape[0]
    pos = PAST_LEN + jnp.arange(t_new, dtype=jnp.int32)
    cos, lo, hi = (jnp.tile(t, (rows // t_new, 1)) for t in _rope_tables(pos))
    bf = lambda: jax.ShapeDtypeStruct((rows, D_ATT), BF16)
    f3 = lambda: jax.ShapeDtypeStruct((rows, D_ATT), F32)
    f5 = lambda: jax.ShapeDtypeStruct((D_ATT, rows), F32)
    out_shape = (bf(), f3(), f3(), f5(), f5(), bf(), bf(), bf(), bf(), bf(), bf(),
                 jax.ShapeDtypeStruct((rows, LANES), F32),
                 jax.ShapeDtypeStruct((rows, 2 * D_MLSTM), F32))
    return pl.pallas_call(
        functools.partial(_inproj_sample_kernel, t_new=t_new),
        out_shape=out_shape,
        compiler_params=pltpu.CompilerParams(vmem_limit_bytes=VMEM_LIMIT),
        name="inproj_sample",
    )(x2, norm_g, w_pad, conv_w, conv_b, bif, cos, lo, hi, *hist)


def _attn_prompt(q, k, v, gate):
    B, S, _ = q.shape
    spec = pl.BlockSpec((1, S, LANES), lambda b, h: (b, 0, h))
    return pl.pallas_call(
        _attn_prompt_kernel,
        grid=(B, D_ATT // LANES),
        in_specs=[spec, spec, spec, spec],
        out_specs=spec,
        out_shape=jax.ShapeDtypeStruct((B, S, D_ATT), BF16),
        scratch_shapes=[
            pltpu.VMEM((5, S, LANES), BF16), pltpu.VMEM((5, S, LANES), F32),
            pltpu.VMEM((5, 16, S // 16, LANES), BF16),
            pltpu.VMEM((N_BACK, 2 * N_BACK), F32),
            pltpu.VMEM((3, 16, S // 16, LANES), F32),
            pltpu.VMEM((3, S, LANES), F32), pltpu.VMEM((3, S, LANES), F32)],
        compiler_params=_params(("arbitrary", "arbitrary")),
        name="attn_prompt",
    )(q, k, v, gate)


def _attn_sample(q, kn, vn, knt, vnt, gate, ck, cv, t_new):
    B, rows, _ = q.shape
    wb = ck.shape[3]
    small = pl.BlockSpec((1, rows, D_ATT), lambda b: (b, 0, 0))
    big = pl.BlockSpec((1, N_HEADS_ATT, HEAD_DIM_ATT, wb), lambda b: (b, 0, 0, 0))
    win_shape = jax.ShapeDtypeStruct((B, N_HEADS_ATT, HEAD_DIM_ATT, wb), F32)
    return pl.pallas_call(
        functools.partial(_attn_sample_kernel, t_new=t_new),
        grid=(B,),
        in_specs=[small, small, small, _const_spec(knt.shape), _const_spec(vnt.shape), small, big, big],
        out_specs=(small, big, big),
        out_shape=(jax.ShapeDtypeStruct((B, rows, D_ATT), BF16), win_shape, win_shape),
        scratch_shapes=[pltpu.VMEM((len(DILATIONS), rows, wb), F32)],
        compiler_params=_params(("arbitrary",)),
        name="attn_sample",
    )(q, kn, vn, knt, vnt, gate, ck, cv)


def _mlstm(q, k, v, gates_t, sig_o, gate_b, norm_g, cn0=None, m0=None):
    B, S, _ = q.shape
    nc = S // CHUNK
    nb = MLSTM_BATCH
    grows = gates_t.shape[1]
    cn_dims = (N_HEADS_MLSTM, HEAD_DIM_MLSTM, 2 * HEAD_DIM_MLSTM)
    tile = pl.BlockSpec((nb, CHUNK, D_MLSTM), lambda b, c: (b, c, 0))
    gspec = pl.BlockSpec((nb, grows, CHUNK), lambda b, c: (b, 0, c))
    cn_spec = pl.BlockSpec((nb,) + cn_dims, lambda b, c: (b, 0, 0, 0))
    m_spec = pl.BlockSpec((nb, SUBLANES, LANES), lambda b, c: (b, 0, 0))
    zero_init = cn0 is None
    in_specs = [tile, tile, tile, gspec, tile, tile, _const_spec((1, D_MLSTM))]
    args = [q, k, v, gates_t, sig_o, gate_b, norm_g]
    if not zero_init:
        in_specs += [cn_spec, m_spec]
        args += [cn0, m0]
    return pl.pallas_call(
        functools.partial(_mlstm_kernel, zero_init=zero_init, scans_given=grows == 2 * SUBLANES),
        grid=(B // nb, nc),
        in_specs=in_specs,
        out_specs=(tile, cn_spec, m_spec),
        out_shape=(jax.ShapeDtypeStruct((B, S, D_MLSTM), BF16),
                   jax.ShapeDtypeStruct((B,) + cn_dims, F32),
                   jax.ShapeDtypeStruct((B, SUBLANES, LANES), F32)),
        scratch_shapes=[pltpu.VMEM((nb,) + cn_dims, F32),
                        pltpu.VMEM((nb, SUBLANES, LANES), F32)],
        compiler_params=_params(("arbitrary", "arbitrary")),
        name="mlstm_prompt" if zero_init else "mlstm_sample",
    )(*args)


def _outproj(x2, ya, yb, w_out, final_g):
    rows = x2.shape[0]
    tm = min(ROW_TILE, rows)
    tile = lambda width: pl.BlockSpec((tm, width), lambda i: (i, 0))
    return pl.pallas_call(
        _outproj_kernel,
        grid=(rows // tm,),
        in_specs=[tile(D_MODEL), tile(D_ATT), tile(D_MLSTM),
                  _const_spec((D_ATT + D_MLSTM, D_MODEL)), _const_spec((1, D_MODEL))],
        out_specs=tile(D_MODEL),
        out_shape=jax.ShapeDtypeStruct((rows, D_MODEL), F32),
        compiler_params=_params(("arbitrary",)),
        name="outproj",
    )(x2, ya, yb, w_out, final_g)


def kernel(x_prompt, x_sample, cache_win_k, cache_win_v, state_conv, state_C, state_n, state_m,
           norm_g, w_in, conv_w, conv_b, b_i, b_f, mlstm_norm_g, w_out, final_norm_g):
    assert w_in.shape[0] == 1, "single-layer model"
    B, S, D = x_prompt.shape
    DB, T, _ = x_sample.shape
    HB, DK = N_HEADS_MLSTM, HEAD_DIM_MLSTM
    wb = cache_win_k.shape[2]
    assert S % ROW_TILE == 0 and S == 16 * N_BACK and wb >= 16 * N_BACK and CONV_WIDTH - 1 <= T <= SUBLANES and DB * T == LANES

    w_pad = jnp.pad(w_in[0], ((0, 0), (0, N_IN_PAD - N_IN))).astype(BF16)
    w_o = w_out[0].astype(BF16)
    g_in = norm_g[0][None, :]
    cw, cb = conv_w[0], conv_b[0][None, :]
    bif = jnp.pad(jnp.concatenate([b_i[0], b_f[0]]), (0, LANES - 2 * HB))[None, :]
    ng = mlstm_norm_g[0][None, :]
    g_fin = final_norm_g[None, :]

    (q_p, k_p, v_p, pk, pv, ga_p, qm_p, km_p, vb_p, so_p, gb_p, gt_p, p_conv) = _inproj_prompt(
        x_prompt, g_in, w_pad, cw, cb, bif)
    ya_p = _attn_prompt(q_p, k_p, v_p, ga_p)
    yb_p, cn_p, m_p = _mlstm(qm_p, km_p, vb_p, gt_p, so_p, gb_p, ng)
    y_prompt = _outproj(x_prompt.reshape(B * S, D), ya_p.reshape(B * S, D_ATT),
                        yb_p.reshape(B * S, D_MLSTM), w_o, g_fin).reshape(B, S, D)

    sc = state_conv[0]
    zrow = jnp.zeros((DB, 1, 2 * D_MLSTM), F32)
    hist = []
    for sh in (1, 2, 3):
        rows_ = [sc[:, CONV_WIDTH - 1 + t - sh:CONV_WIDTH + t - sh] if t < sh else zrow
                 for t in range(T)]
        hist.append(jnp.concatenate(rows_, axis=1).reshape(DB * T, 2 * D_MLSTM))
    (q_s, kn, vn, knt, vnt, ga_s, qm_s, km_s, vb_s, so_s, gb_s, gates_s, qk_s) = _inproj_sample(
        x_sample.reshape(DB * T, D), g_in, w_pad, cw, cb, bif, hist, T)
    r3 = lambda a: a.reshape(DB, T, a.shape[-1])
    pad8 = lambda a: jnp.pad(r3(a), ((0, 0), (0, SUBLANES - T), (0, 0)))
    to_hdp = lambda c: jnp.transpose(c[0], (0, 2, 3, 1))
    from_hdp = lambda c: jnp.transpose(c, (0, 3, 1, 2))[None]
    ya_s, s_k, s_v = _attn_sample(pad8(q_s), pad8(kn), pad8(vn), knt, vnt, pad8(ga_s),
                                  to_hdp(cache_win_k), to_hdp(cache_win_v), T)
    ya_s = ya_s[:, :T]

    pad_t = lambda a: jnp.pad(r3(a), ((0, 0), (0, CHUNK - T), (0, 0)))
    g3 = r3(gates_s)[:, :, :SUBLANES]
    null_gate = jnp.concatenate([jnp.full((HB,), NEG_INF, F32), jnp.zeros((HB,), F32)])
    gt_s = jnp.concatenate([g3, jnp.broadcast_to(null_gate, (DB, CHUNK - T, SUBLANES))], axis=1)
    gt_s = gt_s.transpose(0, 2, 1)
    cn0 = jnp.concatenate([state_C[0], jnp.broadcast_to(state_n[0][..., None], (DB, HB, DK, DK))],
                          axis=-1)
    m0 = jnp.broadcast_to(jnp.pad(state_m[0], ((0, 0), (0, SUBLANES - HB)))[..., None],
                          (DB, SUBLANES, LANES))
    yb_s, cn_s, m_s = _mlstm(pad_t(qm_s), pad_t(km_s), pad_t(vb_s), gt_s, pad_t(so_s), pad_t(gb_s),
                             ng, cn0, m0)
    y_sample = _outproj(x_sample.reshape(DB * T, D), ya_s.reshape(DB * T, D_ATT),
                        yb_s[:, :T].reshape(DB * T, D_MLSTM), w_o, g_fin).reshape(DB, T, D)

    return (y_prompt, y_sample,
            from_hdp(pk), from_hdp(pv), p_conv[None],
            cn_p[None, ..., :DK], cn_p[None, ..., DK], m_p[None, :, :HB, 0],
            from_hdp(s_k), from_hdp(s_v), r3(qk_s)[None, :, T - (CONV_WIDTH - 1):],
            cn_s[None, ..., :DK], cn_s[None, ..., DK], m_s[None, :, :HB, 0])
```

```python
import functools
import math

import jax
import jax.numpy as jnp
import numpy as np
from jax import lax
from jax.experimental import pallas as pl
from jax.experimental.pallas import tpu as pltpu

F32 = jnp.float32
BF16 = jnp.bfloat16

D_MODEL = 1024
D_ATT = 512
N_HEADS_ATT = 8
HEAD_DIM_ATT = 64
D_MLSTM = 512
N_HEADS_MLSTM = 4
HEAD_DIM_MLSTM = 128
ROT_DIM = 16
ROPE_THETA = 500000.0
PAST_LEN = 16384
DILATIONS = (1, 4, 16)
N_BACK = 128
CONV_WIDTH = 4
CHUNK = 128
EPS = 1e-6
NEG_INF = -1e30
LOG2_E = math.log2(math.e)

LANES = 128
SUBLANES = 8
N_IN = 4 * D_ATT + 5 * D_MLSTM + 2 * N_HEADS_MLSTM
N_IN_PAD = 4 * D_ATT + 5 * D_MLSTM + LANES
OFF_QA, OFF_KA, OFF_VA, OFF_ZA = 0, 512, 1024, 1536
OFF_QB, OFF_VB, OFF_OB, OFF_ZB, OFF_G = 2048, 3072, 3584, 4096, 4608
ROW_TILE = 512
MLSTM_BATCH = 4
VMEM_LIMIT = 56 * 1024 * 1024


def _silu(x):
    return x * jax.nn.sigmoid(x)


def _rmsnorm(x, g):
    return x * lax.rsqrt(jnp.mean(x * x, axis=-1, keepdims=True) + EPS) * g


def _rope(u, cos, sin_lo, sin_hi):
    outs = []
    for c in range(u.shape[1] // LANES):
        xs = u[:, c * LANES:(c + 1) * LANES]
        outs.append(xs * cos + pltpu.roll(xs, LANES - ROT_DIM // 2, 1) * sin_lo
                    + pltpu.roll(xs, ROT_DIM // 2, 1) * sin_hi)
    return jnp.concatenate(outs, axis=1)


def _gate_block(ug, bif):
    gz = ug + bif
    lane = lax.broadcasted_iota(jnp.int32, gz.shape, 1)
    logf = jnp.minimum(gz, 0.0) - jnp.log1p(jnp.exp(-jnp.abs(gz)))
    return jnp.where(lane < N_HEADS_MLSTM, gz, logf)


def _segments(hn, w_ref):
    def seg(off, width):
        return jnp.dot(hn, w_ref[:, off:off + width], preferred_element_type=F32)
    return seg


def _inproj_common(seg, cos, sin_lo, sin_hi, q_ref, ga_ref, vb_ref, so_ref, gb_ref):
    q = _rope(seg(OFF_QA, D_ATT), cos, sin_lo, sin_hi) * (HEAD_DIM_ATT ** -0.5 * LOG2_E)
    q_ref[...] = q.astype(BF16).reshape(q_ref.shape)
    k = _rope(seg(OFF_KA, D_ATT), cos, sin_lo, sin_hi)
    v = seg(OFF_VA, D_ATT)
    ga_ref[...] = _silu(seg(OFF_ZA, D_ATT)).astype(BF16).reshape(ga_ref.shape)
    vb_ref[...] = seg(OFF_VB, D_MLSTM).astype(BF16).reshape(vb_ref.shape)
    so_ref[...] = jax.nn.sigmoid(seg(OFF_OB, D_MLSTM)).astype(BF16).reshape(so_ref.shape)
    gb_ref[...] = _silu(seg(OFF_ZB, D_MLSTM)).astype(BF16).reshape(gb_ref.shape)
    return k, v


def _inproj_prompt_kernel(x_ref, g_ref, w_ref, cw_ref, cb_ref, bif_ref, cos_ref, slo_ref, shi_ref,
                          q_ref, kb_ref, vbf_ref, pk_ref, pv_ref, ga_ref, qm_ref, km_ref, vb_ref,
                          so_ref, gb_ref, gt_ref, pconv_ref, xp_ref):
    j = pl.program_id(1)
    tm = x_ref.shape[1]
    hn = _rmsnorm(x_ref[0], g_ref[...]).astype(BF16)
    seg = _segments(hn, w_ref)

    gates = _gate_block(seg(OFF_G, LANES), bif_ref[...])
    assert CHUNK == LANES
    for i in range(tm // CHUNK):
        gt = gates[i * CHUNK:(i + 1) * CHUNK, :].T[0:SUBLANES, :]
        gt_ref[0, 0:SUBLANES, i * CHUNK:(i + 1) * CHUNK] = gt
        gt_ref[0, SUBLANES:2 * SUBLANES, i * CHUNK:(i + 1) * CHUNK] = _gate_scans(gt)

    k, v = _inproj_common(seg, cos_ref[...], slo_ref[...], shi_ref[...],
                          q_ref, ga_ref, vb_ref, so_ref, gb_ref)
    kb_ref[0] = k.astype(BF16)
    vbf_ref[0] = v.astype(BF16)
    pk_ref[0] = k.T.reshape(N_HEADS_ATT, HEAD_DIM_ATT, tm)
    pv_ref[0] = v.T.reshape(N_HEADS_ATT, HEAD_DIM_ATT, tm)

    @pl.when(j == 0)
    def _():
        xp_ref[0:SUBLANES, :] = jnp.zeros((SUBLANES, 2 * D_MLSTM), F32)

    @pl.when(j > 0)
    def _():
        xp_ref[0:SUBLANES, :] = xp_ref[tm:tm + SUBLANES, :]

    xp_ref[SUBLANES:SUBLANES + tm, :] = seg(OFF_QB, 2 * D_MLSTM)
    y = cb_ref[...] + xp_ref[SUBLANES:SUBLANES + tm, :] * cw_ref[3:4, :]
    for jj in range(CONV_WIDTH - 1):
        sh = CONV_WIDTH - 1 - jj
        y = y + xp_ref[SUBLANES - sh:SUBLANES - sh + tm, :] * cw_ref[jj:jj + 1, :]
    y = _silu(y)
    qm_ref[0] = y[:, :D_MLSTM].astype(BF16)
    km_ref[0] = (y[:, D_MLSTM:] * (HEAD_DIM_MLSTM ** -0.5)).astype(BF16)

    @pl.when(j == pl.num_programs(1) - 1)
    def _():
        pconv_ref[0] = xp_ref[tm + SUBLANES - (CONV_WIDTH - 1):tm + SUBLANES, :]


def _inproj_sample_kernel(x_ref, g_ref, w_ref, cw_ref, cb_ref, bif_ref, cos_ref, slo_ref, shi_ref,
                          h1_ref, h2_ref, h3_ref,
                          q_ref, kn_ref, vn_ref, knt_ref, vnt_ref, ga_ref, qm_ref, km_ref, vb_ref,
                          so_ref, gb_ref, gates_ref, qk_ref, *, t_new):
    hn = _rmsnorm(x_ref[...], g_ref[...]).astype(BF16)
    seg = _segments(hn, w_ref)
    k, v = _inproj_common(seg, cos_ref[...], slo_ref[...], shi_ref[...],
                          q_ref, ga_ref, vb_ref, so_ref, gb_ref)
    kn_ref[...] = k
    vn_ref[...] = v
    knt_ref[...] = k.T
    vnt_ref[...] = v.T
    u = seg(OFF_QB, 2 * D_MLSTM)
    qk_ref[...] = u
    t = lax.rem(lax.broadcasted_iota(jnp.int32, u.shape, 0), t_new)
    y = cb_ref[...] + u * cw_ref[3:4, :]
    for sh, h_ref in ((1, h1_ref), (2, h2_ref), (3, h3_ref)):
        prev = jnp.where(t >= sh, pltpu.roll(u, sh, 0), h_ref[...])
        y = y + prev * cw_ref[3 - sh:4 - sh, :]
    y = _silu(y)
    qm_ref[...] = y[:, :D_MLSTM].astype(BF16)
    km_ref[...] = (y[:, D_MLSTM:] * (HEAD_DIM_MLSTM ** -0.5)).astype(BF16)
    gates_ref[...] = _gate_block(seg(OFF_G, LANES), bif_ref[...])


def _attn_prompt_kernel(q_ref, k_ref, v_ref, g_ref, o_ref,
                        src1_ref, src4_ref, src16_ref, bias_ref, st16_ref, st16p_ref, st4_ref):
    seq = q_ref.shape[1]
    n16 = seq // 16
    group = 8
    head_a_full = lax.broadcasted_iota(jnp.int32, (seq, LANES), 1) < HEAD_DIM_ATT
    by16 = lambda x: jnp.swapaxes(x.reshape(n16, 16, LANES), 0, 1)
    q = q_ref[0].astype(F32)
    k = k_ref[0].astype(F32)
    v = v_ref[0].astype(F32)
    ops = (jnp.where(head_a_full, q, 0.0), jnp.where(head_a_full, 0.0, q), k,
           jnp.where(head_a_full, v, 1.0), jnp.where(head_a_full, 1.0, v))
    for i, x in enumerate(ops):
        src4_ref[i] = x
        src1_ref[i] = x.astype(BF16)
    head_a_16 = lax.broadcasted_iota(jnp.int32, (16, n16, LANES), 2) < HEAD_DIM_ATT
    q16, k16, v16 = by16(q), by16(k), by16(v)
    ops16 = (jnp.where(head_a_16, q16, 0.0), jnp.where(head_a_16, 0.0, q16), k16,
             jnp.where(head_a_16, v16, 1.0), jnp.where(head_a_16, 1.0, v16))
    for i, x in enumerate(ops16):
        src16_ref[i] = x.astype(BF16)
    u = lax.broadcasted_iota(jnp.int32, (N_BACK, 2 * N_BACK), 0)
    w = lax.broadcasted_iota(jnp.int32, (N_BACK, 2 * N_BACK), 1)
    bias_ref[...] = jnp.where((w >= u) & (w <= u + N_BACK), jnp.finfo(F32).max, NEG_INF)
    head_a = lax.broadcasted_iota(jnp.int32, (N_BACK, LANES), 1) < HEAD_DIM_ATT

    def partials(gets):
        staged = []
        for get, has_prev in gets:
            if has_prev:
                both = lambda i, get=get: jnp.concatenate([get(i, True), get(i, False)], axis=0)
                bias = bias_ref[...]
            else:
                both = lambda i, get=get: get(i, False)
                bias = bias_ref[:, N_BACK:]
            kk = both(2)
            heads = []
            for qi in (0, 1):
                s = lax.dot_general(get(qi, False), kk, (((1,), (1,)), ((), ())),
                                    preferred_element_type=F32)
                s = jnp.minimum(s, bias)
                mh = jnp.max(s, axis=-1, keepdims=True)
                heads.append((mh, jnp.exp2(s - mh).astype(BF16)))
            staged.append((both, heads))
        out = []
        for both, ((m_a, p_a), (m_b, p_b)) in staged:
            pv_a = jnp.dot(p_a, both(3), preferred_element_type=F32)
            pv_b = jnp.dot(p_b, both(4), preferred_element_type=F32)
            acc = jnp.where(head_a, pv_a, pv_b)
            den = pltpu.roll(jnp.where(head_a, pv_b, pv_a), HEAD_DIM_ATT, 1)
            out.append((jnp.where(head_a, m_a, m_b), den, acc))
        return out

    def get4(qstart, kprev_start):
        def get(i, prev):
            start = kprev_start if prev else qstart
            return src4_ref[i, pl.ds(start, N_BACK, stride=4), :].astype(BF16)
        return get, kprev_start is not None

    def get1(qstart, kprev_start):
        def get(i, prev):
            return src1_ref[i, pl.ds(kprev_start if prev else qstart, N_BACK), :]
        return get, kprev_start is not None

    def keep4(blocks):
        res = partials([get4(qs, ks) for qs, ks in blocks])
        for (qs, _), (m, den, acc) in zip(blocks, res):
            rows = pl.ds(qs, N_BACK, stride=4)
            st4_ref[0, rows, :] = m
            st4_ref[1, rows, :] = den
            st4_ref[2, rows, :] = acc

    def finish(blocks):
        res = partials([get1(qs, ks) for qs, ks in blocks])
        for (qs, _), part in zip(blocks, res):
            rows = pl.ds(qs, N_BACK)
            parts = [part] + [tuple(st[i, rows, :] for i in range(3)) for st in (st4_ref, st16p_ref)]
            m_all = functools.reduce(jnp.maximum, [pt[0] for pt in parts])
            wts = [jnp.exp2(pt[0] - m_all) for pt in parts]
            den = sum(wt * pt[1] for wt, pt in zip(wts, parts))
            num = sum(wt * pt[2] for wt, pt in zip(wts, parts))
            o_ref[0, rows, :] = (num / den * g_ref[0, rows, :].astype(F32)).astype(BF16)

    def body16(g, c):
        rs = [g * group + rr for rr in range(group)]
        res = partials([(lambda i, prev, r=r: src16_ref[i, r], False) for r in rs])
        for r, (m, den, acc) in zip(rs, res):
            st16_ref[0, r] = m
            st16_ref[1, r] = den
            st16_ref[2, r] = acc
        return c
    lax.fori_loop(0, 16 // group, body16, 0)
    for i in range(3):
        st16p_ref[i] = jnp.swapaxes(st16_ref[i], 0, 1).reshape(seq, LANES)

    blocks4 = lambda cc: [(r + 4 * N_BACK * cc, r + 4 * N_BACK * (cc - 1)) for r in range(4)]
    keep4([(r, None) for r in range(4)] + blocks4(1))

    def body4(g, c):
        keep4(blocks4(2 * g) + blocks4(2 * g + 1))
        return c
    lax.fori_loop(1, seq // 4 // N_BACK // 2, body4, 0)

    finish([(0, None)] + [(cc * N_BACK, (cc - 1) * N_BACK) for cc in range(1, group)])

    def body1(g, c):
        starts = [pl.multiple_of((g * group + rr) * N_BACK, N_BACK) for rr in range(group)]
        finish([(st, st - N_BACK) for st in starts])
        return c
    lax.fori_loop(1, seq // N_BACK // group, body1, 0)


def _attn_sample_kernel(q_ref, kn_ref, vn_ref, knt_ref, vnt_ref, g_ref, ck_ref, cv_ref,
                        o_ref, sk_ref, sv_ref, clamp_ref, *, t_new):
    b = pl.program_id(0)
    wb = ck_ref.shape[3]
    hd = HEAD_DIM_ATT
    rows = q_ref.shape[1]

    @pl.when(b == 0)
    def _():
        delta = (wb + lax.broadcasted_iota(jnp.int32, (rows, wb), 0)
                 - lax.broadcasted_iota(jnp.int32, (rows, wb), 1))
        for d, dil in enumerate(DILATIONS):
            ok = ((delta & (dil - 1)) == 0) & (delta >= dil) & (delta <= N_BACK * dil)
            clamp_ref[d] = jnp.where(ok, jnp.finfo(F32).max, NEG_INF)

    tq = lax.broadcasted_iota(jnp.int32, (rows, rows), 0)
    tk = lax.broadcasted_iota(jnp.int32, (rows, rows), 1)
    new_ok = [((tk <= tq) if dil == 1 else (tk == tq)) & (tk < t_new) for dil in DILATIONS]

    lane = lax.broadcasted_iota(jnp.int32, (hd, LANES), 1)
    shift_new = (LANES - t_new) - b * t_new
    nt = (((1,), (1,)), ((), ()))
    outs = []
    for h in range(N_HEADS_ATT):
        hs = slice(h * hd, (h + 1) * hd)
        kt = ck_ref[0, h]
        vt = cv_ref[0, h]
        for old, new_ref, out_ref in ((kt, knt_ref, sk_ref), (vt, vnt_ref, sv_ref)):
            moved = pltpu.roll(old, wb - t_new, axis=1)
            new_cols = pltpu.roll(new_ref[hs, :], shift_new, axis=1)
            out_ref[0, h, :, 0:wb - LANES] = moved[:, 0:wb - LANES]
            out_ref[0, h, :, wb - LANES:wb] = jnp.where(lane < LANES - t_new,
                                                        moved[:, wb - LANES:wb], new_cols)
        qh = q_ref[0, :, hs]
        kn_h = kn_ref[0, :, hs].astype(BF16)
        vn_h = vn_ref[0, :, hs].astype(BF16)
        s_old = jnp.dot(qh, kt.astype(BF16), preferred_element_type=F32)
        s_new = lax.dot_general(qh, kn_h, nt, preferred_element_type=F32)
        ps, pes, ms = [], [], []
        for d in range(len(DILATIONS)):
            so = jnp.minimum(s_old, clamp_ref[d])
            sn = jnp.where(new_ok[d], s_new, NEG_INF)
            m = jnp.maximum(jnp.max(so, axis=-1, keepdims=True), jnp.max(sn, axis=-1, keepdims=True))
            ps.append(jnp.exp2(so - m))
            pes.append(jnp.exp2(sn - m))
            ms.append(m)
        acc = lax.dot_general(jnp.concatenate(ps, axis=0).astype(BF16), vt.astype(BF16), nt,
                              preferred_element_type=F32)
        acc = acc + jnp.dot(jnp.concatenate(pes, axis=0).astype(BF16), vn_h,
                            preferred_element_type=F32)
        m_all = functools.reduce(jnp.maximum, ms)
        den = 0.0
        num = 0.0
        for d in range(len(DILATIONS)):
            wgt = jnp.exp2(ms[d] - m_all)
            den = den + wgt * (jnp.sum(ps[d], axis=-1, keepdims=True)
                               + jnp.sum(pes[d], axis=-1, keepdims=True))
            num = num + wgt * acc[d * rows:(d + 1) * rows]
        outs.append(num / den)
    att = jnp.concatenate(outs, axis=1)
    o_ref[0] = (att * g_ref[0].astype(F32)).astype(BF16)


def _scan_lanes(x, op, fill):
    lane = lax.broadcasted_iota(jnp.int32, x.shape, 1)
    d = 1
    while d < x.shape[1]:
        x = op(x, jnp.where(lane >= d, pltpu.roll(x, d, 1), fill))
        d *= 2
    return x


def _gate_scans(gt):
    b = _scan_lanes(pltpu.roll(gt, N_HEADS_MLSTM, 0), jnp.add, 0.0)
    cm = _scan_lanes(gt - b, jnp.maximum, NEG_INF)
    row = lax.broadcasted_iota(jnp.int32, gt.shape, 0)
    return jnp.where(row < N_HEADS_MLSTM, b, pltpu.roll(cm, N_HEADS_MLSTM, 0))


def _mlstm_kernel(*refs, zero_init, scans_given):
    if zero_init:
        (q_ref, k_ref, v_ref, gt_ref, so_ref, gb_ref, ng_ref,
         y_ref, cn_out_ref, m_out_ref, cn_ref, m_ref) = refs
    else:
        (q_ref, k_ref, v_ref, gt_ref, so_ref, gb_ref, ng_ref, cn0_ref, m0_ref,
         y_ref, cn_out_ref, m_out_ref, cn_ref, m_ref) = refs
    c_idx = pl.program_id(1)
    nb = q_ref.shape[0]
    L = CHUNK
    dh = HEAD_DIM_MLSTM
    nh = N_HEADS_MLSTM

    @pl.when(c_idx == 0)
    def _():
        if zero_init:
            cn_ref[...] = jnp.zeros(cn_ref.shape, F32)
            m_ref[...] = jnp.zeros(m_ref.shape, F32)
        else:
            cn_ref[...] = cn0_ref[...]
            m_ref[...] = m0_ref[...]

    tri_t = lax.broadcasted_iota(jnp.int32, (L, L), 0)
    tri_s = lax.broadcasted_iota(jnp.int32, (L, L), 1)
    causal = tri_t >= tri_s
    ones_blk = jnp.ones((L, dh), BF16)
    nt = (((1,), (1,)), ((), ()))
    pairs = [(bb, h) for bb in range(nb) for h in range(nh)]
    sl = lambda h: slice(h * dh, (h + 1) * dh)

    qk = {(bb, h): lax.dot_general(q_ref[bb, :, sl(h)], k_ref[bb, :, sl(h)], nt,
                                   preferred_element_type=F32) for bb, h in pairs}
    cn_old = {p: cn_ref[p[0], p[1]] for p in pairs}
    qc = {(bb, h): jnp.dot(q_ref[bb, :, sl(h)], cn_old[bb, h].astype(BF16),
                           preferred_element_type=F32) for bb, h in pairs}

    rows, cols, decays, w_rows = [], [], [], []
    for bb in range(nb):
        if scans_given:
            i_row = gt_ref[bb, 0:SUBLANES, :]
            sc = gt_ref[bb, SUBLANES:2 * SUBLANES, :]
        else:
            i_row = gt_ref[bb]
            sc = _gate_scans(i_row)
        b = sc
        cm = pltpu.roll(sc, nh, 0)
        m_prev = m_ref[bb]
        m_t = jnp.maximum(m_prev + b, b + cm)
        inter = jnp.exp(m_prev + b - m_t)
        m_last = jnp.broadcast_to(m_t[:, L - 1:L], m_t.shape)
        b_last = jnp.broadcast_to(b[:, L - 1:L], b.shape)
        decay = jnp.exp(m_prev + b_last - m_last)
        w_row = jnp.exp(b_last - b + i_row - m_last)
        m_ref[bb] = m_last
        stack = jnp.concatenate([b - m_t, inter, jnp.exp(-m_t),
                                 jnp.zeros((L - 3 * SUBLANES, L), F32)], axis=0)
        cols.append(stack.T)
        rows.append(i_row - b)
        w_rows.append(w_row)
        decays.append(jnp.concatenate([decay, decay], axis=1))
    col = lambda bb, kind, h: cols[bb][:, kind * SUBLANES + h:kind * SUBLANES + h + 1]

    sqk = {}
    for bb, h in pairs:
        dlog = col(bb, 0, h) + rows[bb][h:h + 1, :]
        sqk[bb, h] = (qk[bb, h] * jnp.exp(jnp.where(causal, dlog, NEG_INF))).astype(BF16)
    v_one = {(bb, h): jnp.concatenate([v_ref[bb, :, sl(h)], ones_blk], axis=1) for bb, h in pairs}
    pv = {p: jnp.dot(sqk[p], v_one[p], preferred_element_type=F32) for p in pairs}
    for bb, h in pairs:
        tot = col(bb, 1, h) * qc[bb, h] + pv[bb, h]
        hh = tot[:, :dh] / jnp.maximum(jnp.abs(tot[:, dh:]), col(bb, 2, h))
        hh = so_ref[bb, :, sl(h)].astype(F32) * hh
        hh = hh * lax.rsqrt(jnp.mean(hh * hh, axis=-1, keepdims=True) + EPS)
        hh = hh * ng_ref[:, sl(h)]
        y_ref[bb, :, sl(h)] = (hh * gb_ref[bb, :, sl(h)].astype(F32)).astype(BF16)
    for bb, h in pairs:
        kt_w = k_ref[bb, :, sl(h)].T.astype(F32) * w_rows[bb][h:h + 1, :]
        upd = jnp.dot(kt_w.astype(BF16), v_one[bb, h], preferred_element_type=F32)
        cn_ref[bb, h] = decays[bb][h:h + 1, :] * cn_old[bb, h] + upd

    @pl.when(c_idx == pl.num_programs(1) - 1)
    def _():
        cn_out_ref[...] = cn_ref[...]
        m_out_ref[...] = m_ref[...]


def _outproj_kernel(x_ref, ya_ref, yb_ref, w_ref, g_ref, o_ref):
    mix = jnp.dot(ya_ref[...], w_ref[0:D_ATT, :], preferred_element_type=F32)
    mix = mix + jnp.dot(yb_ref[...], w_ref[D_ATT:, :], preferred_element_type=F32)
    o_ref[...] = _rmsnorm(x_ref[...] + mix, g_ref[...])


def _rope_tables(pos):
    half = ROT_DIM // 2
    inv = ROPE_THETA ** (-jnp.arange(half, dtype=F32) * 2.0 / ROT_DIM)
    ang = pos.astype(F32)[:, None] * inv[None, :]
    cos, sin = jnp.cos(ang), jnp.sin(ang)
    n = pos.shape[0]
    one = jnp.ones((n, HEAD_DIM_ATT - ROT_DIM), F32)
    zero = jnp.zeros((n, HEAD_DIM_ATT - ROT_DIM), F32)
    zh = jnp.zeros((n, half), F32)
    cos_t = jnp.concatenate([cos, cos, one], axis=1)
    lo_t = jnp.concatenate([-sin, zh, zero], axis=1)
    hi_t = jnp.concatenate([zh, sin, zero], axis=1)
    rep = lambda t: jnp.concatenate([t, t], axis=1)
    return rep(cos_t), rep(lo_t), rep(hi_t)


def _params(sem):
    return pltpu.CompilerParams(dimension_semantics=sem, vmem_limit_bytes=VMEM_LIMIT)


def _const_spec(shape):
    return pl.BlockSpec(shape, lambda *_: (0,) * len(shape))


def _inproj_prompt(x, norm_g, w_pad, conv_w, conv_b, bif):
    B, S, _ = x.shape
    tm = ROW_TILE
    cos, lo, hi = _rope_tables(jnp.arange(S, dtype=jnp.int32))
    tile = lambda width: pl.BlockSpec((1, tm, width), lambda b, j: (b, j, 0))
    tab = pl.BlockSpec((tm, LANES), lambda b, j: (j, 0))
    bf = lambda: jax.ShapeDtypeStruct((B, S, D_ATT), BF16)
    f5 = lambda: jax.ShapeDtypeStruct((B, N_HEADS_ATT, HEAD_DIM_ATT, S), F32)
    tile5 = pl.BlockSpec((1, N_HEADS_ATT, HEAD_DIM_ATT, tm), lambda b, j: (b, 0, 0, j))
    out_shape = (bf(), bf(), bf(), f5(), f5(), bf(), bf(), bf(), bf(), bf(), bf(),
                 jax.ShapeDtypeStruct((B, 2 * SUBLANES, S), F32),
                 jax.ShapeDtypeStruct((B, CONV_WIDTH - 1, 2 * D_MLSTM), F32))
    out_specs = tuple([tile(D_ATT)] * 3 + [tile5] * 2 + [tile(D_ATT)] * 6) + (
        pl.BlockSpec((1, 2 * SUBLANES, tm), lambda b, j: (b, 0, j)),
        pl.BlockSpec((1, CONV_WIDTH - 1, 2 * D_MLSTM), lambda b, j: (b, 0, 0)))
    return pl.pallas_call(
        _inproj_prompt_kernel,
        grid=(B, S // tm),
        in_specs=[tile(D_MODEL), _const_spec((1, D_MODEL)), _const_spec((D_MODEL, N_IN_PAD)),
                  _const_spec((CONV_WIDTH, 2 * D_MLSTM)), _const_spec((1, 2 * D_MLSTM)),
                  _const_spec((1, LANES)), tab, tab, tab],
        out_specs=out_specs,
        out_shape=out_shape,
        scratch_shapes=[pltpu.VMEM((tm + 2 * SUBLANES, 2 * D_MLSTM), F32)],
        compiler_params=_params(("arbitrary", "arbitrary")),
        name="inproj_prompt",
    )(x, norm_g, w_pad, conv_w, conv_b, bif, cos, lo, hi)


def _inproj_sample(x2, norm_g, w_pad, conv_w, conv_b, bif, hist, t_new):
    rows = x2.shape[0]
    pos = PAST_LEN + jnp.arange(t_new, dtype=jnp.int32)
    cos, lo, hi = (jnp.tile(t, (rows // t_new, 1)) for t in _rope_tables(pos))
    bf = lambda: jax.ShapeDtypeStruct((rows, D_ATT), BF16)
    f3 = lambda: jax.ShapeDtypeStruct((rows, D_ATT), F32)
    f5 = lambda: jax.ShapeDtypeStruct((D_ATT, rows), F32)
    out_shape = (bf(), f3(), f3(), f5(), f5(), bf(), bf(), bf(), bf(), bf(), bf(),
                 jax.ShapeDtypeStruct((rows, LANES), F32),
                 jax.ShapeDtypeStruct((rows, 2 * D_MLSTM), F32))
    return pl.pallas_call(
        functools.partial(_inproj_sample_kernel, t_new=t_new),
        out_shape=out_shape,
        compiler_params=pltpu.CompilerParams(vmem_limit_bytes=VMEM_LIMIT),
        name="inproj_sample",
    )(x2, norm_g, w_pad, conv_w, conv_b, bif, cos, lo, hi, *hist)


def _attn_prompt(q, k, v, gate):
    B, S, _ = q.shape
    spec = pl.BlockSpec((1, S, LANES), lambda b, h: (b, 0, h))
    return pl.pallas_call(
        _attn_prompt_kernel,
        grid=(B, D_ATT // LANES),
        in_specs=[spec, spec, spec, spec],
        out_specs=spec,
        out_shape=jax.ShapeDtypeStruct((B, S, D_ATT), BF16),
        scratch_shapes=[
            pltpu.VMEM((5, S, LANES), BF16), pltpu.VMEM((5, S, LANES), F32),
            pltpu.VMEM((5, 16, S // 16, LANES), BF16),
            pltpu.VMEM((N_BACK, 2 * N_BACK), F32),
            pltpu.VMEM((3, 16, S // 16, LANES), F32),
            pltpu.VMEM((3, S, LANES), F32), pltpu.VMEM((3, S, LANES), F32)],
        compiler_params=_params(("arbitrary", "arbitrary")),
        name="attn_prompt",
    )(q, k, v, gate)


def _attn_sample(q, kn, vn, knt, vnt, gate, ck, cv, t_new):
    B, rows, _ = q.shape
    wb = ck.shape[3]
    small = pl.BlockSpec((1, rows, D_ATT), lambda b: (b, 0, 0))
    big = pl.BlockSpec((1, N_HEADS_ATT, HEAD_DIM_ATT, wb), lambda b: (b, 0, 0, 0))
    win_shape = jax.ShapeDtypeStruct((B, N_HEADS_ATT, HEAD_DIM_ATT, wb), F32)
    return pl.pallas_call(
        functools.partial(_attn_sample_kernel, t_new=t_new),
        grid=(B,),
        in_specs=[small, small, small, _const_spec(knt.shape), _const_spec(vnt.shape), small, big, big],
        out_specs=(small, big, big),
        out_shape=(jax.ShapeDtypeStruct((B, rows, D_ATT), BF16), win_shape, win_shape),
        scratch_shapes=[pltpu.VMEM((len(DILATIONS), rows, wb), F32)],
        compiler_params=_params(("arbitrary",)),
        name="attn_sample",
    )(q, kn, vn, knt, vnt, gate, ck, cv)


def _mlstm(q, k, v, gates_t, sig_o, gate_b, norm_g, cn0=None, m0=None):
    B, S, _ = q.shape
    nc = S // CHUNK
    nb = MLSTM_BATCH
    grows = gates_t.shape[1]
    cn_dims = (N_HEADS_MLSTM, HEAD_DIM_MLSTM, 2 * HEAD_DIM_MLSTM)
    tile = pl.BlockSpec((nb, CHUNK, D_MLSTM), lambda b, c: (b, c, 0))
    gspec = pl.BlockSpec((nb, grows, CHUNK), lambda b, c: (b, 0, c))
    cn_spec = pl.BlockSpec((nb,) + cn_dims, lambda b, c: (b, 0, 0, 0))
    m_spec = pl.BlockSpec((nb, SUBLANES, LANES), lambda b, c: (b, 0, 0))
    zero_init = cn0 is None
    in_specs = [tile, tile, tile, gspec, tile, tile, _const_spec((1, D_MLSTM))]
    args = [q, k, v, gates_t, sig_o, gate_b, norm_g]
    if not zero_init:
        in_specs += [cn_spec, m_spec]
        args += [cn0, m0]
    return pl.pallas_call(
        functools.partial(_mlstm_kernel, zero_init=zero_init, scans_given=grows == 2 * SUBLANES),
        grid=(B // nb, nc),
        in_specs=in_specs,
        out_specs=(tile, cn_spec, m_spec),
        out_shape=(jax.ShapeDtypeStruct((B, S, D_MLSTM), BF16),
                   jax.ShapeDtypeStruct((B,) + cn_dims, F32),
                   jax.ShapeDtypeStruct((B, SUBLANES, LANES), F32)),
        scratch_shapes=[pltpu.VMEM((nb,) + cn_dims, F32),
                        pltpu.VMEM((nb, SUBLANES, LANES), F32)],
        compiler_params=_params(("arbitrary", "arbitrary")),
        name="mlstm_prompt" if zero_init else "mlstm_sample",
    )(*args)


def _outproj(x2, ya, yb, w_out, final_g):
    rows = x2.shape[0]
    tm = min(ROW_TILE, rows)
    tile = lambda width: pl.BlockSpec((tm, width), lambda i: (i, 0))
    return pl.pallas_call(
        _outproj_kernel,
        grid=(rows // tm,),
        in_specs=[tile(D_MODEL), tile(D_ATT), tile(D_MLSTM),
                  _const_spec((D_ATT + D_MLSTM, D_MODEL)), _const_spec((1, D_MODEL))],
        out_specs=tile(D_MODEL),
        out_shape=jax.ShapeDtypeStruct((rows, D_MODEL), F32),
        compiler_params=_params(("arbitrary",)),
        name="outproj",
    )(x2, ya, yb, w_out, final_g)


def kernel(x_prompt, x_sample, cache_win_k, cache_win_v, state_conv, state_C, state_n, state_m,
           norm_g, w_in, conv_w, conv_b, b_i, b_f, mlstm_norm_g, w_out, final_norm_g):
    assert w_in.shape[0] == 1, "single-layer model"
    B, S, D = x_prompt.shape
    DB, T, _ = x_sample.shape
    HB, DK = N_HEADS_MLSTM, HEAD_DIM_MLSTM
    wb = cache_win_k.shape[2]
    assert S % ROW_TILE == 0 and S == 16 * N_BACK and wb >= 16 * N_BACK and CONV_WIDTH - 1 <= T <= SUBLANES and DB * T == LANES

    w_pad = jnp.pad(w_in[0], ((0, 0), (0, N_IN_PAD - N_IN))).astype(BF16)
    w_o = w_out[0].astype(BF16)
    g_in = norm_g[0][None, :]
    cw, cb = conv_w[0], conv_b[0][None, :]
    bif = jnp.pad(jnp.concatenate([b_i[0], b_f[0]]), (0, LANES - 2 * HB))[None, :]
    ng = mlstm_norm_g[0][None, :]
    g_fin = final_norm_g[None, :]

    (q_p, k_p, v_p, pk, pv, ga_p, qm_p, km_p, vb_p, so_p, gb_p, gt_p, p_conv) = _inproj_prompt(
        x_prompt, g_in, w_pad, cw, cb, bif)
    ya_p = _attn_prompt(q_p, k_p, v_p, ga_p)
    yb_p, cn_p, m_p = _mlstm(qm_p, km_p, vb_p, gt_p, so_p, gb_p, ng)
    y_prompt = _outproj(x_prompt.reshape(B * S, D), ya_p.reshape(B * S, D_ATT),
                        yb_p.reshape(B * S, D_MLSTM), w_o, g_fin).reshape(B, S, D)

    sc = state_conv[0]
    zrow = jnp.zeros((DB, 1, 2 * D_MLSTM), F32)
    hist = []
    for sh in (1, 2, 3):
        rows_ = [sc[:, CONV_WIDTH - 1 + t - sh:CONV_WIDTH + t - sh] if t < sh else zrow
                 for t in range(T)]
        hist.append(jnp.concatenate(rows_, axis=1).reshape(DB * T, 2 * D_MLSTM))
    (q_s, kn, vn, knt, vnt, ga_s, qm_s, km_s, vb_s, so_s, gb_s, gates_s, qk_s) = _inproj_sample(
        x_sample.reshape(DB * T, D), g_in, w_pad, cw, cb, bif, hist, T)
    r3 = lambda a: a.reshape(DB, T, a.shape[-1])
    pad8 = lambda a: jnp.pad(r3(a), ((0, 0), (0, SUBLANES - T), (0, 0)))
    to_hdp = lambda c: jnp.transpose(c[0], (0, 2, 3, 1))
    from_hdp = lambda c: jnp.transpose(c, (0, 3, 1, 2))[None]
    ya_s, s_k, s_v = _attn_sample(pad8(q_s), pad8(kn), pad8(vn), knt, vnt, pad8(ga_s),
                                  to_hdp(cache_win_k), to_hdp(cache_win_v), T)
    ya_s = ya_s[:, :T]

    pad_t = lambda a: jnp.pad(r3(a), ((0, 0), (0, CHUNK - T), (0, 0)))
    g3 = r3(gates_s)[:, :, :SUBLANES]
    null_gate = jnp.concatenate([jnp.full((HB,), NEG_INF, F32), jnp.zeros((HB,), F32)])
    gt_s = jnp.concatenate([g3, jnp.broadcast_to(null_gate, (DB, CHUNK - T, SUBLANES))], axis=1)
    gt_s = gt_s.transpose(0, 2, 1)
    cn0 = jnp.concatenate([state_C[0], jnp.broadcast_to(state_n[0][..., None], (DB, HB, DK, DK))],
                          axis=-1)
    m0 = jnp.broadcast_to(jnp.pad(state_m[0], ((0, 0), (0, SUBLANES - HB)))[..., None],
                          (DB, SUBLANES, LANES))
    yb_s, cn_s, m_s = _mlstm(pad_t(qm_s), pad_t(km_s), pad_t(vb_s), gt_s, pad_t(so_s), pad_t(gb_s),
                             ng, cn0, m0)
    y_sample = _outproj(x_sample.reshape(DB * T, D), ya_s.reshape(DB * T, D_ATT),
                        yb_s[:, :T].reshape(DB * T, D_MLSTM), w_o, g_fin).reshape(DB, T, D)

    return (y_prompt, y_sample,
            from_hdp(pk), from_hdp(pv), p_conv[None],
            cn_p[None, ..., :DK], cn_p[None, ..., DK], m_p[None, :, :HB, 0],
            from_hdp(s_k), from_hdp(s_v), r3(qk_s)[None, :, T - (CONV_WIDTH - 1):],
            cn_s[None, ..., :DK], cn_s[None, ..., DK], m_s[None, :, :HB, 0])
```

```python
import functools
import math

import jax
import jax.numpy as jnp
import numpy as np
from jax import lax
from jax.experimental import pallas as pl
from jax.experimental.pallas import tpu as pltpu

F32 = jnp.float32
BF16 = jnp.bfloat16

D_MODEL = 1024
D_ATT = 512
N_HEADS_ATT = 8
HEAD_DIM_ATT = 64
D_MLSTM = 512
N_HEADS_MLSTM = 4
HEAD_DIM_MLSTM = 128
ROT_DIM = 16
ROPE_THETA = 500000.0
PAST_LEN = 16384
DILATIONS = (1, 4, 16)
N_BACK = 128
CONV_WIDTH = 4
CHUNK = 128
EPS = 1e-6
NEG_INF = -1e30
LOG2_E = math.log2(math.e)

LANES = 128
SUBLANES = 8
N_IN = 4 * D_ATT + 5 * D_MLSTM + 2 * N_HEADS_MLSTM
N_IN_PAD = 4 * D_ATT + 5 * D_MLSTM + LANES
OFF_QA, OFF_KA, OFF_VA, OFF_ZA = 0, 512, 1024, 1536
OFF_QB, OFF_VB, OFF_OB, OFF_ZB, OFF_G = 2048, 3072, 3584, 4096, 4608
ROW_TILE = 512
INPROJ_SUBTILE = 128
OUTPROJ_TILE = 2048
OUTPROJ_SUBTILE = 256
MLSTM_BATCH = 4
VMEM_LIMIT = 56 * 1024 * 1024


def _silu(x):
    return x * jax.nn.sigmoid(x)


def _rmsnorm(x, g):
    return x * lax.rsqrt(jnp.mean(x * x, axis=-1, keepdims=True) + EPS) * g


def _rope(u, cos, sin_lo, sin_hi):
    outs = []
    for c in range(u.shape[1] // LANES):
        xs = u[:, c * LANES:(c + 1) * LANES]
        outs.append(xs * cos + pltpu.roll(xs, LANES - ROT_DIM // 2, 1) * sin_lo
                    + pltpu.roll(xs, ROT_DIM // 2, 1) * sin_hi)
    return jnp.concatenate(outs, axis=1)


def _gate_block(ug, bif):
    gz = ug + bif
    lane = lax.broadcasted_iota(jnp.int32, gz.shape, 1)
    logf = jnp.minimum(gz, 0.0) - jnp.log1p(jnp.exp(-jnp.abs(gz)))
    return jnp.where(lane < N_HEADS_MLSTM, gz, logf)


def _segments(hn, w_ref):
    def seg(off, width):
        return jnp.dot(hn, w_ref[:, off:off + width], preferred_element_type=F32)
    return seg


def _inproj_common(seg, cos, sin_lo, sin_hi, q_ref, ga_ref, vb_ref, so_ref, gb_ref):
    q = _rope(seg(OFF_QA, D_ATT), cos, sin_lo, sin_hi) * (HEAD_DIM_ATT ** -0.5 * LOG2_E)
    q_ref[...] = q.astype(BF16).reshape(q_ref.shape)
    k = _rope(seg(OFF_KA, D_ATT), cos, sin_lo, sin_hi)
    v = seg(OFF_VA, D_ATT)
    ga_ref[...] = _silu(seg(OFF_ZA, D_ATT)).astype(BF16).reshape(ga_ref.shape)
    vb_ref[...] = seg(OFF_VB, D_MLSTM).astype(BF16).reshape(vb_ref.shape)
    so_ref[...] = jax.nn.sigmoid(seg(OFF_OB, D_MLSTM)).astype(BF16).reshape(so_ref.shape)
    gb_ref[...] = _silu(seg(OFF_ZB, D_MLSTM)).astype(BF16).reshape(gb_ref.shape)
    return k, v


def _inproj_prompt_kernel(x_ref, g_ref, w_ref, cw_ref, cb_ref, bif_ref, cos_ref, slo_ref, shi_ref,
                          q_ref, kb_ref, vbf_ref, pk_ref, pv_ref, ga_ref, qm_ref, km_ref, vb_ref,
                          so_ref, gb_ref, gt_ref, pconv_ref, xp_ref):
    j = pl.program_id(1)
    tm = x_ref.shape[1]
    sub = INPROJ_SUBTILE
    assert CHUNK == LANES and tm % sub == 0 and sub % CHUNK == 0

    @pl.when(j == 0)
    def _():
        xp_ref[0:SUBLANES, :] = jnp.zeros((SUBLANES, 2 * D_MLSTM), F32)

    @pl.when(j > 0)
    def _():
        xp_ref[0:SUBLANES, :] = xp_ref[tm:tm + SUBLANES, :]

    for r0 in range(0, tm, sub):
        rs = pl.ds(r0, sub)
        part = lambda ref: ref.at[0, rs, :]
        hn = _rmsnorm(x_ref[0, rs, :], g_ref[...]).astype(BF16)
        seg = _segments(hn, w_ref)

        gates = _gate_block(seg(OFF_G, LANES), bif_ref[...])
        for i in range(sub // CHUNK):
            cs = slice(r0 + i * CHUNK, r0 + (i + 1) * CHUNK)
            gt = gates[i * CHUNK:(i + 1) * CHUNK, :].T[0:SUBLANES, :]
            gt_ref[0, 0:SUBLANES, cs] = gt
            gt_ref[0, SUBLANES:2 * SUBLANES, cs] = _gate_scans(gt)

        k, v = _inproj_common(seg, cos_ref[rs, :], slo_ref[rs, :], shi_ref[rs, :],
                              part(q_ref), part(ga_ref), part(vb_ref), part(so_ref), part(gb_ref))
        kb_ref[0, rs, :] = k.astype(BF16)
        vbf_ref[0, rs, :] = v.astype(BF16)
        pk_ref[0, :, :, r0:r0 + sub] = k.T.reshape(N_HEADS_ATT, HEAD_DIM_ATT, sub)
        pv_ref[0, :, :, r0:r0 + sub] = v.T.reshape(N_HEADS_ATT, HEAD_DIM_ATT, sub)

        base = SUBLANES + r0
        xp_ref[base:base + sub, :] = seg(OFF_QB, 2 * D_MLSTM)
        y = cb_ref[...] + xp_ref[base:base + sub, :] * cw_ref[3:4, :]
        for jj in range(CONV_WIDTH - 1):
            sh = CONV_WIDTH - 1 - jj
            y = y + xp_ref[base - sh:base - sh + sub, :] * cw_ref[jj:jj + 1, :]
        y = _silu(y)
        qm_ref[0, rs, :] = y[:, :D_MLSTM].astype(BF16)
        km_ref[0, rs, :] = (y[:, D_MLSTM:] * (HEAD_DIM_MLSTM ** -0.5)).astype(BF16)

    @pl.when(j == pl.num_programs(1) - 1)
    def _():
        pconv_ref[0] = xp_ref[tm + SUBLANES - (CONV_WIDTH - 1):tm + SUBLANES, :]


def _inproj_sample_kernel(x_ref, g_ref, w_ref, cw_ref, cb_ref, bif_ref, cos_ref, slo_ref, shi_ref,
                          h1_ref, h2_ref, h3_ref,
                          q_ref, kn_ref, vn_ref, knt_ref, vnt_ref, ga_ref, qm_ref, km_ref, vb_ref,
                          so_ref, gb_ref, gates_ref, qk_ref, *, t_new):
    hn = _rmsnorm(x_ref[...], g_ref[...]).astype(BF16)
    seg = _segments(hn, w_ref)
    k, v = _inproj_common(seg, cos_ref[...], slo_ref[...], shi_ref[...],
                          q_ref, ga_ref, vb_ref, so_ref, gb_ref)
    kn_ref[...] = k
    vn_ref[...] = v
    knt_ref[...] = k.T
    vnt_ref[...] = v.T
    u = seg(OFF_QB, 2 * D_MLSTM)
    qk_ref[...] = u
    t = lax.rem(lax.broadcasted_iota(jnp.int32, u.shape, 0), t_new)
    y = cb_ref[...] + u * cw_ref[3:4, :]
    for sh, h_ref in ((1, h1_ref), (2, h2_ref), (3, h3_ref)):
        prev = jnp.where(t >= sh, pltpu.roll(u, sh, 0), h_ref[...])
        y = y + prev * cw_ref[3 - sh:4 - sh, :]
    y = _silu(y)
    qm_ref[...] = y[:, :D_MLSTM].astype(BF16)
    km_ref[...] = (y[:, D_MLSTM:] * (HEAD_DIM_MLSTM ** -0.5)).astype(BF16)
    gates_ref[...] = _gate_block(seg(OFF_G, LANES), bif_ref[...])


def _attn_prompt_kernel(q_ref, k_ref, v_ref, g_ref, o_ref,
                        src1_ref, src4_ref, src16_ref, bias_ref, st16_ref, st4_ref):
    seq = q_ref.shape[1]
    n16 = seq // 16
    group = 8
    head_a_full = lax.broadcasted_iota(jnp.int32, (seq, LANES), 1) < HEAD_DIM_ATT
    by16 = lambda x: jnp.swapaxes(x.reshape(n16, 16, LANES), 0, 1)
    def stack(q, k, v, is_a):
        zero, one = jnp.zeros((), q.dtype), jnp.ones((), q.dtype)
        return (jnp.where(is_a, q, zero), jnp.where(is_a, zero, q), k,
                jnp.where(is_a, v, one), jnp.where(is_a, one, v))

    q, k, v = q_ref[0], k_ref[0], v_ref[0]
    for i, x in enumerate(stack(q, k, v, head_a_full)):
        src1_ref[i] = x
    q, k, v = q.astype(F32), k.astype(F32), v.astype(F32)
    for i, x in enumerate(stack(q, k, v, head_a_full)):
        src4_ref[i] = x
    head_a_16 = lax.broadcasted_iota(jnp.int32, (16, n16, LANES), 2) < HEAD_DIM_ATT
    for i, x in enumerate(stack(by16(q).astype(BF16), by16(k).astype(BF16), by16(v).astype(BF16),
                                head_a_16)):
        src16_ref[i] = x
    u = lax.broadcasted_iota(jnp.int32, (N_BACK, 2 * N_BACK), 0)
    w = lax.broadcasted_iota(jnp.int32, (N_BACK, 2 * N_BACK), 1)
    bias_ref[...] = jnp.where((w >= u) & (w <= u + N_BACK), jnp.finfo(F32).max, NEG_INF)
    head_a = lax.broadcasted_iota(jnp.int32, (N_BACK, LANES), 1) < HEAD_DIM_ATT

    def partials(gets):
        staged = []
        for get, has_prev in gets:
            if has_prev:
                both = lambda i, get=get: jnp.concatenate([get(i, True), get(i, False)], axis=0)
                bias = bias_ref[...]
            else:
                both = lambda i, get=get: get(i, False)
                bias = bias_ref[:, N_BACK:]
            kk = both(2)
            heads = []
            for qi in (0, 1):
                s = lax.dot_general(get(qi, False), kk, (((1,), (1,)), ((), ())),
                                    preferred_element_type=F32)
                s = jnp.minimum(s, bias)
                mh = jnp.max(s, axis=-1, keepdims=True)
                heads.append((mh, jnp.exp2(s - mh).astype(BF16)))
            staged.append((both, heads))
        out = []
        for both, ((m_a, p_a), (m_b, p_b)) in staged:
            pv_a = jnp.dot(p_a, both(3), preferred_element_type=F32)
            pv_b = jnp.dot(p_b, both(4), preferred_element_type=F32)
            acc = jnp.where(head_a, pv_a, pv_b)
            den = pltpu.roll(jnp.where(head_a, pv_b, pv_a), HEAD_DIM_ATT, 1)
            out.append((jnp.where(head_a, m_a, m_b), den, acc))
        return out

    def get4(qstart, kprev_start):
        def get(i, prev):
            start = kprev_start if prev else qstart
            return src4_ref[i, pl.ds(start, N_BACK, stride=4), :].astype(BF16)
        return get, kprev_start is not None

    def get1(qstart, kprev_start):
        def get(i, prev):
            return src1_ref[i, pl.ds(kprev_start if prev else qstart, N_BACK), :]
        return get, kprev_start is not None

    def keep4(blocks):
        res = partials([get4(qs, ks) for qs, ks in blocks])
        for (qs, _), (m, den, acc) in zip(blocks, res):
            rows = pl.ds(qs, N_BACK, stride=4)
            st4_ref[0, rows, :] = m
            st4_ref[1, rows, :] = den
            st4_ref[2, rows, :] = acc

    def finish(blocks):
        res = partials([get1(qs, ks) for qs, ks in blocks])
        for (qs, _), part in zip(blocks, res):
            rows = pl.ds(qs, N_BACK)
            parts = [part] + [tuple(st[i, rows, :] for i in range(3)) for st in (st4_ref, st16_ref)]
            m_all = functools.reduce(jnp.maximum, [pt[0] for pt in parts])
            wts = [jnp.exp2(pt[0] - m_all) for pt in parts]
            den = sum(wt * pt[1] for wt, pt in zip(wts, parts))
            num = sum(wt * pt[2] for wt, pt in zip(wts, parts))
            o_ref[0, rows, :] = (num / den * g_ref[0, rows, :].astype(F32)).astype(BF16)

    def body16(g, c):
        rs = [g * group + rr for rr in range(group)]
        res = partials([(lambda i, prev, r=r: src16_ref[i, r], False) for r in rs])
        for r, part in zip(rs, res):
            for i in range(3):
                st16_ref[i, pl.ds(r, N_BACK, stride=16), :] = part[i]
        return c
    lax.fori_loop(0, 16 // group, body16, 0)

    blocks4 = lambda cc: [(r + 4 * N_BACK * cc, r + 4 * N_BACK * (cc - 1)) for r in range(4)]
    keep4([(r, None) for r in range(4)] + blocks4(1))

    def body4(g, c):
        keep4(blocks4(2 * g) + blocks4(2 * g + 1))
        return c
    lax.fori_loop(1, seq // 4 // N_BACK // 2, body4, 0)

    finish([(0, None)] + [(cc * N_BACK, (cc - 1) * N_BACK) for cc in range(1, group)])

    def body1(g, c):
        starts = [pl.multiple_of((g * group + rr) * N_BACK, N_BACK) for rr in range(group)]
        finish([(st, st - N_BACK) for st in starts])
        return c
    lax.fori_loop(1, seq // N_BACK // group, body1, 0)


def _attn_sample_kernel(q_ref, kn_ref, vn_ref, knt_ref, vnt_ref, g_ref, ck_ref, cv_ref,
                        o_ref, sk_ref, sv_ref, clamp_ref, *, t_new):
    b = pl.program_id(0)
    wb = ck_ref.shape[3]
    hd = HEAD_DIM_ATT
    rows = q_ref.shape[1]

    @pl.when(b == 0)
    def _():
        delta = (wb + lax.broadcasted_iota(jnp.int32, (rows, wb), 0)
                 - lax.broadcasted_iota(jnp.int32, (rows, wb), 1))
        for d, dil in enumerate(DILATIONS):
            ok = ((delta & (dil - 1)) == 0) & (delta >= dil) & (delta <= N_BACK * dil)
            clamp_ref[d] = jnp.where(ok, jnp.finfo(F32).max, NEG_INF)

    tq = lax.broadcasted_iota(jnp.int32, (rows, rows), 0)
    tk = lax.broadcasted_iota(jnp.int32, (rows, rows), 1)
    new_ok = [((tk <= tq) if dil == 1 else (tk == tq)) & (tk < t_new) for dil in DILATIONS]

    lane = lax.broadcasted_iota(jnp.int32, (hd, LANES), 1)
    shift_new = (LANES - t_new) - b * t_new
    nt = (((1,), (1,)), ((), ()))
    outs = []
    for h in range(N_HEADS_ATT):
        hs = slice(h * hd, (h + 1) * hd)
        kt = ck_ref[0, h]
        vt = cv_ref[0, h]
        for old, new_ref, out_ref in ((kt, knt_ref, sk_ref), (vt, vnt_ref, sv_ref)):
            moved = pltpu.roll(old, wb - t_new, axis=1)
            new_cols = pltpu.roll(new_ref[hs, :], shift_new, axis=1)
            out_ref[0, h, :, 0:wb - LANES] = moved[:, 0:wb - LANES]
            out_ref[0, h, :, wb - LANES:wb] = jnp.where(lane < LANES - t_new,
                                                        moved[:, wb - LANES:wb], new_cols)
        qh = q_ref[0, :, hs]
        kn_h = kn_ref[0, :, hs].astype(BF16)
        vn_h = vn_ref[0, :, hs].astype(BF16)
        s_old = jnp.dot(qh, kt.astype(BF16), preferred_element_type=F32)
        s_new = lax.dot_general(qh, kn_h, nt, preferred_element_type=F32)
        ps, pes, ms = [], [], []
        for d in range(len(DILATIONS)):
            so = jnp.minimum(s_old, clamp_ref[d])
            sn = jnp.where(new_ok[d], s_new, NEG_INF)
            m = jnp.maximum(jnp.max(so, axis=-1, keepdims=True), jnp.max(sn, axis=-1, keepdims=True))
            ps.append(jnp.exp2(so - m))
            pes.append(jnp.exp2(sn - m))
            ms.append(m)
        acc = lax.dot_general(jnp.concatenate(ps, axis=0).astype(BF16), vt.astype(BF16), nt,
                              preferred_element_type=F32)
        acc = acc + jnp.dot(jnp.concatenate(pes, axis=0).astype(BF16), vn_h,
                            preferred_element_type=F32)
        m_all = functools.reduce(jnp.maximum, ms)
        den = 0.0
        num = 0.0
        for d in range(len(DILATIONS)):
            wgt = jnp.exp2(ms[d] - m_all)
            den = den + wgt * (jnp.sum(ps[d], axis=-1, keepdims=True)
                               + jnp.sum(pes[d], axis=-1, keepdims=True))
            num = num + wgt * acc[d * rows:(d + 1) * rows]
        outs.append(num / den)
    att = jnp.concatenate(outs, axis=1)
    o_ref[0] = (att * g_ref[0].astype(F32)).astype(BF16)


def _scan_lanes(x, op, fill):
    lane = lax.broadcasted_iota(jnp.int32, x.shape, 1)
    d = 1
    while d < x.shape[1]:
        x = op(x, jnp.where(lane >= d, pltpu.roll(x, d, 1), fill))
        d *= 2
    return x


def _gate_scans(gt):
    b = _scan_lanes(pltpu.roll(gt, N_HEADS_MLSTM, 0), jnp.add, 0.0)
    cm = _scan_lanes(gt - b, jnp.maximum, NEG_INF)
    row = lax.broadcasted_iota(jnp.int32, gt.shape, 0)
    return jnp.where(row < N_HEADS_MLSTM, b, pltpu.roll(cm, N_HEADS_MLSTM, 0))


def _mlstm_kernel(*refs, zero_init, scans_given):
    if zero_init:
        (q_ref, k_ref, v_ref, gt_ref, so_ref, gb_ref, ng_ref,
         y_ref, cn_out_ref, m_out_ref, cn_ref, m_ref) = refs
    else:
        (q_ref, k_ref, v_ref, gt_ref, so_ref, gb_ref, ng_ref, cn0_ref, m0_ref,
         y_ref, cn_out_ref, m_out_ref, cn_ref, m_ref) = refs
    c_idx = pl.program_id(1)
    nb = q_ref.shape[0]
    L = CHUNK
    dh = HEAD_DIM_MLSTM
    nh = N_HEADS_MLSTM

    @pl.when(c_idx == 0)
    def _():
        if zero_init:
            cn_ref[...] = jnp.zeros(cn_ref.shape, F32)
            m_ref[...] = jnp.zeros(m_ref.shape, F32)
        else:
            cn_ref[...] = cn0_ref[...]
            m_ref[...] = m0_ref[...]

    tri_t = lax.broadcasted_iota(jnp.int32, (L, L), 0)
    tri_s = lax.broadcasted_iota(jnp.int32, (L, L), 1)
    causal = tri_t >= tri_s
    ones_blk = jnp.ones((L, dh), BF16)
    nt = (((1,), (1,)), ((), ()))
    pairs = [(bb, h) for bb in range(nb) for h in range(nh)]
    sl = lambda h: slice(h * dh, (h + 1) * dh)

    qk = {(bb, h): lax.dot_general(q_ref[bb, :, sl(h)], k_ref[bb, :, sl(h)], nt,
                                   preferred_element_type=F32) for bb, h in pairs}
    cn_old = {p: cn_ref[p[0], p[1]] for p in pairs}
    qc = {(bb, h): jnp.dot(q_ref[bb, :, sl(h)], cn_old[bb, h].astype(BF16),
                           preferred_element_type=F32) for bb, h in pairs}

    rows, cols, decays, w_rows = [], [], [], []
    for bb in range(nb):
        if scans_given:
            i_row = gt_ref[bb, 0:SUBLANES, :]
            sc = gt_ref[bb, SUBLANES:2 * SUBLANES, :]
        else:
            i_row = gt_ref[bb]
            sc = _gate_scans(i_row)
        b = sc
        cm = pltpu.roll(sc, nh, 0)
        m_prev = m_ref[bb]
        m_t = jnp.maximum(m_prev + b, b + cm)
        inter = jnp.exp(m_prev + b - m_t)
        m_last = jnp.broadcast_to(m_t[:, L - 1:L], m_t.shape)
        b_last = jnp.broadcast_to(b[:, L - 1:L], b.shape)
        decay = jnp.exp(m_prev + b_last - m_last)
        w_row = jnp.exp(b_last - b + i_row - m_last)
        m_ref[bb] = m_last
        stack = jnp.concatenate([b - m_t, inter, jnp.exp(-m_t),
                                 jnp.zeros((L - 3 * SUBLANES, L), F32)], axis=0)
        cols.append(stack.T)
        rows.append(i_row - b)
        w_rows.append(w_row)
        decays.append(jnp.concatenate([decay, decay], axis=1))
    col = lambda bb, kind, h: cols[bb][:, kind * SUBLANES + h:kind * SUBLANES + h + 1]

    sqk = {}
    for bb, h in pairs:
        dlog = col(bb, 0, h) + rows[bb][h:h + 1, :]
        sqk[bb, h] = (qk[bb, h] * jnp.exp(jnp.where(causal, dlog, NEG_INF))).astype(BF16)
    v_one = {(bb, h): jnp.concatenate([v_ref[bb, :, sl(h)], ones_blk], axis=1) for bb, h in pairs}
    pv = {p: jnp.dot(sqk[p], v_one[p], preferred_element_type=F32) for p in pairs}
    for bb, h in pairs:
        tot = col(bb, 1, h) * qc[bb, h] + pv[bb, h]
        hh = tot[:, :dh] / jnp.maximum(jnp.abs(tot[:, dh:]), col(bb, 2, h))
        hh = so_ref[bb, :, sl(h)].astype(F32) * hh
        hh = hh * lax.rsqrt(jnp.mean(hh * hh, axis=-1, keepdims=True) + EPS)
        hh = hh * ng_ref[:, sl(h)]
        y_ref[bb, :, sl(h)] = (hh * gb_ref[bb, :, sl(h)].astype(F32)).astype(BF16)
    for bb, h in pairs:
        kt_w = k_ref[bb, :, sl(h)].T.astype(F32) * w_rows[bb][h:h + 1, :]
        upd = jnp.dot(kt_w.astype(BF16), v_one[bb, h], preferred_element_type=F32)
        cn_ref[bb, h] = decays[bb][h:h + 1, :] * cn_old[bb, h] + upd

    @pl.when(c_idx == pl.num_programs(1) - 1)
    def _():
        cn_out_ref[...] = cn_ref[...]
        m_out_ref[...] = m_ref[...]


def _outproj_kernel(x_ref, ya_ref, yb_ref, w_ref, g_ref, o_ref):
    rows = x_ref.shape[0]
    sub = min(OUTPROJ_SUBTILE, rows)
    for r0 in range(0, rows, sub):
        rs = pl.ds(r0, sub)
        mix = jnp.dot(ya_ref[rs, :], w_ref[0:D_ATT, :], preferred_element_type=F32)
        mix = mix + jnp.dot(yb_ref[rs, :], w_ref[D_ATT:, :], preferred_element_type=F32)
        o_ref[rs, :] = _rmsnorm(x_ref[rs, :] + mix, g_ref[...])


def _rope_tables(pos):
    half = ROT_DIM // 2
    inv = ROPE_THETA ** (-jnp.arange(half, dtype=F32) * 2.0 / ROT_DIM)
    ang = pos.astype(F32)[:, None] * inv[None, :]
    cos, sin = jnp.cos(ang), jnp.sin(ang)
    n = pos.shape[0]
    one = jnp.ones((n, HEAD_DIM_ATT - ROT_DIM), F32)
    zero = jnp.zeros((n, HEAD_DIM_ATT - ROT_DIM), F32)
    zh = jnp.zeros((n, half), F32)
    cos_t = jnp.concatenate([cos, cos, one], axis=1)
    lo_t = jnp.concatenate([-sin, zh, zero], axis=1)
    hi_t = jnp.concatenate([zh, sin, zero], axis=1)
    rep = lambda t: jnp.concatenate([t, t], axis=1)
    return rep(cos_t), rep(lo_t), rep(hi_t)


def _params(sem):
    return pltpu.CompilerParams(dimension_semantics=sem, vmem_limit_bytes=VMEM_LIMIT)


def _const_spec(shape):
    return pl.BlockSpec(shape, lambda *_: (0,) * len(shape))


def _inproj_prompt(x, norm_g, w_pad, conv_w, conv_b, bif):
    B, S, _ = x.shape
    tm = ROW_TILE
    cos, lo, hi = _rope_tables(jnp.arange(S, dtype=jnp.int32))
    tile = lambda width: pl.BlockSpec((1, tm, width), lambda b, j: (b, j, 0))
    tab = pl.BlockSpec((tm, LANES), lambda b, j: (j, 0))
    bf = lambda: jax.ShapeDtypeStruct((B, S, D_ATT), BF16)
    f5 = lambda: jax.ShapeDtypeStruct((B, N_HEADS_ATT, HEAD_DIM_ATT, S), F32)
    tile5 = pl.BlockSpec((1, N_HEADS_ATT, HEAD_DIM_ATT, tm), lambda b, j: (b, 0, 0, j))
    out_shape = (bf(), bf(), bf(), f5(), f5(), bf(), bf(), bf(), bf(), bf(), bf(),
                 jax.ShapeDtypeStruct((B, 2 * SUBLANES, S), F32),
                 jax.ShapeDtypeStruct((B, CONV_WIDTH - 1, 2 * D_MLSTM), F32))
    out_specs = tuple([tile(D_ATT)] * 3 + [tile5] * 2 + [tile(D_ATT)] * 6) + (
        pl.BlockSpec((1, 2 * SUBLANES, tm), lambda b, j: (b, 0, j)),
        pl.BlockSpec((1, CONV_WIDTH - 1, 2 * D_MLSTM), lambda b, j: (b, 0, 0)))
    return pl.pallas_call(
        _inproj_prompt_kernel,
        grid=(B, S // tm),
        in_specs=[tile(D_MODEL), _const_spec((1, D_MODEL)), _const_spec((D_MODEL, N_IN_PAD)),
                  _const_spec((CONV_WIDTH, 2 * D_MLSTM)), _const_spec((1, 2 * D_MLSTM)),
                  _const_spec((1, LANES)), tab, tab, tab],
        out_specs=out_specs,
        out_shape=out_shape,
        scratch_shapes=[pltpu.VMEM((tm + 2 * SUBLANES, 2 * D_MLSTM), F32)],
        compiler_params=_params(("arbitrary", "arbitrary")),
        name="inproj_prompt",
    )(x, norm_g, w_pad, conv_w, conv_b, bif, cos, lo, hi)


def _inproj_sample(x2, norm_g, w_pad, conv_w, conv_b, bif, hist, t_new):
    rows = x2.shape[0]
    pos = PAST_LEN + jnp.arange(t_new, dtype=jnp.int32)
    cos, lo, hi = (jnp.tile(t, (rows // t_new, 1)) for t in _rope_tables(pos))
    bf = lambda: jax.ShapeDtypeStruct((rows, D_ATT), BF16)
    f3 = lambda: jax.ShapeDtypeStruct((rows, D_ATT), F32)
    f5 = lambda: jax.ShapeDtypeStruct((D_ATT, rows), F32)
    out_shape = (bf(), f3(), f3(), f5(), f5(), bf(), bf(), bf(), bf(), bf(), bf(),
                 jax.ShapeDtypeStruct((rows, LANES), F32),
                 jax.ShapeDtypeStruct((rows, 2 * D_MLSTM), F32))
    return pl.pallas_call(
        functools.partial(_inproj_sample_kernel, t_new=t_new),
        out_shape=out_shape,
        compiler_params=pltpu.CompilerParams(vmem_limit_bytes=VMEM_LIMIT),
        name="inproj_sample",
    )(x2, norm_g, w_pad, conv_w, conv_b, bif, cos, lo, hi, *hist)


def _attn_prompt(q, k, v, gate):
    B, S, _ = q.shape
    spec = pl.BlockSpec((1, S, LANES), lambda b, h: (b, 0, h))
    return pl.pallas_call(
        _attn_prompt_kernel,
        grid=(B, D_ATT // LANES),
        in_specs=[spec, spec, spec, spec],
        out_specs=spec,
        out_shape=jax.ShapeDtypeStruct((B, S, D_ATT), BF16),
        scratch_shapes=[
            pltpu.VMEM((5, S, LANES), BF16), pltpu.VMEM((5, S, LANES), F32),
            pltpu.VMEM((5, 16, S // 16, LANES), BF16),
            pltpu.VMEM((N_BACK, 2 * N_BACK), F32),
            pltpu.VMEM((3, S, LANES), F32), pltpu.VMEM((3, S, LANES), F32)],
        compiler_params=_params(("arbitrary", "arbitrary")),
        name="attn_prompt",
    )(q, k, v, gate)


def _attn_sample(q, kn, vn, knt, vnt, gate, ck, cv, t_new):
    B, rows, _ = q.shape
    wb = ck.shape[3]
    small = pl.BlockSpec((1, rows, D_ATT), lambda b: (b, 0, 0))
    big = pl.BlockSpec((1, N_HEADS_ATT, HEAD_DIM_ATT, wb), lambda b: (b, 0, 0, 0))
    win_shape = jax.ShapeDtypeStruct((B, N_HEADS_ATT, HEAD_DIM_ATT, wb), F32)
    return pl.pallas_call(
        functools.partial(_attn_sample_kernel, t_new=t_new),
        grid=(B,),
        in_specs=[small, small, small, _const_spec(knt.shape), _const_spec(vnt.shape), small, big, big],
        out_specs=(small, big, big),
        out_shape=(jax.ShapeDtypeStruct((B, rows, D_ATT), BF16), win_shape, win_shape),
        scratch_shapes=[pltpu.VMEM((len(DILATIONS), rows, wb), F32)],
        compiler_params=_params(("arbitrary",)),
        name="attn_sample",
    )(q, kn, vn, knt, vnt, gate, ck, cv)


def _mlstm(q, k, v, gates_t, sig_o, gate_b, norm_g, cn0=None, m0=None):
    B, S, _ = q.shape
    nc = S // CHUNK
    nb = MLSTM_BATCH
    grows = gates_t.shape[1]
    cn_dims = (N_HEADS_MLSTM, HEAD_DIM_MLSTM, 2 * HEAD_DIM_MLSTM)
    tile = pl.BlockSpec((nb, CHUNK, D_MLSTM), lambda b, c: (b, c, 0))
    gspec = pl.BlockSpec((nb, grows, CHUNK), lambda b, c: (b, 0, c))
    cn_spec = pl.BlockSpec((nb,) + cn_dims, lambda b, c: (b, 0, 0, 0))
    m_spec = pl.BlockSpec((nb, SUBLANES, LANES), lambda b, c: (b, 0, 0))
    zero_init = cn0 is None
    in_specs = [tile, tile, tile, gspec, tile, tile, _const_spec((1, D_MLSTM))]
    args = [q, k, v, gates_t, sig_o, gate_b, norm_g]
    if not zero_init:
        in_specs += [cn_spec, m_spec]
        args += [cn0, m0]
    return pl.pallas_call(
        functools.partial(_mlstm_kernel, zero_init=zero_init, scans_given=grows == 2 * SUBLANES),
        grid=(B // nb, nc),
        in_specs=in_specs,
        out_specs=(tile, cn_spec, m_spec),
        out_shape=(jax.ShapeDtypeStruct((B, S, D_MLSTM), BF16),
                   jax.ShapeDtypeStruct((B,) + cn_dims, F32),
                   jax.ShapeDtypeStruct((B, SUBLANES, LANES), F32)),
        scratch_shapes=[pltpu.VMEM((nb,) + cn_dims, F32),
                        pltpu.VMEM((nb, SUBLANES, LANES), F32)],
        compiler_params=_params(("arbitrary", "arbitrary")),
        name="mlstm_prompt" if zero_init else "mlstm_sample",
    )(*args)


def _outproj(x2, ya, yb, w_out, final_g):
    rows = x2.shape[0]
    tm = min(OUTPROJ_TILE, rows)
    tile = lambda width: pl.BlockSpec((tm, width), lambda i: (i, 0))
    return pl.pallas_call(
        _outproj_kernel,
        grid=(rows // tm,),
        in_specs=[tile(D_MODEL), tile(D_ATT), tile(D_MLSTM),
                  _const_spec((D_ATT + D_MLSTM, D_MODEL)), _const_spec((1, D_MODEL))],
        out_specs=tile(D_MODEL),
        out_shape=jax.ShapeDtypeStruct((rows, D_MODEL), F32),
        compiler_params=_params(("arbitrary",)),
        name="outproj",
    )(x2, ya, yb, w_out, final_g)


def kernel(x_prompt, x_sample, cache_win_k, cache_win_v, state_conv, state_C, state_n, state_m,
           norm_g, w_in, conv_w, conv_b, b_i, b_f, mlstm_norm_g, w_out, final_norm_g):
    assert w_in.shape[0] == 1, "single-layer model"
    B, S, D = x_prompt.shape
    DB, T, _ = x_sample.shape
    HB, DK = N_HEADS_MLSTM, HEAD_DIM_MLSTM
    wb = cache_win_k.shape[2]
    assert S % ROW_TILE == 0 and S == 16 * N_BACK and wb >= 16 * N_BACK and CONV_WIDTH - 1 <= T <= SUBLANES and DB * T == LANES

    w_pad = jnp.pad(w_in[0], ((0, 0), (0, N_IN_PAD - N_IN))).astype(BF16)
    w_o = w_out[0].astype(BF16)
    g_in = norm_g[0][None, :]
    cw, cb = conv_w[0], conv_b[0][None, :]
    bif = jnp.pad(jnp.concatenate([b_i[0], b_f[0]]), (0, LANES - 2 * HB))[None, :]
    ng = mlstm_norm_g[0][None, :]
    g_fin = final_norm_g[None, :]

    (q_p, k_p, v_p, pk, pv, ga_p, qm_p, km_p, vb_p, so_p, gb_p, gt_p, p_conv) = _inproj_prompt(
        x_prompt, g_in, w_pad, cw, cb, bif)
    ya_p = _attn_prompt(q_p, k_p, v_p, ga_p)
    yb_p, cn_p, m_p = _mlstm(qm_p, km_p, vb_p, gt_p, so_p, gb_p, ng)
    y_prompt = _outproj(x_prompt.reshape(B * S, D), ya_p.reshape(B * S, D_ATT),
                        yb_p.reshape(B * S, D_MLSTM), w_o, g_fin).reshape(B, S, D)

    sc = state_conv[0]
    zrow = jnp.zeros((DB, 1, 2 * D_MLSTM), F32)
    hist = []
    for sh in (1, 2, 3):
        rows_ = [sc[:, CONV_WIDTH - 1 + t - sh:CONV_WIDTH + t - sh] if t < sh else zrow
                 for t in range(T)]
        hist.append(jnp.concatenate(rows_, axis=1).reshape(DB * T, 2 * D_MLSTM))
    (q_s, kn, vn, knt, vnt, ga_s, qm_s, km_s, vb_s, so_s, gb_s, gates_s, qk_s) = _inproj_sample(
        x_sample.reshape(DB * T, D), g_in, w_pad, cw, cb, bif, hist, T)
    r3 = lambda a: a.reshape(DB, T, a.shape[-1])
    pad8 = lambda a: jnp.pad(r3(a), ((0, 0), (0, SUBLANES - T), (0, 0)))
    to_hdp = lambda c: jnp.transpose(c[0], (0, 2, 3, 1))
    from_hdp = lambda c: jnp.transpose(c, (0, 3, 1, 2))[None]
    ya_s, s_k, s_v = _attn_sample(pad8(q_s), pad8(kn), pad8(vn), knt, vnt, pad8(ga_s),
                                  to_hdp(cache_win_k), to_hdp(cache_win_v), T)
    ya_s = ya_s[:, :T]

    pad_t = lambda a: jnp.pad(r3(a), ((0, 0), (0, CHUNK - T), (0, 0)))
    g3 = r3(gates_s)[:, :, :SUBLANES]
    null_gate = jnp.concatenate([jnp.full((HB,), NEG_INF, F32), jnp.zeros((HB,), F32)])
    gt_s = jnp.concatenate([g3, jnp.broadcast_to(null_gate, (DB, CHUNK - T, SUBLANES))], axis=1)
    gt_s = gt_s.transpose(0, 2, 1)
    cn0 = jnp.concatenate([state_C[0], jnp.broadcast_to(state_n[0][..., None], (DB, HB, DK, DK))],
                          axis=-1)
    m0 = jnp.broadcast_to(jnp.pad(state_m[0], ((0, 0), (0, SUBLANES - HB)))[..., None],
                          (DB, SUBLANES, LANES))
    yb_s, cn_s, m_s = _mlstm(pad_t(qm_s), pad_t(km_s), pad_t(vb_s), gt_s, pad_t(so_s), pad_t(gb_s),
                             ng, cn0, m0)
    y_sample = _outproj(x_sample.reshape(DB * T, D), ya_s.reshape(DB * T, D_ATT),
                        yb_s[:, :T].reshape(DB * T, D_MLSTM), w_o, g_fin).reshape(DB, T, D)

    return (y_prompt, y_sample,
            from_hdp(pk), from_hdp(pv), p_conv[None],
            cn_p[None, ..., :DK], cn_p[None, ..., DK], m_p[None, :, :HB, 0],
            from_hdp(s_k), from_hdp(s_v), r3(qk_s)[None, :, T - (CONV_WIDTH - 1):],
            cn_s[None, ..., :DK], cn_s[None, ..., DK], m_s[None, :, :HB, 0])
```

```python
import functools
import math

import jax
import jax.numpy as jnp
import numpy as np
from jax import lax
from jax.experimental import pallas as pl
from jax.experimental.pallas import tpu as pltpu

F32 = jnp.float32
BF16 = jnp.bfloat16

D_MODEL = 1024
D_ATT = 512
N_HEADS_ATT = 8
HEAD_DIM_ATT = 64
D_MLSTM = 512
N_HEADS_MLSTM = 4
HEAD_DIM_MLSTM = 128
ROT_DIM = 16
ROPE_THETA = 500000.0
PAST_LEN = 16384
DILATIONS = (1, 4, 16)
N_BACK = 128
CONV_WIDTH = 4
CHUNK = 128
EPS = 1e-6
NEG_INF = -1e30
LOG2_E = math.log2(math.e)

LANES = 128
SUBLANES = 8
N_IN = 4 * D_ATT + 5 * D_MLSTM + 2 * N_HEADS_MLSTM
N_IN_PAD = 4 * D_ATT + 5 * D_MLSTM + LANES
OFF_QA, OFF_KA, OFF_VA, OFF_ZA = 0, 512, 1024, 1536
OFF_QB, OFF_VB, OFF_OB, OFF_ZB, OFF_G = 2048, 3072, 3584, 4096, 4608
ROW_TILE = 512
INPROJ_SUBTILE = 128
OUTPROJ_TILE = 2048
OUTPROJ_SUBTILE = 256
MLSTM_BATCH = 4
VMEM_LIMIT = 56 * 1024 * 1024


def _silu(x):
    return x * jax.nn.sigmoid(x)


def _rmsnorm(x, g):
    return x * lax.rsqrt(jnp.mean(x * x, axis=-1, keepdims=True) + EPS) * g


def _rope(u, cos, sin_lo, sin_hi):
    outs = []
    for c in range(u.shape[1] // LANES):
        xs = u[:, c * LANES:(c + 1) * LANES]
        outs.append(xs * cos + pltpu.roll(xs, LANES - ROT_DIM // 2, 1) * sin_lo
                    + pltpu.roll(xs, ROT_DIM // 2, 1) * sin_hi)
    return jnp.concatenate(outs, axis=1)


def _gate_block(ug, bif):
    gz = ug + bif
    lane = lax.broadcasted_iota(jnp.int32, gz.shape, 1)
    logf = jnp.minimum(gz, 0.0) - jnp.log1p(jnp.exp(-jnp.abs(gz)))
    return jnp.where(lane < N_HEADS_MLSTM, gz, logf)


def _segments(hn, w_ref):
    def seg(off, width):
        return jnp.dot(hn, w_ref[:, off:off + width], preferred_element_type=F32)
    return seg


def _inproj_common(seg, cos, sin_lo, sin_hi, q_ref, ga_ref, vb_ref, so_ref, gb_ref):
    q = _rope(seg(OFF_QA, D_ATT), cos, sin_lo, sin_hi) * (HEAD_DIM_ATT ** -0.5 * LOG2_E)
    q_ref[...] = q.astype(BF16).reshape(q_ref.shape)
    k = _rope(seg(OFF_KA, D_ATT), cos, sin_lo, sin_hi)
    v = seg(OFF_VA, D_ATT)
    ga_ref[...] = _silu(seg(OFF_ZA, D_ATT)).astype(BF16).reshape(ga_ref.shape)
    vb_ref[...] = seg(OFF_VB, D_MLSTM).astype(BF16).reshape(vb_ref.shape)
    so_ref[...] = jax.nn.sigmoid(seg(OFF_OB, D_MLSTM)).astype(BF16).reshape(so_ref.shape)
    gb_ref[...] = _silu(seg(OFF_ZB, D_MLSTM)).astype(BF16).reshape(gb_ref.shape)
    return k, v


def _inproj_prompt_kernel(x_ref, g_ref, w_ref, cw_ref, cb_ref, bif_ref, cos_ref, slo_ref, shi_ref,
                          q_ref, kb_ref, vbf_ref, pk_ref, pv_ref, ga_ref, qm_ref, km_ref, vb_ref,
                          so_ref, gb_ref, gt_ref, pconv_ref, xp_ref):
    j = pl.program_id(1)
    tm = x_ref.shape[1]
    sub = INPROJ_SUBTILE
    assert CHUNK == LANES and tm % sub == 0 and sub % CHUNK == 0

    @pl.when(j == 0)
    def _():
        xp_ref[0:SUBLANES, :] = jnp.zeros((SUBLANES, 2 * D_MLSTM), F32)

    @pl.when(j > 0)
    def _():
        xp_ref[0:SUBLANES, :] = xp_ref[tm:tm + SUBLANES, :]

    for r0 in range(0, tm, sub):
        rs = pl.ds(r0, sub)
        part = lambda ref: ref.at[0, rs, :]
        hn = _rmsnorm(x_ref[0, rs, :], g_ref[...]).astype(BF16)
        seg = _segments(hn, w_ref)

        gates = _gate_block(seg(OFF_G, LANES), bif_ref[...])
        for i in range(sub // CHUNK):
            cs = slice(r0 + i * CHUNK, r0 + (i + 1) * CHUNK)
            gt = gates[i * CHUNK:(i + 1) * CHUNK, :].T[0:SUBLANES, :]
            gt_ref[0, 0:SUBLANES, cs] = gt
            gt_ref[0, SUBLANES:2 * SUBLANES, cs] = _gate_scans(gt)

        k, v = _inproj_common(seg, cos_ref[rs, :], slo_ref[rs, :], shi_ref[rs, :],
                              part(q_ref), part(ga_ref), part(vb_ref), part(so_ref), part(gb_ref))
        kb_ref[0, rs, :] = k.astype(BF16)
        vbf_ref[0, rs, :] = v.astype(BF16)
        pk_ref[0, :, :, r0:r0 + sub] = k.T.reshape(N_HEADS_ATT, HEAD_DIM_ATT, sub)
        pv_ref[0, :, :, r0:r0 + sub] = v.T.reshape(N_HEADS_ATT, HEAD_DIM_ATT, sub)

        base = SUBLANES + r0
        xp_ref[base:base + sub, :] = seg(OFF_QB, 2 * D_MLSTM)
        y = cb_ref[...] + xp_ref[base:base + sub, :] * cw_ref[3:4, :]
        for jj in range(CONV_WIDTH - 1):
            sh = CONV_WIDTH - 1 - jj
            y = y + xp_ref[base - sh:base - sh + sub, :] * cw_ref[jj:jj + 1, :]
        y = _silu(y)
        qm_ref[0, rs, :] = y[:, :D_MLSTM].astype(BF16)
        km_ref[0, rs, :] = (y[:, D_MLSTM:] * (HEAD_DIM_MLSTM ** -0.5)).astype(BF16)

    @pl.when(j == pl.num_programs(1) - 1)
    def _():
        pconv_ref[0] = xp_ref[tm + SUBLANES - (CONV_WIDTH - 1):tm + SUBLANES, :]


def _inproj_sample_kernel(x_ref, g_ref, w_ref, cw_ref, cb_ref, bif_ref, cos_ref, slo_ref, shi_ref,
                          h1_ref, h2_ref, h3_ref,
                          q_ref, kn_ref, vn_ref, knt_ref, vnt_ref, ga_ref, qm_ref, km_ref, vb_ref,
                          so_ref, gb_ref, gates_ref, qk_ref, *, t_new):
    hn = _rmsnorm(x_ref[...], g_ref[...]).astype(BF16)
    seg = _segments(hn, w_ref)
    k, v = _inproj_common(seg, cos_ref[...], slo_ref[...], shi_ref[...],
                          q_ref, ga_ref, vb_ref, so_ref, gb_ref)
    kn_ref[...] = k
    vn_ref[...] = v
    knt_ref[...] = k.T
    vnt_ref[...] = v.T
    u = seg(OFF_QB, 2 * D_MLSTM)
    qk_ref[...] = u
    t = lax.rem(lax.broadcasted_iota(jnp.int32, u.shape, 0), t_new)
    y = cb_ref[...] + u * cw_ref[3:4, :]
    for sh, h_ref in ((1, h1_ref), (2, h2_ref), (3, h3_ref)):
        prev = jnp.where(t >= sh, pltpu.roll(u, sh, 0), h_ref[...])
        y = y + prev * cw_ref[3 - sh:4 - sh, :]
    y = _silu(y)
    qm_ref[...] = y[:, :D_MLSTM].astype(BF16)
    km_ref[...] = (y[:, D_MLSTM:] * (HEAD_DIM_MLSTM ** -0.5)).astype(BF16)
    gates_ref[...] = _gate_block(seg(OFF_G, LANES), bif_ref[...])


def _attn_prompt_kernel(q_ref, k_ref, v_ref, g_ref, o_ref,
                        src1_ref, src4_ref, src16_ref, bias_ref, st16r_ref, st16_ref, st4_ref):
    seq = q_ref.shape[1]
    n16 = seq // 16
    group = 8
    head_a_full = lax.broadcasted_iota(jnp.int32, (seq, LANES), 1) < HEAD_DIM_ATT
    by16 = lambda x: jnp.swapaxes(x.reshape(n16, 16, LANES), 0, 1)
    def stack(q, k, v, is_a):
        zero, one = jnp.zeros((), q.dtype), jnp.ones((), q.dtype)
        return (jnp.where(is_a, q, zero), jnp.where(is_a, zero, q), k,
                jnp.where(is_a, v, one), jnp.where(is_a, one, v))

    q, k, v = q_ref[0], k_ref[0], v_ref[0]
    for i, x in enumerate(stack(q, k, v, head_a_full)):
        src1_ref[i] = x
    q, k, v = q.astype(F32), k.astype(F32), v.astype(F32)
    for i, x in enumerate(stack(q, k, v, head_a_full)):
        src4_ref[i] = x
    head_a_16 = lax.broadcasted_iota(jnp.int32, (16, n16, LANES), 2) < HEAD_DIM_ATT
    for i, x in enumerate(stack(by16(q).astype(BF16), by16(k).astype(BF16), by16(v).astype(BF16),
                                head_a_16)):
        src16_ref[i] = x
    u = lax.broadcasted_iota(jnp.int32, (N_BACK, 2 * N_BACK), 0)
    w = lax.broadcasted_iota(jnp.int32, (N_BACK, 2 * N_BACK), 1)
    bias_ref[...] = jnp.where((w >= u) & (w <= u + N_BACK), jnp.finfo(F32).max, NEG_INF)
    head_a = lax.broadcasted_iota(jnp.int32, (N_BACK, LANES), 1) < HEAD_DIM_ATT

    def partials(gets):
        staged = []
        for get, has_prev in gets:
            if has_prev:
                both = lambda i, get=get: jnp.concatenate([get(i, True), get(i, False)], axis=0)
                bias = bias_ref[...]
            else:
                both = lambda i, get=get: get(i, False)
                bias = bias_ref[:, N_BACK:]
            kk = both(2)
            heads = []
            for qi in (0, 1):
                s = lax.dot_general(get(qi, False), kk, (((1,), (1,)), ((), ())),
                                    preferred_element_type=F32)
                s = jnp.minimum(s, bias)
                mh = jnp.max(s, axis=-1, keepdims=True)
                heads.append((mh, jnp.exp2(s - mh).astype(BF16)))
            staged.append((both, heads))
        out = []
        for both, ((m_a, p_a), (m_b, p_b)) in staged:
            pv_a = jnp.dot(p_a, both(3), preferred_element_type=F32)
            pv_b = jnp.dot(p_b, both(4), preferred_element_type=F32)
            acc = jnp.where(head_a, pv_a, pv_b)
            den = pltpu.roll(jnp.where(head_a, pv_b, pv_a), HEAD_DIM_ATT, 1)
            out.append((jnp.where(head_a, m_a, m_b), den, acc))
        return out

    def get4(qstart, kprev_start):
        def get(i, prev):
            start = kprev_start if prev else qstart
            return src4_ref[i, pl.ds(start, N_BACK, stride=4), :].astype(BF16)
        return get, kprev_start is not None

    def get1(qstart, kprev_start):
        def get(i, prev):
            return src1_ref[i, pl.ds(kprev_start if prev else qstart, N_BACK), :]
        return get, kprev_start is not None

    def keep4(blocks):
        res = partials([get4(qs, ks) for qs, ks in blocks])
        for (qs, _), (m, den, acc) in zip(blocks, res):
            rows = pl.ds(qs, N_BACK, stride=4)
            st4_ref[0, rows, :] = m
            st4_ref[1, rows, :] = den
            st4_ref[2, rows, :] = acc

    def finish(blocks):
        res = partials([get1(qs, ks) for qs, ks in blocks])
        for (qs, _), part in zip(blocks, res):
            rows = pl.ds(qs, N_BACK)
            parts = [part] + [tuple(st[i, rows, :] for i in range(3)) for st in (st4_ref, st16_ref)]
            m_all = functools.reduce(jnp.maximum, [pt[0] for pt in parts])
            wts = [jnp.exp2(pt[0] - m_all) for pt in parts]
            den = sum(wt * pt[1] for wt, pt in zip(wts, parts))
            num = sum(wt * pt[2] for wt, pt in zip(wts, parts))
            o_ref[0, rows, :] = (num / den * g_ref[0, rows, :].astype(F32)).astype(BF16)

    def body16(g, c):
        rs = [g * group + rr for rr in range(group)]
        res = partials([(lambda i, prev, r=r: src16_ref[i, r], False) for r in rs])
        for r, part in zip(rs, res):
            for i in range(3):
                st16r_ref[i, r] = part[i]
        return c
    lax.fori_loop(0, 16 // group, body16, 0)
    for i in range(3):
        st16_ref[i] = jnp.swapaxes(st16r_ref[i], 0, 1).reshape(seq, LANES)

    blocks4 = lambda cc: [(r + 4 * N_BACK * cc, r + 4 * N_BACK * (cc - 1)) for r in range(4)]
    keep4([(r, None) for r in range(4)] + blocks4(1))

    def body4(g, c):
        keep4(blocks4(2 * g) + blocks4(2 * g + 1))
        return c
    lax.fori_loop(1, seq // 4 // N_BACK // 2, body4, 0)

    finish([(0, None)] + [(cc * N_BACK, (cc - 1) * N_BACK) for cc in range(1, group)])

    def body1(g, c):
        starts = [pl.multiple_of((g * group + rr) * N_BACK, N_BACK) for rr in range(group)]
        finish([(st, st - N_BACK) for st in starts])
        return c
    lax.fori_loop(1, seq // N_BACK // group, body1, 0)


def _attn_sample_kernel(q_ref, kn_ref, vn_ref, knt_ref, vnt_ref, g_ref, ck_ref, cv_ref,
                        o_ref, sk_ref, sv_ref, clamp_ref, *, t_new):
    b = pl.program_id(0)
    wb = ck_ref.shape[3]
    hd = HEAD_DIM_ATT
    rows = q_ref.shape[1]

    @pl.when(b == 0)
    def _():
        delta = (wb + lax.broadcasted_iota(jnp.int32, (rows, wb), 0)
                 - lax.broadcasted_iota(jnp.int32, (rows, wb), 1))
        for d, dil in enumerate(DILATIONS):
            ok = ((delta & (dil - 1)) == 0) & (delta >= dil) & (delta <= N_BACK * dil)
            clamp_ref[d] = jnp.where(ok, jnp.finfo(F32).max, NEG_INF)

    tq = lax.broadcasted_iota(jnp.int32, (rows, rows), 0)
    tk = lax.broadcasted_iota(jnp.int32, (rows, rows), 1)
    new_ok = [((tk <= tq) if dil == 1 else (tk == tq)) & (tk < t_new) for dil in DILATIONS]

    lane = lax.broadcasted_iota(jnp.int32, (hd, LANES), 1)
    shift_new = (LANES - t_new) - b * t_new
    nt = (((1,), (1,)), ((), ()))
    outs = []
    for h in range(N_HEADS_ATT):
        hs = slice(h * hd, (h + 1) * hd)
        kt = ck_ref[0, h]
        vt = cv_ref[0, h]
        for old, new_ref, out_ref in ((kt, knt_ref, sk_ref), (vt, vnt_ref, sv_ref)):
            moved = pltpu.roll(old, wb - t_new, axis=1)
            new_cols = pltpu.roll(new_ref[hs, :], shift_new, axis=1)
            out_ref[0, h, :, 0:wb - LANES] = moved[:, 0:wb - LANES]
            out_ref[0, h, :, wb - LANES:wb] = jnp.where(lane < LANES - t_new,
                                                        moved[:, wb - LANES:wb], new_cols)
        qh = q_ref[0, :, hs]
        kn_h = kn_ref[0, :, hs].astype(BF16)
        vn_h = vn_ref[0, :, hs].astype(BF16)
        s_old = jnp.dot(qh, kt.astype(BF16), preferred_element_type=F32)
        s_new = lax.dot_general(qh, kn_h, nt, preferred_element_type=F32)
        ps, pes, ms = [], [], []
        for d in range(len(DILATIONS)):
            so = jnp.minimum(s_old, clamp_ref[d])
            sn = jnp.where(new_ok[d], s_new, NEG_INF)
            m = jnp.maximum(jnp.max(so, axis=-1, keepdims=True), jnp.max(sn, axis=-1, keepdims=True))
            ps.append(jnp.exp2(so - m))
            pes.append(jnp.exp2(sn - m))
            ms.append(m)
        acc = lax.dot_general(jnp.concatenate(ps, axis=0).astype(BF16), vt.astype(BF16), nt,
                              preferred_element_type=F32)
        acc = acc + jnp.dot(jnp.concatenate(pes, axis=0).astype(BF16), vn_h,
                            preferred_element_type=F32)
        m_all = functools.reduce(jnp.maximum, ms)
        den = 0.0
        num = 0.0
        for d in range(len(DILATIONS)):
            wgt = jnp.exp2(ms[d] - m_all)
            den = den + wgt * (jnp.sum(ps[d], axis=-1, keepdims=True)
                               + jnp.sum(pes[d], axis=-1, keepdims=True))
            num = num + wgt * acc[d * rows:(d + 1) * rows]
        outs.append(num / den)
    att = jnp.concatenate(outs, axis=1)
    o_ref[0] = (att * g_ref[0].astype(F32)).astype(BF16)


def _scan_lanes(x, op, fill):
    lane = lax.broadcasted_iota(jnp.int32, x.shape, 1)
    d = 1
    while d < x.shape[1]:
        x = op(x, jnp.where(lane >= d, pltpu.roll(x, d, 1), fill))
        d *= 2
    return x


def _gate_scans(gt):
    b = _scan_lanes(pltpu.roll(gt, N_HEADS_MLSTM, 0), jnp.add, 0.0)
    cm = _scan_lanes(gt - b, jnp.maximum, NEG_INF)
    row = lax.broadcasted_iota(jnp.int32, gt.shape, 0)
    return jnp.where(row < N_HEADS_MLSTM, b, pltpu.roll(cm, N_HEADS_MLSTM, 0))


def _mlstm_kernel(*refs, zero_init, scans_given, fuse_out):
    refs = list(refs)
    q_ref, k_ref, v_ref, gt_ref, so_ref, gb_ref, ng_ref = refs[:7]
    del refs[:7]
    if not zero_init:
        c0_ref, n0_ref, m0_ref = refs[:3]
        del refs[:3]
    if fuse_out:
        x_ref, ya_ref, wo_ref, fg_ref = refs[:4]
        del refs[:4]
    y_ref, c_out_ref, n_out_ref, m_out_ref, cn_ref, m_ref = refs
    c_idx = pl.program_id(1)
    nb = q_ref.shape[0]
    L = CHUNK
    dh = HEAD_DIM_MLSTM
    nh = N_HEADS_MLSTM

    @pl.when(c_idx == 0)
    def _():
        if zero_init:
            cn_ref[...] = jnp.zeros(cn_ref.shape, F32)
            m_ref[...] = jnp.zeros(m_ref.shape, F32)
        else:
            m_ref[...] = m0_ref[...]
            for bb in range(nb):
                for h in range(nh):
                    n_rows = jnp.broadcast_to(n0_ref[bb, h:h + 1, :], (dh, dh))
                    cn_ref[bb, h] = jnp.concatenate([c0_ref[bb, h], n_rows.T], axis=1)

    tri_t = lax.broadcasted_iota(jnp.int32, (L, L), 0)
    tri_s = lax.broadcasted_iota(jnp.int32, (L, L), 1)
    causal = tri_t >= tri_s
    ones_blk = jnp.ones((L, dh), BF16)
    nt = (((1,), (1,)), ((), ()))
    pairs = [(bb, h) for bb in range(nb) for h in range(nh)]
    sl = lambda h: slice(h * dh, (h + 1) * dh)

    qk = {(bb, h): lax.dot_general(q_ref[bb, :, sl(h)], k_ref[bb, :, sl(h)], nt,
                                   preferred_element_type=F32) for bb, h in pairs}
    cn_old = {p: cn_ref[p[0], p[1]] for p in pairs}
    qc = {(bb, h): jnp.dot(q_ref[bb, :, sl(h)], cn_old[bb, h].astype(BF16),
                           preferred_element_type=F32) for bb, h in pairs}

    rows, cols, decays, w_rows = [], [], [], []
    for bb in range(nb):
        if scans_given:
            i_row = gt_ref[bb, 0:SUBLANES, :]
            sc = gt_ref[bb, SUBLANES:2 * SUBLANES, :]
        else:
            i_row = gt_ref[bb]
            sc = _gate_scans(i_row)
        b = sc
        cm = pltpu.roll(sc, nh, 0)
        m_prev = m_ref[bb]
        m_t = jnp.maximum(m_prev + b, b + cm)
        inter = jnp.exp(m_prev + b - m_t)
        m_last = jnp.broadcast_to(m_t[:, L - 1:L], m_t.shape)
        b_last = jnp.broadcast_to(b[:, L - 1:L], b.shape)
        decay = jnp.exp(m_prev + b_last - m_last)
        w_row = jnp.exp(b_last - b + i_row - m_last)
        m_ref[bb] = m_last
        stack = jnp.concatenate([b - m_t, inter, jnp.exp(-m_t),
                                 jnp.zeros((L - 3 * SUBLANES, L), F32)], axis=0)
        cols.append(stack.T)
        rows.append(i_row - b)
        w_rows.append(w_row)
        decays.append(jnp.concatenate([decay, decay], axis=1))
    col = lambda bb, kind, h: cols[bb][:, kind * SUBLANES + h:kind * SUBLANES + h + 1]

    sqk = {}
    for bb, h in pairs:
        dlog = col(bb, 0, h) + rows[bb][h:h + 1, :]
        sqk[bb, h] = (qk[bb, h] * jnp.exp(jnp.where(causal, dlog, NEG_INF))).astype(BF16)
    v_one = {(bb, h): jnp.concatenate([v_ref[bb, :, sl(h)], ones_blk], axis=1) for bb, h in pairs}
    pv = {p: jnp.dot(sqk[p], v_one[p], preferred_element_type=F32) for p in pairs}
    yb = {}
    for bb, h in pairs:
        tot = col(bb, 1, h) * qc[bb, h] + pv[bb, h]
        hh = tot[:, :dh] / jnp.maximum(jnp.abs(tot[:, dh:]), col(bb, 2, h))
        hh = so_ref[bb, :, sl(h)].astype(F32) * hh
        hh = hh * lax.rsqrt(jnp.mean(hh * hh, axis=-1, keepdims=True) + EPS)
        hh = hh * ng_ref[:, sl(h)]
        yb[bb, h] = (hh * gb_ref[bb, :, sl(h)].astype(F32)).astype(BF16)
    if fuse_out:
        yb_all = jnp.concatenate([jnp.concatenate([yb[bb, h] for h in range(nh)], axis=1)
                                  for bb in range(nb)], axis=0)
        ya_all = ya_ref[...].reshape(nb * L, D_ATT)
        mix = jnp.dot(ya_all, wo_ref[0:D_ATT, :], preferred_element_type=F32)
        mix = mix + jnp.dot(yb_all, wo_ref[D_ATT:, :], preferred_element_type=F32)
        res = _rmsnorm(x_ref[...].reshape(nb * L, D_MODEL) + mix, fg_ref[...])
        y_ref[...] = res.reshape(nb, L, D_MODEL)
    else:
        for bb, h in pairs:
            y_ref[bb, :, sl(h)] = yb[bb, h]
    for bb, h in pairs:
        kt_w = k_ref[bb, :, sl(h)].T.astype(F32) * w_rows[bb][h:h + 1, :]
        upd = jnp.dot(kt_w.astype(BF16), v_one[bb, h], preferred_element_type=F32)
        cn_ref[bb, h] = decays[bb][h:h + 1, :] * cn_old[bb, h] + upd

    @pl.when(c_idx == pl.num_programs(1) - 1)
    def _():
        m_out_ref[...] = m_ref[...]
        for bb in range(nb):
            n_rows = []
            for h in range(nh):
                cn = cn_ref[bb, h]
                c_out_ref[bb, h] = cn[:, :dh]
                n_rows.append(cn[:, dh:].T[0:1, :])
            n_out_ref[bb] = jnp.concatenate(n_rows, axis=0)


def _outproj_kernel(x_ref, ya_ref, yb_ref, w_ref, g_ref, o_ref):
    rows = x_ref.shape[0]
    sub = min(OUTPROJ_SUBTILE, rows)
    for r0 in range(0, rows, sub):
        rs = pl.ds(r0, sub)
        mix = jnp.dot(ya_ref[rs, :], w_ref[0:D_ATT, :], preferred_element_type=F32)
        mix = mix + jnp.dot(yb_ref[rs, :], w_ref[D_ATT:, :], preferred_element_type=F32)
        o_ref[rs, :] = _rmsnorm(x_ref[rs, :] + mix, g_ref[...])


def _rope_tables(pos):
    half = ROT_DIM // 2
    inv = ROPE_THETA ** (-jnp.arange(half, dtype=F32) * 2.0 / ROT_DIM)
    ang = pos.astype(F32)[:, None] * inv[None, :]
    cos, sin = jnp.cos(ang), jnp.sin(ang)
    n = pos.shape[0]
    one = jnp.ones((n, HEAD_DIM_ATT - ROT_DIM), F32)
    zero = jnp.zeros((n, HEAD_DIM_ATT - ROT_DIM), F32)
    zh = jnp.zeros((n, half), F32)
    cos_t = jnp.concatenate([cos, cos, one], axis=1)
    lo_t = jnp.concatenate([-sin, zh, zero], axis=1)
    hi_t = jnp.concatenate([zh, sin, zero], axis=1)
    rep = lambda t: jnp.concatenate([t, t], axis=1)
    return rep(cos_t), rep(lo_t), rep(hi_t)


def _params(sem):
    return pltpu.CompilerParams(dimension_semantics=sem, vmem_limit_bytes=VMEM_LIMIT)


def _const_spec(shape):
    return pl.BlockSpec(shape, lambda *_: (0,) * len(shape))


def _inproj_prompt(x, norm_g, w_pad, conv_w, conv_b, bif):
    B, S, _ = x.shape
    tm = ROW_TILE
    cos, lo, hi = _rope_tables(jnp.arange(S, dtype=jnp.int32))
    tile = lambda width: pl.BlockSpec((1, tm, width), lambda b, j: (b, j, 0))
    tab = pl.BlockSpec((tm, LANES), lambda b, j: (j, 0))
    bf = lambda: jax.ShapeDtypeStruct((B, S, D_ATT), BF16)
    f5 = lambda: jax.ShapeDtypeStruct((B, N_HEADS_ATT, HEAD_DIM_ATT, S), F32)
    tile5 = pl.BlockSpec((1, N_HEADS_ATT, HEAD_DIM_ATT, tm), lambda b, j: (b, 0, 0, j))
    out_shape = (bf(), bf(), bf(), f5(), f5(), bf(), bf(), bf(), bf(), bf(), bf(),
                 jax.ShapeDtypeStruct((B, 2 * SUBLANES, S), F32),
                 jax.ShapeDtypeStruct((B, CONV_WIDTH - 1, 2 * D_MLSTM), F32))
    out_specs = tuple([tile(D_ATT)] * 3 + [tile5] * 2 + [tile(D_ATT)] * 6) + (
        pl.BlockSpec((1, 2 * SUBLANES, tm), lambda b, j: (b, 0, j)),
        pl.BlockSpec((1, CONV_WIDTH - 1, 2 * D_MLSTM), lambda b, j: (b, 0, 0)))
    return pl.pallas_call(
        _inproj_prompt_kernel,
        grid=(B, S // tm),
        in_specs=[tile(D_MODEL), _const_spec((1, D_MODEL)), _const_spec((D_MODEL, N_IN_PAD)),
                  _const_spec((CONV_WIDTH, 2 * D_MLSTM)), _const_spec((1, 2 * D_MLSTM)),
                  _const_spec((1, LANES)), tab, tab, tab],
        out_specs=out_specs,
        out_shape=out_shape,
        scratch_shapes=[pltpu.VMEM((tm + 2 * SUBLANES, 2 * D_MLSTM), F32)],
        compiler_params=_params(("arbitrary", "arbitrary")),
        name="inproj_prompt",
    )(x, norm_g, w_pad, conv_w, conv_b, bif, cos, lo, hi)


def _inproj_sample(x2, norm_g, w_pad, conv_w, conv_b, bif, hist, t_new):
    rows = x2.shape[0]
    pos = PAST_LEN + jnp.arange(t_new, dtype=jnp.int32)
    cos, lo, hi = (jnp.tile(t, (rows // t_new, 1)) for t in _rope_tables(pos))
    bf = lambda: jax.ShapeDtypeStruct((rows, D_ATT), BF16)
    f3 = lambda: jax.ShapeDtypeStruct((rows, D_ATT), F32)
    f5 = lambda: jax.ShapeDtypeStruct((D_ATT, rows), F32)
    out_shape = (bf(), f3(), f3(), f5(), f5(), bf(), bf(), bf(), bf(), bf(), bf(),
                 jax.ShapeDtypeStruct((rows, LANES), F32),
                 jax.ShapeDtypeStruct((rows, 2 * D_MLSTM), F32))
    return pl.pallas_call(
        functools.partial(_inproj_sample_kernel, t_new=t_new),
        out_shape=out_shape,
        compiler_params=pltpu.CompilerParams(vmem_limit_bytes=VMEM_LIMIT),
        name="inproj_sample",
    )(x2, norm_g, w_pad, conv_w, conv_b, bif, cos, lo, hi, *hist)


def _attn_prompt(q, k, v, gate):
    B, S, _ = q.shape
    spec = pl.BlockSpec((1, S, LANES), lambda b, h: (b, 0, h))
    return pl.pallas_call(
        _attn_prompt_kernel,
        grid=(B, D_ATT // LANES),
        in_specs=[spec, spec, spec, spec],
        out_specs=spec,
        out_shape=jax.ShapeDtypeStruct((B, S, D_ATT), BF16),
        scratch_shapes=[
            pltpu.VMEM((5, S, LANES), BF16), pltpu.VMEM((5, S, LANES), F32),
            pltpu.VMEM((5, 16, S // 16, LANES), BF16),
            pltpu.VMEM((N_BACK, 2 * N_BACK), F32),
            pltpu.VMEM((3, 16, S // 16, LANES), F32),
            pltpu.VMEM((3, S, LANES), F32), pltpu.VMEM((3, S, LANES), F32)],
        compiler_params=_params(("arbitrary", "arbitrary")),
        name="attn_prompt",
    )(q, k, v, gate)


def _attn_sample(q, kn, vn, knt, vnt, gate, ck, cv, t_new):
    B, rows, _ = q.shape
    wb = ck.shape[3]
    small = pl.BlockSpec((1, rows, D_ATT), lambda b: (b, 0, 0))
    big = pl.BlockSpec((1, N_HEADS_ATT, HEAD_DIM_ATT, wb), lambda b: (b, 0, 0, 0))
    win_shape = jax.ShapeDtypeStruct((B, N_HEADS_ATT, HEAD_DIM_ATT, wb), F32)
    return pl.pallas_call(
        functools.partial(_attn_sample_kernel, t_new=t_new),
        grid=(B,),
        in_specs=[small, small, small, _const_spec(knt.shape), _const_spec(vnt.shape), small, big, big],
        out_specs=(small, big, big),
        out_shape=(jax.ShapeDtypeStruct((B, rows, D_ATT), BF16), win_shape, win_shape),
        scratch_shapes=[pltpu.VMEM((len(DILATIONS), rows, wb), F32)],
        compiler_params=_params(("arbitrary",)),
        name="attn_sample",
    )(q, kn, vn, knt, vnt, gate, ck, cv)


def _mlstm(q, k, v, gates_t, sig_o, gate_b, norm_g, state=None, out_proj=None):
    B, S, _ = q.shape
    nc = S // CHUNK
    nb = MLSTM_BATCH
    nh, dh = N_HEADS_MLSTM, HEAD_DIM_MLSTM
    grows = gates_t.shape[1]
    tile = lambda width: pl.BlockSpec((nb, CHUNK, width), lambda b, c: (b, c, 0))
    gspec = pl.BlockSpec((nb, grows, CHUNK), lambda b, c: (b, 0, c))
    c_spec = pl.BlockSpec((nb, nh, dh, dh), lambda b, c: (b, 0, 0, 0))
    n_spec = pl.BlockSpec((nb, nh, dh), lambda b, c: (b, 0, 0))
    m_spec = pl.BlockSpec((nb, SUBLANES, LANES), lambda b, c: (b, 0, 0))
    zero_init = state is None
    fuse_out = out_proj is not None
    in_specs = [tile(D_MLSTM)] * 3 + [gspec] + [tile(D_MLSTM)] * 2 + [_const_spec((1, D_MLSTM))]
    args = [q, k, v, gates_t, sig_o, gate_b, norm_g]
    if not zero_init:
        in_specs += [c_spec, n_spec, m_spec]
        args += list(state)
    if fuse_out:
        in_specs += [tile(D_MODEL), tile(D_ATT), _const_spec((D_ATT + D_MLSTM, D_MODEL)),
                     _const_spec((1, D_MODEL))]
        args += list(out_proj)
        y_spec, y_shape = tile(D_MODEL), jax.ShapeDtypeStruct((B, S, D_MODEL), F32)
    else:
        y_spec, y_shape = tile(D_MLSTM), jax.ShapeDtypeStruct((B, S, D_MLSTM), BF16)
    return pl.pallas_call(
        functools.partial(_mlstm_kernel, zero_init=zero_init, scans_given=grows == 2 * SUBLANES,
                          fuse_out=fuse_out),
        grid=(B // nb, nc),
        in_specs=in_specs,
        out_specs=(y_spec, c_spec, n_spec, m_spec),
        out_shape=(y_shape,
                   jax.ShapeDtypeStruct((B, nh, dh, dh), F32),
                   jax.ShapeDtypeStruct((B, nh, dh), F32),
                   jax.ShapeDtypeStruct((B, SUBLANES, LANES), F32)),
        scratch_shapes=[pltpu.VMEM((nb, nh, dh, 2 * dh), F32),
                        pltpu.VMEM((nb, SUBLANES, LANES), F32)],
        compiler_params=_params(("arbitrary", "arbitrary")),
        name="mlstm_prompt" if zero_init else "mlstm_sample",
    )(*args)


def _outproj(x2, ya, yb, w_out, final_g):
    rows = x2.shape[0]
    tm = min(OUTPROJ_TILE, rows)
    tile = lambda width: pl.BlockSpec((tm, width), lambda i: (i, 0))
    return pl.pallas_call(
        _outproj_kernel,
        grid=(rows // tm,),
        in_specs=[tile(D_MODEL), tile(D_ATT), tile(D_MLSTM),
                  _const_spec((D_ATT + D_MLSTM, D_MODEL)), _const_spec((1, D_MODEL))],
        out_specs=tile(D_MODEL),
        out_shape=jax.ShapeDtypeStruct((rows, D_MODEL), F32),
        compiler_params=_params(("arbitrary",)),
        name="outproj",
    )(x2, ya, yb, w_out, final_g)


def kernel(x_prompt, x_sample, cache_win_k, cache_win_v, state_conv, state_C, state_n, state_m,
           norm_g, w_in, conv_w, conv_b, b_i, b_f, mlstm_norm_g, w_out, final_norm_g):
    assert w_in.shape[0] == 1, "single-layer model"
    B, S, D = x_prompt.shape
    DB, T, _ = x_sample.shape
    HB, DK = N_HEADS_MLSTM, HEAD_DIM_MLSTM
    wb = cache_win_k.shape[2]
    assert S % ROW_TILE == 0 and S == 16 * N_BACK and wb >= 16 * N_BACK and CONV_WIDTH - 1 <= T <= SUBLANES and DB * T == LANES

    w_pad = jnp.pad(w_in[0], ((0, 0), (0, N_IN_PAD - N_IN))).astype(BF16)
    w_o = w_out[0].astype(BF16)
    g_in = norm_g[0][None, :]
    cw, cb = conv_w[0], conv_b[0][None, :]
    bif = jnp.pad(jnp.concatenate([b_i[0], b_f[0]]), (0, LANES - 2 * HB))[None, :]
    ng = mlstm_norm_g[0][None, :]
    g_fin = final_norm_g[None, :]

    (q_p, k_p, v_p, pk, pv, ga_p, qm_p, km_p, vb_p, so_p, gb_p, gt_p, p_conv) = _inproj_prompt(
        x_prompt, g_in, w_pad, cw, cb, bif)
    ya_p = _attn_prompt(q_p, k_p, v_p, ga_p)
    y_prompt, c_p, n_p, m_p = _mlstm(qm_p, km_p, vb_p, gt_p, so_p, gb_p, ng,
                                     out_proj=(x_prompt, ya_p, w_o, g_fin))

    sc = state_conv[0]
    zrow = jnp.zeros((DB, 1, 2 * D_MLSTM), F32)
    hist = []
    for sh in (1, 2, 3):
        rows_ = [sc[:, CONV_WIDTH - 1 + t - sh:CONV_WIDTH + t - sh] if t < sh else zrow
                 for t in range(T)]
        hist.append(jnp.concatenate(rows_, axis=1).reshape(DB * T, 2 * D_MLSTM))
    (q_s, kn, vn, knt, vnt, ga_s, qm_s, km_s, vb_s, so_s, gb_s, gates_s, qk_s) = _inproj_sample(
        x_sample.reshape(DB * T, D), g_in, w_pad, cw, cb, bif, hist, T)
    r3 = lambda a: a.reshape(DB, T, a.shape[-1])
    pad8 = lambda a: jnp.pad(r3(a), ((0, 0), (0, SUBLANES - T), (0, 0)))
    to_hdp = lambda c: jnp.transpose(c[0], (0, 2, 3, 1))
    from_hdp = lambda c: jnp.transpose(c, (0, 3, 1, 2))[None]
    ya_s, s_k, s_v = _attn_sample(pad8(q_s), pad8(kn), pad8(vn), knt, vnt, pad8(ga_s),
                                  to_hdp(cache_win_k), to_hdp(cache_win_v), T)
    ya_s = ya_s[:, :T]

    pad_t = lambda a: jnp.pad(r3(a), ((0, 0), (0, CHUNK - T), (0, 0)))
    g3 = r3(gates_s)[:, :, :SUBLANES]
    null_gate = jnp.concatenate([jnp.full((HB,), NEG_INF, F32), jnp.zeros((HB,), F32)])
    gt_s = jnp.concatenate([g3, jnp.broadcast_to(null_gate, (DB, CHUNK - T, SUBLANES))], axis=1)
    gt_s = gt_s.transpose(0, 2, 1)
    m0 = jnp.broadcast_to(jnp.pad(state_m[0], ((0, 0), (0, SUBLANES - HB)))[..., None],
                          (DB, SUBLANES, LANES))
    yb_s, c_s, n_s, m_s = _mlstm(pad_t(qm_s), pad_t(km_s), pad_t(vb_s), gt_s, pad_t(so_s),
                                 pad_t(gb_s), ng, state=(state_C[0], state_n[0], m0))
    y_sample = _outproj(x_sample.reshape(DB * T, D), ya_s.reshape(DB * T, D_ATT),
                        yb_s[:, :T].reshape(DB * T, D_MLSTM), w_o, g_fin).reshape(DB, T, D)

    return (y_prompt, y_sample,
            from_hdp(pk), from_hdp(pv), p_conv[None],
            c_p[None], n_p[None], m_p[None, :, :HB, 0],
            from_hdp(s_k), from_hdp(s_v), r3(qk_s)[None, :, T - (CONV_WIDTH - 1):],
            c_s[None], n_s[None], m_s[None, :, :HB, 0])
```

```python
import functools
import math

import jax
import jax.numpy as jnp
import numpy as np
from jax import lax
from jax.experimental import pallas as pl
from jax.experimental.pallas import tpu as pltpu

F32 = jnp.float32
BF16 = jnp.bfloat16

D_MODEL = 1024
D_ATT = 512
N_HEADS_ATT = 8
HEAD_DIM_ATT = 64
D_MLSTM = 512
N_HEADS_MLSTM = 4
HEAD_DIM_MLSTM = 128
ROT_DIM = 16
ROPE_THETA = 500000.0
PAST_LEN = 16384
DILATIONS = (1, 4, 16)
N_BACK = 128
CONV_WIDTH = 4
CHUNK = 128
EPS = 1e-6
NEG_INF = -1e30
LOG2_E = math.log2(math.e)

LANES = 128
SUBLANES = 8
OFF_QA, OFF_KA, OFF_VA, OFF_ZA = 0, 512, 1024, 1536
OFF_QB, OFF_VB, OFF_OB, OFF_ZB, OFF_G = 2048, 3072, 3584, 4096, 4608
ROW_TILE = 512
INPROJ_SUBTILE = 128
OUTPROJ_TILE = 2048
OUTPROJ_SUBTILE = 256
MLSTM_BATCH = 4
VMEM_LIMIT = 56 * 1024 * 1024


def _silu(x):
    return x * jax.nn.sigmoid(x)


def _rmsnorm(x, g):
    return x * lax.rsqrt(jnp.mean(x * x, axis=-1, keepdims=True) + EPS) * g


def _rope(u, cos, sin_lo, sin_hi):
    outs = []
    for c in range(u.shape[1] // LANES):
        xs = u[:, c * LANES:(c + 1) * LANES]
        outs.append(xs * cos + pltpu.roll(xs, LANES - ROT_DIM // 2, 1) * sin_lo
                    + pltpu.roll(xs, ROT_DIM // 2, 1) * sin_hi)
    return jnp.concatenate(outs, axis=1)


def _gate_block(ug, bif):
    gz = ug + bif
    lane = lax.broadcasted_iota(jnp.int32, gz.shape, 1)
    logf = jnp.minimum(gz, 0.0) - jnp.log1p(jnp.exp(-jnp.abs(gz)))
    return jnp.where(lane < N_HEADS_MLSTM, gz, logf)


def _segments(hn, w_ref, wg_ref):
    def seg(off, width):
        rhs = wg_ref[...] if off == OFF_G else w_ref[:, off:off + width]
        return jnp.dot(hn, rhs, preferred_element_type=F32)
    return seg


def _inproj_common(seg, cos, sin_lo, sin_hi, q_ref, ga_ref, vb_ref, so_ref, gb_ref):
    q = _rope(seg(OFF_QA, D_ATT), cos, sin_lo, sin_hi) * (HEAD_DIM_ATT ** -0.5 * LOG2_E)
    q_ref[...] = q.astype(BF16).reshape(q_ref.shape)
    k = _rope(seg(OFF_KA, D_ATT), cos, sin_lo, sin_hi)
    v = seg(OFF_VA, D_ATT)
    ga_ref[...] = _silu(seg(OFF_ZA, D_ATT)).astype(BF16).reshape(ga_ref.shape)
    vb_ref[...] = seg(OFF_VB, D_MLSTM).astype(BF16).reshape(vb_ref.shape)
    so_ref[...] = jax.nn.sigmoid(seg(OFF_OB, D_MLSTM)).astype(BF16).reshape(so_ref.shape)
    gb_ref[...] = _silu(seg(OFF_ZB, D_MLSTM)).astype(BF16).reshape(gb_ref.shape)
    return k, v


def _inproj_prompt_kernel(x_ref, g_ref, w_ref, wg_ref, cw_ref, cb_ref, bif_ref, cos_ref, slo_ref,
                          shi_ref,
                          q_ref, kb_ref, vbf_ref, pk_ref, pv_ref, ga_ref, qm_ref, km_ref, vb_ref,
                          so_ref, gb_ref, gt_ref, pconv_ref, xp_ref):
    j = pl.program_id(1)
    tm = x_ref.shape[1]
    sub = INPROJ_SUBTILE
    assert CHUNK == LANES and tm % sub == 0 and sub % CHUNK == 0

    @pl.when(j == 0)
    def _():
        xp_ref[0:SUBLANES, :] = jnp.zeros((SUBLANES, 2 * D_MLSTM), F32)

    @pl.when(j > 0)
    def _():
        xp_ref[0:SUBLANES, :] = xp_ref[tm:tm + SUBLANES, :]

    for r0 in range(0, tm, sub):
        rs = pl.ds(r0, sub)
        part = lambda ref: ref.at[0, rs, :]
        hn = _rmsnorm(x_ref[0, rs, :], g_ref[...]).astype(BF16)
        seg = _segments(hn, w_ref, wg_ref)

        gates = _gate_block(seg(OFF_G, LANES), bif_ref[...])
        for i in range(sub // CHUNK):
            cs = slice(r0 + i * CHUNK, r0 + (i + 1) * CHUNK)
            gt = gates[i * CHUNK:(i + 1) * CHUNK, :].T[0:SUBLANES, :]
            gt_ref[0, 0:SUBLANES, cs] = gt
            gt_ref[0, SUBLANES:2 * SUBLANES, cs] = _gate_scans(gt)

        k, v = _inproj_common(seg, cos_ref[rs, :], slo_ref[rs, :], shi_ref[rs, :],
                              part(q_ref), part(ga_ref), part(vb_ref), part(so_ref), part(gb_ref))
        kb_ref[0, rs, :] = k.astype(BF16)
        vbf_ref[0, rs, :] = v.astype(BF16)
        pk_ref[0, :, :, r0:r0 + sub] = k.T.reshape(N_HEADS_ATT, HEAD_DIM_ATT, sub)
        pv_ref[0, :, :, r0:r0 + sub] = v.T.reshape(N_HEADS_ATT, HEAD_DIM_ATT, sub)

        base = SUBLANES + r0
        xp_ref[base:base + sub, :] = seg(OFF_QB, 2 * D_MLSTM)
        y = cb_ref[...] + xp_ref[base:base + sub, :] * cw_ref[3:4, :]
        for jj in range(CONV_WIDTH - 1):
            sh = CONV_WIDTH - 1 - jj
            y = y + xp_ref[base - sh:base - sh + sub, :] * cw_ref[jj:jj + 1, :]
        y = _silu(y)
        qm_ref[0, rs, :] = y[:, :D_MLSTM].astype(BF16)
        km_ref[0, rs, :] = (y[:, D_MLSTM:] * (HEAD_DIM_MLSTM ** -0.5)).astype(BF16)

    @pl.when(j == pl.num_programs(1) - 1)
    def _():
        pconv_ref[0] = xp_ref[tm + SUBLANES - (CONV_WIDTH - 1):tm + SUBLANES, :]


def _inproj_sample_kernel(x_ref, g_ref, w_ref, wg_ref, cw_ref, cb_ref, bif_ref, cos_ref, slo_ref,
                          shi_ref,
                          h1_ref, h2_ref, h3_ref,
                          q_ref, kn_ref, vn_ref, knt_ref, vnt_ref, ga_ref, qm_ref, km_ref, vb_ref,
                          so_ref, gb_ref, gates_ref, qk_ref, *, t_new):
    hn = _rmsnorm(x_ref[...], g_ref[...]).astype(BF16)
    seg = _segments(hn, w_ref, wg_ref)
    k, v = _inproj_common(seg, cos_ref[...], slo_ref[...], shi_ref[...],
                          q_ref, ga_ref, vb_ref, so_ref, gb_ref)
    kn_ref[...] = k
    vn_ref[...] = v
    knt_ref[...] = k.T
    vnt_ref[...] = v.T
    u = seg(OFF_QB, 2 * D_MLSTM)
    qk_ref[...] = u
    t = lax.rem(lax.broadcasted_iota(jnp.int32, u.shape, 0), t_new)
    y = cb_ref[...] + u * cw_ref[3:4, :]
    for sh, h_ref in ((1, h1_ref), (2, h2_ref), (3, h3_ref)):
        prev = jnp.where(t >= sh, pltpu.roll(u, sh, 0), h_ref[...])
        y = y + prev * cw_ref[3 - sh:4 - sh, :]
    y = _silu(y)
    qm_ref[...] = y[:, :D_MLSTM].astype(BF16)
    km_ref[...] = (y[:, D_MLSTM:] * (HEAD_DIM_MLSTM ** -0.5)).astype(BF16)
    gates_ref[...] = _gate_block(seg(OFF_G, LANES), bif_ref[...])


def _attn_prompt_kernel(q_ref, k_ref, v_ref, g_ref, o_ref,
                        src1_ref, src4_ref, src16_ref, bias_ref, st16r_ref, st16_ref, st4_ref):
    seq = q_ref.shape[1]
    n16 = seq // 16
    group = 8
    head_a_full = lax.broadcasted_iota(jnp.int32, (seq, LANES), 1) < HEAD_DIM_ATT
    by16 = lambda x: jnp.swapaxes(x.reshape(n16, 16, LANES), 0, 1)
    def stack(q, k, v, is_a):
        zero, one = jnp.zeros((), q.dtype), jnp.ones((), q.dtype)
        return (jnp.where(is_a, q, zero), jnp.where(is_a, zero, q), k,
                jnp.where(is_a, v, one), jnp.where(is_a, one, v))

    q, k, v = q_ref[0], k_ref[0], v_ref[0]
    for i, x in enumerate(stack(q, k, v, head_a_full)):
        src1_ref[i] = x
    q, k, v = q.astype(F32), k.astype(F32), v.astype(F32)
    for i, x in enumerate(stack(q, k, v, head_a_full)):
        src4_ref[i] = x
    head_a_16 = lax.broadcasted_iota(jnp.int32, (16, n16, LANES), 2) < HEAD_DIM_ATT
    for i, x in enumerate(stack(by16(q).astype(BF16), by16(k).astype(BF16), by16(v).astype(BF16),
                                head_a_16)):
        src16_ref[i] = x
    u = lax.broadcasted_iota(jnp.int32, (N_BACK, 2 * N_BACK), 0)
    w = lax.broadcasted_iota(jnp.int32, (N_BACK, 2 * N_BACK), 1)
    bias_ref[...] = jnp.where((w >= u) & (w <= u + N_BACK), jnp.finfo(F32).max, NEG_INF)
    head_a = lax.broadcasted_iota(jnp.int32, (N_BACK, LANES), 1) < HEAD_DIM_ATT

    def partials(gets):
        staged = []
        for get, has_prev in gets:
            if has_prev:
                both = lambda i, get=get: jnp.concatenate([get(i, True), get(i, False)], axis=0)
                bias = bias_ref[...]
            else:
                both = lambda i, get=get: get(i, False)
                bias = bias_ref[:, N_BACK:]
            kk = both(2)
            heads = []
            for qi in (0, 1):
                s = lax.dot_general(get(qi, False), kk, (((1,), (1,)), ((), ())),
                                    preferred_element_type=F32)
                s = jnp.minimum(s, bias)
                mh = jnp.max(s, axis=-1, keepdims=True)
                heads.append((mh, jnp.exp2(s - mh).astype(BF16)))
            staged.append((both, heads))
        out = []
        for both, ((m_a, p_a), (m_b, p_b)) in staged:
            pv_a = jnp.dot(p_a, both(3), preferred_element_type=F32)
            pv_b = jnp.dot(p_b, both(4), preferred_element_type=F32)
            acc = jnp.where(head_a, pv_a, pv_b)
            den = pltpu.roll(jnp.where(head_a, pv_b, pv_a), HEAD_DIM_ATT, 1)
            out.append((jnp.where(head_a, m_a, m_b), den, acc))
        return out

    def get4(qstart, kprev_start):
        def get(i, prev):
            start = kprev_start if prev else qstart
            return src4_ref[i, pl.ds(start, N_BACK, stride=4), :].astype(BF16)
        return get, kprev_start is not None

    def get1(qstart, kprev_start):
        def get(i, prev):
            return src1_ref[i, pl.ds(kprev_start if prev else qstart, N_BACK), :]
        return get, kprev_start is not None

    def keep4(blocks):
        res = partials([get4(qs, ks) for qs, ks in blocks])
        for (qs, _), (m, den, acc) in zip(blocks, res):
            rows = pl.ds(qs, N_BACK, stride=4)
            st4_ref[0, rows, :] = m
            st4_ref[1, rows, :] = den
            st4_ref[2, rows, :] = acc

    def finish(blocks):
        res = partials([get1(qs, ks) for qs, ks in blocks])
        for (qs, _), part in zip(blocks, res):
            rows = pl.ds(qs, N_BACK)
            parts = [part] + [tuple(st[i, rows, :] for i in range(3)) for st in (st4_ref, st16_ref)]
            m_all = functools.reduce(jnp.maximum, [pt[0] for pt in parts])
            wts = [jnp.exp2(pt[0] - m_all) for pt in parts]
            den = sum(wt * pt[1] for wt, pt in zip(wts, parts))
            num = sum(wt * pt[2] for wt, pt in zip(wts, parts))
            o_ref[0, rows, :] = (num / den * g_ref[0, rows, :].astype(F32)).astype(BF16)

    def body16(g, c):
        rs = [g * group + rr for rr in range(group)]
        res = partials([(lambda i, prev, r=r: src16_ref[i, r], False) for r in rs])
        for r, part in zip(rs, res):
            for i in range(3):
                st16r_ref[i, r] = part[i]
        return c
    lax.fori_loop(0, 16 // group, body16, 0)
    for i in range(3):
        st16_ref[i] = jnp.swapaxes(st16r_ref[i], 0, 1).reshape(seq, LANES)

    blocks4 = lambda cc: [(r + 4 * N_BACK * cc, r + 4 * N_BACK * (cc - 1)) for r in range(4)]
    keep4([(r, None) for r in range(4)] + blocks4(1))

    def body4(g, c):
        keep4(blocks4(2 * g) + blocks4(2 * g + 1))
        return c
    lax.fori_loop(1, seq // 4 // N_BACK // 2, body4, 0)

    finish([(0, None)] + [(cc * N_BACK, (cc - 1) * N_BACK) for cc in range(1, group)])

    def body1(g, c):
        starts = [pl.multiple_of((g * group + rr) * N_BACK, N_BACK) for rr in range(group)]
        finish([(st, st - N_BACK) for st in starts])
        return c
    lax.fori_loop(1, seq // N_BACK // group, body1, 0)


def _attn_sample_kernel(q_ref, kn_ref, vn_ref, knt_ref, vnt_ref, g_ref, ck_ref, cv_ref,
                        o_ref, sk_ref, sv_ref, clamp_ref, *, t_new):
    b = pl.program_id(0)
    wb = ck_ref.shape[3]
    hd = HEAD_DIM_ATT
    rows = q_ref.shape[1]

    @pl.when(b == 0)
    def _():
        delta = (wb + lax.broadcasted_iota(jnp.int32, (rows, wb), 0)
                 - lax.broadcasted_iota(jnp.int32, (rows, wb), 1))
        for d, dil in enumerate(DILATIONS):
            ok = ((delta & (dil - 1)) == 0) & (delta >= dil) & (delta <= N_BACK * dil)
            clamp_ref[d] = jnp.where(ok, jnp.finfo(F32).max, NEG_INF)

    tq = lax.broadcasted_iota(jnp.int32, (rows, rows), 0)
    tk = lax.broadcasted_iota(jnp.int32, (rows, rows), 1)
    new_ok = [((tk <= tq) if dil == 1 else (tk == tq)) & (tk < t_new) for dil in DILATIONS]

    lane = lax.broadcasted_iota(jnp.int32, (hd, LANES), 1)
    shift_new = (LANES - t_new) - b * t_new
    nt = (((1,), (1,)), ((), ()))
    outs = []
    for h in range(N_HEADS_ATT):
        hs = slice(h * hd, (h + 1) * hd)
        kt = ck_ref[0, h]
        vt = cv_ref[0, h]
        for old, new_ref, out_ref in ((kt, knt_ref, sk_ref), (vt, vnt_ref, sv_ref)):
            moved = pltpu.roll(old, wb - t_new, axis=1)
            new_cols = pltpu.roll(new_ref[hs, :], shift_new, axis=1)
            out_ref[0, h, :, 0:wb - LANES] = moved[:, 0:wb - LANES]
            out_ref[0, h, :, wb - LANES:wb] = jnp.where(lane < LANES - t_new,
                                                        moved[:, wb - LANES:wb], new_cols)
        qh = q_ref[0, :, hs]
        kn_h = kn_ref[0, :, hs].astype(BF16)
        vn_h = vn_ref[0, :, hs].astype(BF16)
        s_old = jnp.dot(qh, kt.astype(BF16), preferred_element_type=F32)
        s_new = lax.dot_general(qh, kn_h, nt, preferred_element_type=F32)
        ps, pes, ms = [], [], []
        for d in range(len(DILATIONS)):
            so = jnp.minimum(s_old, clamp_ref[d])
            sn = jnp.where(new_ok[d], s_new, NEG_INF)
            m = jnp.maximum(jnp.max(so, axis=-1, keepdims=True), jnp.max(sn, axis=-1, keepdims=True))
            ps.append(jnp.exp2(so - m))
            pes.append(jnp.exp2(sn - m))
            ms.append(m)
        acc = lax.dot_general(jnp.concatenate(ps, axis=0).astype(BF16), vt.astype(BF16), nt,
                              preferred_element_type=F32)
        acc = acc + jnp.dot(jnp.concatenate(pes, axis=0).astype(BF16), vn_h,
                            preferred_element_type=F32)
        m_all = functools.reduce(jnp.maximum, ms)
        den = 0.0
        num = 0.0
        for d in range(len(DILATIONS)):
            wgt = jnp.exp2(ms[d] - m_all)
            den = den + wgt * (jnp.sum(ps[d], axis=-1, keepdims=True)
                               + jnp.sum(pes[d], axis=-1, keepdims=True))
            num = num + wgt * acc[d * rows:(d + 1) * rows]
        outs.append(num / den)
    att = jnp.concatenate(outs, axis=1)
    o_ref[0] = (att * g_ref[0].astype(F32)).astype(BF16)


def _scan_lanes(x, op, fill):
    lane = lax.broadcasted_iota(jnp.int32, x.shape, 1)
    d = 1
    while d < x.shape[1]:
        x = op(x, jnp.where(lane >= d, pltpu.roll(x, d, 1), fill))
        d *= 2
    return x


def _gate_scans(gt):
    b = _scan_lanes(pltpu.roll(gt, N_HEADS_MLSTM, 0), jnp.add, 0.0)
    cm = _scan_lanes(gt - b, jnp.maximum, NEG_INF)
    row = lax.broadcasted_iota(jnp.int32, gt.shape, 0)
    return jnp.where(row < N_HEADS_MLSTM, b, pltpu.roll(cm, N_HEADS_MLSTM, 0))


def _mlstm_kernel(*refs, zero_init, scans_given, fuse_out):
    refs = list(refs)
    q_ref, k_ref, v_ref, gt_ref, so_ref, gb_ref, ng_ref = refs[:7]
    del refs[:7]
    if not zero_init:
        c0_ref, n0_ref, m0_ref = refs[:3]
        del refs[:3]
    if fuse_out:
        x_ref, ya_ref, wo_ref, fg_ref = refs[:4]
        del refs[:4]
    y_ref, c_out_ref, n_out_ref, m_out_ref, cn_ref, m_ref = refs
    c_idx = pl.program_id(1)
    nb = q_ref.shape[0]
    L = CHUNK
    dh = HEAD_DIM_MLSTM
    nh = N_HEADS_MLSTM

    @pl.when(c_idx == 0)
    def _():
        if zero_init:
            cn_ref[...] = jnp.zeros(cn_ref.shape, F32)
            m_ref[...] = jnp.zeros(m_ref.shape, F32)
        else:
            m_ref[...] = m0_ref[...]
            for bb in range(nb):
                for h in range(nh):
                    n_rows = jnp.broadcast_to(n0_ref[bb, h:h + 1, :], (dh, dh))
                    cn_ref[bb, h] = jnp.concatenate([c0_ref[bb, h], n_rows.T], axis=1)

    tri_t = lax.broadcasted_iota(jnp.int32, (L, L), 0)
    tri_s = lax.broadcasted_iota(jnp.int32, (L, L), 1)
    causal = tri_t >= tri_s
    ones_blk = jnp.ones((L, dh), BF16)
    nt = (((1,), (1,)), ((), ()))
    pairs = [(bb, h) for bb in range(nb) for h in range(nh)]
    sl = lambda h: slice(h * dh, (h + 1) * dh)

    qk = {(bb, h): lax.dot_general(q_ref[bb, :, sl(h)], k_ref[bb, :, sl(h)], nt,
                                   preferred_element_type=F32) for bb, h in pairs}
    cn_old = {p: cn_ref[p[0], p[1]] for p in pairs}
    qc = {(bb, h): jnp.dot(q_ref[bb, :, sl(h)], cn_old[bb, h].astype(BF16),
                           preferred_element_type=F32) for bb, h in pairs}
    if fuse_out:
        mix_a = x_ref[...].reshape(nb * L, D_MODEL) + jnp.dot(
            ya_ref[...].reshape(nb * L, D_ATT), wo_ref[0:D_ATT, :], preferred_element_type=F32)

    rows, cols, decays, w_rows = [], [], [], []
    for bb in range(nb):
        if scans_given:
            i_row = gt_ref[bb, 0:SUBLANES, :]
            sc = gt_ref[bb, SUBLANES:2 * SUBLANES, :]
        else:
            i_row = gt_ref[bb]
            sc = _gate_scans(i_row)
        b = sc
        cm = pltpu.roll(sc, nh, 0)
        m_prev = m_ref[bb]
        m_t = jnp.maximum(m_prev + b, b + cm)
        inter = jnp.exp(m_prev + b - m_t)
        m_last = jnp.broadcast_to(m_t[:, L - 1:L], m_t.shape)
        b_last = jnp.broadcast_to(b[:, L - 1:L], b.shape)
        decay = jnp.exp(m_prev + b_last - m_last)
        w_row = jnp.exp(b_last - b + i_row - m_last)
        m_ref[bb] = m_last
        stack = jnp.concatenate([b - m_t, inter, jnp.exp(-m_t),
                                 jnp.zeros((L - 3 * SUBLANES, L), F32)], axis=0)
        cols.append(stack.T)
        rows.append(i_row - b)
        w_rows.append(w_row)
        decays.append(jnp.concatenate([decay, decay], axis=1))
    col = lambda bb, kind, h: cols[bb][:, kind * SUBLANES + h:kind * SUBLANES + h + 1]

    sqk = {}
    for bb, h in pairs:
        dlog = col(bb, 0, h) + rows[bb][h:h + 1, :]
        sqk[bb, h] = (qk[bb, h] * jnp.exp(jnp.where(causal, dlog, NEG_INF))).astype(BF16)
    v_one = {(bb, h): jnp.concatenate([v_ref[bb, :, sl(h)], ones_blk], axis=1) for bb, h in pairs}
    pv = {p: jnp.dot(sqk[p], v_one[p], preferred_element_type=F32) for p in pairs}
    yb = {}
    for bb, h in pairs:
        tot = col(bb, 1, h) * qc[bb, h] + pv[bb, h]
        hh = tot[:, :dh] / jnp.maximum(jnp.abs(tot[:, dh:]), col(bb, 2, h))
        hh = so_ref[bb, :, sl(h)].astype(F32) * hh
        hh = hh * lax.rsqrt(jnp.mean(hh * hh, axis=-1, keepdims=True) + EPS)
        hh = hh * ng_ref[:, sl(h)]
        yb[bb, h] = (hh * gb_ref[bb, :, sl(h)].astype(F32)).astype(BF16)
    if fuse_out:
        yb_all = jnp.concatenate([jnp.concatenate([yb[bb, h] for h in range(nh)], axis=1)
                                  for bb in range(nb)], axis=0)
        mix = mix_a + jnp.dot(yb_all, wo_ref[D_ATT:, :], preferred_element_type=F32)
        res = _rmsnorm(mix, fg_ref[...])
        y_ref[...] = res.reshape(nb, L, D_MODEL)
    else:
        for bb, h in pairs:
            y_ref[bb, :, sl(h)] = yb[bb, h]
    for bb, h in pairs:
        kt_w = k_ref[bb, :, sl(h)].T.astype(F32) * w_rows[bb][h:h + 1, :]
        upd = jnp.dot(kt_w.astype(BF16), v_one[bb, h], preferred_element_type=F32)
        cn_ref[bb, h] = decays[bb][h:h + 1, :] * cn_old[bb, h] + upd

    @pl.when(c_idx == pl.num_programs(1) - 1)
    def _():
        m_out_ref[...] = m_ref[...]
        for bb in range(nb):
            n_rows = []
            for h in range(nh):
                cn = cn_ref[bb, h]
                c_out_ref[bb, h] = cn[:, :dh]
                n_rows.append(cn[:, dh:].T[0:1, :])
            n_out_ref[bb] = jnp.concatenate(n_rows, axis=0)


def _outproj_kernel(x_ref, ya_ref, yb_ref, w_ref, g_ref, o_ref):
    rows = x_ref.shape[0]
    sub = min(OUTPROJ_SUBTILE, rows)
    for r0 in range(0, rows, sub):
        rs = pl.ds(r0, sub)
        mix = jnp.dot(ya_ref[rs, :], w_ref[0:D_ATT, :], preferred_element_type=F32)
        mix = mix + jnp.dot(yb_ref[rs, :], w_ref[D_ATT:, :], preferred_element_type=F32)
        o_ref[rs, :] = _rmsnorm(x_ref[rs, :] + mix, g_ref[...])


def _rope_tables(pos):
    half = ROT_DIM // 2
    inv = ROPE_THETA ** (-jnp.arange(half, dtype=F32) * 2.0 / ROT_DIM)
    ang = pos.astype(F32)[:, None] * inv[None, :]
    cos, sin = jnp.cos(ang), jnp.sin(ang)
    n = pos.shape[0]
    one = jnp.ones((n, HEAD_DIM_ATT - ROT_DIM), F32)
    zero = jnp.zeros((n, HEAD_DIM_ATT - ROT_DIM), F32)
    zh = jnp.zeros((n, half), F32)
    cos_t = jnp.concatenate([cos, cos, one], axis=1)
    lo_t = jnp.concatenate([-sin, zh, zero], axis=1)
    hi_t = jnp.concatenate([zh, sin, zero], axis=1)
    rep = lambda t: jnp.concatenate([t, t], axis=1)
    return rep(cos_t), rep(lo_t), rep(hi_t)


def _params(sem):
    return pltpu.CompilerParams(dimension_semantics=sem, vmem_limit_bytes=VMEM_LIMIT)


def _const_spec(shape):
    return pl.BlockSpec(shape, lambda *_: (0,) * len(shape))


def _inproj_prompt(x, norm_g, w_main, w_gate, conv_w, conv_b, bif):
    B, S, _ = x.shape
    tm = ROW_TILE
    cos, lo, hi = _rope_tables(jnp.arange(S, dtype=jnp.int32))
    tile = lambda width: pl.BlockSpec((1, tm, width), lambda b, j: (b, j, 0))
    tab = pl.BlockSpec((tm, LANES), lambda b, j: (j, 0))
    bf = lambda: jax.ShapeDtypeStruct((B, S, D_ATT), BF16)
    f5 = lambda: jax.ShapeDtypeStruct((B, N_HEADS_ATT, HEAD_DIM_ATT, S), F32)
    tile5 = pl.BlockSpec((1, N_HEADS_ATT, HEAD_DIM_ATT, tm), lambda b, j: (b, 0, 0, j))
    out_shape = (bf(), bf(), bf(), f5(), f5(), bf(), bf(), bf(), bf(), bf(), bf(),
                 jax.ShapeDtypeStruct((B, 2 * SUBLANES, S), F32),
                 jax.ShapeDtypeStruct((B, CONV_WIDTH - 1, 2 * D_MLSTM), F32))
    out_specs = tuple([tile(D_ATT)] * 3 + [tile5] * 2 + [tile(D_ATT)] * 6) + (
        pl.BlockSpec((1, 2 * SUBLANES, tm), lambda b, j: (b, 0, j)),
        pl.BlockSpec((1, CONV_WIDTH - 1, 2 * D_MLSTM), lambda b, j: (b, 0, 0)))
    return pl.pallas_call(
        _inproj_prompt_kernel,
        grid=(B, S // tm),
        in_specs=[tile(D_MODEL), _const_spec((1, D_MODEL)), _const_spec(w_main.shape),
                  _const_spec(w_gate.shape),
                  _const_spec((CONV_WIDTH, 2 * D_MLSTM)), _const_spec((1, 2 * D_MLSTM)),
                  _const_spec((1, LANES)), tab, tab, tab],
        out_specs=out_specs,
        out_shape=out_shape,
        scratch_shapes=[pltpu.VMEM((tm + 2 * SUBLANES, 2 * D_MLSTM), F32)],
        compiler_params=_params(("arbitrary", "arbitrary")),
        name="inproj_prompt",
    )(x, norm_g, w_main, w_gate, conv_w, conv_b, bif, cos, lo, hi)


def _inproj_sample(x2, norm_g, w_main, w_gate, conv_w, conv_b, bif, hist, t_new):
    rows = x2.shape[0]
    pos = PAST_LEN + jnp.arange(t_new, dtype=jnp.int32)
    cos, lo, hi = (jnp.tile(t, (rows // t_new, 1)) for t in _rope_tables(pos))
    bf = lambda: jax.ShapeDtypeStruct((rows, D_ATT), BF16)
    f3 = lambda: jax.ShapeDtypeStruct((rows, D_ATT), F32)
    f5 = lambda: jax.ShapeDtypeStruct((D_ATT, rows), F32)
    out_shape = (bf(), f3(), f3(), f5(), f5(), bf(), bf(), bf(), bf(), bf(), bf(),
                 jax.ShapeDtypeStruct((rows, LANES), F32),
                 jax.ShapeDtypeStruct((rows, 2 * D_MLSTM), F32))
    return pl.pallas_call(
        functools.partial(_inproj_sample_kernel, t_new=t_new),
        out_shape=out_shape,
        compiler_params=pltpu.CompilerParams(vmem_limit_bytes=VMEM_LIMIT),
        name="inproj_sample",
    )(x2, norm_g, w_main, w_gate, conv_w, conv_b, bif, cos, lo, hi, *hist)


def _attn_prompt(q, k, v, gate):
    B, S, _ = q.shape
    spec = pl.BlockSpec((1, S, LANES), lambda b, h: (b, 0, h))
    return pl.pallas_call(
        _attn_prompt_kernel,
        grid=(B, D_ATT // LANES),
        in_specs=[spec, spec, spec, spec],
        out_specs=spec,
        out_shape=jax.ShapeDtypeStruct((B, S, D_ATT), BF16),
        scratch_shapes=[
            pltpu.VMEM((5, S, LANES), BF16), pltpu.VMEM((5, S, LANES), F32),
            pltpu.VMEM((5, 16, S // 16, LANES), BF16),
            pltpu.VMEM((N_BACK, 2 * N_BACK), F32),
            pltpu.VMEM((3, 16, S // 16, LANES), F32),
            pltpu.VMEM((3, S, LANES), F32), pltpu.VMEM((3, S, LANES), F32)],
        compiler_params=_params(("arbitrary", "arbitrary")),
        name="attn_prompt",
    )(q, k, v, gate)


def _attn_sample(q, kn, vn, knt, vnt, gate, ck, cv, t_new):
    B, rows, _ = q.shape
    wb = ck.shape[3]
    small = pl.BlockSpec((1, rows, D_ATT), lambda b: (b, 0, 0))
    big = pl.BlockSpec((1, N_HEADS_ATT, HEAD_DIM_ATT, wb), lambda b: (b, 0, 0, 0))
    win_shape = jax.ShapeDtypeStruct((B, N_HEADS_ATT, HEAD_DIM_ATT, wb), F32)
    return pl.pallas_call(
        functools.partial(_attn_sample_kernel, t_new=t_new),
        grid=(B,),
        in_specs=[small, small, small, _const_spec(knt.shape), _const_spec(vnt.shape), small, big, big],
        out_specs=(small, big, big),
        out_shape=(jax.ShapeDtypeStruct((B, rows, D_ATT), BF16), win_shape, win_shape),
        scratch_shapes=[pltpu.VMEM((len(DILATIONS), rows, wb), F32)],
        compiler_params=_params(("arbitrary",)),
        name="attn_sample",
    )(q, kn, vn, knt, vnt, gate, ck, cv)


def _mlstm(q, k, v, gates_t, sig_o, gate_b, norm_g, state=None, out_proj=None):
    B, S, _ = q.shape
    nc = S // CHUNK
    nb = MLSTM_BATCH
    nh, dh = N_HEADS_MLSTM, HEAD_DIM_MLSTM
    grows = gates_t.shape[1]
    tile = lambda width: pl.BlockSpec((nb, CHUNK, width), lambda b, c: (b, c, 0))
    gspec = pl.BlockSpec((nb, grows, CHUNK), lambda b, c: (b, 0, c))
    c_spec = pl.BlockSpec((nb, nh, dh, dh), lambda b, c: (b, 0, 0, 0))
    n_spec = pl.BlockSpec((nb, nh, dh), lambda b, c: (b, 0, 0))
    m_spec = pl.BlockSpec((nb, SUBLANES, LANES), lambda b, c: (b, 0, 0))
    zero_init = state is None
    fuse_out = out_proj is not None
    in_specs = [tile(D_MLSTM)] * 3 + [gspec] + [tile(D_MLSTM)] * 2 + [_const_spec((1, D_MLSTM))]
    args = [q, k, v, gates_t, sig_o, gate_b, norm_g]
    if not zero_init:
        in_specs += [c_spec, n_spec, m_spec]
        args += list(state)
    if fuse_out:
        in_specs += [tile(D_MODEL), tile(D_ATT), _const_spec((D_ATT + D_MLSTM, D_MODEL)),
                     _const_spec((1, D_MODEL))]
        args += list(out_proj)
        y_spec, y_shape = tile(D_MODEL), jax.ShapeDtypeStruct((B, S, D_MODEL), F32)
    else:
        y_spec, y_shape = tile(D_MLSTM), jax.ShapeDtypeStruct((B, S, D_MLSTM), BF16)
    return pl.pallas_call(
        functools.partial(_mlstm_kernel, zero_init=zero_init, scans_given=grows == 2 * SUBLANES,
                          fuse_out=fuse_out),
        grid=(B // nb, nc),
        in_specs=in_specs,
        out_specs=(y_spec, c_spec, n_spec, m_spec),
        out_shape=(y_shape,
                   jax.ShapeDtypeStruct((B, nh, dh, dh), F32),
                   jax.ShapeDtypeStruct((B, nh, dh), F32),
                   jax.ShapeDtypeStruct((B, SUBLANES, LANES), F32)),
        scratch_shapes=[pltpu.VMEM((nb, nh, dh, 2 * dh), F32),
                        pltpu.VMEM((nb, SUBLANES, LANES), F32)],
        compiler_params=_params(("arbitrary", "arbitrary")),
        name="mlstm_prompt" if zero_init else "mlstm_sample",
    )(*args)


def _outproj(x2, ya, yb, w_out, final_g):
    rows = x2.shape[0]
    tm = min(OUTPROJ_TILE, rows)
    tile = lambda width: pl.BlockSpec((tm, width), lambda i: (i, 0))
    return pl.pallas_call(
        _outproj_kernel,
        grid=(rows // tm,),
        in_specs=[tile(D_MODEL), tile(D_ATT), tile(D_MLSTM),
                  _const_spec((D_ATT + D_MLSTM, D_MODEL)), _const_spec((1, D_MODEL))],
        out_specs=tile(D_MODEL),
        out_shape=jax.ShapeDtypeStruct((rows, D_MODEL), F32),
        compiler_params=_params(("arbitrary",)),
        name="outproj",
    )(x2, ya, yb, w_out, final_g)


def kernel(x_prompt, x_sample, cache_win_k, cache_win_v, state_conv, state_C, state_n, state_m,
           norm_g, w_in, conv_w, conv_b, b_i, b_f, mlstm_norm_g, w_out, final_norm_g):
    assert w_in.shape[0] == 1, "single-layer model"
    B, S, D = x_prompt.shape
    DB, T, _ = x_sample.shape
    HB, DK = N_HEADS_MLSTM, HEAD_DIM_MLSTM
    wb = cache_win_k.shape[2]
    assert S % ROW_TILE == 0 and S == 16 * N_BACK and wb >= 16 * N_BACK and CONV_WIDTH - 1 <= T <= SUBLANES and DB * T == LANES

    w_main = w_in[0][:, :OFF_G].astype(BF16)
    w_gate = jnp.pad(w_in[0][:, OFF_G:], ((0, 0), (0, LANES - 2 * HB))).astype(BF16)
    w_o = w_out[0].astype(BF16)
    g_in = norm_g[0][None, :]
    cw, cb = conv_w[0], conv_b[0][None, :]
    bif = jnp.pad(jnp.concatenate([b_i[0], b_f[0]]), (0, LANES - 2 * HB))[None, :]
    ng = mlstm_norm_g[0][None, :]
    g_fin = final_norm_g[None, :]

    (q_p, k_p, v_p, pk, pv, ga_p, qm_p, km_p, vb_p, so_p, gb_p, gt_p, p_conv) = _inproj_prompt(
        x_prompt, g_in, w_main, w_gate, cw, cb, bif)
    ya_p = _attn_prompt(q_p, k_p, v_p, ga_p)
    y_prompt, c_p, n_p, m_p = _mlstm(qm_p, km_p, vb_p, gt_p, so_p, gb_p, ng,
                                     out_proj=(x_prompt, ya_p, w_o, g_fin))

    sc = state_conv[0]
    zrow = jnp.zeros((DB, 1, 2 * D_MLSTM), F32)
    hist = []
    for sh in (1, 2, 3):
        rows_ = [sc[:, CONV_WIDTH - 1 + t - sh:CONV_WIDTH + t - sh] if t < sh else zrow
                 for t in range(T)]
        hist.append(jnp.concatenate(rows_, axis=1).reshape(DB * T, 2 * D_MLSTM))
    (q_s, kn, vn, knt, vnt, ga_s, qm_s, km_s, vb_s, so_s, gb_s, gates_s, qk_s) = _inproj_sample(
        x_sample.reshape(DB * T, D), g_in, w_main, w_gate, cw, cb, bif, hist, T)
    r3 = lambda a: a.reshape(DB, T, a.shape[-1])
    pad8 = lambda a: jnp.pad(r3(a), ((0, 0), (0, SUBLANES - T), (0, 0)))
    to_hdp = lambda c: jnp.transpose(c[0], (0, 2, 3, 1))
    from_hdp = lambda c: jnp.transpose(c, (0, 3, 1, 2))[None]
    ya_s, s_k, s_v = _attn_sample(pad8(q_s), pad8(kn), pad8(vn), knt, vnt, pad8(ga_s),
                                  to_hdp(cache_win_k), to_hdp(cache_win_v), T)
    ya_s = ya_s[:, :T]

    pad_t = lambda a: jnp.pad(r3(a), ((0, 0), (0, CHUNK - T), (0, 0)))
    g3 = r3(gates_s)[:, :, :SUBLANES]
    null_gate = jnp.concatenate([jnp.full((HB,), NEG_INF, F32), jnp.zeros((HB,), F32)])
    gt_s = jnp.concatenate([g3, jnp.broadcast_to(null_gate, (DB, CHUNK - T, SUBLANES))], axis=1)
    gt_s = gt_s.transpose(0, 2, 1)
    m0 = jnp.broadcast_to(jnp.pad(state_m[0], ((0, 0), (0, SUBLANES - HB)))[..., None],
                          (DB, SUBLANES, LANES))
    yb_s, c_s, n_s, m_s = _mlstm(pad_t(qm_s), pad_t(km_s), pad_t(vb_s), gt_s, pad_t(so_s),
                                 pad_t(gb_s), ng, state=(state_C[0], state_n[0], m0))
    y_sample = _outproj(x_sample.reshape(DB * T, D), ya_s.reshape(DB * T, D_ATT),
                        yb_s[:, :T].reshape(DB * T, D_MLSTM), w_o, g_fin).reshape(DB, T, D)

    return (y_prompt, y_sample,
            from_hdp(pk), from_hdp(pv), p_conv[None],
            c_p[None], n_p[None], m_p[None, :, :HB, 0],
            from_hdp(s_k), from_hdp(s_v), r3(qk_s)[None, :, T - (CONV_WIDTH - 1):],
            c_s[None], n_s[None], m_s[None, :, :HB, 0])
```

```python
import functools
import math

import jax
import jax.numpy as jnp
import numpy as np
from jax import lax
from jax.experimental import pallas as pl
from jax.experimental.pallas import tpu as pltpu

F32 = jnp.float32
BF16 = jnp.bfloat16

D_MODEL = 1024
D_ATT = 512
N_HEADS_ATT = 8
HEAD_DIM_ATT = 64
D_MLSTM = 512
N_HEADS_MLSTM = 4
HEAD_DIM_MLSTM = 128
ROT_DIM = 16
ROPE_THETA = 500000.0
PAST_LEN = 16384
DILATIONS = (1, 4, 16)
N_BACK = 128
CONV_WIDTH = 4
CHUNK = 128
EPS = 1e-6
NEG_INF = -1e30
LOG2_E = math.log2(math.e)

LANES = 128
SUBLANES = 8
OFF_QA, OFF_KA, OFF_VA, OFF_ZA = 0, 512, 1024, 1536
OFF_QB, OFF_VB, OFF_OB, OFF_ZB, OFF_G = 2048, 3072, 3584, 4096, 4608
ROW_TILE = 512
INPROJ_SUBTILE = 128
OUTPROJ_TILE = 2048
OUTPROJ_SUBTILE = 256
MLSTM_BATCH = 4
VMEM_LIMIT = 56 * 1024 * 1024


def _silu(x):
    return x * jax.nn.sigmoid(x)


def _rmsnorm(x, g):
    return x * lax.rsqrt(jnp.mean(x * x, axis=-1, keepdims=True) + EPS) * g


def _rope(u, cos, sin_lo, sin_hi):
    outs = []
    for c in range(u.shape[1] // LANES):
        xs = u[:, c * LANES:(c + 1) * LANES]
        outs.append(xs * cos + pltpu.roll(xs, LANES - ROT_DIM // 2, 1) * sin_lo
                    + pltpu.roll(xs, ROT_DIM // 2, 1) * sin_hi)
    return jnp.concatenate(outs, axis=1)


def _gate_block(ug, bif):
    gz = ug + bif
    lane = lax.broadcasted_iota(jnp.int32, gz.shape, 1)
    logf = jnp.minimum(gz, 0.0) - jnp.log1p(jnp.exp(-jnp.abs(gz)))
    return jnp.where(lane < N_HEADS_MLSTM, gz, logf)


def _segments(hn, w_ref, wg_ref):
    def seg(off, width):
        rhs = wg_ref[...] if off == OFF_G else w_ref[:, off:off + width]
        return jnp.dot(hn, rhs, preferred_element_type=F32)
    return seg


def _inproj_common(seg, cos, sin_lo, sin_hi, q_ref, ga_ref, vb_ref, so_ref, gb_ref):
    q = _rope(seg(OFF_QA, D_ATT), cos, sin_lo, sin_hi) * (HEAD_DIM_ATT ** -0.5 * LOG2_E)
    q_ref[...] = q.astype(BF16).reshape(q_ref.shape)
    k = _rope(seg(OFF_KA, D_ATT), cos, sin_lo, sin_hi)
    v = seg(OFF_VA, D_ATT)
    ga_ref[...] = _silu(seg(OFF_ZA, D_ATT)).astype(BF16).reshape(ga_ref.shape)
    vb_ref[...] = seg(OFF_VB, D_MLSTM).astype(BF16).reshape(vb_ref.shape)
    so_ref[...] = jax.nn.sigmoid(seg(OFF_OB, D_MLSTM)).astype(BF16).reshape(so_ref.shape)
    gb_ref[...] = _silu(seg(OFF_ZB, D_MLSTM)).astype(BF16).reshape(gb_ref.shape)
    return k, v


def _inproj_prompt_kernel(x_ref, g_ref, w_ref, wg_ref, cw_ref, cb_ref, bif_ref, cos_ref, slo_ref,
                          shi_ref,
                          q_ref, kb_ref, vbf_ref, pk_ref, pv_ref, ga_ref, qm_ref, km_ref, vb_ref,
                          so_ref, gb_ref, gt_ref, pconv_ref, xp_ref):
    j = pl.program_id(1)
    tm = x_ref.shape[1]
    sub = INPROJ_SUBTILE
    assert CHUNK == LANES and tm % sub == 0 and sub % CHUNK == 0

    @pl.when(j == 0)
    def _():
        xp_ref[0:SUBLANES, :] = jnp.zeros((SUBLANES, 2 * D_MLSTM), F32)

    @pl.when(j > 0)
    def _():
        xp_ref[0:SUBLANES, :] = xp_ref[tm:tm + SUBLANES, :]

    for r0 in range(0, tm, sub):
        rs = pl.ds(r0, sub)
        part = lambda ref: ref.at[0, rs, :]
        hn = _rmsnorm(x_ref[0, rs, :], g_ref[...]).astype(BF16)
        seg = _segments(hn, w_ref, wg_ref)

        gates = _gate_block(seg(OFF_G, LANES), bif_ref[...])
        for i in range(sub // CHUNK):
            cs = slice(r0 + i * CHUNK, r0 + (i + 1) * CHUNK)
            gt = gates[i * CHUNK:(i + 1) * CHUNK, :].T[0:SUBLANES, :]
            gt_ref[0, 0:SUBLANES, cs] = gt
            gt_ref[0, SUBLANES:2 * SUBLANES, cs] = _gate_scans(gt)

        k, v = _inproj_common(seg, cos_ref[rs, :], slo_ref[rs, :], shi_ref[rs, :],
                              part(q_ref), part(ga_ref), part(vb_ref), part(so_ref), part(gb_ref))
        kb_ref[0, rs, :] = k.astype(BF16)
        vbf_ref[0, rs, :] = v.astype(BF16)
        pk_ref[0, :, :, r0:r0 + sub] = k.T.reshape(N_HEADS_ATT, HEAD_DIM_ATT, sub)
        pv_ref[0, :, :, r0:r0 + sub] = v.T.reshape(N_HEADS_ATT, HEAD_DIM_ATT, sub)

        base = SUBLANES + r0
        xp_ref[base:base + sub, :] = seg(OFF_QB, 2 * D_MLSTM)
        y = cb_ref[...] + xp_ref[base:base + sub, :] * cw_ref[3:4, :]
        for jj in range(CONV_WIDTH - 1):
            sh = CONV_WIDTH - 1 - jj
            y = y + xp_ref[base - sh:base - sh + sub, :] * cw_ref[jj:jj + 1, :]
        y = _silu(y)
        qm_ref[0, rs, :] = y[:, :D_MLSTM].astype(BF16)
        km_ref[0, rs, :] = (y[:, D_MLSTM:] * (HEAD_DIM_MLSTM ** -0.5)).astype(BF16)

    @pl.when(j == pl.num_programs(1) - 1)
    def _():
        pconv_ref[0] = xp_ref[tm + SUBLANES - (CONV_WIDTH - 1):tm + SUBLANES, :]


def _inproj_sample_kernel(x_ref, g_ref, w_ref, wg_ref, cw_ref, cb_ref, bif_ref, cos_ref, slo_ref,
                          shi_ref,
                          h1_ref, h2_ref, h3_ref,
                          q_ref, kn_ref, vn_ref, knt_ref, vnt_ref, ga_ref, qm_ref, km_ref, vb_ref,
                          so_ref, gb_ref, gates_ref, qk_ref, *, t_new):
    hn = _rmsnorm(x_ref[...], g_ref[...]).astype(BF16)
    seg = _segments(hn, w_ref, wg_ref)
    k, v = _inproj_common(seg, cos_ref[...], slo_ref[...], shi_ref[...],
                          q_ref, ga_ref, vb_ref, so_ref, gb_ref)
    kn_ref[...] = k
    vn_ref[...] = v
    knt_ref[...] = k.T
    vnt_ref[...] = v.T
    u = seg(OFF_QB, 2 * D_MLSTM)
    qk_ref[...] = u
    t = lax.rem(lax.broadcasted_iota(jnp.int32, u.shape, 0), t_new)
    y = cb_ref[...] + u * cw_ref[3:4, :]
    for sh, h_ref in ((1, h1_ref), (2, h2_ref), (3, h3_ref)):
        prev = jnp.where(t >= sh, pltpu.roll(u, sh, 0), h_ref[...])
        y = y + prev * cw_ref[3 - sh:4 - sh, :]
    y = _silu(y)
    qm_ref[...] = y[:, :D_MLSTM].astype(BF16)
    km_ref[...] = (y[:, D_MLSTM:] * (HEAD_DIM_MLSTM ** -0.5)).astype(BF16)
    gates_ref[...] = _gate_block(seg(OFF_G, LANES), bif_ref[...])


def _attn_prompt_kernel(q_ref, k_ref, v_ref, g_ref, o_ref,
                        src1_ref, src4_ref, src16_ref, bias_ref, st16r_ref, st16_ref, st4_ref):
    seq = q_ref.shape[1]
    n16 = seq // 16
    group = 8
    head_a_full = lax.broadcasted_iota(jnp.int32, (seq, LANES), 1) < HEAD_DIM_ATT
    by16 = lambda x: jnp.swapaxes(x.reshape(n16, 16, LANES), 0, 1)
    def stack(q, k, v, is_a):
        zero, one = jnp.zeros((), q.dtype), jnp.ones((), q.dtype)
        return (jnp.where(is_a, q, zero), jnp.where(is_a, zero, q), k,
                jnp.where(is_a, v, one), jnp.where(is_a, one, v))

    q, k, v = q_ref[0], k_ref[0], v_ref[0]
    for i, x in enumerate(stack(q, k, v, head_a_full)):
        src1_ref[i] = x
    q, k, v = q.astype(F32), k.astype(F32), v.astype(F32)
    for i, x in enumerate(stack(q, k, v, head_a_full)):
        src4_ref[i] = x
    head_a_16 = lax.broadcasted_iota(jnp.int32, (16, n16, LANES), 2) < HEAD_DIM_ATT
    for i, x in enumerate(stack(by16(q).astype(BF16), by16(k).astype(BF16), by16(v).astype(BF16),
                                head_a_16)):
        src16_ref[i] = x
    u = lax.broadcasted_iota(jnp.int32, (N_BACK, 2 * N_BACK), 0)
    w = lax.broadcasted_iota(jnp.int32, (N_BACK, 2 * N_BACK), 1)
    bias_ref[...] = jnp.where((w >= u) & (w <= u + N_BACK), jnp.finfo(F32).max, NEG_INF)
    head_a = lax.broadcasted_iota(jnp.int32, (N_BACK, LANES), 1) < HEAD_DIM_ATT

    def partials(gets):
        staged = []
        for get, has_prev in gets:
            if has_prev:
                both = lambda i, get=get: jnp.concatenate([get(i, True), get(i, False)], axis=0)
                bias = bias_ref[...]
            else:
                both = lambda i, get=get: get(i, False)
                bias = bias_ref[:, N_BACK:]
            kk = both(2)
            heads = []
            for qi in (0, 1):
                s = lax.dot_general(get(qi, False), kk, (((1,), (1,)), ((), ())),
                                    preferred_element_type=F32)
                s = jnp.minimum(s, bias)
                mh = jnp.max(s, axis=-1, keepdims=True)
                heads.append((mh, jnp.exp2(s - mh).astype(BF16)))
            staged.append((both, heads))
        out = []
        for both, ((m_a, p_a), (m_b, p_b)) in staged:
            pv_a = jnp.dot(p_a, both(3), preferred_element_type=F32)
            pv_b = jnp.dot(p_b, both(4), preferred_element_type=F32)
            acc = jnp.where(head_a, pv_a, pv_b)
            den = pltpu.roll(jnp.where(head_a, pv_b, pv_a), HEAD_DIM_ATT, 1)
            out.append((jnp.where(head_a, m_a, m_b), den, acc))
        return out

    def get4(qstart, kprev_start):
        def get(i, prev):
            start = kprev_start if prev else qstart
            return src4_ref[i, pl.ds(start, N_BACK, stride=4), :].astype(BF16)
        return get, kprev_start is not None

    def get1(qstart, kprev_start):
        def get(i, prev):
            return src1_ref[i, pl.ds(kprev_start if prev else qstart, N_BACK), :]
        return get, kprev_start is not None

    def keep4(blocks):
        res = partials([get4(qs, ks) for qs, ks in blocks])
        for (qs, _), (m, den, acc) in zip(blocks, res):
            rows = pl.ds(qs, N_BACK, stride=4)
            st4_ref[0, rows, :] = m
            st4_ref[1, rows, :] = den
            st4_ref[2, rows, :] = acc

    def finish(blocks):
        res = partials([get1(qs, ks) for qs, ks in blocks])
        for (qs, _), part in zip(blocks, res):
            rows = pl.ds(qs, N_BACK)
            parts = [part] + [tuple(st[i, rows, :] for i in range(3)) for st in (st4_ref, st16_ref)]
            m_all = functools.reduce(jnp.maximum, [pt[0] for pt in parts])
            wts = [jnp.exp2(pt[0] - m_all) for pt in parts]
            den = sum(wt * pt[1] for wt, pt in zip(wts, parts))
            num = sum(wt * pt[2] for wt, pt in zip(wts, parts))
            o_ref[0, rows, :] = (num / den * g_ref[0, rows, :].astype(F32)).astype(BF16)

    def body16(g, c):
        rs = [g * group + rr for rr in range(group)]
        res = partials([(lambda i, prev, r=r: src16_ref[i, r], False) for r in rs])
        for r, part in zip(rs, res):
            for i in range(3):
                st16r_ref[i, r] = part[i]
        return c
    lax.fori_loop(0, 16 // group, body16, 0)
    for i in range(3):
        st16_ref[i] = jnp.swapaxes(st16r_ref[i], 0, 1).reshape(seq, LANES)

    blocks4 = lambda cc: [(r + 4 * N_BACK * cc, r + 4 * N_BACK * (cc - 1)) for r in range(4)]
    keep4([(r, None) for r in range(4)] + blocks4(1))

    def body4(g, c):
        keep4(blocks4(2 * g) + blocks4(2 * g + 1))
        return c
    lax.fori_loop(1, seq // 4 // N_BACK // 2, body4, 0)

    finish([(0, None)] + [(cc * N_BACK, (cc - 1) * N_BACK) for cc in range(1, group)])

    def body1(g, c):
        starts = [pl.multiple_of((g * group + rr) * N_BACK, N_BACK) for rr in range(group)]
        finish([(st, st - N_BACK) for st in starts])
        return c
    lax.fori_loop(1, seq // N_BACK // group, body1, 0)


def _attn_sample_kernel(q_ref, kn_ref, vn_ref, knt_ref, vnt_ref, g_ref, ck_ref, cv_ref,
                        o_ref, sk_ref, sv_ref, clamp_ref, *, t_new):
    b = pl.program_id(0)
    wb = ck_ref.shape[3]
    hd = HEAD_DIM_ATT
    rows = q_ref.shape[1]

    @pl.when(b == 0)
    def _():
        delta = (wb + lax.broadcasted_iota(jnp.int32, (rows, wb), 0)
                 - lax.broadcasted_iota(jnp.int32, (rows, wb), 1))
        for d, dil in enumerate(DILATIONS):
            ok = ((delta & (dil - 1)) == 0) & (delta >= dil) & (delta <= N_BACK * dil)
            clamp_ref[d] = jnp.where(ok, jnp.finfo(F32).max, NEG_INF)

    tq = lax.broadcasted_iota(jnp.int32, (rows, rows), 0)
    tk = lax.broadcasted_iota(jnp.int32, (rows, rows), 1)
    new_ok = [((tk <= tq) if dil == 1 else (tk == tq)) & (tk < t_new) for dil in DILATIONS]

    lane = lax.broadcasted_iota(jnp.int32, (hd, LANES), 1)
    shift_new = (LANES - t_new) - b * t_new
    nt = (((1,), (1,)), ((), ()))
    outs = []
    for h in range(N_HEADS_ATT):
        hs = slice(h * hd, (h + 1) * hd)
        kt = ck_ref[0, h]
        vt = cv_ref[0, h]
        for old, new_ref, out_ref in ((kt, knt_ref, sk_ref), (vt, vnt_ref, sv_ref)):
            moved = pltpu.roll(old, wb - t_new, axis=1)
            new_cols = pltpu.roll(new_ref[hs, :], shift_new, axis=1)
            out_ref[0, h, :, 0:wb - LANES] = moved[:, 0:wb - LANES]
            out_ref[0, h, :, wb - LANES:wb] = jnp.where(lane < LANES - t_new,
                                                        moved[:, wb - LANES:wb], new_cols)
        qh = q_ref[0, :, hs]
        kn_h = kn_ref[0, :, hs].astype(BF16)
        vn_h = vn_ref[0, :, hs].astype(BF16)
        s_old = jnp.dot(qh, kt.astype(BF16), preferred_element_type=F32)
        s_new = lax.dot_general(qh, kn_h, nt, preferred_element_type=F32)
        ps, pes, ms = [], [], []
        for d in range(len(DILATIONS)):
            so = jnp.minimum(s_old, clamp_ref[d])
            sn = jnp.where(new_ok[d], s_new, NEG_INF)
            m = jnp.maximum(jnp.max(so, axis=-1, keepdims=True), jnp.max(sn, axis=-1, keepdims=True))
            ps.append(jnp.exp2(so - m))
            pes.append(jnp.exp2(sn - m))
            ms.append(m)
        acc = lax.dot_general(jnp.concatenate(ps, axis=0).astype(BF16), vt.astype(BF16), nt,
                              preferred_element_type=F32)
        acc = acc + jnp.dot(jnp.concatenate(pes, axis=0).astype(BF16), vn_h,
                            preferred_element_type=F32)
        m_all = functools.reduce(jnp.maximum, ms)
        den = 0.0
        num = 0.0
        for d in range(len(DILATIONS)):
            wgt = jnp.exp2(ms[d] - m_all)
            den = den + wgt * (jnp.sum(ps[d], axis=-1, keepdims=True)
                               + jnp.sum(pes[d], axis=-1, keepdims=True))
            num = num + wgt * acc[d * rows:(d + 1) * rows]
        outs.append(num / den)
    att = jnp.concatenate(outs, axis=1)
    o_ref[0] = (att * g_ref[0].astype(F32)).astype(BF16)


def _scan_lanes(x, op, fill):
    lane = lax.broadcasted_iota(jnp.int32, x.shape, 1)
    d = 1
    while d < x.shape[1]:
        x = op(x, jnp.where(lane >= d, pltpu.roll(x, d, 1), fill))
        d *= 2
    return x


def _gate_scans(gt):
    b = _scan_lanes(pltpu.roll(gt, N_HEADS_MLSTM, 0), jnp.add, 0.0)
    cm = _scan_lanes(gt - b, jnp.maximum, NEG_INF)
    row = lax.broadcasted_iota(jnp.int32, gt.shape, 0)
    return jnp.where(row < N_HEADS_MLSTM, b, pltpu.roll(cm, N_HEADS_MLSTM, 0))


def _mlstm_kernel(*refs, zero_init, scans_given, fuse_out):
    refs = list(refs)
    q_ref, k_ref, v_ref, gt_ref, so_ref, gb_ref, ng_ref = refs[:7]
    del refs[:7]
    if not zero_init:
        c0_ref, n0_ref, m0_ref = refs[:3]
        del refs[:3]
    if fuse_out:
        x_ref, ya_ref, wo_ref, fg_ref = refs[:4]
        del refs[:4]
    y_ref, c_out_ref, n_out_ref, m_out_ref, cn_ref, m_ref = refs
    c_idx = pl.program_id(1)
    nb = q_ref.shape[0]
    L = CHUNK
    dh = HEAD_DIM_MLSTM
    nh = N_HEADS_MLSTM

    @pl.when(c_idx == 0)
    def _():
        if zero_init:
            cn_ref[...] = jnp.zeros(cn_ref.shape, F32)
            m_ref[...] = jnp.zeros(m_ref.shape, F32)
        else:
            m_ref[...] = m0_ref[...]
            for bb in range(nb):
                for h in range(nh):
                    n_rows = jnp.broadcast_to(n0_ref[bb, h:h + 1, :], (dh, dh))
                    cn_ref[bb, h] = jnp.concatenate([c0_ref[bb, h], n_rows.T], axis=1)

    tri_t = lax.broadcasted_iota(jnp.int32, (L, L), 0)
    tri_s = lax.broadcasted_iota(jnp.int32, (L, L), 1)
    causal = tri_t >= tri_s
    ones_blk = jnp.ones((L, dh), BF16)
    nt = (((1,), (1,)), ((), ()))
    pairs = [(bb, h) for bb in range(nb) for h in range(nh)]
    sl = lambda h: slice(h * dh, (h + 1) * dh)

    qk = {(bb, h): lax.dot_general(q_ref[bb, :, sl(h)], k_ref[bb, :, sl(h)], nt,
                                   preferred_element_type=F32) for bb, h in pairs}
    cn_old = {p: cn_ref[p[0], p[1]] for p in pairs}
    if fuse_out:
        mix_a = x_ref[...].reshape(nb * L, D_MODEL) + jnp.dot(
            ya_ref[...].reshape(nb * L, D_ATT), wo_ref[0:D_ATT, :], preferred_element_type=F32)

    rows, cols, decays, w_rows = [], [], [], []
    for bb in range(nb):
        if scans_given:
            i_row = gt_ref[bb, 0:SUBLANES, :]
            sc = gt_ref[bb, SUBLANES:2 * SUBLANES, :]
        else:
            i_row = gt_ref[bb]
            sc = _gate_scans(i_row)
        b = sc
        cm = pltpu.roll(sc, nh, 0)
        m_prev = m_ref[bb]
        m_t = jnp.maximum(m_prev + b, b + cm)
        inter = jnp.exp(m_prev + b - m_t)
        m_last = jnp.broadcast_to(m_t[:, L - 1:L], m_t.shape)
        b_last = jnp.broadcast_to(b[:, L - 1:L], b.shape)
        decay = jnp.exp(m_prev + b_last - m_last)
        w_row = jnp.exp(b_last - b + i_row - m_last)
        m_ref[bb] = m_last
        stack = jnp.concatenate([b - m_t, inter, jnp.exp(-m_t),
                                 jnp.zeros((L - 3 * SUBLANES, L), F32)], axis=0)
        cols.append(stack.T)
        rows.append(i_row - b)
        w_rows.append(w_row)
        decays.append(jnp.concatenate([decay, decay], axis=1))
    col = lambda bb, kind, h: cols[bb][:, kind * SUBLANES + h:kind * SUBLANES + h + 1]

    sqk = {}
    for bb, h in pairs:
        dlog = col(bb, 0, h) + rows[bb][h:h + 1, :]
        sqk[bb, h] = (qk[bb, h] * jnp.exp(jnp.where(causal, dlog, NEG_INF))).astype(BF16)
    v_one = {(bb, h): jnp.concatenate([v_ref[bb, :, sl(h)], ones_blk], axis=1) for bb, h in pairs}
    tots = {}
    for bb, h in pairs:
        q_dec = (col(bb, 1, h) * q_ref[bb, :, sl(h)].astype(F32)).astype(BF16)
        lhs = jnp.concatenate([sqk[bb, h], q_dec], axis=1)
        rhs = jnp.concatenate([v_one[bb, h], cn_old[bb, h].astype(BF16)], axis=0)
        tots[bb, h] = jnp.dot(lhs, rhs, preferred_element_type=F32)
    yb = {}
    for bb, h in pairs:
        tot = tots[bb, h]
        hh = tot[:, :dh] / jnp.maximum(jnp.abs(tot[:, dh:]), col(bb, 2, h))
        hh = so_ref[bb, :, sl(h)].astype(F32) * hh
        hh = hh * lax.rsqrt(jnp.mean(hh * hh, axis=-1, keepdims=True) + EPS)
        hh = hh * ng_ref[:, sl(h)]
        yb[bb, h] = (hh * gb_ref[bb, :, sl(h)].astype(F32)).astype(BF16)
    if fuse_out:
        yb_all = jnp.concatenate([jnp.concatenate([yb[bb, h] for h in range(nh)], axis=1)
                                  for bb in range(nb)], axis=0)
        mix = mix_a + jnp.dot(yb_all, wo_ref[D_ATT:, :], preferred_element_type=F32)
        res = _rmsnorm(mix, fg_ref[...])
        y_ref[...] = res.reshape(nb, L, D_MODEL)
    else:
        for bb, h in pairs:
            y_ref[bb, :, sl(h)] = yb[bb, h]
    for bb, h in pairs:
        kt_w = k_ref[bb, :, sl(h)].T.astype(F32) * w_rows[bb][h:h + 1, :]
        upd = jnp.dot(kt_w.astype(BF16), v_one[bb, h], preferred_element_type=F32)
        cn_ref[bb, h] = decays[bb][h:h + 1, :] * cn_old[bb, h] + upd

    @pl.when(c_idx == pl.num_programs(1) - 1)
    def _():
        m_out_ref[...] = m_ref[...]
        for bb in range(nb):
            n_rows = []
            for h in range(nh):
                cn = cn_ref[bb, h]
                c_out_ref[bb, h] = cn[:, :dh]
                n_rows.append(cn[:, dh:].T[0:1, :])
            n_out_ref[bb] = jnp.concatenate(n_rows, axis=0)


def _outproj_kernel(x_ref, ya_ref, yb_ref, w_ref, g_ref, o_ref):
    rows = x_ref.shape[0]
    sub = min(OUTPROJ_SUBTILE, rows)
    for r0 in range(0, rows, sub):
        rs = pl.ds(r0, sub)
        mix = jnp.dot(ya_ref[rs, :], w_ref[0:D_ATT, :], preferred_element_type=F32)
        mix = mix + jnp.dot(yb_ref[rs, :], w_ref[D_ATT:, :], preferred_element_type=F32)
        o_ref[rs, :] = _rmsnorm(x_ref[rs, :] + mix, g_ref[...])


def _rope_tables(pos):
    half = ROT_DIM // 2
    inv = ROPE_THETA ** (-jnp.arange(half, dtype=F32) * 2.0 / ROT_DIM)
    ang = pos.astype(F32)[:, None] * inv[None, :]
    cos, sin = jnp.cos(ang), jnp.sin(ang)
    n = pos.shape[0]
    one = jnp.ones((n, HEAD_DIM_ATT - ROT_DIM), F32)
    zero = jnp.zeros((n, HEAD_DIM_ATT - ROT_DIM), F32)
    zh = jnp.zeros((n, half), F32)
    cos_t = jnp.concatenate([cos, cos, one], axis=1)
    lo_t = jnp.concatenate([-sin, zh, zero], axis=1)
    hi_t = jnp.concatenate([zh, sin, zero], axis=1)
    rep = lambda t: jnp.concatenate([t, t], axis=1)
    return rep(cos_t), rep(lo_t), rep(hi_t)


def _params(sem):
    return pltpu.CompilerParams(dimension_semantics=sem, vmem_limit_bytes=VMEM_LIMIT)


def _const_spec(shape):
    return pl.BlockSpec(shape, lambda *_: (0,) * len(shape))


def _inproj_prompt(x, norm_g, w_main, w_gate, conv_w, conv_b, bif):
    B, S, _ = x.shape
    tm = ROW_TILE
    cos, lo, hi = _rope_tables(jnp.arange(S, dtype=jnp.int32))
    tile = lambda width: pl.BlockSpec((1, tm, width), lambda b, j: (b, j, 0))
    tab = pl.BlockSpec((tm, LANES), lambda b, j: (j, 0))
    bf = lambda: jax.ShapeDtypeStruct((B, S, D_ATT), BF16)
    f5 = lambda: jax.ShapeDtypeStruct((B, N_HEADS_ATT, HEAD_DIM_ATT, S), F32)
    tile5 = pl.BlockSpec((1, N_HEADS_ATT, HEAD_DIM_ATT, tm), lambda b, j: (b, 0, 0, j))
    out_shape = (bf(), bf(), bf(), f5(), f5(), bf(), bf(), bf(), bf(), bf(), bf(),
                 jax.ShapeDtypeStruct((B, 2 * SUBLANES, S), F32),
                 jax.ShapeDtypeStruct((B, CONV_WIDTH - 1, 2 * D_MLSTM), F32))
    out_specs = tuple([tile(D_ATT)] * 3 + [tile5] * 2 + [tile(D_ATT)] * 6) + (
        pl.BlockSpec((1, 2 * SUBLANES, tm), lambda b, j: (b, 0, j)),
        pl.BlockSpec((1, CONV_WIDTH - 1, 2 * D_MLSTM), lambda b, j: (b, 0, 0)))
    return pl.pallas_call(
        _inproj_prompt_kernel,
        grid=(B, S // tm),
        in_specs=[tile(D_MODEL), _const_spec((1, D_MODEL)), _const_spec((D_MODEL, OFF_G)),
                  _const_spec(w_gate.shape),
                  _const_spec((CONV_WIDTH, 2 * D_MLSTM)), _const_spec((1, 2 * D_MLSTM)),
                  _const_spec((1, LANES)), tab, tab, tab],
        out_specs=out_specs,
        out_shape=out_shape,
        scratch_shapes=[pltpu.VMEM((tm + 2 * SUBLANES, 2 * D_MLSTM), F32)],
        compiler_params=_params(("arbitrary", "arbitrary")),
        name="inproj_prompt",
    )(x, norm_g, w_main, w_gate, conv_w, conv_b, bif, cos, lo, hi)


def _inproj_sample(x2, norm_g, w_main, w_gate, conv_w, conv_b, bif, hist, t_new):
    rows = x2.shape[0]
    pos = PAST_LEN + jnp.arange(t_new, dtype=jnp.int32)
    cos, lo, hi = (jnp.tile(t, (rows // t_new, 1)) for t in _rope_tables(pos))
    bf = lambda: jax.ShapeDtypeStruct((rows, D_ATT), BF16)
    f3 = lambda: jax.ShapeDtypeStruct((rows, D_ATT), F32)
    f5 = lambda: jax.ShapeDtypeStruct((D_ATT, rows), F32)
    out_shape = (bf(), f3(), f3(), f5(), f5(), bf(), bf(), bf(), bf(), bf(), bf(),
                 jax.ShapeDtypeStruct((rows, LANES), F32),
                 jax.ShapeDtypeStruct((rows, 2 * D_MLSTM), F32))
    return pl.pallas_call(
        functools.partial(_inproj_sample_kernel, t_new=t_new),
        out_shape=out_shape,
        compiler_params=pltpu.CompilerParams(vmem_limit_bytes=VMEM_LIMIT),
        name="inproj_sample",
    )(x2, norm_g, w_main, w_gate, conv_w, conv_b, bif, cos, lo, hi, *hist)


def _attn_prompt(q, k, v, gate):
    B, S, _ = q.shape
    spec = pl.BlockSpec((1, S, LANES), lambda b, h: (b, 0, h))
    return pl.pallas_call(
        _attn_prompt_kernel,
        grid=(B, D_ATT // LANES),
        in_specs=[spec, spec, spec, spec],
        out_specs=spec,
        out_shape=jax.ShapeDtypeStruct((B, S, D_ATT), BF16),
        scratch_shapes=[
            pltpu.VMEM((5, S, LANES), BF16), pltpu.VMEM((5, S, LANES), F32),
            pltpu.VMEM((5, 16, S // 16, LANES), BF16),
            pltpu.VMEM((N_BACK, 2 * N_BACK), F32),
            pltpu.VMEM((3, 16, S // 16, LANES), F32),
            pltpu.VMEM((3, S, LANES), F32), pltpu.VMEM((3, S, LANES), F32)],
        compiler_params=_params(("arbitrary", "arbitrary")),
        name="attn_prompt",
    )(q, k, v, gate)


def _attn_sample(q, kn, vn, knt, vnt, gate, ck, cv, t_new):
    B, rows, _ = q.shape
    wb = ck.shape[3]
    small = pl.BlockSpec((1, rows, D_ATT), lambda b: (b, 0, 0))
    big = pl.BlockSpec((1, N_HEADS_ATT, HEAD_DIM_ATT, wb), lambda b: (b, 0, 0, 0))
    win_shape = jax.ShapeDtypeStruct((B, N_HEADS_ATT, HEAD_DIM_ATT, wb), F32)
    return pl.pallas_call(
        functools.partial(_attn_sample_kernel, t_new=t_new),
        grid=(B,),
        in_specs=[small, small, small, _const_spec(knt.shape), _const_spec(vnt.shape), small, big, big],
        out_specs=(small, big, big),
        out_shape=(jax.ShapeDtypeStruct((B, rows, D_ATT), BF16), win_shape, win_shape),
        scratch_shapes=[pltpu.VMEM((len(DILATIONS), rows, wb), F32)],
        compiler_params=_params(("arbitrary",)),
        name="attn_sample",
    )(q, kn, vn, knt, vnt, gate, ck, cv)


def _mlstm(q, k, v, gates_t, sig_o, gate_b, norm_g, state=None, out_proj=None):
    B, S, _ = q.shape
    nc = S // CHUNK
    nb = MLSTM_BATCH
    nh, dh = N_HEADS_MLSTM, HEAD_DIM_MLSTM
    grows = gates_t.shape[1]
    tile = lambda width: pl.BlockSpec((nb, CHUNK, width), lambda b, c: (b, c, 0))
    gspec = pl.BlockSpec((nb, grows, CHUNK), lambda b, c: (b, 0, c))
    c_spec = pl.BlockSpec((nb, nh, dh, dh), lambda b, c: (b, 0, 0, 0))
    n_spec = pl.BlockSpec((nb, nh, dh), lambda b, c: (b, 0, 0))
    m_spec = pl.BlockSpec((nb, SUBLANES, LANES), lambda b, c: (b, 0, 0))
    zero_init = state is None
    fuse_out = out_proj is not None
    in_specs = [tile(D_MLSTM)] * 3 + [gspec] + [tile(D_MLSTM)] * 2 + [_const_spec((1, D_MLSTM))]
    args = [q, k, v, gates_t, sig_o, gate_b, norm_g]
    if not zero_init:
        in_specs += [c_spec, n_spec, m_spec]
        args += list(state)
    if fuse_out:
        in_specs += [tile(D_MODEL), tile(D_ATT), _const_spec((D_ATT + D_MLSTM, D_MODEL)),
                     _const_spec((1, D_MODEL))]
        args += list(out_proj)
        y_spec, y_shape = tile(D_MODEL), jax.ShapeDtypeStruct((B, S, D_MODEL), F32)
    else:
        y_spec, y_shape = tile(D_MLSTM), jax.ShapeDtypeStruct((B, S, D_MLSTM), BF16)
    return pl.pallas_call(
        functools.partial(_mlstm_kernel, zero_init=zero_init, scans_given=grows == 2 * SUBLANES,
                          fuse_out=fuse_out),
        grid=(B // nb, nc),
        in_specs=in_specs,
        out_specs=(y_spec, c_spec, n_spec, m_spec),
        out_shape=(y_shape,
                   jax.ShapeDtypeStruct((B, nh, dh, dh), F32),
                   jax.ShapeDtypeStruct((B, nh, dh), F32),
                   jax.ShapeDtypeStruct((B, SUBLANES, LANES), F32)),
        scratch_shapes=[pltpu.VMEM((nb, nh, dh, 2 * dh), F32),
                        pltpu.VMEM((nb, SUBLANES, LANES), F32)],
        compiler_params=_params(("arbitrary", "arbitrary")),
        name="mlstm_prompt" if zero_init else "mlstm_sample",
    )(*args)


def _outproj(x2, ya, yb, w_out, final_g):
    rows = x2.shape[0]
    tm = min(OUTPROJ_TILE, rows)
    tile = lambda width: pl.BlockSpec((tm, width), lambda i: (i, 0))
    return pl.pallas_call(
        _outproj_kernel,
        grid=(rows // tm,),
        in_specs=[tile(D_MODEL), tile(D_ATT), tile(D_MLSTM),
                  _const_spec((D_ATT + D_MLSTM, D_MODEL)), _const_spec((1, D_MODEL))],
        out_specs=tile(D_MODEL),
        out_shape=jax.ShapeDtypeStruct((rows, D_MODEL), F32),
        compiler_params=_params(("arbitrary",)),
        name="outproj",
    )(x2, ya, yb, w_out, final_g)


def kernel(x_prompt, x_sample, cache_win_k, cache_win_v, state_conv, state_C, state_n, state_m,
           norm_g, w_in, conv_w, conv_b, b_i, b_f, mlstm_norm_g, w_out, final_norm_g):
    assert w_in.shape[0] == 1, "single-layer model"
    B, S, D = x_prompt.shape
    DB, T, _ = x_sample.shape
    HB, DK = N_HEADS_MLSTM, HEAD_DIM_MLSTM
    wb = cache_win_k.shape[2]
    assert S % ROW_TILE == 0 and S == 16 * N_BACK and wb >= 16 * N_BACK and CONV_WIDTH - 1 <= T <= SUBLANES and DB * T == LANES

    w_main = w_in[0].astype(BF16)
    w_gate = jnp.pad(w_main[:, OFF_G:], ((0, 0), (0, LANES - 2 * HB)))
    w_o = w_out[0].astype(BF16)
    g_in = norm_g[0][None, :]
    cw, cb = conv_w[0], conv_b[0][None, :]
    bif = jnp.pad(jnp.concatenate([b_i[0], b_f[0]]), (0, LANES - 2 * HB))[None, :]
    ng = mlstm_norm_g[0][None, :]
    g_fin = final_norm_g[None, :]

    (q_p, k_p, v_p, pk, pv, ga_p, qm_p, km_p, vb_p, so_p, gb_p, gt_p, p_conv) = _inproj_prompt(
        x_prompt, g_in, w_main, w_gate, cw, cb, bif)
    ya_p = _attn_prompt(q_p, k_p, v_p, ga_p)
    y_prompt, c_p, n_p, m_p = _mlstm(qm_p, km_p, vb_p, gt_p, so_p, gb_p, ng,
                                     out_proj=(x_prompt, ya_p, w_o, g_fin))

    sc = state_conv[0]
    zrow = jnp.zeros((DB, 1, 2 * D_MLSTM), F32)
    hist = []
    for sh in (1, 2, 3):
        rows_ = [sc[:, CONV_WIDTH - 1 + t - sh:CONV_WIDTH + t - sh] if t < sh else zrow
                 for t in range(T)]
        hist.append(jnp.concatenate(rows_, axis=1).reshape(DB * T, 2 * D_MLSTM))
    (q_s, kn, vn, knt, vnt, ga_s, qm_s, km_s, vb_s, so_s, gb_s, gates_s, qk_s) = _inproj_sample(
        x_sample.reshape(DB * T, D), g_in, w_main, w_gate, cw, cb, bif, hist, T)
    r3 = lambda a: a.reshape(DB, T, a.shape[-1])
    pad8 = lambda a: jnp.pad(r3(a), ((0, 0), (0, SUBLANES - T), (0, 0)))
    to_hdp = lambda c: jnp.transpose(c[0], (0, 2, 3, 1))
    from_hdp = lambda c: jnp.transpose(c, (0, 3, 1, 2))[None]
    ya_s, s_k, s_v = _attn_sample(pad8(q_s), pad8(kn), pad8(vn), knt, vnt, pad8(ga_s),
                                  to_hdp(cache_win_k), to_hdp(cache_win_v), T)
    ya_s = ya_s[:, :T]

    pad_t = lambda a: jnp.pad(r3(a), ((0, 0), (0, CHUNK - T), (0, 0)))
    g3 = r3(gates_s)[:, :, :SUBLANES]
    null_gate = jnp.concatenate([jnp.full((HB,), NEG_INF, F32), jnp.zeros((HB,), F32)])
    gt_s = jnp.concatenate([g3, jnp.broadcast_to(null_gate, (DB, CHUNK - T, SUBLANES))], axis=1)
    gt_s = gt_s.transpose(0, 2, 1)
    m0 = jnp.broadcast_to(jnp.pad(state_m[0], ((0, 0), (0, SUBLANES - HB)))[..., None],
                          (DB, SUBLANES, LANES))
    yb_s, c_s, n_s, m_s = _mlstm(pad_t(qm_s), pad_t(km_s), pad_t(vb_s), gt_s, pad_t(so_s),
                                 pad_t(gb_s), ng, state=(state_C[0], state_n[0], m0))
    y_sample = _outproj(x_sample.reshape(DB * T, D), ya_s.reshape(DB * T, D_ATT),
                        yb_s[:, :T].reshape(DB * T, D_MLSTM), w_o, g_fin).reshape(DB, T, D)

    return (y_prompt, y_sample,
            from_hdp(pk), from_hdp(pv), p_conv[None],
            c_p[None], n_p[None], m_p[None, :, :HB, 0],
            from_hdp(s_k), from_hdp(s_v), r3(qk_s)[None, :, T - (CONV_WIDTH - 1):],
            c_s[None], n_s[None], m_s[None, :, :HB, 0])
```

```python
import functools
import math

import jax
import jax.numpy as jnp
import numpy as np
from jax import lax
from jax.experimental import pallas as pl
from jax.experimental.pallas import tpu as pltpu

F32 = jnp.float32
BF16 = jnp.bfloat16

D_MODEL = 1024
D_ATT = 512
N_HEADS_ATT = 8
HEAD_DIM_ATT = 64
D_MLSTM = 512
N_HEADS_MLSTM = 4
HEAD_DIM_MLSTM = 128
ROT_DIM = 16
ROPE_THETA = 500000.0
PAST_LEN = 16384
DILATIONS = (1, 4, 16)
N_BACK = 128
CONV_WIDTH = 4
CHUNK = 128
EPS = 1e-6
NEG_INF = -1e30
LOG2_E = math.log2(math.e)

LANES = 128
SUBLANES = 8
OFF_QA, OFF_KA, OFF_VA, OFF_ZA = 0, 512, 1024, 1536
OFF_QB, OFF_VB, OFF_OB, OFF_ZB, OFF_G = 2048, 3072, 3584, 4096, 4608
ROW_TILE = 512
INPROJ_SUBTILE = 128
OUTPROJ_TILE = 2048
OUTPROJ_SUBTILE = 256
MLSTM_BATCH = 4
VMEM_LIMIT = 56 * 1024 * 1024


def _silu(x):
    return x * jax.nn.sigmoid(x)


def _rmsnorm(x, g):
    return x * lax.rsqrt(jnp.mean(x * x, axis=-1, keepdims=True) + EPS) * g


def _rope(u, cos, sin_lo, sin_hi):
    outs = []
    for c in range(u.shape[1] // LANES):
        xs = u[:, c * LANES:(c + 1) * LANES]
        outs.append(xs * cos + pltpu.roll(xs, LANES - ROT_DIM // 2, 1) * sin_lo
                    + pltpu.roll(xs, ROT_DIM // 2, 1) * sin_hi)
    return jnp.concatenate(outs, axis=1)


def _gate_block(ug, bif):
    gz = ug + bif
    lane = lax.broadcasted_iota(jnp.int32, gz.shape, 1)
    logf = jnp.minimum(gz, 0.0) - jnp.log1p(jnp.exp(-jnp.abs(gz)))
    return jnp.where(lane < N_HEADS_MLSTM, gz, logf)


def _segments(hn, w_ref, wg_ref):
    def seg(off, width):
        rhs = wg_ref[...] if off == OFF_G else w_ref[:, off:off + width]
        return jnp.dot(hn, rhs, preferred_element_type=F32)
    return seg


def _inproj_common(seg, cos, sin_lo, sin_hi, q_ref, ga_ref, vb_ref, so_ref, gb_ref):
    q = _rope(seg(OFF_QA, D_ATT), cos, sin_lo, sin_hi) * (HEAD_DIM_ATT ** -0.5 * LOG2_E)
    q_ref[...] = q.astype(BF16).reshape(q_ref.shape)
    k = _rope(seg(OFF_KA, D_ATT), cos, sin_lo, sin_hi)
    v = seg(OFF_VA, D_ATT)
    ga_ref[...] = _silu(seg(OFF_ZA, D_ATT)).astype(BF16).reshape(ga_ref.shape)
    vb_ref[...] = seg(OFF_VB, D_MLSTM).astype(BF16).reshape(vb_ref.shape)
    so_ref[...] = jax.nn.sigmoid(seg(OFF_OB, D_MLSTM)).astype(BF16).reshape(so_ref.shape)
    gb_ref[...] = _silu(seg(OFF_ZB, D_MLSTM)).astype(BF16).reshape(gb_ref.shape)
    return k, v


def _inproj_prompt_kernel(x_ref, g_ref, w_ref, wg_ref, cw_ref, cb_ref, bif_ref, cos_ref, slo_ref,
                          shi_ref,
                          q_ref, kb_ref, vbf_ref, pk_ref, pv_ref, ga_ref, qm_ref, km_ref, vb_ref,
                          so_ref, gb_ref, gt_ref, pconv_ref, xp_ref):
    j = pl.program_id(1)
    tm = x_ref.shape[1]
    sub = INPROJ_SUBTILE
    assert CHUNK == LANES and tm % sub == 0 and sub % CHUNK == 0

    @pl.when(j == 0)
    def _():
        xp_ref[0:SUBLANES, :] = jnp.zeros((SUBLANES, 2 * D_MLSTM), F32)

    @pl.when(j > 0)
    def _():
        xp_ref[0:SUBLANES, :] = xp_ref[tm:tm + SUBLANES, :]

    for r0 in range(0, tm, sub):
        rs = pl.ds(r0, sub)
        part = lambda ref: ref.at[0, rs, :]
        hn = _rmsnorm(x_ref[0, rs, :], g_ref[...]).astype(BF16)
        seg = _segments(hn, w_ref, wg_ref)

        gates = _gate_block(seg(OFF_G, LANES), bif_ref[...])
        for i in range(sub // CHUNK):
            cs = slice(r0 + i * CHUNK, r0 + (i + 1) * CHUNK)
            gt = gates[i * CHUNK:(i + 1) * CHUNK, :].T[0:SUBLANES, :]
            gt_ref[0, 0:SUBLANES, cs] = gt
            gt_ref[0, SUBLANES:2 * SUBLANES, cs] = _gate_scans(gt)

        k, v = _inproj_common(seg, cos_ref[rs, :], slo_ref[rs, :], shi_ref[rs, :],
                              part(q_ref), part(ga_ref), part(vb_ref), part(so_ref), part(gb_ref))
        kb_ref[0, rs, :] = k.astype(BF16)
        vbf_ref[0, rs, :] = v.astype(BF16)
        pk_ref[0, :, :, r0:r0 + sub] = k.T.reshape(N_HEADS_ATT, HEAD_DIM_ATT, sub)
        pv_ref[0, :, :, r0:r0 + sub] = v.T.reshape(N_HEADS_ATT, HEAD_DIM_ATT, sub)

        base = SUBLANES + r0
        xp_ref[base:base + sub, :] = seg(OFF_QB, 2 * D_MLSTM)
        y = cb_ref[...] + xp_ref[base:base + sub, :] * cw_ref[3:4, :]
        for jj in range(CONV_WIDTH - 1):
            sh = CONV_WIDTH - 1 - jj
            y = y + xp_ref[base - sh:base - sh + sub, :] * cw_ref[jj:jj + 1, :]
        y = _silu(y)
        qm_ref[0, rs, :] = y[:, :D_MLSTM].astype(BF16)
        km_ref[0, rs, :] = (y[:, D_MLSTM:] * (HEAD_DIM_MLSTM ** -0.5)).astype(BF16)

    @pl.when(j == pl.num_programs(1) - 1)
    def _():
        pconv_ref[0] = xp_ref[tm + SUBLANES - (CONV_WIDTH - 1):tm + SUBLANES, :]


def _inproj_sample_kernel(x_ref, g_ref, w_ref, wg_ref, cw_ref, cb_ref, bif_ref, cos_ref, slo_ref,
                          shi_ref,
                          h1_ref, h2_ref, h3_ref,
                          q_ref, kn_ref, vn_ref, knt_ref, vnt_ref, ga_ref, qm_ref, km_ref, vb_ref,
                          so_ref, gb_ref, gates_ref, qk_ref, *, t_new):
    hn = _rmsnorm(x_ref[...], g_ref[...]).astype(BF16)
    seg = _segments(hn, w_ref, wg_ref)
    k, v = _inproj_common(seg, cos_ref[...], slo_ref[...], shi_ref[...],
                          q_ref, ga_ref, vb_ref, so_ref, gb_ref)
    kn_ref[...] = k
    vn_ref[...] = v
    knt_ref[...] = k.T
    vnt_ref[...] = v.T
    u = seg(OFF_QB, 2 * D_MLSTM)
    qk_ref[...] = u
    t = lax.rem(lax.broadcasted_iota(jnp.int32, u.shape, 0), t_new)
    y = cb_ref[...] + u * cw_ref[3:4, :]
    for sh, h_ref in ((1, h1_ref), (2, h2_ref), (3, h3_ref)):
        prev = jnp.where(t >= sh, pltpu.roll(u, sh, 0), h_ref[...])
        y = y + prev * cw_ref[3 - sh:4 - sh, :]
    y = _silu(y)
    qm_ref[...] = y[:, :D_MLSTM].astype(BF16)
    km_ref[...] = (y[:, D_MLSTM:] * (HEAD_DIM_MLSTM ** -0.5)).astype(BF16)
    gates_ref[...] = _gate_block(seg(OFF_G, LANES), bif_ref[...])


def _attn_prompt_kernel(q_ref, k_ref, v_ref, g_ref, o_ref,
                        src1_ref, src4_ref, src16_ref, bias_ref, st16r_ref, st16_ref, st4_ref):
    seq = q_ref.shape[1]
    n16 = seq // 16
    group = 8
    head_a_full = lax.broadcasted_iota(jnp.int32, (seq, LANES), 1) < HEAD_DIM_ATT

    def stack(q, k, v, is_a):
        zero, one = jnp.zeros((), q.dtype), jnp.ones((), q.dtype)
        return (jnp.where(is_a, q, zero), jnp.where(is_a, zero, q), k,
                jnp.where(is_a, v, one), jnp.where(is_a, one, v))

    q, k, v = q_ref[0], k_ref[0], v_ref[0]
    for i, x in enumerate(stack(q, k, v, head_a_full)):
        src1_ref[i] = x
    for i, x in enumerate(stack(q.astype(F32), k.astype(F32), v.astype(F32), head_a_full)):
        src4_ref[i] = x
    blk = 16 * 16
    pa = lax.broadcasted_iota(jnp.int32, (blk, blk), 0)
    pb = lax.broadcasted_iota(jnp.int32, (blk, blk), 1)
    perm = jnp.where(pb == 16 * (pa & 15) + (pa >> 4), 1.0, 0.0).astype(BF16)
    is_a_blk = lax.broadcasted_iota(jnp.int32, (16, 16, LANES), 2) < HEAD_DIM_ATT
    for j in range(seq // blk):
        rows = slice(j * blk, (j + 1) * blk)
        parts = [jnp.dot(perm, x[rows], preferred_element_type=F32).astype(BF16).reshape(16, 16, LANES)
                 for x in (q, k, v)]
        for i, x in enumerate(stack(*parts, is_a_blk)):
            src16_ref[i, :, 16 * j:16 * (j + 1), :] = x
    u = lax.broadcasted_iota(jnp.int32, (N_BACK, 2 * N_BACK), 0)
    w = lax.broadcasted_iota(jnp.int32, (N_BACK, 2 * N_BACK), 1)
    bias_ref[...] = jnp.where((w >= u) & (w <= u + N_BACK), jnp.finfo(F32).max, NEG_INF)
    head_a = lax.broadcasted_iota(jnp.int32, (N_BACK, LANES), 1) < HEAD_DIM_ATT

    def partials(gets):
        staged = []
        for get, has_prev in gets:
            if has_prev:
                both = lambda i, get=get: jnp.concatenate([get(i, True), get(i, False)], axis=0)
                bias = bias_ref[...]
            else:
                both = lambda i, get=get: get(i, False)
                bias = bias_ref[:, N_BACK:]
            kk = both(2)
            heads = []
            for qi in (0, 1):
                s = lax.dot_general(get(qi, False), kk, (((1,), (1,)), ((), ())),
                                    preferred_element_type=F32)
                s = jnp.minimum(s, bias)
                mh = jnp.max(s, axis=-1, keepdims=True)
                heads.append((mh, jnp.exp2(s - mh).astype(BF16)))
            staged.append((both, heads))
        out = []
        for both, ((m_a, p_a), (m_b, p_b)) in staged:
            pv_a = jnp.dot(p_a, both(3), preferred_element_type=F32)
            pv_b = jnp.dot(p_b, both(4), preferred_element_type=F32)
            acc = jnp.where(head_a, pv_a, pv_b)
            den = pltpu.roll(jnp.where(head_a, pv_b, pv_a), HEAD_DIM_ATT, 1)
            out.append((jnp.where(head_a, m_a, m_b), den, acc))
        return out

    def get4(qstart, kprev_start):
        def get(i, prev):
            start = kprev_start if prev else qstart
            return src4_ref[i, pl.ds(start, N_BACK, stride=4), :].astype(BF16)
        return get, kprev_start is not None

    def get1(qstart, kprev_start):
        def get(i, prev):
            return src1_ref[i, pl.ds(kprev_start if prev else qstart, N_BACK), :]
        return get, kprev_start is not None

    def keep4(blocks):
        res = partials([get4(qs, ks) for qs, ks in blocks])
        for (qs, _), (m, den, acc) in zip(blocks, res):
            rows = pl.ds(qs, N_BACK, stride=4)
            st4_ref[0, rows, :] = m
            st4_ref[1, rows, :] = den
            st4_ref[2, rows, :] = acc

    def finish(blocks):
        res = partials([get1(qs, ks) for qs, ks in blocks])
        for (qs, _), part in zip(blocks, res):
            rows = pl.ds(qs, N_BACK)
            parts = [part] + [tuple(st[i, rows, :] for i in range(3)) for st in (st4_ref, st16_ref)]
            m_all = functools.reduce(jnp.maximum, [pt[0] for pt in parts])
            wts = [jnp.exp2(pt[0] - m_all) for pt in parts]
            den = sum(wt * pt[1] for wt, pt in zip(wts, parts))
            num = sum(wt * pt[2] for wt, pt in zip(wts, parts))
            o_ref[0, rows, :] = (num / den * g_ref[0, rows, :].astype(F32)).astype(BF16)

    def body16(g, c):
        rs = [g * group + rr for rr in range(group)]
        res = partials([(lambda i, prev, r=r: src16_ref[i, r], False) for r in rs])
        for r, part in zip(rs, res):
            for i in range(3):
                st16r_ref[i, r] = part[i]
        return c
    lax.fori_loop(0, 16 // group, body16, 0)
    for i in range(3):
        st16_ref[i] = jnp.swapaxes(st16r_ref[i], 0, 1).reshape(seq, LANES)

    blocks4 = lambda cc: [(r + 4 * N_BACK * cc, r + 4 * N_BACK * (cc - 1)) for r in range(4)]
    keep4([(r, None) for r in range(4)] + blocks4(1))

    def body4(g, c):
        keep4(blocks4(2 * g) + blocks4(2 * g + 1))
        return c
    lax.fori_loop(1, seq // 4 // N_BACK // 2, body4, 0)

    finish([(0, None)] + [(cc * N_BACK, (cc - 1) * N_BACK) for cc in range(1, group)])

    def body1(g, c):
        starts = [pl.multiple_of((g * group + rr) * N_BACK, N_BACK) for rr in range(group)]
        finish([(st, st - N_BACK) for st in starts])
        return c
    lax.fori_loop(1, seq // N_BACK // group, body1, 0)


def _attn_sample_kernel(q_ref, kn_ref, vn_ref, knt_ref, vnt_ref, g_ref, ck_ref, cv_ref,
                        o_ref, sk_ref, sv_ref, clamp_ref, *, t_new):
    b = pl.program_id(0)
    wb = ck_ref.shape[3]
    hd = HEAD_DIM_ATT
    rows = q_ref.shape[1]

    @pl.when(b == 0)
    def _():
        delta = (wb + lax.broadcasted_iota(jnp.int32, (rows, wb), 0)
                 - lax.broadcasted_iota(jnp.int32, (rows, wb), 1))
        for d, dil in enumerate(DILATIONS):
            ok = ((delta & (dil - 1)) == 0) & (delta >= dil) & (delta <= N_BACK * dil)
            clamp_ref[d] = jnp.where(ok, jnp.finfo(F32).max, NEG_INF)

    tq = lax.broadcasted_iota(jnp.int32, (rows, rows), 0)
    tk = lax.broadcasted_iota(jnp.int32, (rows, rows), 1)
    new_ok = [((tk <= tq) if dil == 1 else (tk == tq)) & (tk < t_new) for dil in DILATIONS]

    lane = lax.broadcasted_iota(jnp.int32, (hd, LANES), 1)
    shift_new = (LANES - t_new) - b * t_new
    nt = (((1,), (1,)), ((), ()))
    outs = []
    for h in range(N_HEADS_ATT):
        hs = slice(h * hd, (h + 1) * hd)
        kt = ck_ref[0, h]
        vt = cv_ref[0, h]
        for old, new_ref, out_ref in ((kt, knt_ref, sk_ref), (vt, vnt_ref, sv_ref)):
            moved = pltpu.roll(old, wb - t_new, axis=1)
            new_cols = pltpu.roll(new_ref[hs, :], shift_new, axis=1)
            out_ref[0, h, :, 0:wb - LANES] = moved[:, 0:wb - LANES]
            out_ref[0, h, :, wb - LANES:wb] = jnp.where(lane < LANES - t_new,
                                                        moved[:, wb - LANES:wb], new_cols)
        qh = q_ref[0, :, hs]
        kn_h = kn_ref[0, :, hs].astype(BF16)
        vn_h = vn_ref[0, :, hs].astype(BF16)
        s_old = jnp.dot(qh, kt.astype(BF16), preferred_element_type=F32)
        s_new = lax.dot_general(qh, kn_h, nt, preferred_element_type=F32)
        ps, pes, ms = [], [], []
        for d in range(len(DILATIONS)):
            so = jnp.minimum(s_old, clamp_ref[d])
            sn = jnp.where(new_ok[d], s_new, NEG_INF)
            m = jnp.maximum(jnp.max(so, axis=-1, keepdims=True), jnp.max(sn, axis=-1, keepdims=True))
            ps.append(jnp.exp2(so - m))
            pes.append(jnp.exp2(sn - m))
            ms.append(m)
        acc = lax.dot_general(jnp.concatenate(ps, axis=0).astype(BF16), vt.astype(BF16), nt,
                              preferred_element_type=F32)
        acc = acc + jnp.dot(jnp.concatenate(pes, axis=0).astype(BF16), vn_h,
                            preferred_element_type=F32)
        m_all = functools.reduce(jnp.maximum, ms)
        den = 0.0
        num = 0.0
        for d in range(len(DILATIONS)):
            wgt = jnp.exp2(ms[d] - m_all)
            den = den + wgt * (jnp.sum(ps[d], axis=-1, keepdims=True)
                               + jnp.sum(pes[d], axis=-1, keepdims=True))
            num = num + wgt * acc[d * rows:(d + 1) * rows]
        outs.append(num / den)
    att = jnp.concatenate(outs, axis=1)
    o_ref[0] = (att * g_ref[0].astype(F32)).astype(BF16)


def _scan_lanes(x, op, fill):
    lane = lax.broadcasted_iota(jnp.int32, x.shape, 1)
    d = 1
    while d < x.shape[1]:
        x = op(x, jnp.where(lane >= d, pltpu.roll(x, d, 1), fill))
        d *= 2
    return x


def _gate_scans(gt):
    b = _scan_lanes(pltpu.roll(gt, N_HEADS_MLSTM, 0), jnp.add, 0.0)
    cm = _scan_lanes(gt - b, jnp.maximum, NEG_INF)
    row = lax.broadcasted_iota(jnp.int32, gt.shape, 0)
    return jnp.where(row < N_HEADS_MLSTM, b, pltpu.roll(cm, N_HEADS_MLSTM, 0))


def _mlstm_kernel(*refs, zero_init, scans_given, fuse_out):
    refs = list(refs)
    q_ref, k_ref, v_ref, gt_ref, so_ref, gb_ref, ng_ref = refs[:7]
    del refs[:7]
    if not zero_init:
        c0_ref, n0_ref, m0_ref = refs[:3]
        del refs[:3]
    if fuse_out:
        x_ref, ya_ref, wo_ref, fg_ref = refs[:4]
        del refs[:4]
    y_ref, c_out_ref, n_out_ref, m_out_ref, cn_ref, m_ref = refs
    c_idx = pl.program_id(1)
    nb = q_ref.shape[0]
    L = CHUNK
    dh = HEAD_DIM_MLSTM
    nh = N_HEADS_MLSTM

    @pl.when(c_idx == 0)
    def _():
        if zero_init:
            cn_ref[...] = jnp.zeros(cn_ref.shape, F32)
            m_ref[...] = jnp.zeros(m_ref.shape, F32)
        else:
            m_ref[...] = m0_ref[...]
            for bb in range(nb):
                for h in range(nh):
                    n_rows = jnp.broadcast_to(n0_ref[bb, h:h + 1, :], (dh, dh))
                    cn_ref[bb, h] = jnp.concatenate([c0_ref[bb, h], n_rows.T], axis=1)

    tri_t = lax.broadcasted_iota(jnp.int32, (L, L), 0)
    tri_s = lax.broadcasted_iota(jnp.int32, (L, L), 1)
    causal = tri_t >= tri_s
    ones_blk = jnp.ones((L, dh), BF16)
    nt = (((1,), (1,)), ((), ()))
    pairs = [(bb, h) for bb in range(nb) for h in range(nh)]
    sl = lambda h: slice(h * dh, (h + 1) * dh)

    qk = {(bb, h): lax.dot_general(q_ref[bb, :, sl(h)], k_ref[bb, :, sl(h)], nt,
                                   preferred_element_type=F32) for bb, h in pairs}
    cn_old = {p: cn_ref[p[0], p[1]] for p in pairs}
    if fuse_out:
        mix_a = x_ref[...].reshape(nb * L, D_MODEL) + jnp.dot(
            ya_ref[...].reshape(nb * L, D_ATT), wo_ref[0:D_ATT, :], preferred_element_type=F32)

    rows, cols, decays, w_rows = [], [], [], []
    for bb in range(nb):
        if scans_given:
            i_row = gt_ref[bb, 0:SUBLANES, :]
            sc = gt_ref[bb, SUBLANES:2 * SUBLANES, :]
        else:
            i_row = gt_ref[bb]
            sc = _gate_scans(i_row)
        b = sc
        cm = pltpu.roll(sc, nh, 0)
        m_prev = m_ref[bb]
        m_t = jnp.maximum(m_prev + b, b + cm)
        inter = jnp.exp(m_prev + b - m_t)
        m_last = jnp.broadcast_to(m_t[:, L - 1:L], m_t.shape)
        b_last = jnp.broadcast_to(b[:, L - 1:L], b.shape)
        decay = jnp.exp(m_prev + b_last - m_last)
        w_row = jnp.exp(b_last - b + i_row - m_last)
        m_ref[bb] = m_last
        stack = jnp.concatenate([b - m_t, inter, jnp.exp(-m_t),
                                 jnp.zeros((L - 3 * SUBLANES, L), F32)], axis=0)
        cols.append(stack.T)
        rows.append(i_row - b)
        w_rows.append(w_row)
        decays.append(jnp.concatenate([decay, decay], axis=1))
    col = lambda bb, kind, h: cols[bb][:, kind * SUBLANES + h:kind * SUBLANES + h + 1]

    sqk = {}
    for bb, h in pairs:
        dlog = col(bb, 0, h) + rows[bb][h:h + 1, :]
        sqk[bb, h] = (qk[bb, h] * jnp.exp(jnp.where(causal, dlog, NEG_INF))).astype(BF16)
    v_one = {(bb, h): jnp.concatenate([v_ref[bb, :, sl(h)], ones_blk], axis=1) for bb, h in pairs}
    tots = {}
    for bb, h in pairs:
        q_dec = (col(bb, 1, h) * q_ref[bb, :, sl(h)].astype(F32)).astype(BF16)
        lhs = jnp.concatenate([sqk[bb, h], q_dec], axis=1)
        rhs = jnp.concatenate([v_one[bb, h], cn_old[bb, h].astype(BF16)], axis=0)
        tots[bb, h] = jnp.dot(lhs, rhs, preferred_element_type=F32)
    yb = {}
    for bb, h in pairs:
        tot = tots[bb, h]
        hh = tot[:, :dh] / jnp.maximum(jnp.abs(tot[:, dh:]), col(bb, 2, h))
        hh = so_ref[bb, :, sl(h)].astype(F32) * hh
        hh = hh * lax.rsqrt(jnp.mean(hh * hh, axis=-1, keepdims=True) + EPS)
        hh = hh * ng_ref[:, sl(h)]
        yb[bb, h] = (hh * gb_ref[bb, :, sl(h)].astype(F32)).astype(BF16)
    if fuse_out:
        yb_all = jnp.concatenate([jnp.concatenate([yb[bb, h] for h in range(nh)], axis=1)
                                  for bb in range(nb)], axis=0)
        mix = mix_a + jnp.dot(yb_all, wo_ref[D_ATT:, :], preferred_element_type=F32)
        res = _rmsnorm(mix, fg_ref[...])
        y_ref[...] = res.reshape(nb, L, D_MODEL)
    else:
        for bb, h in pairs:
            y_ref[bb, :, sl(h)] = yb[bb, h]
    for bb, h in pairs:
        kt_w = k_ref[bb, :, sl(h)].T.astype(F32) * w_rows[bb][h:h + 1, :]
        upd = jnp.dot(kt_w.astype(BF16), v_one[bb, h], preferred_element_type=F32)
        cn_ref[bb, h] = decays[bb][h:h + 1, :] * cn_old[bb, h] + upd

    @pl.when(c_idx == pl.num_programs(1) - 1)
    def _():
        m_out_ref[...] = m_ref[...]
        for bb in range(nb):
            n_rows = []
            for h in range(nh):
                cn = cn_ref[bb, h]
                c_out_ref[bb, h] = cn[:, :dh]
                n_rows.append(cn[:, dh:].T[0:1, :])
            n_out_ref[bb] = jnp.concatenate(n_rows, axis=0)


def _outproj_kernel(x_ref, ya_ref, yb_ref, w_ref, g_ref, o_ref):
    rows = x_ref.shape[0]
    sub = min(OUTPROJ_SUBTILE, rows)
    for r0 in range(0, rows, sub):
        rs = pl.ds(r0, sub)
        mix = jnp.dot(ya_ref[rs, :], w_ref[0:D_ATT, :], preferred_element_type=F32)
        mix = mix + jnp.dot(yb_ref[rs, :], w_ref[D_ATT:, :], preferred_element_type=F32)
        o_ref[rs, :] = _rmsnorm(x_ref[rs, :] + mix, g_ref[...])


def _rope_tables(pos):
    half = ROT_DIM // 2
    inv = ROPE_THETA ** (-jnp.arange(half, dtype=F32) * 2.0 / ROT_DIM)
    ang = pos.astype(F32)[:, None] * inv[None, :]
    cos, sin = jnp.cos(ang), jnp.sin(ang)
    n = pos.shape[0]
    one = jnp.ones((n, HEAD_DIM_ATT - ROT_DIM), F32)
    zero = jnp.zeros((n, HEAD_DIM_ATT - ROT_DIM), F32)
    zh = jnp.zeros((n, half), F32)
    cos_t = jnp.concatenate([cos, cos, one], axis=1)
    lo_t = jnp.concatenate([-sin, zh, zero], axis=1)
    hi_t = jnp.concatenate([zh, sin, zero], axis=1)
    rep = lambda t: jnp.concatenate([t, t], axis=1)
    return rep(cos_t), rep(lo_t), rep(hi_t)


def _params(sem):
    return pltpu.CompilerParams(dimension_semantics=sem, vmem_limit_bytes=VMEM_LIMIT)


def _const_spec(shape):
    return pl.BlockSpec(shape, lambda *_: (0,) * len(shape))


def _inproj_prompt(x, norm_g, w_main, w_gate, conv_w, conv_b, bif):
    B, S, _ = x.shape
    tm = ROW_TILE
    cos, lo, hi = _rope_tables(jnp.arange(S, dtype=jnp.int32))
    tile = lambda width: pl.BlockSpec((1, tm, width), lambda b, j: (b, j, 0))
    tab = pl.BlockSpec((tm, LANES), lambda b, j: (j, 0))
    bf = lambda: jax.ShapeDtypeStruct((B, S, D_ATT), BF16)
    f5 = lambda: jax.ShapeDtypeStruct((B, N_HEADS_ATT, HEAD_DIM_ATT, S), F32)
    tile5 = pl.BlockSpec((1, N_HEADS_ATT, HEAD_DIM_ATT, tm), lambda b, j: (b, 0, 0, j))
    out_shape = (bf(), bf(), bf(), f5(), f5(), bf(), bf(), bf(), bf(), bf(), bf(),
                 jax.ShapeDtypeStruct((B, 2 * SUBLANES, S), F32),
                 jax.ShapeDtypeStruct((B, CONV_WIDTH - 1, 2 * D_MLSTM), F32))
    out_specs = tuple([tile(D_ATT)] * 3 + [tile5] * 2 + [tile(D_ATT)] * 6) + (
        pl.BlockSpec((1, 2 * SUBLANES, tm), lambda b, j: (b, 0, j)),
        pl.BlockSpec((1, CONV_WIDTH - 1, 2 * D_MLSTM), lambda b, j: (b, 0, 0)))
    return pl.pallas_call(
        _inproj_prompt_kernel,
        grid=(B, S // tm),
        in_specs=[tile(D_MODEL), _const_spec((1, D_MODEL)), _const_spec((D_MODEL, OFF_G)),
                  _const_spec(w_gate.shape),
                  _const_spec((CONV_WIDTH, 2 * D_MLSTM)), _const_spec((1, 2 * D_MLSTM)),
                  _const_spec((1, LANES)), tab, tab, tab],
        out_specs=out_specs,
        out_shape=out_shape,
        scratch_shapes=[pltpu.VMEM((tm + 2 * SUBLANES, 2 * D_MLSTM), F32)],
        compiler_params=_params(("arbitrary", "arbitrary")),
        name="inproj_prompt",
    )(x, norm_g, w_main, w_gate, conv_w, conv_b, bif, cos, lo, hi)


def _inproj_sample(x2, norm_g, w_main, w_gate, conv_w, conv_b, bif, hist, t_new):
    rows = x2.shape[0]
    pos = PAST_LEN + jnp.arange(t_new, dtype=jnp.int32)
    cos, lo, hi = (jnp.tile(t, (rows // t_new, 1)) for t in _rope_tables(pos))
    bf = lambda: jax.ShapeDtypeStruct((rows, D_ATT), BF16)
    f3 = lambda: jax.ShapeDtypeStruct((rows, D_ATT), F32)
    f5 = lambda: jax.ShapeDtypeStruct((D_ATT, rows), F32)
    out_shape = (bf(), f3(), f3(), f5(), f5(), bf(), bf(), bf(), bf(), bf(), bf(),
                 jax.ShapeDtypeStruct((rows, LANES), F32),
                 jax.ShapeDtypeStruct((rows, 2 * D_MLSTM), F32))
    return pl.pallas_call(
        functools.partial(_inproj_sample_kernel, t_new=t_new),
        out_shape=out_shape,
        compiler_params=pltpu.CompilerParams(vmem_limit_bytes=VMEM_LIMIT),
        name="inproj_sample",
    )(x2, norm_g, w_main, w_gate, conv_w, conv_b, bif, cos, lo, hi, *hist)


def _attn_prompt(q, k, v, gate):
    B, S, _ = q.shape
    spec = pl.BlockSpec((1, S, LANES), lambda b, h: (b, 0, h))
    return pl.pallas_call(
        _attn_prompt_kernel,
        grid=(B, D_ATT // LANES),
        in_specs=[spec, spec, spec, spec],
        out_specs=spec,
        out_shape=jax.ShapeDtypeStruct((B, S, D_ATT), BF16),
        scratch_shapes=[
            pltpu.VMEM((5, S, LANES), BF16), pltpu.VMEM((5, S, LANES), F32),
            pltpu.VMEM((5, 16, S // 16, LANES), BF16),
            pltpu.VMEM((N_BACK, 2 * N_BACK), F32),
            pltpu.VMEM((3, 16, S // 16, LANES), F32),
            pltpu.VMEM((3, S, LANES), F32), pltpu.VMEM((3, S, LANES), F32)],
        compiler_params=_params(("arbitrary", "arbitrary")),
        name="attn_prompt",
    )(q, k, v, gate)


def _attn_sample(q, kn, vn, knt, vnt, gate, ck, cv, t_new):
    B, rows, _ = q.shape
    wb = ck.shape[3]
    small = pl.BlockSpec((1, rows, D_ATT), lambda b: (b, 0, 0))
    big = pl.BlockSpec((1, N_HEADS_ATT, HEAD_DIM_ATT, wb), lambda b: (b, 0, 0, 0))
    win_shape = jax.ShapeDtypeStruct((B, N_HEADS_ATT, HEAD_DIM_ATT, wb), F32)
    return pl.pallas_call(
        functools.partial(_attn_sample_kernel, t_new=t_new),
        grid=(B,),
        in_specs=[small, small, small, _const_spec(knt.shape), _const_spec(vnt.shape), small, big, big],
        out_specs=(small, big, big),
        out_shape=(jax.ShapeDtypeStruct((B, rows, D_ATT), BF16), win_shape, win_shape),
        scratch_shapes=[pltpu.VMEM((len(DILATIONS), rows, wb), F32)],
        compiler_params=_params(("arbitrary",)),
        name="attn_sample",
    )(q, kn, vn, knt, vnt, gate, ck, cv)


def _mlstm(q, k, v, gates_t, sig_o, gate_b, norm_g, state=None, out_proj=None):
    B, S, _ = q.shape
    nc = S // CHUNK
    nb = MLSTM_BATCH
    nh, dh = N_HEADS_MLSTM, HEAD_DIM_MLSTM
    grows = gates_t.shape[1]
    tile = lambda width: pl.BlockSpec((nb, CHUNK, width), lambda b, c: (b, c, 0))
    gspec = pl.BlockSpec((nb, grows, CHUNK), lambda b, c: (b, 0, c))
    c_spec = pl.BlockSpec((nb, nh, dh, dh), lambda b, c: (b, 0, 0, 0))
    n_spec = pl.BlockSpec((nb, nh, dh), lambda b, c: (b, 0, 0))
    m_spec = pl.BlockSpec((nb, SUBLANES, LANES), lambda b, c: (b, 0, 0))
    zero_init = state is None
    fuse_out = out_proj is not None
    in_specs = [tile(D_MLSTM)] * 3 + [gspec] + [tile(D_MLSTM)] * 2 + [_const_spec((1, D_MLSTM))]
    args = [q, k, v, gates_t, sig_o, gate_b, norm_g]
    if not zero_init:
        in_specs += [c_spec, n_spec, m_spec]
        args += list(state)
    if fuse_out:
        in_specs += [tile(D_MODEL), tile(D_ATT), _const_spec((D_ATT + D_MLSTM, D_MODEL)),
                     _const_spec((1, D_MODEL))]
        args += list(out_proj)
        y_spec, y_shape = tile(D_MODEL), jax.ShapeDtypeStruct((B, S, D_MODEL), F32)
    else:
        y_spec, y_shape = tile(D_MLSTM), jax.ShapeDtypeStruct((B, S, D_MLSTM), BF16)
    return pl.pallas_call(
        functools.partial(_mlstm_kernel, zero_init=zero_init, scans_given=grows == 2 * SUBLANES,
                          fuse_out=fuse_out),
        grid=(B // nb, nc),
        in_specs=in_specs,
        out_specs=(y_spec, c_spec, n_spec, m_spec),
        out_shape=(y_shape,
                   jax.ShapeDtypeStruct((B, nh, dh, dh), F32),
                   jax.ShapeDtypeStruct((B, nh, dh), F32),
                   jax.ShapeDtypeStruct((B, SUBLANES, LANES), F32)),
        scratch_shapes=[pltpu.VMEM((nb, nh, dh, 2 * dh), F32),
                        pltpu.VMEM((nb, SUBLANES, LANES), F32)],
        compiler_params=_params(("arbitrary", "arbitrary")),
        name="mlstm_prompt" if zero_init else "mlstm_sample",
    )(*args)


def _outproj(x2, ya, yb, w_out, final_g):
    rows = x2.shape[0]
    tm = min(OUTPROJ_TILE, rows)
    tile = lambda width: pl.BlockSpec((tm, width), lambda i: (i, 0))
    return pl.pallas_call(
        _outproj_kernel,
        grid=(rows // tm,),
        in_specs=[tile(D_MODEL), tile(D_ATT), tile(D_MLSTM),
                  _const_spec((D_ATT + D_MLSTM, D_MODEL)), _const_spec((1, D_MODEL))],
        out_specs=tile(D_MODEL),
        out_shape=jax.ShapeDtypeStruct((rows, D_MODEL), F32),
        compiler_params=_params(("arbitrary",)),
        name="outproj",
    )(x2, ya, yb, w_out, final_g)


def kernel(x_prompt, x_sample, cache_win_k, cache_win_v, state_conv, state_C, state_n, state_m,
           norm_g, w_in, conv_w, conv_b, b_i, b_f, mlstm_norm_g, w_out, final_norm_g):
    assert w_in.shape[0] == 1, "single-layer model"
    B, S, D = x_prompt.shape
    DB, T, _ = x_sample.shape
    HB, DK = N_HEADS_MLSTM, HEAD_DIM_MLSTM
    wb = cache_win_k.shape[2]
    assert S % ROW_TILE == 0 and S == 16 * N_BACK and wb >= 16 * N_BACK and CONV_WIDTH - 1 <= T <= SUBLANES and DB * T == LANES

    w_main = w_in[0].astype(BF16)
    w_gate = jnp.pad(w_main[:, OFF_G:], ((0, 0), (0, LANES - 2 * HB)))
    w_o = w_out[0].astype(BF16)
    g_in = norm_g[0][None, :]
    cw, cb = conv_w[0], conv_b[0][None, :]
    bif = jnp.pad(jnp.concatenate([b_i[0], b_f[0]]), (0, LANES - 2 * HB))[None, :]
    ng = mlstm_norm_g[0][None, :]
    g_fin = final_norm_g[None, :]

    (q_p, k_p, v_p, pk, pv, ga_p, qm_p, km_p, vb_p, so_p, gb_p, gt_p, p_conv) = _inproj_prompt(
        x_prompt, g_in, w_main, w_gate, cw, cb, bif)
    ya_p = _attn_prompt(q_p, k_p, v_p, ga_p)
    y_prompt, c_p, n_p, m_p = _mlstm(qm_p, km_p, vb_p, gt_p, so_p, gb_p, ng,
                                     out_proj=(x_prompt, ya_p, w_o, g_fin))

    sc = state_conv[0]
    zrow = jnp.zeros((DB, 1, 2 * D_MLSTM), F32)
    hist = []
    for sh in (1, 2, 3):
        rows_ = [sc[:, CONV_WIDTH - 1 + t - sh:CONV_WIDTH + t - sh] if t < sh else zrow
                 for t in range(T)]
        hist.append(jnp.concatenate(rows_, axis=1).reshape(DB * T, 2 * D_MLSTM))
    (q_s, kn, vn, knt, vnt, ga_s, qm_s, km_s, vb_s, so_s, gb_s, gates_s, qk_s) = _inproj_sample(
        x_sample.reshape(DB * T, D), g_in, w_main, w_gate, cw, cb, bif, hist, T)
    r3 = lambda a: a.reshape(DB, T, a.shape[-1])
    pad8 = lambda a: jnp.pad(r3(a), ((0, 0), (0, SUBLANES - T), (0, 0)))
    to_hdp = lambda c: jnp.transpose(c[0], (0, 2, 3, 1))
    from_hdp = lambda c: jnp.transpose(c, (0, 3, 1, 2))[None]
    ya_s, s_k, s_v = _attn_sample(pad8(q_s), pad8(kn), pad8(vn), knt, vnt, pad8(ga_s),
                                  to_hdp(cache_win_k), to_hdp(cache_win_v), T)
    ya_s = ya_s[:, :T]

    pad_t = lambda a: jnp.pad(r3(a), ((0, 0), (0, CHUNK - T), (0, 0)))
    g3 = r3(gates_s)[:, :, :SUBLANES]
    null_gate = jnp.concatenate([jnp.full((HB,), NEG_INF, F32), jnp.zeros((HB,), F32)])
    gt_s = jnp.concatenate([g3, jnp.broadcast_to(null_gate, (DB, CHUNK - T, SUBLANES))], axis=1)
    gt_s = gt_s.transpose(0, 2, 1)
    m0 = jnp.broadcast_to(jnp.pad(state_m[0], ((0, 0), (0, SUBLANES - HB)))[..., None],
                          (DB, SUBLANES, LANES))
    yb_s, c_s, n_s, m_s = _mlstm(pad_t(qm_s), pad_t(km_s), pad_t(vb_s), gt_s, pad_t(so_s),
                                 pad_t(gb_s), ng, state=(state_C[0], state_n[0], m0))
    y_sample = _outproj(x_sample.reshape(DB * T, D), ya_s.reshape(DB * T, D_ATT),
                        yb_s[:, :T].reshape(DB * T, D_MLSTM), w_o, g_fin).reshape(DB, T, D)

    return (y_prompt, y_sample,
            from_hdp(pk), from_hdp(pv), p_conv[None],
            c_p[None], n_p[None], m_p[None, :, :HB, 0],
            from_hdp(s_k), from_hdp(s_v), r3(qk_s)[None, :, T - (CONV_WIDTH - 1):],
            c_s[None], n_s[None], m_s[None, :, :HB, 0])
```

```python
import functools
import math

import jax
import jax.numpy as jnp
import numpy as np
from jax import lax
from jax.experimental import pallas as pl
from jax.experimental.pallas import tpu as pltpu

F32 = jnp.float32
BF16 = jnp.bfloat16

D_MODEL = 1024
D_ATT = 512
N_HEADS_ATT = 8
HEAD_DIM_ATT = 64
D_MLSTM = 512
N_HEADS_MLSTM = 4
HEAD_DIM_MLSTM = 128
ROT_DIM = 16
ROPE_THETA = 500000.0
PAST_LEN = 16384
DILATIONS = (1, 4, 16)
N_BACK = 128
CONV_WIDTH = 4
CHUNK = 128
EPS = 1e-6
NEG_INF = -1e30
LOG2_E = math.log2(math.e)

LANES = 128
SUBLANES = 8
OFF_QA, OFF_KA, OFF_VA, OFF_ZA = 0, 512, 1024, 1536
OFF_QB, OFF_VB, OFF_OB, OFF_ZB, OFF_G = 2048, 3072, 3584, 4096, 4608
ROW_TILE = 512
INPROJ_SUBTILE = 256
OUTPROJ_TILE = 2048
OUTPROJ_SUBTILE = 256
MLSTM_BATCH = 4
VMEM_LIMIT = 56 * 1024 * 1024


def _silu(x):
    return x * jax.nn.sigmoid(x)


def _rmsnorm(x, g):
    return x * lax.rsqrt(jnp.mean(x * x, axis=-1, keepdims=True) + EPS) * g


def _rope(u, cos, sin_lo, sin_hi):
    outs = []
    for c in range(u.shape[1] // LANES):
        xs = u[:, c * LANES:(c + 1) * LANES]
        outs.append(xs * cos + pltpu.roll(xs, LANES - ROT_DIM // 2, 1) * sin_lo
                    + pltpu.roll(xs, ROT_DIM // 2, 1) * sin_hi)
    return jnp.concatenate(outs, axis=1)


def _gate_block(ug, bif):
    gz = ug + bif
    lane = lax.broadcasted_iota(jnp.int32, gz.shape, 1)
    logf = jnp.minimum(gz, 0.0) - jnp.log1p(jnp.exp(-jnp.abs(gz)))
    return jnp.where(lane < N_HEADS_MLSTM, gz, logf)


def _segments(hn, w_ref, wg_ref):
    def seg(off, width):
        rhs = wg_ref[...] if off == OFF_G else w_ref[:, off:off + width]
        return jnp.dot(hn, rhs, preferred_element_type=F32)
    return seg


def _inproj_common(seg, cos, sin_lo, sin_hi, q_ref, ga_ref, vb_ref, so_ref, gb_ref):
    q = _rope(seg(OFF_QA, D_ATT), cos, sin_lo, sin_hi) * (HEAD_DIM_ATT ** -0.5 * LOG2_E)
    q_ref[...] = q.astype(BF16).reshape(q_ref.shape)
    k = _rope(seg(OFF_KA, D_ATT), cos, sin_lo, sin_hi)
    v = seg(OFF_VA, D_ATT)
    ga_ref[...] = _silu(seg(OFF_ZA, D_ATT)).astype(BF16).reshape(ga_ref.shape)
    vb_ref[...] = seg(OFF_VB, D_MLSTM).astype(BF16).reshape(vb_ref.shape)
    so_ref[...] = jax.nn.sigmoid(seg(OFF_OB, D_MLSTM)).astype(BF16).reshape(so_ref.shape)
    gb_ref[...] = _silu(seg(OFF_ZB, D_MLSTM)).astype(BF16).reshape(gb_ref.shape)
    return k, v


def _inproj_prompt_kernel(x_ref, g_ref, w_ref, wg_ref, cw_ref, cb_ref, bif_ref, cos_ref, slo_ref,
                          shi_ref,
                          q_ref, kb_ref, vbf_ref, pk_ref, pv_ref, ga_ref, qm_ref, km_ref, vb_ref,
                          so_ref, gb_ref, gt_ref, pconv_ref, xp_ref):
    j = pl.program_id(1)
    tm = x_ref.shape[1]
    sub = INPROJ_SUBTILE
    assert CHUNK == LANES and tm % sub == 0 and sub % CHUNK == 0

    @pl.when(j == 0)
    def _():
        xp_ref[0:SUBLANES, :] = jnp.zeros((SUBLANES, 2 * D_MLSTM), F32)

    @pl.when(j > 0)
    def _():
        xp_ref[0:SUBLANES, :] = xp_ref[tm:tm + SUBLANES, :]

    for r0 in range(0, tm, sub):
        rs = pl.ds(r0, sub)
        part = lambda ref: ref.at[0, rs, :]
        hn = _rmsnorm(x_ref[0, rs, :], g_ref[...]).astype(BF16)
        seg = _segments(hn, w_ref, wg_ref)

        gates = _gate_block(seg(OFF_G, LANES), bif_ref[...])
        for i in range(sub // CHUNK):
            cs = slice(r0 + i * CHUNK, r0 + (i + 1) * CHUNK)
            gt = gates[i * CHUNK:(i + 1) * CHUNK, :].T[0:SUBLANES, :]
            gt_ref[0, 0:SUBLANES, cs] = gt
            gt_ref[0, SUBLANES:2 * SUBLANES, cs] = _gate_scans(gt)

        k, v = _inproj_common(seg, cos_ref[rs, :], slo_ref[rs, :], shi_ref[rs, :],
                              part(q_ref), part(ga_ref), part(vb_ref), part(so_ref), part(gb_ref))
        kb_ref[0, rs, :] = k.astype(BF16)
        vbf_ref[0, rs, :] = v.astype(BF16)
        pk_ref[0, :, :, r0:r0 + sub] = k.T.reshape(N_HEADS_ATT, HEAD_DIM_ATT, sub)
        pv_ref[0, :, :, r0:r0 + sub] = v.T.reshape(N_HEADS_ATT, HEAD_DIM_ATT, sub)

        base = SUBLANES + r0
        xp_ref[base:base + sub, :] = seg(OFF_QB, 2 * D_MLSTM)
        y = cb_ref[...] + xp_ref[base:base + sub, :] * cw_ref[3:4, :]
        for jj in range(CONV_WIDTH - 1):
            sh = CONV_WIDTH - 1 - jj
            y = y + xp_ref[base - sh:base - sh + sub, :] * cw_ref[jj:jj + 1, :]
        y = _silu(y)
        qm_ref[0, rs, :] = y[:, :D_MLSTM].astype(BF16)
        km_ref[0, rs, :] = (y[:, D_MLSTM:] * (HEAD_DIM_MLSTM ** -0.5)).astype(BF16)

    @pl.when(j == pl.num_programs(1) - 1)
    def _():
        pconv_ref[0] = xp_ref[tm + SUBLANES - (CONV_WIDTH - 1):tm + SUBLANES, :]


def _inproj_sample_kernel(x_ref, g_ref, w_ref, wg_ref, cw_ref, cb_ref, bif_ref, cos_ref, slo_ref,
                          shi_ref,
                          h1_ref, h2_ref, h3_ref,
                          q_ref, kn_ref, vn_ref, knt_ref, vnt_ref, ga_ref, mix_ref, gates_ref, qk_ref,
                          *, t_new):
    qm_ref, km_ref, vb_ref, so_ref, gb_ref = (
        mix_ref.at[:, pl.ds(i * D_MLSTM, D_MLSTM)] for i in range(5))
    hn = _rmsnorm(x_ref[...], g_ref[...]).astype(BF16)
    seg = _segments(hn, w_ref, wg_ref)
    k, v = _inproj_common(seg, cos_ref[...], slo_ref[...], shi_ref[...],
                          q_ref, ga_ref, vb_ref, so_ref, gb_ref)
    kn_ref[...] = k
    vn_ref[...] = v
    knt_ref[...] = k.T
    vnt_ref[...] = v.T
    u = seg(OFF_QB, 2 * D_MLSTM)
    qk_ref[...] = u
    t = lax.rem(lax.broadcasted_iota(jnp.int32, u.shape, 0), t_new)
    y = cb_ref[...] + u * cw_ref[3:4, :]
    for sh, h_ref in ((1, h1_ref), (2, h2_ref), (3, h3_ref)):
        prev = jnp.where(t >= sh, pltpu.roll(u, sh, 0), h_ref[...])
        y = y + prev * cw_ref[3 - sh:4 - sh, :]
    y = _silu(y)
    qm_ref[...] = y[:, :D_MLSTM].astype(BF16)
    km_ref[...] = (y[:, D_MLSTM:] * (HEAD_DIM_MLSTM ** -0.5)).astype(BF16)
    gates_ref[...] = _gate_block(seg(OFF_G, LANES), bif_ref[...])


def _attn_prompt_kernel(q_ref, k_ref, v_ref, g_ref, o_ref,
                        src1_ref, src4_ref, src16_ref, bias_ref, st16r_ref, st16_ref, st4_ref):
    seq = q_ref.shape[1]
    n16 = seq // 16
    group = 8
    head_a_full = lax.broadcasted_iota(jnp.int32, (seq, LANES), 1) < HEAD_DIM_ATT

    def stack(q, k, v, is_a):
        zero, one = jnp.zeros((), q.dtype), jnp.ones((), q.dtype)
        return (jnp.where(is_a, q, zero), jnp.where(is_a, zero, q), k,
                jnp.where(is_a, v, one), jnp.where(is_a, one, v))

    q, k, v = q_ref[0], k_ref[0], v_ref[0]
    for i, x in enumerate(stack(q, k, v, head_a_full)):
        src1_ref[i] = x
    for i, x in enumerate(stack(q.astype(F32), k.astype(F32), v.astype(F32), head_a_full)):
        src4_ref[i] = x
    blk = 16 * 16
    pa = lax.broadcasted_iota(jnp.int32, (blk, blk), 0)
    pb = lax.broadcasted_iota(jnp.int32, (blk, blk), 1)
    perm = jnp.where(pb == 16 * (pa & 15) + (pa >> 4), 1.0, 0.0).astype(BF16)
    is_a_blk = lax.broadcasted_iota(jnp.int32, (16, 16, LANES), 2) < HEAD_DIM_ATT
    for j in range(seq // blk):
        rows = slice(j * blk, (j + 1) * blk)
        parts = [jnp.dot(perm, x[rows], preferred_element_type=F32).astype(BF16).reshape(16, 16, LANES)
                 for x in (q, k, v)]
        for i, x in enumerate(stack(*parts, is_a_blk)):
            src16_ref[i, :, 16 * j:16 * (j + 1), :] = x
    u = lax.broadcasted_iota(jnp.int32, (N_BACK, 2 * N_BACK), 0)
    w = lax.broadcasted_iota(jnp.int32, (N_BACK, 2 * N_BACK), 1)
    bias_ref[...] = jnp.where((w >= u) & (w <= u + N_BACK), jnp.finfo(F32).max, NEG_INF)
    head_a = lax.broadcasted_iota(jnp.int32, (N_BACK, LANES), 1) < HEAD_DIM_ATT

    def partials(gets):
        staged = []
        for get, has_prev in gets:
            if has_prev:
                both = lambda i, get=get: jnp.concatenate([get(i, True), get(i, False)], axis=0)
                bias = bias_ref[...]
            else:
                both = lambda i, get=get: get(i, False)
                bias = bias_ref[:, N_BACK:]
            kk = both(2)
            heads = []
            for qi in (0, 1):
                s = lax.dot_general(get(qi, False), kk, (((1,), (1,)), ((), ())),
                                    preferred_element_type=F32)
                s = jnp.minimum(s, bias)
                mh = jnp.max(s, axis=-1, keepdims=True)
                heads.append((mh, jnp.exp2(s - mh).astype(BF16)))
            staged.append((both, heads))
        out = []
        for both, ((m_a, p_a), (m_b, p_b)) in staged:
            pv_a = jnp.dot(p_a, both(3), preferred_element_type=F32)
            pv_b = jnp.dot(p_b, both(4), preferred_element_type=F32)
            acc = jnp.where(head_a, pv_a, pv_b)
            den = pltpu.roll(jnp.where(head_a, pv_b, pv_a), HEAD_DIM_ATT, 1)
            out.append((jnp.where(head_a, m_a, m_b), den, acc))
        return out

    def get4(qstart, kprev_start):
        def get(i, prev):
            start = kprev_start if prev else qstart
            return src4_ref[i, pl.ds(start, N_BACK, stride=4), :].astype(BF16)
        return get, kprev_start is not None

    def get1(qstart, kprev_start):
        def get(i, prev):
            return src1_ref[i, pl.ds(kprev_start if prev else qstart, N_BACK), :]
        return get, kprev_start is not None

    def keep4(blocks):
        res = partials([get4(qs, ks) for qs, ks in blocks])
        for (qs, _), (m, den, acc) in zip(blocks, res):
            rows = pl.ds(qs, N_BACK, stride=4)
            st4_ref[0, rows, :] = m
            st4_ref[1, rows, :] = den
            st4_ref[2, rows, :] = acc

    def finish(blocks):
        res = partials([get1(qs, ks) for qs, ks in blocks])
        for (qs, _), part in zip(blocks, res):
            rows = pl.ds(qs, N_BACK)
            parts = [part] + [tuple(st[i, rows, :] for i in range(3)) for st in (st4_ref, st16_ref)]
            m_all = functools.reduce(jnp.maximum, [pt[0] for pt in parts])
            wts = [jnp.exp2(pt[0] - m_all) for pt in parts]
            den = sum(wt * pt[1] for wt, pt in zip(wts, parts))
            num = sum(wt * pt[2] for wt, pt in zip(wts, parts))
            o_ref[0, rows, :] = (num / den * g_ref[0, rows, :].astype(F32)).astype(BF16)

    def body16(g, c):
        rs = [g * group + rr for rr in range(group)]
        res = partials([(lambda i, prev, r=r: src16_ref[i, r], False) for r in rs])
        for r, part in zip(rs, res):
            for i in range(3):
                st16r_ref[i, r] = part[i]
        return c
    lax.fori_loop(0, 16 // group, body16, 0)
    for i in range(3):
        st16_ref[i] = jnp.swapaxes(st16r_ref[i], 0, 1).reshape(seq, LANES)

    blocks4 = lambda cc: [(r + 4 * N_BACK * cc, r + 4 * N_BACK * (cc - 1)) for r in range(4)]
    keep4([(r, None) for r in range(4)] + blocks4(1))

    def body4(g, c):
        keep4(blocks4(2 * g) + blocks4(2 * g + 1))
        return c
    lax.fori_loop(1, seq // 4 // N_BACK // 2, body4, 0)

    finish([(0, None)] + [(cc * N_BACK, (cc - 1) * N_BACK) for cc in range(1, group)])

    def body1(g, c):
        starts = [pl.multiple_of((g * group + rr) * N_BACK, N_BACK) for rr in range(group)]
        finish([(st, st - N_BACK) for st in starts])
        return c
    lax.fori_loop(1, seq // N_BACK // group, body1, 0)


def _attn_sample_kernel(q_ref, kn_ref, vn_ref, knt_ref, vnt_ref, g_ref, ck_ref, cv_ref,
                        o_ref, sk_ref, sv_ref, clamp_ref, *, t_new):
    b = pl.program_id(0)
    wb = ck_ref.shape[3]
    hd = HEAD_DIM_ATT
    rows = q_ref.shape[1]

    @pl.when(b == 0)
    def _():
        delta = (wb + lax.broadcasted_iota(jnp.int32, (rows, wb), 0)
                 - lax.broadcasted_iota(jnp.int32, (rows, wb), 1))
        for d, dil in enumerate(DILATIONS):
            ok = ((delta & (dil - 1)) == 0) & (delta >= dil) & (delta <= N_BACK * dil)
            clamp_ref[d] = jnp.where(ok, jnp.finfo(F32).max, NEG_INF)

    tq = lax.broadcasted_iota(jnp.int32, (rows, rows), 0)
    tk = lax.broadcasted_iota(jnp.int32, (rows, rows), 1)
    new_ok = [((tk <= tq) if dil == 1 else (tk == tq)) & (tk < t_new) for dil in DILATIONS]

    lane = lax.broadcasted_iota(jnp.int32, (hd, LANES), 1)
    shift_new = (LANES - t_new) - b * t_new
    nt = (((1,), (1,)), ((), ()))
    outs = []
    for h in range(N_HEADS_ATT):
        hs = slice(h * hd, (h + 1) * hd)
        kt = ck_ref[0, h]
        vt = cv_ref[0, h]
        for old, new_ref, out_ref in ((kt, knt_ref, sk_ref), (vt, vnt_ref, sv_ref)):
            moved = pltpu.roll(old, wb - t_new, axis=1)
            new_cols = pltpu.roll(new_ref[hs, :], shift_new, axis=1)
            out_ref[0, h, :, 0:wb - LANES] = moved[:, 0:wb - LANES]
            out_ref[0, h, :, wb - LANES:wb] = jnp.where(lane < LANES - t_new,
                                                        moved[:, wb - LANES:wb], new_cols)
        qh = q_ref[0, :, hs]
        kn_h = kn_ref[0, :, hs].astype(BF16)
        vn_h = vn_ref[0, :, hs].astype(BF16)
        s_old = jnp.dot(qh, kt.astype(BF16), preferred_element_type=F32)
        s_new = lax.dot_general(qh, kn_h, nt, preferred_element_type=F32)
        ps, pes, ms = [], [], []
        for d in range(len(DILATIONS)):
            so = jnp.minimum(s_old, clamp_ref[d])
            sn = jnp.where(new_ok[d], s_new, NEG_INF)
            m = jnp.maximum(jnp.max(so, axis=-1, keepdims=True), jnp.max(sn, axis=-1, keepdims=True))
            ps.append(jnp.exp2(so - m))
            pes.append(jnp.exp2(sn - m))
            ms.append(m)
        acc = lax.dot_general(jnp.concatenate(ps, axis=0).astype(BF16), vt.astype(BF16), nt,
                              preferred_element_type=F32)
        acc = acc + jnp.dot(jnp.concatenate(pes, axis=0).astype(BF16), vn_h,
                            preferred_element_type=F32)
        m_all = functools.reduce(jnp.maximum, ms)
        den = 0.0
        num = 0.0
        for d in range(len(DILATIONS)):
            wgt = jnp.exp2(ms[d] - m_all)
            den = den + wgt * (jnp.sum(ps[d], axis=-1, keepdims=True)
                               + jnp.sum(pes[d], axis=-1, keepdims=True))
            num = num + wgt * acc[d * rows:(d + 1) * rows]
        outs.append(num / den)
    att = jnp.concatenate(outs, axis=1)
    o_ref[0] = (att * g_ref[0].astype(F32)).astype(BF16)


def _scan_lanes(x, op, fill):
    lane = lax.broadcasted_iota(jnp.int32, x.shape, 1)
    d = 1
    while d < x.shape[1]:
        x = op(x, jnp.where(lane >= d, pltpu.roll(x, d, 1), fill))
        d *= 2
    return x


def _gate_scans(gt):
    b = _scan_lanes(pltpu.roll(gt, N_HEADS_MLSTM, 0), jnp.add, 0.0)
    cm = _scan_lanes(gt - b, jnp.maximum, NEG_INF)
    row = lax.broadcasted_iota(jnp.int32, gt.shape, 0)
    return jnp.where(row < N_HEADS_MLSTM, b, pltpu.roll(cm, N_HEADS_MLSTM, 0))


def _mlstm_kernel(*refs, zero_init, scans_given, fuse_out):
    refs = list(refs)
    q_ref, k_ref, v_ref, gt_ref, so_ref, gb_ref, ng_ref = refs[:7]
    del refs[:7]
    if not zero_init:
        c0_ref, n0_ref, m0_ref = refs[:3]
        del refs[:3]
    if fuse_out:
        x_ref, ya_ref, wo_ref, fg_ref = refs[:4]
        del refs[:4]
    y_ref, c_out_ref, n_out_ref, m_out_ref, cn_ref, m_ref = refs
    c_idx = pl.program_id(1)
    nb = q_ref.shape[0]
    L = CHUNK
    dh = HEAD_DIM_MLSTM
    nh = N_HEADS_MLSTM

    @pl.when(c_idx == 0)
    def _():
        if zero_init:
            cn_ref[...] = jnp.zeros(cn_ref.shape, F32)
            m_ref[...] = jnp.zeros(m_ref.shape, F32)
        else:
            m_ref[...] = m0_ref[...]
            for bb in range(nb):
                for h in range(nh):
                    n_rows = jnp.broadcast_to(n0_ref[bb, h:h + 1, :], (dh, dh))
                    cn_ref[bb, h] = jnp.concatenate([c0_ref[bb, h], n_rows.T], axis=1)

    tri_t = lax.broadcasted_iota(jnp.int32, (L, L), 0)
    tri_s = lax.broadcasted_iota(jnp.int32, (L, L), 1)
    causal = tri_t >= tri_s
    ones_blk = jnp.ones((L, dh), BF16)
    nt = (((1,), (1,)), ((), ()))
    pairs = [(bb, h) for bb in range(nb) for h in range(nh)]
    sl = lambda h: slice(h * dh, (h + 1) * dh)

    qk = {(bb, h): lax.dot_general(q_ref[bb, :, sl(h)], k_ref[bb, :, sl(h)], nt,
                                   preferred_element_type=F32) for bb, h in pairs}
    cn_old = {p: cn_ref[p[0], p[1]] for p in pairs}
    if fuse_out:
        mix_a = x_ref[...].reshape(nb * L, D_MODEL) + jnp.dot(
            ya_ref[...].reshape(nb * L, D_ATT), wo_ref[0:D_ATT, :], preferred_element_type=F32)

    rows, cols, decays, w_rows = [], [], [], []
    for bb in range(nb):
        if scans_given:
            i_row = gt_ref[bb, 0:SUBLANES, :]
            sc = gt_ref[bb, SUBLANES:2 * SUBLANES, :]
        else:
            i_row = gt_ref[bb]
            sc = _gate_scans(i_row)
        b = sc
        cm = pltpu.roll(sc, nh, 0)
        m_prev = m_ref[bb]
        m_t = jnp.maximum(m_prev + b, b + cm)
        inter = jnp.exp(m_prev + b - m_t)
        m_last = jnp.broadcast_to(m_t[:, L - 1:L], m_t.shape)
        b_last = jnp.broadcast_to(b[:, L - 1:L], b.shape)
        decay = jnp.exp(m_prev + b_last - m_last)
        w_row = jnp.exp(b_last - b + i_row - m_last)
        m_ref[bb] = m_last
        stack = jnp.concatenate([b - m_t, inter, jnp.exp(-m_t),
                                 jnp.zeros((L - 3 * SUBLANES, L), F32)], axis=0)
        cols.append(stack.T)
        rows.append(i_row - b)
        w_rows.append(w_row)
        decays.append(jnp.concatenate([decay, decay], axis=1))
    col = lambda bb, kind, h: cols[bb][:, kind * SUBLANES + h:kind * SUBLANES + h + 1]

    sqk = {}
    for bb, h in pairs:
        dlog = col(bb, 0, h) + rows[bb][h:h + 1, :]
        sqk[bb, h] = (qk[bb, h] * jnp.exp(jnp.where(causal, dlog, NEG_INF))).astype(BF16)
    v_one = {(bb, h): jnp.concatenate([v_ref[bb, :, sl(h)], ones_blk], axis=1) for bb, h in pairs}
    tots = {}
    for bb, h in pairs:
        q_dec = (col(bb, 1, h) * q_ref[bb, :, sl(h)].astype(F32)).astype(BF16)
        lhs = jnp.concatenate([sqk[bb, h], q_dec], axis=1)
        rhs = jnp.concatenate([v_one[bb, h], cn_old[bb, h].astype(BF16)], axis=0)
        tots[bb, h] = jnp.dot(lhs, rhs, preferred_element_type=F32)
    yb = {}
    for bb, h in pairs:
        tot = tots[bb, h]
        hh = tot[:, :dh] / jnp.maximum(jnp.abs(tot[:, dh:]), col(bb, 2, h))
        hh = so_ref[bb, :, sl(h)].astype(F32) * hh
        hh = hh * lax.rsqrt(jnp.mean(hh * hh, axis=-1, keepdims=True) + EPS)
        hh = hh * ng_ref[:, sl(h)]
        yb[bb, h] = (hh * gb_ref[bb, :, sl(h)].astype(F32)).astype(BF16)
    if fuse_out:
        yb_all = jnp.concatenate([jnp.concatenate([yb[bb, h] for h in range(nh)], axis=1)
                                  for bb in range(nb)], axis=0)
        mix = mix_a + jnp.dot(yb_all, wo_ref[D_ATT:, :], preferred_element_type=F32)
        res = _rmsnorm(mix, fg_ref[...])
        y_ref[...] = res.reshape(nb, L, D_MODEL)
    else:
        for bb, h in pairs:
            y_ref[bb, :, sl(h)] = yb[bb, h]
    for bb, h in pairs:
        kt_w = k_ref[bb, :, sl(h)].T.astype(F32) * w_rows[bb][h:h + 1, :]
        upd = jnp.dot(kt_w.astype(BF16), v_one[bb, h], preferred_element_type=F32)
        cn_ref[bb, h] = decays[bb][h:h + 1, :] * cn_old[bb, h] + upd

    @pl.when(c_idx == pl.num_programs(1) - 1)
    def _():
        m_out_ref[...] = m_ref[...]
        for bb in range(nb):
            n_rows = []
            for h in range(nh):
                cn = cn_ref[bb, h]
                c_out_ref[bb, h] = cn[:, :dh]
                n_rows.append(cn[:, dh:].T[0:1, :])
            n_out_ref[bb] = jnp.concatenate(n_rows, axis=0)


def _outproj_kernel(x_ref, ya_ref, yb_ref, w_ref, g_ref, o_ref):
    rows = x_ref.shape[0]
    sub = min(OUTPROJ_SUBTILE, rows)
    for r0 in range(0, rows, sub):
        rs = pl.ds(r0, sub)
        mix = jnp.dot(ya_ref[rs, :], w_ref[0:D_ATT, :], preferred_element_type=F32)
        mix = mix + jnp.dot(yb_ref[rs, :], w_ref[D_ATT:, :], preferred_element_type=F32)
        o_ref[rs, :] = _rmsnorm(x_ref[rs, :] + mix, g_ref[...])


def _rope_tables(pos):
    half = ROT_DIM // 2
    f32 = np.float32
    inv = f32(ROPE_THETA) ** (-np.arange(half, dtype=f32) * f32(2.0) / f32(ROT_DIM))
    ang = np.asarray(pos, dtype=f32)[:, None] * inv[None, :].astype(f32)
    cos, sin = np.cos(ang).astype(f32), np.sin(ang).astype(f32)
    n = ang.shape[0]
    one = np.ones((n, HEAD_DIM_ATT - ROT_DIM), f32)
    zero = np.zeros((n, HEAD_DIM_ATT - ROT_DIM), f32)
    zh = np.zeros((n, half), f32)
    cos_t = np.concatenate([cos, cos, one], axis=1)
    lo_t = np.concatenate([-sin, zh, zero], axis=1)
    hi_t = np.concatenate([zh, sin, zero], axis=1)
    rep = lambda t: np.concatenate([t, t], axis=1)
    return rep(cos_t), rep(lo_t), rep(hi_t)


def _params(sem):
    return pltpu.CompilerParams(dimension_semantics=sem, vmem_limit_bytes=VMEM_LIMIT)


def _const_spec(shape):
    return pl.BlockSpec(shape, lambda *_: (0,) * len(shape))


def _inproj_prompt(x, norm_g, w_main, w_gate, conv_w, conv_b, bif):
    B, S, _ = x.shape
    tm = ROW_TILE
    cos, lo, hi = _rope_tables(np.arange(S))
    tile = lambda width: pl.BlockSpec((1, tm, width), lambda b, j: (b, j, 0))
    tab = pl.BlockSpec((tm, LANES), lambda b, j: (j, 0))
    bf = lambda: jax.ShapeDtypeStruct((B, S, D_ATT), BF16)
    f5 = lambda: jax.ShapeDtypeStruct((B, N_HEADS_ATT, HEAD_DIM_ATT, S), F32)
    tile5 = pl.BlockSpec((1, N_HEADS_ATT, HEAD_DIM_ATT, tm), lambda b, j: (b, 0, 0, j))
    out_shape = (bf(), bf(), bf(), f5(), f5(), bf(), bf(), bf(), bf(), bf(), bf(),
                 jax.ShapeDtypeStruct((B, 2 * SUBLANES, S), F32),
                 jax.ShapeDtypeStruct((B, CONV_WIDTH - 1, 2 * D_MLSTM), F32))
    out_specs = tuple([tile(D_ATT)] * 3 + [tile5] * 2 + [tile(D_ATT)] * 6) + (
        pl.BlockSpec((1, 2 * SUBLANES, tm), lambda b, j: (b, 0, j)),
        pl.BlockSpec((1, CONV_WIDTH - 1, 2 * D_MLSTM), lambda b, j: (b, 0, 0)))
    return pl.pallas_call(
        _inproj_prompt_kernel,
        grid=(B, S // tm),
        in_specs=[tile(D_MODEL), _const_spec((1, D_MODEL)), _const_spec((D_MODEL, OFF_G)),
                  _const_spec(w_gate.shape),
                  _const_spec((CONV_WIDTH, 2 * D_MLSTM)), _const_spec((1, 2 * D_MLSTM)),
                  _const_spec((1, LANES)), tab, tab, tab],
        out_specs=out_specs,
        out_shape=out_shape,
        scratch_shapes=[pltpu.VMEM((tm + 2 * SUBLANES, 2 * D_MLSTM), F32)],
        compiler_params=_params(("arbitrary", "arbitrary")),
        name="inproj_prompt",
    )(x, norm_g, w_main, w_gate, conv_w, conv_b, bif, cos, lo, hi)


def _inproj_sample(x2, norm_g, w_main, w_gate, conv_w, conv_b, bif, hist, t_new):
    rows = x2.shape[0]
    pos = PAST_LEN + np.arange(t_new)
    cos, lo, hi = (np.tile(t, (rows // t_new, 1)) for t in _rope_tables(pos))
    bf = lambda: jax.ShapeDtypeStruct((rows, D_ATT), BF16)
    f3 = lambda: jax.ShapeDtypeStruct((rows, D_ATT), F32)
    f5 = lambda: jax.ShapeDtypeStruct((D_ATT, rows), F32)
    out_shape = (bf(), f3(), f3(), f5(), f5(), bf(),
                 jax.ShapeDtypeStruct((rows, 5 * D_MLSTM), BF16),
                 jax.ShapeDtypeStruct((rows, LANES), F32),
                 jax.ShapeDtypeStruct((rows, 2 * D_MLSTM), F32))
    return pl.pallas_call(
        functools.partial(_inproj_sample_kernel, t_new=t_new),
        out_shape=out_shape,
        compiler_params=pltpu.CompilerParams(vmem_limit_bytes=VMEM_LIMIT),
        name="inproj_sample",
    )(x2, norm_g, w_main, w_gate, conv_w, conv_b, bif, cos, lo, hi, *hist)


def _attn_prompt(q, k, v, gate):
    B, S, _ = q.shape
    spec = pl.BlockSpec((1, S, LANES), lambda b, h: (b, 0, h))
    return pl.pallas_call(
        _attn_prompt_kernel,
        grid=(B, D_ATT // LANES),
        in_specs=[spec, spec, spec, spec],
        out_specs=spec,
        out_shape=jax.ShapeDtypeStruct((B, S, D_ATT), BF16),
        scratch_shapes=[
            pltpu.VMEM((5, S, LANES), BF16), pltpu.VMEM((5, S, LANES), F32),
            pltpu.VMEM((5, 16, S // 16, LANES), BF16),
            pltpu.VMEM((N_BACK, 2 * N_BACK), F32),
            pltpu.VMEM((3, 16, S // 16, LANES), F32),
            pltpu.VMEM((3, S, LANES), F32), pltpu.VMEM((3, S, LANES), F32)],
        compiler_params=_params(("arbitrary", "arbitrary")),
        name="attn_prompt",
    )(q, k, v, gate)


def _attn_sample(q, kn, vn, knt, vnt, gate, ck, cv, t_new):
    B, rows, _ = q.shape
    wb = ck.shape[3]
    small = pl.BlockSpec((1, rows, D_ATT), lambda b: (b, 0, 0))
    big = pl.BlockSpec((1, N_HEADS_ATT, HEAD_DIM_ATT, wb), lambda b: (b, 0, 0, 0))
    win_shape = jax.ShapeDtypeStruct((B, N_HEADS_ATT, HEAD_DIM_ATT, wb), F32)
    return pl.pallas_call(
        functools.partial(_attn_sample_kernel, t_new=t_new),
        grid=(B,),
        in_specs=[small, small, small, _const_spec(knt.shape), _const_spec(vnt.shape), small, big, big],
        out_specs=(small, big, big),
        out_shape=(jax.ShapeDtypeStruct((B, rows, D_ATT), BF16), win_shape, win_shape),
        scratch_shapes=[pltpu.VMEM((len(DILATIONS), rows, wb), F32)],
        compiler_params=_params(("arbitrary",)),
        name="attn_sample",
    )(q, kn, vn, knt, vnt, gate, ck, cv)


def _mlstm(q, k, v, gates_t, sig_o, gate_b, norm_g, state=None, out_proj=None,
           lane_blocks=(0, 0, 0, 0, 0)):
    B, S, _ = q.shape
    nc = S // CHUNK
    nb = MLSTM_BATCH
    nh, dh = N_HEADS_MLSTM, HEAD_DIM_MLSTM
    grows = gates_t.shape[1]
    tile = lambda width, lb=0: pl.BlockSpec((nb, CHUNK, width), lambda b, c: (b, c, lb))
    op = [tile(D_MLSTM, lb) for lb in lane_blocks]
    gspec = pl.BlockSpec((nb, grows, CHUNK), lambda b, c: (b, 0, c))
    c_spec = pl.BlockSpec((nb, nh, dh, dh), lambda b, c: (b, 0, 0, 0))
    n_spec = pl.BlockSpec((nb, nh, dh), lambda b, c: (b, 0, 0))
    m_spec = pl.BlockSpec((nb, SUBLANES, LANES), lambda b, c: (b, 0, 0))
    zero_init = state is None
    fuse_out = out_proj is not None
    in_specs = op[:3] + [gspec] + op[3:] + [_const_spec((1, D_MLSTM))]
    args = [q, k, v, gates_t, sig_o, gate_b, norm_g]
    if not zero_init:
        in_specs += [c_spec, n_spec, m_spec]
        args += list(state)
    if fuse_out:
        in_specs += [tile(D_MODEL), tile(D_ATT), _const_spec((D_ATT + D_MLSTM, D_MODEL)),
                     _const_spec((1, D_MODEL))]
        args += list(out_proj)
        y_spec, y_shape = tile(D_MODEL), jax.ShapeDtypeStruct((B, S, D_MODEL), F32)
    else:
        y_spec, y_shape = tile(D_MLSTM), jax.ShapeDtypeStruct((B, S, D_MLSTM), BF16)
    return pl.pallas_call(
        functools.partial(_mlstm_kernel, zero_init=zero_init, scans_given=grows == 2 * SUBLANES,
                          fuse_out=fuse_out),
        grid=(B // nb, nc),
        in_specs=in_specs,
        out_specs=(y_spec, c_spec, n_spec, m_spec),
        out_shape=(y_shape,
                   jax.ShapeDtypeStruct((B, nh, dh, dh), F32),
                   jax.ShapeDtypeStruct((B, nh, dh), F32),
                   jax.ShapeDtypeStruct((B, SUBLANES, LANES), F32)),
        scratch_shapes=[pltpu.VMEM((nb, nh, dh, 2 * dh), F32),
                        pltpu.VMEM((nb, SUBLANES, LANES), F32)],
        compiler_params=_params(("arbitrary", "arbitrary")),
        name="mlstm_prompt" if zero_init else "mlstm_sample",
    )(*args)


def _outproj(x2, ya, yb, w_out, final_g):
    rows = x2.shape[0]
    tm = min(OUTPROJ_TILE, rows)
    tile = lambda width: pl.BlockSpec((tm, width), lambda i: (i, 0))
    return pl.pallas_call(
        _outproj_kernel,
        grid=(rows // tm,),
        in_specs=[tile(D_MODEL), tile(D_ATT), tile(D_MLSTM),
                  _const_spec((D_ATT + D_MLSTM, D_MODEL)), _const_spec((1, D_MODEL))],
        out_specs=tile(D_MODEL),
        out_shape=jax.ShapeDtypeStruct((rows, D_MODEL), F32),
        compiler_params=_params(("arbitrary",)),
        name="outproj",
    )(x2, ya, yb, w_out, final_g)


def kernel(x_prompt, x_sample, cache_win_k, cache_win_v, state_conv, state_C, state_n, state_m,
           norm_g, w_in, conv_w, conv_b, b_i, b_f, mlstm_norm_g, w_out, final_norm_g):
    assert w_in.shape[0] == 1, "single-layer model"
    B, S, D = x_prompt.shape
    DB, T, _ = x_sample.shape
    HB, DK = N_HEADS_MLSTM, HEAD_DIM_MLSTM
    wb = cache_win_k.shape[2]
    assert S % ROW_TILE == 0 and S == 16 * N_BACK and wb >= 16 * N_BACK and CONV_WIDTH - 1 <= T <= SUBLANES and DB * T == LANES

    w_main = w_in[0].astype(BF16)
    w_gate = jnp.pad(w_main[:, OFF_G:], ((0, 0), (0, LANES - 2 * HB)))
    w_o = w_out[0].astype(BF16)
    g_in = norm_g[0][None, :]
    cw, cb = conv_w[0], conv_b[0][None, :]
    bif = jnp.pad(jnp.concatenate([b_i[0], b_f[0]]), (0, LANES - 2 * HB))[None, :]
    ng = mlstm_norm_g[0][None, :]
    g_fin = final_norm_g[None, :]

    (q_p, k_p, v_p, pk, pv, ga_p, qm_p, km_p, vb_p, so_p, gb_p, gt_p, p_conv) = _inproj_prompt(
        x_prompt, g_in, w_main, w_gate, cw, cb, bif)
    ya_p = _attn_prompt(q_p, k_p, v_p, ga_p)
    y_prompt, c_p, n_p, m_p = _mlstm(qm_p, km_p, vb_p, gt_p, so_p, gb_p, ng,
                                     out_proj=(x_prompt, ya_p, w_o, g_fin))

    sc = state_conv[0]
    zrow = jnp.zeros((DB, 1, 2 * D_MLSTM), F32)
    hist = []
    for sh in (1, 2, 3):
        rows_ = [sc[:, CONV_WIDTH - 1 + t - sh:CONV_WIDTH + t - sh] if t < sh else zrow
                 for t in range(T)]
        hist.append(jnp.concatenate(rows_, axis=1).reshape(DB * T, 2 * D_MLSTM))
    (q_s, kn, vn, knt, vnt, ga_s, mix_s, gates_s, qk_s) = _inproj_sample(
        x_sample.reshape(DB * T, D), g_in, w_main, w_gate, cw, cb, bif, hist, T)
    r3 = lambda a: a.reshape(DB, T, a.shape[-1])
    pad8 = lambda a: jnp.pad(r3(a), ((0, 0), (0, SUBLANES - T), (0, 0)))
    to_hdp = lambda c: jnp.transpose(c[0], (0, 2, 3, 1))
    from_hdp = lambda c: jnp.transpose(c, (0, 3, 1, 2))[None]
    ya_s, s_k, s_v = _attn_sample(pad8(q_s), pad8(kn), pad8(vn), knt, vnt, pad8(ga_s),
                                  to_hdp(cache_win_k), to_hdp(cache_win_v), T)
    ya_s = ya_s[:, :T]

    pad_t = lambda a: jnp.pad(r3(a), ((0, 0), (0, CHUNK - T), (0, 0)))
    g3 = r3(gates_s)[:, :, :SUBLANES]
    null_gate = jnp.concatenate([jnp.full((HB,), NEG_INF, F32), jnp.zeros((HB,), F32)])
    gt_s = jnp.concatenate([g3, jnp.broadcast_to(null_gate, (DB, CHUNK - T, SUBLANES))], axis=1)
    gt_s = gt_s.transpose(0, 2, 1)
    m0 = jnp.broadcast_to(jnp.pad(state_m[0], ((0, 0), (0, SUBLANES - HB)))[..., None],
                          (DB, SUBLANES, LANES))
    mix_p = pad_t(mix_s)
    yb_s, c_s, n_s, m_s = _mlstm(mix_p, mix_p, mix_p, gt_s, mix_p, mix_p, ng,
                                 state=(state_C[0], state_n[0], m0), lane_blocks=(0, 1, 2, 3, 4))
    y_sample = _outproj(x_sample.reshape(DB * T, D), ya_s.reshape(DB * T, D_ATT),
                        yb_s[:, :T].reshape(DB * T, D_MLSTM), w_o, g_fin).reshape(DB, T, D)

    return (y_prompt, y_sample,
            from_hdp(pk), from_hdp(pv), p_conv[None],
            c_p[None], n_p[None], m_p[None, :, :HB, 0],
            from_hdp(s_k), from_hdp(s_v), r3(qk_s)[None, :, T - (CONV_WIDTH - 1):],
            c_s[None], n_s[None], m_s[None, :, :HB, 0])
```

```python
import functools
import math

import jax
import jax.numpy as jnp
import numpy as np
from jax import lax
from jax.experimental import pallas as pl
from jax.experimental.pallas import tpu as pltpu

F32 = jnp.float32
BF16 = jnp.bfloat16

D_MODEL = 1024
D_ATT = 512
N_HEADS_ATT = 8
HEAD_DIM_ATT = 64
D_MLSTM = 512
N_HEADS_MLSTM = 4
HEAD_DIM_MLSTM = 128
ROT_DIM = 16
ROPE_THETA = 500000.0
PAST_LEN = 16384
DILATIONS = (1, 4, 16)
N_BACK = 128
CONV_WIDTH = 4
CHUNK = 128
EPS = 1e-6
NEG_INF = -1e30
LOG2_E = math.log2(math.e)

LANES = 128
SUBLANES = 8
OFF_QA, OFF_KA, OFF_VA, OFF_ZA = 0, 512, 1024, 1536
OFF_QB, OFF_VB, OFF_OB, OFF_ZB, OFF_G = 2048, 3072, 3584, 4096, 4608
ROW_TILE = 512
INPROJ_SUBTILE = 128
OUTPROJ_TILE = 2048
OUTPROJ_SUBTILE = 256
MLSTM_BATCH = 4
VMEM_LIMIT = 56 * 1024 * 1024


def _silu(x):
    return x * jax.nn.sigmoid(x)


def _rmsnorm(x, g):
    return x * lax.rsqrt(jnp.mean(x * x, axis=-1, keepdims=True) + EPS) * g


def _rope(u, cos, sin_lo, sin_hi):
    outs = []
    for c in range(u.shape[1] // LANES):
        xs = u[:, c * LANES:(c + 1) * LANES]
        outs.append(xs * cos + pltpu.roll(xs, LANES - ROT_DIM // 2, 1) * sin_lo
                    + pltpu.roll(xs, ROT_DIM // 2, 1) * sin_hi)
    return jnp.concatenate(outs, axis=1)


def _gate_block(ug, bif):
    gz = ug + bif
    lane = lax.broadcasted_iota(jnp.int32, gz.shape, 1)
    logf = jnp.minimum(gz, 0.0) - jnp.log1p(jnp.exp(-jnp.abs(gz)))
    return jnp.where(lane < N_HEADS_MLSTM, gz, logf)


def _segments(hn, w_ref, wg_ref):
    def seg(off, width):
        rhs = wg_ref[...] if off == OFF_G else w_ref[:, off:off + width]
        return jnp.dot(hn, rhs, preferred_element_type=F32)
    return seg


def _inproj_common(seg, cos, sin_lo, sin_hi, q_ref, ga_ref, vb_ref, so_ref, gb_ref):
    q = _rope(seg(OFF_QA, D_ATT), cos, sin_lo, sin_hi) * (HEAD_DIM_ATT ** -0.5 * LOG2_E)
    q_ref[...] = q.astype(BF16).reshape(q_ref.shape)
    k = _rope(seg(OFF_KA, D_ATT), cos, sin_lo, sin_hi)
    v = seg(OFF_VA, D_ATT)
    ga_ref[...] = _silu(seg(OFF_ZA, D_ATT)).astype(BF16).reshape(ga_ref.shape)
    vb_ref[...] = seg(OFF_VB, D_MLSTM).astype(BF16).reshape(vb_ref.shape)
    so_ref[...] = jax.nn.sigmoid(seg(OFF_OB, D_MLSTM)).astype(BF16).reshape(so_ref.shape)
    gb_ref[...] = _silu(seg(OFF_ZB, D_MLSTM)).astype(BF16).reshape(gb_ref.shape)
    return k, v


def _inproj_prompt_kernel(x_ref, g_ref, w_ref, wg_ref, cw_ref, cb_ref, bif_ref, cos_ref, slo_ref,
                          shi_ref,
                          q_ref, kb_ref, vbf_ref, pk_ref, pv_ref, ga_ref, qm_ref, km_ref, vb_ref,
                          so_ref, gb_ref, gt_ref, pconv_ref, xp_ref):
    j = pl.program_id(1)
    tm = x_ref.shape[1]
    sub = INPROJ_SUBTILE
    assert CHUNK == LANES and tm % sub == 0 and sub % CHUNK == 0

    @pl.when(j == 0)
    def _():
        xp_ref[0:SUBLANES, :] = jnp.zeros((SUBLANES, 2 * D_MLSTM), F32)

    @pl.when(j > 0)
    def _():
        xp_ref[0:SUBLANES, :] = xp_ref[tm:tm + SUBLANES, :]

    for r0 in range(0, tm, sub):
        rs = pl.ds(r0, sub)
        part = lambda ref: ref.at[0, rs, :]
        hn = _rmsnorm(x_ref[0, rs, :], g_ref[...]).astype(BF16)
        seg = _segments(hn, w_ref, wg_ref)

        gates = _gate_block(seg(OFF_G, LANES), bif_ref[...])
        for i in range(sub // CHUNK):
            cs = slice(r0 + i * CHUNK, r0 + (i + 1) * CHUNK)
            gt = gates[i * CHUNK:(i + 1) * CHUNK, :].T[0:SUBLANES, :]
            gt_ref[0, 0:SUBLANES, cs] = gt
            gt_ref[0, SUBLANES:2 * SUBLANES, cs] = _gate_scans(gt)

        k, v = _inproj_common(seg, cos_ref[rs, :], slo_ref[rs, :], shi_ref[rs, :],
                              part(q_ref), part(ga_ref), part(vb_ref), part(so_ref), part(gb_ref))
        kb_ref[0, rs, :] = k.astype(BF16)
        vbf_ref[0, rs, :] = v.astype(BF16)
        pk_ref[0, :, :, r0:r0 + sub] = k.T.reshape(N_HEADS_ATT, HEAD_DIM_ATT, sub)
        pv_ref[0, :, :, r0:r0 + sub] = v.T.reshape(N_HEADS_ATT, HEAD_DIM_ATT, sub)

        base = SUBLANES + r0
        xp_ref[base:base + sub, :] = seg(OFF_QB, 2 * D_MLSTM)
        y = cb_ref[...] + xp_ref[base:base + sub, :] * cw_ref[3:4, :]
        for jj in range(CONV_WIDTH - 1):
            sh = CONV_WIDTH - 1 - jj
            y = y + xp_ref[base - sh:base - sh + sub, :] * cw_ref[jj:jj + 1, :]
        y = _silu(y)
        qm_ref[0, rs, :] = y[:, :D_MLSTM].astype(BF16)
        km_ref[0, rs, :] = (y[:, D_MLSTM:] * (HEAD_DIM_MLSTM ** -0.5)).astype(BF16)

    @pl.when(j == pl.num_programs(1) - 1)
    def _():
        pconv_ref[0] = xp_ref[tm + SUBLANES - (CONV_WIDTH - 1):tm + SUBLANES, :]


def _inproj_sample_kernel(x_ref, g_ref, w_ref, wg_ref, cw_ref, cb_ref, bif_ref, cos_ref, slo_ref,
                          shi_ref,
                          h1_ref, h2_ref, h3_ref,
                          q_ref, kn_ref, vn_ref, knt_ref, vnt_ref, ga_ref, mix_ref, gates_ref, qk_ref,
                          *, t_new):
    qm_ref, km_ref, vb_ref, so_ref, gb_ref = (
        mix_ref.at[:, pl.ds(i * D_MLSTM, D_MLSTM)] for i in range(5))
    hn = _rmsnorm(x_ref[...], g_ref[...]).astype(BF16)
    seg = _segments(hn, w_ref, wg_ref)
    k, v = _inproj_common(seg, cos_ref[...], slo_ref[...], shi_ref[...],
                          q_ref, ga_ref, vb_ref, so_ref, gb_ref)
    kn_ref[...] = k
    vn_ref[...] = v
    knt_ref[...] = k.T
    vnt_ref[...] = v.T
    u = seg(OFF_QB, 2 * D_MLSTM)
    qk_ref[...] = u
    t = lax.rem(lax.broadcasted_iota(jnp.int32, u.shape, 0), t_new)
    y = cb_ref[...] + u * cw_ref[3:4, :]
    for sh, h_ref in ((1, h1_ref), (2, h2_ref), (3, h3_ref)):
        prev = jnp.where(t >= sh, pltpu.roll(u, sh, 0), h_ref[...])
        y = y + prev * cw_ref[3 - sh:4 - sh, :]
    y = _silu(y)
    qm_ref[...] = y[:, :D_MLSTM].astype(BF16)
    km_ref[...] = (y[:, D_MLSTM:] * (HEAD_DIM_MLSTM ** -0.5)).astype(BF16)
    gates_ref[...] = _gate_block(seg(OFF_G, LANES), bif_ref[...])


def _attn_prompt_kernel(q_ref, k_ref, v_ref, g_ref, o_ref,
                        src1_ref, src4_ref, src16_ref, bias_ref, st16r_ref, st16_ref, st4_ref):
    seq = q_ref.shape[1]
    n16 = seq // 16
    group = 8
    head_a_full = lax.broadcasted_iota(jnp.int32, (seq, LANES), 1) < HEAD_DIM_ATT

    def stack(q, k, v, is_a):
        zero, one = jnp.zeros((), q.dtype), jnp.ones((), q.dtype)
        return (jnp.where(is_a, q, zero), jnp.where(is_a, zero, q), k,
                jnp.where(is_a, v, one), jnp.where(is_a, one, v))

    q, k, v = q_ref[0], k_ref[0], v_ref[0]
    for i, x in enumerate(stack(q, k, v, head_a_full)):
        src1_ref[i] = x
    for i, x in enumerate(stack(q.astype(F32), k.astype(F32), v.astype(F32), head_a_full)):
        src4_ref[i] = x
    blk = 16 * 16
    pa = lax.broadcasted_iota(jnp.int32, (blk, blk), 0)
    pb = lax.broadcasted_iota(jnp.int32, (blk, blk), 1)
    perm = jnp.where(pb == 16 * (pa & 15) + (pa >> 4), 1.0, 0.0).astype(BF16)
    is_a_blk = lax.broadcasted_iota(jnp.int32, (16, 16, LANES), 2) < HEAD_DIM_ATT
    for j in range(seq // blk):
        rows = slice(j * blk, (j + 1) * blk)
        parts = [jnp.dot(perm, x[rows], preferred_element_type=F32).astype(BF16).reshape(16, 16, LANES)
                 for x in (q, k, v)]
        for i, x in enumerate(stack(*parts, is_a_blk)):
            src16_ref[i, :, 16 * j:16 * (j + 1), :] = x
    u = lax.broadcasted_iota(jnp.int32, (N_BACK, 2 * N_BACK), 0)
    w = lax.broadcasted_iota(jnp.int32, (N_BACK, 2 * N_BACK), 1)
    bias_ref[...] = jnp.where((w >= u) & (w <= u + N_BACK), jnp.finfo(F32).max, NEG_INF)
    head_a = lax.broadcasted_iota(jnp.int32, (N_BACK, LANES), 1) < HEAD_DIM_ATT

    def partials(gets):
        staged = []
        for get, has_prev in gets:
            if has_prev:
                both = lambda i, get=get: jnp.concatenate([get(i, True), get(i, False)], axis=0)
                bias = bias_ref[...]
            else:
                both = lambda i, get=get: get(i, False)
                bias = bias_ref[:, N_BACK:]
            kk = both(2)
            heads = []
            for qi in (0, 1):
                s = lax.dot_general(get(qi, False), kk, (((1,), (1,)), ((), ())),
                                    preferred_element_type=F32)
                s = jnp.minimum(s, bias)
                mh = jnp.max(s, axis=-1, keepdims=True)
                heads.append((mh, jnp.exp2(s - mh).astype(BF16)))
            staged.append((both, heads))
        out = []
        for both, ((m_a, p_a), (m_b, p_b)) in staged:
            pv_a = jnp.dot(p_a, both(3), preferred_element_type=F32)
            pv_b = jnp.dot(p_b, both(4), preferred_element_type=F32)
            acc = jnp.where(head_a, pv_a, pv_b)
            den = pltpu.roll(jnp.where(head_a, pv_b, pv_a), HEAD_DIM_ATT, 1)
            out.append((jnp.where(head_a, m_a, m_b), den, acc))
        return out

    def get4(qstart, kprev_start):
        def get(i, prev):
            start = kprev_start if prev else qstart
            return src4_ref[i, pl.ds(start, N_BACK, stride=4), :].astype(BF16)
        return get, kprev_start is not None

    def get1(qstart, kprev_start):
        def get(i, prev):
            return src1_ref[i, pl.ds(kprev_start if prev else qstart, N_BACK), :]
        return get, kprev_start is not None

    def keep4(blocks):
        res = partials([get4(qs, ks) for qs, ks in blocks])
        for (qs, _), (m, den, acc) in zip(blocks, res):
            rows = pl.ds(qs, N_BACK, stride=4)
            st4_ref[0, rows, :] = m
            st4_ref[1, rows, :] = den
            st4_ref[2, rows, :] = acc

    def finish(blocks):
        res = partials([get1(qs, ks) for qs, ks in blocks])
        for (qs, _), part in zip(blocks, res):
            rows = pl.ds(qs, N_BACK)
            parts = [part] + [tuple(st[i, rows, :] for i in range(3)) for st in (st4_ref, st16_ref)]
            m_all = functools.reduce(jnp.maximum, [pt[0] for pt in parts])
            wts = [jnp.exp2(pt[0] - m_all) for pt in parts]
            den = sum(wt * pt[1] for wt, pt in zip(wts, parts))
            num = sum(wt * pt[2] for wt, pt in zip(wts, parts))
            o_ref[0, rows, :] = (num / den * g_ref[0, rows, :].astype(F32)).astype(BF16)

    for g in range(16 // group):
        rs = [g * group + rr for rr in range(group)]
        res = partials([(lambda i, prev, r=r: src16_ref[i, r], False) for r in rs])
        for r, part in zip(rs, res):
            for i in range(3):
                st16r_ref[i, r] = part[i]
    for i in range(3):
        st16_ref[i] = jnp.swapaxes(st16r_ref[i], 0, 1).reshape(seq, LANES)

    blocks4 = lambda cc: [(r + 4 * N_BACK * cc, r + 4 * N_BACK * (cc - 1)) for r in range(4)]
    keep4([(r, None) for r in range(4)] + blocks4(1))

    def body4(g, c):
        keep4(blocks4(2 * g) + blocks4(2 * g + 1))
        return c
    lax.fori_loop(1, seq // 4 // N_BACK // 2, body4, 0)

    finish([(0, None)] + [(cc * N_BACK, (cc - 1) * N_BACK) for cc in range(1, group)])

    def body1(g, c):
        starts = [pl.multiple_of((g * group + rr) * N_BACK, N_BACK) for rr in range(group)]
        finish([(st, st - N_BACK) for st in starts])
        return c
    lax.fori_loop(1, seq // N_BACK // group, body1, 0)


def _attn_sample_kernel(q_ref, kn_ref, vn_ref, knt_ref, vnt_ref, g_ref, ck_ref, cv_ref,
                        o_ref, sk_ref, sv_ref, clamp_ref, *, t_new):
    b = pl.program_id(0)
    wb = ck_ref.shape[3]
    hd = HEAD_DIM_ATT
    rows = q_ref.shape[1]

    @pl.when(b == 0)
    def _():
        delta = (wb + lax.broadcasted_iota(jnp.int32, (rows, wb), 0)
                 - lax.broadcasted_iota(jnp.int32, (rows, wb), 1))
        for d, dil in enumerate(DILATIONS):
            ok = ((delta & (dil - 1)) == 0) & (delta >= dil) & (delta <= N_BACK * dil)
            clamp_ref[d] = jnp.where(ok, jnp.finfo(F32).max, NEG_INF)

    tq = lax.broadcasted_iota(jnp.int32, (rows, rows), 0)
    tk = lax.broadcasted_iota(jnp.int32, (rows, rows), 1)
    new_ok = [((tk <= tq) if dil == 1 else (tk == tq)) & (tk < t_new) for dil in DILATIONS]

    lane = lax.broadcasted_iota(jnp.int32, (hd, LANES), 1)
    shift_new = (LANES - t_new) - b * t_new
    nt = (((1,), (1,)), ((), ()))
    outs = []
    for h in range(N_HEADS_ATT):
        hs = slice(h * hd, (h + 1) * hd)
        kt = ck_ref[0, h]
        vt = cv_ref[0, h]
        for old, new_ref, out_ref in ((kt, knt_ref, sk_ref), (vt, vnt_ref, sv_ref)):
            moved = pltpu.roll(old, wb - t_new, axis=1)
            new_cols = pltpu.roll(new_ref[hs, :], shift_new, axis=1)
            out_ref[0, h, :, 0:wb - LANES] = moved[:, 0:wb - LANES]
            out_ref[0, h, :, wb - LANES:wb] = jnp.where(lane < LANES - t_new,
                                                        moved[:, wb - LANES:wb], new_cols)
        qh = q_ref[0, :, hs]
        kn_h = kn_ref[0, :, hs].astype(BF16)
        vn_h = vn_ref[0, :, hs].astype(BF16)
        s_old = jnp.dot(qh, kt.astype(BF16), preferred_element_type=F32)
        s_new = lax.dot_general(qh, kn_h, nt, preferred_element_type=F32)
        ps, pes, ms = [], [], []
        for d in range(len(DILATIONS)):
            so = jnp.minimum(s_old, clamp_ref[d])
            sn = jnp.where(new_ok[d], s_new, NEG_INF)
            m = jnp.maximum(jnp.max(so, axis=-1, keepdims=True), jnp.max(sn, axis=-1, keepdims=True))
            ps.append(jnp.exp2(so - m))
            pes.append(jnp.exp2(sn - m))
            ms.append(m)
        acc = lax.dot_general(jnp.concatenate(ps, axis=0).astype(BF16), vt.astype(BF16), nt,
                              preferred_element_type=F32)
        acc = acc + jnp.dot(jnp.concatenate(pes, axis=0).astype(BF16), vn_h,
                            preferred_element_type=F32)
        m_all = functools.reduce(jnp.maximum, ms)
        den = 0.0
        num = 0.0
        for d in range(len(DILATIONS)):
            wgt = jnp.exp2(ms[d] - m_all)
            den = den + wgt * (jnp.sum(ps[d], axis=-1, keepdims=True)
                               + jnp.sum(pes[d], axis=-1, keepdims=True))
            num = num + wgt * acc[d * rows:(d + 1) * rows]
        outs.append(num / den)
    att = jnp.concatenate(outs, axis=1)
    o_ref[0] = (att * g_ref[0].astype(F32)).astype(BF16)


def _scan_lanes(x, op, fill):
    lane = lax.broadcasted_iota(jnp.int32, x.shape, 1)
    d = 1
    while d < x.shape[1]:
        x = op(x, jnp.where(lane >= d, pltpu.roll(x, d, 1), fill))
        d *= 2
    return x


def _gate_scans(gt):
    b = _scan_lanes(pltpu.roll(gt, N_HEADS_MLSTM, 0), jnp.add, 0.0)
    cm = _scan_lanes(gt - b, jnp.maximum, NEG_INF)
    row = lax.broadcasted_iota(jnp.int32, gt.shape, 0)
    return jnp.where(row < N_HEADS_MLSTM, b, pltpu.roll(cm, N_HEADS_MLSTM, 0))


def _mlstm_kernel(*refs, zero_init, scans_given, fuse_out):
    refs = list(refs)
    q_ref, k_ref, v_ref, gt_ref, so_ref, gb_ref, ng_ref = refs[:7]
    del refs[:7]
    if not zero_init:
        c0_ref, n0_ref, m0_ref = refs[:3]
        del refs[:3]
    if fuse_out:
        x_ref, ya_ref, wo_ref, fg_ref = refs[:4]
        del refs[:4]
    y_ref, c_out_ref, n_out_ref, m_out_ref, cn_ref, m_ref = refs
    c_idx = pl.program_id(1)
    nb = q_ref.shape[0]
    L = CHUNK
    dh = HEAD_DIM_MLSTM
    nh = N_HEADS_MLSTM

    @pl.when(c_idx == 0)
    def _():
        if zero_init:
            cn_ref[...] = jnp.zeros(cn_ref.shape, F32)
            m_ref[...] = jnp.zeros(m_ref.shape, F32)
        else:
            m_ref[...] = m0_ref[...]
            for bb in range(nb):
                for h in range(nh):
                    n_rows = jnp.broadcast_to(n0_ref[bb, h:h + 1, :], (dh, dh))
                    cn_ref[bb, h] = jnp.concatenate([c0_ref[bb, h], n_rows.T], axis=1)

    tri_t = lax.broadcasted_iota(jnp.int32, (L, L), 0)
    tri_s = lax.broadcasted_iota(jnp.int32, (L, L), 1)
    causal = tri_t >= tri_s
    ones_blk = jnp.ones((L, dh), BF16)
    nt = (((1,), (1,)), ((), ()))
    pairs = [(bb, h) for bb in range(nb) for h in range(nh)]
    sl = lambda h: slice(h * dh, (h + 1) * dh)

    qk = {(bb, h): lax.dot_general(q_ref[bb, :, sl(h)], k_ref[bb, :, sl(h)], nt,
                                   preferred_element_type=F32) for bb, h in pairs}
    cn_old = {p: cn_ref[p[0], p[1]] for p in pairs}
    if fuse_out:
        mix_a = x_ref[...].reshape(nb * L, D_MODEL) + jnp.dot(
            ya_ref[...].reshape(nb * L, D_ATT), wo_ref[0:D_ATT, :], preferred_element_type=F32)

    rows, cols, decays, w_rows = [], [], [], []
    for bb in range(nb):
        if scans_given:
            i_row = gt_ref[bb, 0:SUBLANES, :]
            sc = gt_ref[bb, SUBLANES:2 * SUBLANES, :]
        else:
            i_row = gt_ref[bb]
            sc = _gate_scans(i_row)
        b = sc
        cm = pltpu.roll(sc, nh, 0)
        m_prev = m_ref[bb]
        m_t = jnp.maximum(m_prev + b, b + cm)
        inter = jnp.exp(m_prev + b - m_t)
        m_last = jnp.broadcast_to(m_t[:, L - 1:L], m_t.shape)
        b_last = jnp.broadcast_to(b[:, L - 1:L], b.shape)
        decay = jnp.exp(m_prev + b_last - m_last)
        w_row = jnp.exp(b_last - b + i_row - m_last)
        m_ref[bb] = m_last
        stack = jnp.concatenate([b - m_t, inter, jnp.exp(-m_t),
                                 jnp.zeros((L - 3 * SUBLANES, L), F32)], axis=0)
        cols.append(stack.T)
        rows.append(i_row - b)
        w_rows.append(w_row)
        decays.append(jnp.concatenate([decay, decay], axis=1))
    col = lambda bb, kind, h: cols[bb][:, kind * SUBLANES + h:kind * SUBLANES + h + 1]

    sqk = {}
    for bb, h in pairs:
        dlog = col(bb, 0, h) + rows[bb][h:h + 1, :]
        sqk[bb, h] = (qk[bb, h] * jnp.exp(jnp.where(causal, dlog, NEG_INF))).astype(BF16)
    v_one = {(bb, h): jnp.concatenate([v_ref[bb, :, sl(h)], ones_blk], axis=1) for bb, h in pairs}
    tots = {}
    for bb, h in pairs:
        q_dec = (col(bb, 1, h) * q_ref[bb, :, sl(h)].astype(F32)).astype(BF16)
        lhs = jnp.concatenate([sqk[bb, h], q_dec], axis=1)
        rhs = jnp.concatenate([v_one[bb, h], cn_old[bb, h].astype(BF16)], axis=0)
        tots[bb, h] = jnp.dot(lhs, rhs, preferred_element_type=F32)
    yb = {}
    for bb, h in pairs:
        tot = tots[bb, h]
        hh = tot[:, :dh] / jnp.maximum(jnp.abs(tot[:, dh:]), col(bb, 2, h))
        hh = so_ref[bb, :, sl(h)].astype(F32) * hh
        hh = hh * lax.rsqrt(jnp.mean(hh * hh, axis=-1, keepdims=True) + EPS)
        hh = hh * ng_ref[:, sl(h)]
        yb[bb, h] = (hh * gb_ref[bb, :, sl(h)].astype(F32)).astype(BF16)
    if fuse_out:
        yb_all = jnp.concatenate([jnp.concatenate([yb[bb, h] for h in range(nh)], axis=1)
                                  for bb in range(nb)], axis=0)
        mix = mix_a + jnp.dot(yb_all, wo_ref[D_ATT:, :], preferred_element_type=F32)
        res = _rmsnorm(mix, fg_ref[...])
        y_ref[...] = res.reshape(nb, L, D_MODEL)
    else:
        for bb, h in pairs:
            y_ref[bb, :, sl(h)] = yb[bb, h]
    for bb, h in pairs:
        kt_w = k_ref[bb, :, sl(h)].T.astype(F32) * w_rows[bb][h:h + 1, :]
        upd = jnp.dot(kt_w.astype(BF16), v_one[bb, h], preferred_element_type=F32)
        cn_ref[bb, h] = decays[bb][h:h + 1, :] * cn_old[bb, h] + upd

    @pl.when(c_idx == pl.num_programs(1) - 1)
    def _():
        m_out_ref[...] = m_ref[...]
        for bb in range(nb):
            n_rows = []
            for h in range(nh):
                cn = cn_ref[bb, h]
                c_out_ref[bb, h] = cn[:, :dh]
                n_rows.append(cn[:, dh:].T[0:1, :])
            n_out_ref[bb] = jnp.concatenate(n_rows, axis=0)


def _outproj_kernel(x_ref, ya_ref, yb_ref, w_ref, g_ref, o_ref):
    rows = x_ref.shape[0]
    sub = min(OUTPROJ_SUBTILE, rows)
    for r0 in range(0, rows, sub):
        rs = pl.ds(r0, sub)
        mix = jnp.dot(ya_ref[rs, :], w_ref[0:D_ATT, :], preferred_element_type=F32)
        mix = mix + jnp.dot(yb_ref[rs, :], w_ref[D_ATT:, :], preferred_element_type=F32)
        o_ref[rs, :] = _rmsnorm(x_ref[rs, :] + mix, g_ref[...])


def _rope_tables(pos):
    half = ROT_DIM // 2
    f32 = np.float32
    inv = f32(ROPE_THETA) ** (-np.arange(half, dtype=f32) * f32(2.0) / f32(ROT_DIM))
    ang = np.asarray(pos, dtype=f32)[:, None] * inv[None, :].astype(f32)
    cos, sin = np.cos(ang).astype(f32), np.sin(ang).astype(f32)
    n = ang.shape[0]
    one = np.ones((n, HEAD_DIM_ATT - ROT_DIM), f32)
    zero = np.zeros((n, HEAD_DIM_ATT - ROT_DIM), f32)
    zh = np.zeros((n, half), f32)
    cos_t = np.concatenate([cos, cos, one], axis=1)
    lo_t = np.concatenate([-sin, zh, zero], axis=1)
    hi_t = np.concatenate([zh, sin, zero], axis=1)
    rep = lambda t: np.concatenate([t, t], axis=1)
    return rep(cos_t), rep(lo_t), rep(hi_t)


def _params(sem):
    return pltpu.CompilerParams(dimension_semantics=sem, vmem_limit_bytes=VMEM_LIMIT)


def _const_spec(shape):
    return pl.BlockSpec(shape, lambda *_: (0,) * len(shape))


def _inproj_prompt(x, norm_g, w_main, w_gate, conv_w, conv_b, bif):
    B, S, _ = x.shape
    tm = ROW_TILE
    cos, lo, hi = _rope_tables(np.arange(S))
    tile = lambda width: pl.BlockSpec((1, tm, width), lambda b, j: (b, j, 0))
    tab = pl.BlockSpec((tm, LANES), lambda b, j: (j, 0))
    bf = lambda: jax.ShapeDtypeStruct((B, S, D_ATT), BF16)
    f5 = lambda: jax.ShapeDtypeStruct((B, N_HEADS_ATT, HEAD_DIM_ATT, S), F32)
    tile5 = pl.BlockSpec((1, N_HEADS_ATT, HEAD_DIM_ATT, tm), lambda b, j: (b, 0, 0, j))
    out_shape = (bf(), bf(), bf(), f5(), f5(), bf(), bf(), bf(), bf(), bf(), bf(),
                 jax.ShapeDtypeStruct((B, 2 * SUBLANES, S), F32),
                 jax.ShapeDtypeStruct((B, CONV_WIDTH - 1, 2 * D_MLSTM), F32))
    out_specs = tuple([tile(D_ATT)] * 3 + [tile5] * 2 + [tile(D_ATT)] * 6) + (
        pl.BlockSpec((1, 2 * SUBLANES, tm), lambda b, j: (b, 0, j)),
        pl.BlockSpec((1, CONV_WIDTH - 1, 2 * D_MLSTM), lambda b, j: (b, 0, 0)))
    return pl.pallas_call(
        _inproj_prompt_kernel,
        grid=(B, S // tm),
        in_specs=[tile(D_MODEL), _const_spec((1, D_MODEL)), _const_spec((D_MODEL, OFF_G)),
                  _const_spec(w_gate.shape),
                  _const_spec((CONV_WIDTH, 2 * D_MLSTM)), _const_spec((1, 2 * D_MLSTM)),
                  _const_spec((1, LANES)), tab, tab, tab],
        out_specs=out_specs,
        out_shape=out_shape,
        scratch_shapes=[pltpu.VMEM((tm + 2 * SUBLANES, 2 * D_MLSTM), F32)],
        compiler_params=_params(("arbitrary", "arbitrary")),
        name="inproj_prompt",
    )(x, norm_g, w_main, w_gate, conv_w, conv_b, bif, cos, lo, hi)


def _inproj_sample(x2, norm_g, w_main, w_gate, conv_w, conv_b, bif, hist, t_new):
    rows = x2.shape[0]
    pos = PAST_LEN + np.arange(t_new)
    cos, lo, hi = (np.tile(t, (rows // t_new, 1)) for t in _rope_tables(pos))
    bf = lambda: jax.ShapeDtypeStruct((rows, D_ATT), BF16)
    f3 = lambda: jax.ShapeDtypeStruct((rows, D_ATT), F32)
    f5 = lambda: jax.ShapeDtypeStruct((D_ATT, rows), F32)
    out_shape = (bf(), f3(), f3(), f5(), f5(), bf(),
                 jax.ShapeDtypeStruct((rows, 5 * D_MLSTM), BF16),
                 jax.ShapeDtypeStruct((rows, LANES), F32),
                 jax.ShapeDtypeStruct((rows, 2 * D_MLSTM), F32))
    return pl.pallas_call(
        functools.partial(_inproj_sample_kernel, t_new=t_new),
        out_shape=out_shape,
        compiler_params=pltpu.CompilerParams(vmem_limit_bytes=VMEM_LIMIT),
        name="inproj_sample",
    )(x2, norm_g, w_main, w_gate, conv_w, conv_b, bif, cos, lo, hi, *hist)


def _attn_prompt(q, k, v, gate):
    B, S, _ = q.shape
    spec = pl.BlockSpec((1, S, LANES), lambda b, h: (b, 0, h))
    return pl.pallas_call(
        _attn_prompt_kernel,
        grid=(B, D_ATT // LANES),
        in_specs=[spec, spec, spec, spec],
        out_specs=spec,
        out_shape=jax.ShapeDtypeStruct((B, S, D_ATT), BF16),
        scratch_shapes=[
            pltpu.VMEM((5, S, LANES), BF16), pltpu.VMEM((5, S, LANES), F32),
            pltpu.VMEM((5, 16, S // 16, LANES), BF16),
            pltpu.VMEM((N_BACK, 2 * N_BACK), F32),
            pltpu.VMEM((3, 16, S // 16, LANES), F32),
            pltpu.VMEM((3, S, LANES), F32), pltpu.VMEM((3, S, LANES), F32)],
        compiler_params=_params(("arbitrary", "arbitrary")),
        name="attn_prompt",
    )(q, k, v, gate)


def _attn_sample(q, kn, vn, knt, vnt, gate, ck, cv, t_new):
    B, rows, _ = q.shape
    wb = ck.shape[3]
    small = pl.BlockSpec((1, rows, D_ATT), lambda b: (b, 0, 0))
    big = pl.BlockSpec((1, N_HEADS_ATT, HEAD_DIM_ATT, wb), lambda b: (b, 0, 0, 0))
    win_shape = jax.ShapeDtypeStruct((B, N_HEADS_ATT, HEAD_DIM_ATT, wb), F32)
    return pl.pallas_call(
        functools.partial(_attn_sample_kernel, t_new=t_new),
        grid=(B,),
        in_specs=[small, small, small, _const_spec(knt.shape), _const_spec(vnt.shape), small, big, big],
        out_specs=(small, big, big),
        out_shape=(jax.ShapeDtypeStruct((B, rows, D_ATT), BF16), win_shape, win_shape),
        scratch_shapes=[pltpu.VMEM((len(DILATIONS), rows, wb), F32)],
        compiler_params=_params(("arbitrary",)),
        name="attn_sample",
    )(q, kn, vn, knt, vnt, gate, ck, cv)


def _mlstm(q, k, v, gates_t, sig_o, gate_b, norm_g, state=None, out_proj=None,
           lane_blocks=(0, 0, 0, 0, 0)):
    B, S, _ = q.shape
    nc = S // CHUNK
    nb = MLSTM_BATCH
    nh, dh = N_HEADS_MLSTM, HEAD_DIM_MLSTM
    grows = gates_t.shape[1]
    tile = lambda width, lb=0: pl.BlockSpec((nb, CHUNK, width), lambda b, c: (b, c, lb))
    op = [tile(D_MLSTM, lb) for lb in lane_blocks]
    gspec = pl.BlockSpec((nb, grows, CHUNK), lambda b, c: (b, 0, c))
    c_spec = pl.BlockSpec((nb, nh, dh, dh), lambda b, c: (b, 0, 0, 0))
    n_spec = pl.BlockSpec((nb, nh, dh), lambda b, c: (b, 0, 0))
    m_spec = pl.BlockSpec((nb, SUBLANES, LANES), lambda b, c: (b, 0, 0))
    zero_init = state is None
    fuse_out = out_proj is not None
    in_specs = op[:3] + [gspec] + op[3:] + [_const_spec((1, D_MLSTM))]
    args = [q, k, v, gates_t, sig_o, gate_b, norm_g]
    if not zero_init:
        in_specs += [c_spec, n_spec, m_spec]
        args += list(state)
    if fuse_out:
        in_specs += [tile(D_MODEL), tile(D_ATT), _const_spec((D_ATT + D_MLSTM, D_MODEL)),
                     _const_spec((1, D_MODEL))]
        args += list(out_proj)
        y_spec, y_shape = tile(D_MODEL), jax.ShapeDtypeStruct((B, S, D_MODEL), F32)
    else:
        y_spec, y_shape = tile(D_MLSTM), jax.ShapeDtypeStruct((B, S, D_MLSTM), BF16)
    return pl.pallas_call(
        functools.partial(_mlstm_kernel, zero_init=zero_init, scans_given=grows == 2 * SUBLANES,
                          fuse_out=fuse_out),
        grid=(B // nb, nc),
        in_specs=in_specs,
        out_specs=(y_spec, c_spec, n_spec, m_spec),
        out_shape=(y_shape,
                   jax.ShapeDtypeStruct((B, nh, dh, dh), F32),
                   jax.ShapeDtypeStruct((B, nh, dh), F32),
                   jax.ShapeDtypeStruct((B, SUBLANES, LANES), F32)),
        scratch_shapes=[pltpu.VMEM((nb, nh, dh, 2 * dh), F32),
                        pltpu.VMEM((nb, SUBLANES, LANES), F32)],
        compiler_params=_params(("arbitrary", "arbitrary")),
        name="mlstm_prompt" if zero_init else "mlstm_sample",
    )(*args)


def _outproj(x2, ya, yb, w_out, final_g):
    rows = x2.shape[0]
    tm = min(OUTPROJ_TILE, rows)
    tile = lambda width: pl.BlockSpec((tm, width), lambda i: (i, 0))
    return pl.pallas_call(
        _outproj_kernel,
        grid=(rows // tm,),
        in_specs=[tile(D_MODEL), tile(D_ATT), tile(D_MLSTM),
                  _const_spec((D_ATT + D_MLSTM, D_MODEL)), _const_spec((1, D_MODEL))],
        out_specs=tile(D_MODEL),
        out_shape=jax.ShapeDtypeStruct((rows, D_MODEL), F32),
        compiler_params=_params(("arbitrary",)),
        name="outproj",
    )(x2, ya, yb, w_out, final_g)


def kernel(x_prompt, x_sample, cache_win_k, cache_win_v, state_conv, state_C, state_n, state_m,
           norm_g, w_in, conv_w, conv_b, b_i, b_f, mlstm_norm_g, w_out, final_norm_g):
    assert w_in.shape[0] == 1, "single-layer model"
    B, S, D = x_prompt.shape
    DB, T, _ = x_sample.shape
    HB, DK = N_HEADS_MLSTM, HEAD_DIM_MLSTM
    wb = cache_win_k.shape[2]
    assert S % ROW_TILE == 0 and S == 16 * N_BACK and wb >= 16 * N_BACK and CONV_WIDTH - 1 <= T <= SUBLANES and DB * T == LANES

    w_main = w_in[0].astype(BF16)
    w_gate = jnp.pad(w_main[:, OFF_G:], ((0, 0), (0, LANES - 2 * HB)))
    w_o = w_out[0].astype(BF16)
    g_in = norm_g[0][None, :]
    cw, cb = conv_w[0], conv_b[0][None, :]
    bif = jnp.pad(jnp.concatenate([b_i[0], b_f[0]]), (0, LANES - 2 * HB))[None, :]
    ng = mlstm_norm_g[0][None, :]
    g_fin = final_norm_g[None, :]

    (q_p, k_p, v_p, pk, pv, ga_p, qm_p, km_p, vb_p, so_p, gb_p, gt_p, p_conv) = _inproj_prompt(
        x_prompt, g_in, w_main, w_gate, cw, cb, bif)
    ya_p = _attn_prompt(q_p, k_p, v_p, ga_p)
    y_prompt, c_p, n_p, m_p = _mlstm(qm_p, km_p, vb_p, gt_p, so_p, gb_p, ng,
                                     out_proj=(x_prompt, ya_p, w_o, g_fin))

    sc = state_conv[0]
    zrow = jnp.zeros((DB, 1, 2 * D_MLSTM), F32)
    hist = []
    for sh in (1, 2, 3):
        rows_ = [sc[:, CONV_WIDTH - 1 + t - sh:CONV_WIDTH + t - sh] if t < sh else zrow
                 for t in range(T)]
        hist.append(jnp.concatenate(rows_, axis=1).reshape(DB * T, 2 * D_MLSTM))
    (q_s, kn, vn, knt, vnt, ga_s, mix_s, gates_s, qk_s) = _inproj_sample(
        x_sample.reshape(DB * T, D), g_in, w_main, w_gate, cw, cb, bif, hist, T)
    r3 = lambda a: a.reshape(DB, T, a.shape[-1])
    pad8 = lambda a: jnp.pad(r3(a), ((0, 0), (0, SUBLANES - T), (0, 0)))
    to_hdp = lambda c: jnp.transpose(c[0], (0, 2, 3, 1))
    from_hdp = lambda c: jnp.transpose(c, (0, 3, 1, 2))[None]
    ya_s, s_k, s_v = _attn_sample(pad8(q_s), pad8(kn), pad8(vn), knt, vnt, pad8(ga_s),
                                  to_hdp(cache_win_k), to_hdp(cache_win_v), T)
    ya_s = ya_s[:, :T]

    pad_t = lambda a: jnp.pad(r3(a), ((0, 0), (0, CHUNK - T), (0, 0)))
    g3 = r3(gates_s)[:, :, :SUBLANES]
    null_gate = jnp.concatenate([jnp.full((HB,), NEG_INF, F32), jnp.zeros((HB,), F32)])
    gt_s = jnp.concatenate([g3, jnp.broadcast_to(null_gate, (DB, CHUNK - T, SUBLANES))], axis=1)
    gt_s = gt_s.transpose(0, 2, 1)
    m0 = jnp.broadcast_to(jnp.pad(state_m[0], ((0, 0), (0, SUBLANES - HB)))[..., None],
                          (DB, SUBLANES, LANES))
    mix_p = pad_t(mix_s)
    yb_s, c_s, n_s, m_s = _mlstm(mix_p, mix_p, mix_p, gt_s, mix_p, mix_p, ng,
                                 state=(state_C[0], state_n[0], m0), lane_blocks=(0, 1, 2, 3, 4))
    y_sample = _outproj(x_sample.reshape(DB * T, D), ya_s.reshape(DB * T, D_ATT),
                        yb_s[:, :T].reshape(DB * T, D_MLSTM), w_o, g_fin).reshape(DB, T, D)

    return (y_prompt, y_sample,
            from_hdp(pk), from_hdp(pv), p_conv[None],
            c_p[None], n_p[None], m_p[None, :, :HB, 0],
            from_hdp(s_k), from_hdp(s_v), r3(qk_s)[None, :, T - (CONV_WIDTH - 1):],
            c_s[None], n_s[None], m_s[None, :, :HB, 0])
```

```python
import functools
import math

import jax
import jax.numpy as jnp
import numpy as np
from jax import lax
from jax.experimental import pallas as pl
from jax.experimental.pallas import tpu as pltpu

F32 = jnp.float32
BF16 = jnp.bfloat16

D_MODEL = 1024
D_ATT = 512
N_HEADS_ATT = 8
HEAD_DIM_ATT = 64
D_MLSTM = 512
N_HEADS_MLSTM = 4
HEAD_DIM_MLSTM = 128
ROT_DIM = 16
ROPE_THETA = 500000.0
PAST_LEN = 16384
DILATIONS = (1, 4, 16)
N_BACK = 128
CONV_WIDTH = 4
CHUNK = 128
EPS = 1e-6
NEG_INF = -1e30
LOG2_E = math.log2(math.e)

LANES = 128
SUBLANES = 8
OFF_QA, OFF_KA, OFF_VA, OFF_ZA = (i * D_ATT for i in range(4))
OFF_QB, _, OFF_VB, OFF_OB, OFF_ZB, OFF_G = (4 * D_ATT + i * D_MLSTM for i in range(6))
ROW_TILE = 512
INPROJ_SUBTILE = 128
OUTPROJ_TILE = 2048
OUTPROJ_SUBTILE = 256
MLSTM_BATCH = 4
VMEM_LIMIT = 56 * 1024 * 1024


def _silu(x):
    return x * jax.nn.sigmoid(x)


def _rmsnorm(x, g):
    return x * lax.rsqrt(jnp.mean(x * x, axis=-1, keepdims=True) + EPS) * g


def _rope(u, cos, sin_lo, sin_hi):
    outs = []
    for c in range(u.shape[1] // LANES):
        xs = u[:, c * LANES:(c + 1) * LANES]
        outs.append(xs * cos + pltpu.roll(xs, LANES - ROT_DIM // 2, 1) * sin_lo
                    + pltpu.roll(xs, ROT_DIM // 2, 1) * sin_hi)
    return jnp.concatenate(outs, axis=1)


def _gate_block(ug, bif):
    gz = ug + bif
    lane = lax.broadcasted_iota(jnp.int32, gz.shape, 1)
    logf = jnp.minimum(gz, 0.0) - jnp.log1p(jnp.exp(-jnp.abs(gz)))
    return jnp.where(lane < N_HEADS_MLSTM, gz, logf)


def _segments(hn, w_ref, wg_ref):
    def seg(off, width):
        rhs = wg_ref[...] if off == OFF_G else w_ref[:, off:off + width]
        return jnp.dot(hn, rhs, preferred_element_type=F32)
    return seg


def _inproj_common(seg, cos, sin_lo, sin_hi, q_ref, ga_ref, vb_ref, so_ref, gb_ref):
    q = _rope(seg(OFF_QA, D_ATT), cos, sin_lo, sin_hi) * (HEAD_DIM_ATT ** -0.5 * LOG2_E)
    q_ref[...] = q.astype(BF16).reshape(q_ref.shape)
    k = _rope(seg(OFF_KA, D_ATT), cos, sin_lo, sin_hi)
    v = seg(OFF_VA, D_ATT)
    ga_ref[...] = _silu(seg(OFF_ZA, D_ATT)).astype(BF16).reshape(ga_ref.shape)
    vb_ref[...] = seg(OFF_VB, D_MLSTM).astype(BF16).reshape(vb_ref.shape)
    so_ref[...] = jax.nn.sigmoid(seg(OFF_OB, D_MLSTM)).astype(BF16).reshape(so_ref.shape)
    gb_ref[...] = _silu(seg(OFF_ZB, D_MLSTM)).astype(BF16).reshape(gb_ref.shape)
    return k, v


def _inproj_prompt_kernel(x_ref, g_ref, w_ref, wg_ref, cw_ref, cb_ref, bif_ref, cos_ref, slo_ref,
                          shi_ref,
                          q_ref, kb_ref, vbf_ref, pk_ref, pv_ref, ga_ref, qm_ref, km_ref, vb_ref,
                          so_ref, gb_ref, gt_ref, pconv_ref, xp_ref):
    j = pl.program_id(1)
    tm = x_ref.shape[1]
    sub = INPROJ_SUBTILE
    assert CHUNK == LANES and tm % sub == 0 and sub % CHUNK == 0

    @pl.when(j == 0)
    def _():
        xp_ref[0:SUBLANES, :] = jnp.zeros((SUBLANES, 2 * D_MLSTM), F32)

    @pl.when(j > 0)
    def _():
        xp_ref[0:SUBLANES, :] = xp_ref[tm:tm + SUBLANES, :]

    for r0 in range(0, tm, sub):
        rs = pl.ds(r0, sub)
        part = lambda ref: ref.at[0, rs, :]
        hn = _rmsnorm(x_ref[0, rs, :], g_ref[...]).astype(BF16)
        seg = _segments(hn, w_ref, wg_ref)

        gates = _gate_block(seg(OFF_G, LANES), bif_ref[...])
        for i in range(sub // CHUNK):
            cs = slice(r0 + i * CHUNK, r0 + (i + 1) * CHUNK)
            gt = gates[i * CHUNK:(i + 1) * CHUNK, :].T[0:SUBLANES, :]
            gt_ref[0, 0:SUBLANES, cs] = gt
            gt_ref[0, SUBLANES:2 * SUBLANES, cs] = _gate_scans(gt)

        k, v = _inproj_common(seg, cos_ref[rs, :], slo_ref[rs, :], shi_ref[rs, :],
                              part(q_ref), part(ga_ref), part(vb_ref), part(so_ref), part(gb_ref))
        kb_ref[0, rs, :] = k.astype(BF16)
        vbf_ref[0, rs, :] = v.astype(BF16)
        pk_ref[0, :, :, r0:r0 + sub] = k.T.reshape(N_HEADS_ATT, HEAD_DIM_ATT, sub)
        pv_ref[0, :, :, r0:r0 + sub] = v.T.reshape(N_HEADS_ATT, HEAD_DIM_ATT, sub)

        base = SUBLANES + r0
        xp_ref[base:base + sub, :] = seg(OFF_QB, 2 * D_MLSTM)
        y = cb_ref[...] + xp_ref[base:base + sub, :] * cw_ref[3:4, :]
        for jj in range(CONV_WIDTH - 1):
            sh = CONV_WIDTH - 1 - jj
            y = y + xp_ref[base - sh:base - sh + sub, :] * cw_ref[jj:jj + 1, :]
        y = _silu(y)
        qm_ref[0, rs, :] = y[:, :D_MLSTM].astype(BF16)
        km_ref[0, rs, :] = (y[:, D_MLSTM:] * (HEAD_DIM_MLSTM ** -0.5)).astype(BF16)

    @pl.when(j == pl.num_programs(1) - 1)
    def _():
        pconv_ref[0] = xp_ref[tm + SUBLANES - (CONV_WIDTH - 1):tm + SUBLANES, :]


def _inproj_sample_kernel(x_ref, g_ref, w_ref, wg_ref, cw_ref, cb_ref, bif_ref, cos_ref, slo_ref,
                          shi_ref,
                          h1_ref, h2_ref, h3_ref,
                          q_ref, kn_ref, vn_ref, knt_ref, vnt_ref, ga_ref, mix_ref, gates_ref, qk_ref,
                          *, t_new):
    qm_ref, km_ref, vb_ref, so_ref, gb_ref = (
        mix_ref.at[:, pl.ds(i * D_MLSTM, D_MLSTM)] for i in range(5))
    hn = _rmsnorm(x_ref[...], g_ref[...]).astype(BF16)
    seg = _segments(hn, w_ref, wg_ref)
    k, v = _inproj_common(seg, cos_ref[...], slo_ref[...], shi_ref[...],
                          q_ref, ga_ref, vb_ref, so_ref, gb_ref)
    kn_ref[...] = k
    vn_ref[...] = v
    knt_ref[...] = k.T
    vnt_ref[...] = v.T
    u = seg(OFF_QB, 2 * D_MLSTM)
    qk_ref[...] = u
    t = lax.rem(lax.broadcasted_iota(jnp.int32, u.shape, 0), t_new)
    y = cb_ref[...] + u * cw_ref[3:4, :]
    for sh, h_ref in ((1, h1_ref), (2, h2_ref), (3, h3_ref)):
        prev = jnp.where(t >= sh, pltpu.roll(u, sh, 0), h_ref[...])
        y = y + prev * cw_ref[3 - sh:4 - sh, :]
    y = _silu(y)
    qm_ref[...] = y[:, :D_MLSTM].astype(BF16)
    km_ref[...] = (y[:, D_MLSTM:] * (HEAD_DIM_MLSTM ** -0.5)).astype(BF16)
    gates_ref[...] = _gate_block(seg(OFF_G, LANES), bif_ref[...])


def _attn_prompt_kernel(q_ref, k_ref, v_ref, g_ref, o_ref,
                        src1_ref, src4_ref, src16_ref, bias_ref, st16r_ref, st16_ref, st4_ref):
    seq = q_ref.shape[1]
    group = 8
    head_a_full = lax.broadcasted_iota(jnp.int32, (seq, LANES), 1) < HEAD_DIM_ATT

    def stack(q, k, v, is_a):
        zero, one = jnp.zeros((), q.dtype), jnp.ones((), q.dtype)
        return (jnp.where(is_a, q, zero), jnp.where(is_a, zero, q), k,
                jnp.where(is_a, v, one), jnp.where(is_a, one, v))

    q, k, v = q_ref[0], k_ref[0], v_ref[0]
    for i, x in enumerate(stack(q, k, v, head_a_full)):
        src1_ref[i] = x
    for i, x in enumerate(stack(q.astype(F32), k.astype(F32), v.astype(F32), head_a_full)):
        src4_ref[i] = x
    blk = 16 * 16
    pa = lax.broadcasted_iota(jnp.int32, (blk, blk), 0)
    pb = lax.broadcasted_iota(jnp.int32, (blk, blk), 1)
    perm = jnp.where(pb == 16 * (pa & 15) + (pa >> 4), 1.0, 0.0).astype(BF16)
    is_a_blk = lax.broadcasted_iota(jnp.int32, (16, 16, LANES), 2) < HEAD_DIM_ATT
    for j in range(seq // blk):
        rows = slice(j * blk, (j + 1) * blk)
        parts = [jnp.dot(perm, x[rows], preferred_element_type=F32).astype(BF16).reshape(16, 16, LANES)
                 for x in (q, k, v)]
        for i, x in enumerate(stack(*parts, is_a_blk)):
            src16_ref[i, :, 16 * j:16 * (j + 1), :] = x
    u = lax.broadcasted_iota(jnp.int32, (N_BACK, 2 * N_BACK), 0)
    w = lax.broadcasted_iota(jnp.int32, (N_BACK, 2 * N_BACK), 1)
    bias_ref[...] = jnp.where((w >= u) & (w <= u + N_BACK), jnp.finfo(F32).max, NEG_INF)
    head_a = lax.broadcasted_iota(jnp.int32, (N_BACK, LANES), 1) < HEAD_DIM_ATT

    def partials(gets):
        staged = []
        for get, has_prev in gets:
            if has_prev:
                both = lambda i, get=get: jnp.concatenate([get(i, True), get(i, False)], axis=0)
                bias = bias_ref[...]
            else:
                both = lambda i, get=get: get(i, False)
                bias = bias_ref[:, N_BACK:]
            kk = both(2)
            heads = []
            for qi in (0, 1):
                s = lax.dot_general(get(qi, False), kk, (((1,), (1,)), ((), ())),
                                    preferred_element_type=F32)
                s = jnp.minimum(s, bias)
                mh = jnp.max(s, axis=-1, keepdims=True)
                heads.append((mh, jnp.exp2(s - mh).astype(BF16)))
            staged.append((both, heads))
        out = []
        for both, ((m_a, p_a), (m_b, p_b)) in staged:
            pv_a = jnp.dot(p_a, both(3), preferred_element_type=F32)
            pv_b = jnp.dot(p_b, both(4), preferred_element_type=F32)
            acc = jnp.where(head_a, pv_a, pv_b)
            den = pltpu.roll(jnp.where(head_a, pv_b, pv_a), HEAD_DIM_ATT, 1)
            out.append((jnp.where(head_a, m_a, m_b), den, acc))
        return out

    def get4(qstart, kprev_start):
        def get(i, prev):
            start = kprev_start if prev else qstart
            return src4_ref[i, pl.ds(start, N_BACK, stride=4), :].astype(BF16)
        return get, kprev_start is not None

    def get1(qstart, kprev_start):
        def get(i, prev):
            return src1_ref[i, pl.ds(kprev_start if prev else qstart, N_BACK), :]
        return get, kprev_start is not None

    def keep4(blocks):
        res = partials([get4(qs, ks) for qs, ks in blocks])
        for (qs, _), (m, den, acc) in zip(blocks, res):
            rows = pl.ds(qs, N_BACK, stride=4)
            st4_ref[0, rows, :] = m
            st4_ref[1, rows, :] = den
            st4_ref[2, rows, :] = acc

    def finish(blocks):
        res = partials([get1(qs, ks) for qs, ks in blocks])
        for (qs, _), part in zip(blocks, res):
            rows = pl.ds(qs, N_BACK)
            parts = [part] + [tuple(st[i, rows, :] for i in range(3)) for st in (st4_ref, st16_ref)]
            m_all = functools.reduce(jnp.maximum, [pt[0] for pt in parts])
            wts = [jnp.exp2(pt[0] - m_all) for pt in parts]
            den = sum(wt * pt[1] for wt, pt in zip(wts, parts))
            num = sum(wt * pt[2] for wt, pt in zip(wts, parts))
            o_ref[0, rows, :] = (num / den * g_ref[0, rows, :].astype(F32)).astype(BF16)

    for g in range(16 // group):
        rs = [g * group + rr for rr in range(group)]
        res = partials([(lambda i, prev, r=r: src16_ref[i, r], False) for r in rs])
        for r, part in zip(rs, res):
            for i in range(3):
                st16r_ref[i, r] = part[i]
    for i in range(3):
        st16_ref[i] = jnp.swapaxes(st16r_ref[i], 0, 1).reshape(seq, LANES)

    blocks4 = lambda cc: [(r + 4 * N_BACK * cc, r + 4 * N_BACK * (cc - 1)) for r in range(4)]
    keep4([(r, None) for r in range(4)] + blocks4(1))

    def body4(g, c):
        keep4(blocks4(2 * g) + blocks4(2 * g + 1))
        return c
    lax.fori_loop(1, seq // 4 // N_BACK // 2, body4, 0)

    finish([(0, None)] + [(cc * N_BACK, (cc - 1) * N_BACK) for cc in range(1, group)])

    def body1(g, c):
        starts = [pl.multiple_of((g * group + rr) * N_BACK, N_BACK) for rr in range(group)]
        finish([(st, st - N_BACK) for st in starts])
        return c
    lax.fori_loop(1, seq // N_BACK // group, body1, 0)


def _attn_sample_kernel(q_ref, kn_ref, vn_ref, knt_ref, vnt_ref, g_ref, ck_ref, cv_ref,
                        o_ref, sk_ref, sv_ref, clamp_ref, *, t_new):
    b = pl.program_id(0)
    wb = ck_ref.shape[3]
    hd = HEAD_DIM_ATT
    rows = q_ref.shape[1]

    @pl.when(b == 0)
    def _():
        delta = (wb + lax.broadcasted_iota(jnp.int32, (rows, wb), 0)
                 - lax.broadcasted_iota(jnp.int32, (rows, wb), 1))
        for d, dil in enumerate(DILATIONS):
            ok = ((delta & (dil - 1)) == 0) & (delta >= dil) & (delta <= N_BACK * dil)
            clamp_ref[d] = jnp.where(ok, jnp.finfo(F32).max, NEG_INF)

    tq = lax.broadcasted_iota(jnp.int32, (rows, rows), 0)
    tk = lax.broadcasted_iota(jnp.int32, (rows, rows), 1)
    new_ok = [((tk <= tq) if dil == 1 else (tk == tq)) & (tk < t_new) for dil in DILATIONS]

    lane = lax.broadcasted_iota(jnp.int32, (hd, LANES), 1)
    shift_new = (LANES - t_new) - b * t_new
    nt = (((1,), (1,)), ((), ()))
    outs = []
    for h in range(N_HEADS_ATT):
        hs = slice(h * hd, (h + 1) * hd)
        kt = ck_ref[0, h]
        vt = cv_ref[0, h]
        for old, new_ref, out_ref in ((kt, knt_ref, sk_ref), (vt, vnt_ref, sv_ref)):
            moved = pltpu.roll(old, wb - t_new, axis=1)
            new_cols = pltpu.roll(new_ref[hs, :], shift_new, axis=1)
            out_ref[0, h, :, 0:wb - LANES] = moved[:, 0:wb - LANES]
            out_ref[0, h, :, wb - LANES:wb] = jnp.where(lane < LANES - t_new,
                                                        moved[:, wb - LANES:wb], new_cols)
        qh = q_ref[0, :, hs]
        kn_h = kn_ref[0, :, hs].astype(BF16)
        vn_h = vn_ref[0, :, hs].astype(BF16)
        s_old = jnp.dot(qh, kt.astype(BF16), preferred_element_type=F32)
        s_new = lax.dot_general(qh, kn_h, nt, preferred_element_type=F32)
        ps, pes, ms = [], [], []
        for d in range(len(DILATIONS)):
            so = jnp.minimum(s_old, clamp_ref[d])
            sn = jnp.where(new_ok[d], s_new, NEG_INF)
            m = jnp.maximum(jnp.max(so, axis=-1, keepdims=True), jnp.max(sn, axis=-1, keepdims=True))
            ps.append(jnp.exp2(so - m))
            pes.append(jnp.exp2(sn - m))
            ms.append(m)
        acc = lax.dot_general(jnp.concatenate(ps, axis=0).astype(BF16), vt.astype(BF16), nt,
                              preferred_element_type=F32)
        acc = acc + jnp.dot(jnp.concatenate(pes, axis=0).astype(BF16), vn_h,
                            preferred_element_type=F32)
        m_all = functools.reduce(jnp.maximum, ms)
        den = 0.0
        num = 0.0
        for d in range(len(DILATIONS)):
            wgt = jnp.exp2(ms[d] - m_all)
            den = den + wgt * (jnp.sum(ps[d], axis=-1, keepdims=True)
                               + jnp.sum(pes[d], axis=-1, keepdims=True))
            num = num + wgt * acc[d * rows:(d + 1) * rows]
        outs.append(num / den)
    att = jnp.concatenate(outs, axis=1)
    o_ref[0] = (att * g_ref[0].astype(F32)).astype(BF16)


def _scan_lanes(x, op, fill):
    lane = lax.broadcasted_iota(jnp.int32, x.shape, 1)
    d = 1
    while d < x.shape[1]:
        x = op(x, jnp.where(lane >= d, pltpu.roll(x, d, 1), fill))
        d *= 2
    return x


def _gate_scans(gt):
    b = _scan_lanes(pltpu.roll(gt, N_HEADS_MLSTM, 0), jnp.add, 0.0)
    cm = _scan_lanes(gt - b, jnp.maximum, NEG_INF)
    row = lax.broadcasted_iota(jnp.int32, gt.shape, 0)
    return jnp.where(row < N_HEADS_MLSTM, b, pltpu.roll(cm, N_HEADS_MLSTM, 0))


def _mlstm_kernel(*refs, zero_init, scans_given, fuse_out):
    refs = list(refs)
    q_ref, k_ref, v_ref, gt_ref, so_ref, gb_ref, ng_ref = refs[:7]
    del refs[:7]
    if not zero_init:
        c0_ref, n0_ref, m0_ref = refs[:3]
        del refs[:3]
    if fuse_out:
        x_ref, ya_ref, wo_ref, fg_ref = refs[:4]
        del refs[:4]
    y_ref, c_out_ref, n_out_ref, m_out_ref, cn_ref, m_ref = refs
    c_idx = pl.program_id(1)
    nb = q_ref.shape[0]
    L = CHUNK
    dh = HEAD_DIM_MLSTM
    nh = N_HEADS_MLSTM

    @pl.when(c_idx == 0)
    def _():
        if zero_init:
            cn_ref[...] = jnp.zeros(cn_ref.shape, F32)
            m_ref[...] = jnp.zeros(m_ref.shape, F32)
        else:
            m_ref[...] = m0_ref[...]
            for bb in range(nb):
                for h in range(nh):
                    n_rows = jnp.broadcast_to(n0_ref[bb, h:h + 1, :], (dh, dh))
                    cn_ref[bb, h] = jnp.concatenate([c0_ref[bb, h], n_rows.T], axis=1)

    tri_t = lax.broadcasted_iota(jnp.int32, (L, L), 0)
    tri_s = lax.broadcasted_iota(jnp.int32, (L, L), 1)
    causal = tri_t >= tri_s
    ones_blk = jnp.ones((L, dh), BF16)
    nt = (((1,), (1,)), ((), ()))
    pairs = [(bb, h) for bb in range(nb) for h in range(nh)]
    sl = lambda h: slice(h * dh, (h + 1) * dh)

    qk = {(bb, h): lax.dot_general(q_ref[bb, :, sl(h)], k_ref[bb, :, sl(h)], nt,
                                   preferred_element_type=F32) for bb, h in pairs}
    cn_old = {p: cn_ref[p[0], p[1]] for p in pairs}
    if fuse_out:
        mix_a = x_ref[...].reshape(nb * L, D_MODEL) + jnp.dot(
            ya_ref[...].reshape(nb * L, D_ATT), wo_ref[0:D_ATT, :], preferred_element_type=F32)

    rows, cols, decays, w_rows = [], [], [], []
    for bb in range(nb):
        if scans_given:
            i_row = gt_ref[bb, 0:SUBLANES, :]
            sc = gt_ref[bb, SUBLANES:2 * SUBLANES, :]
        else:
            i_row = gt_ref[bb]
            sc = _gate_scans(i_row)
        b = sc
        cm = pltpu.roll(sc, nh, 0)
        m_prev = m_ref[bb]
        m_t = jnp.maximum(m_prev + b, b + cm)
        inter = jnp.exp(m_prev + b - m_t)
        m_last = jnp.broadcast_to(m_t[:, L - 1:L], m_t.shape)
        b_last = jnp.broadcast_to(b[:, L - 1:L], b.shape)
        decay = jnp.exp(m_prev + b_last - m_last)
        w_row = jnp.exp(b_last - b + i_row - m_last)
        m_ref[bb] = m_last
        stack = jnp.concatenate([b - m_t, inter, jnp.exp(-m_t),
                                 jnp.zeros((L - 3 * SUBLANES, L), F32)], axis=0)
        cols.append(stack.T)
        rows.append(i_row - b)
        w_rows.append(w_row)
        decays.append(jnp.concatenate([decay, decay], axis=1))
    col = lambda bb, kind, h: cols[bb][:, kind * SUBLANES + h:kind * SUBLANES + h + 1]

    sqk = {}
    for bb, h in pairs:
        dlog = col(bb, 0, h) + rows[bb][h:h + 1, :]
        sqk[bb, h] = (qk[bb, h] * jnp.exp(jnp.where(causal, dlog, NEG_INF))).astype(BF16)
    v_one = {(bb, h): jnp.concatenate([v_ref[bb, :, sl(h)], ones_blk], axis=1) for bb, h in pairs}
    tots = {}
    for bb, h in pairs:
        q_dec = (col(bb, 1, h) * q_ref[bb, :, sl(h)].astype(F32)).astype(BF16)
        lhs = jnp.concatenate([sqk[bb, h], q_dec], axis=1)
        rhs = jnp.concatenate([v_one[bb, h], cn_old[bb, h].astype(BF16)], axis=0)
        tots[bb, h] = jnp.dot(lhs, rhs, preferred_element_type=F32)
    yb = {}
    for bb, h in pairs:
        tot = tots[bb, h]
        hh = tot[:, :dh] / jnp.maximum(jnp.abs(tot[:, dh:]), col(bb, 2, h))
        hh = so_ref[bb, :, sl(h)].astype(F32) * hh
        hh = hh * lax.rsqrt(jnp.mean(hh * hh, axis=-1, keepdims=True) + EPS)
        hh = hh * ng_ref[:, sl(h)]
        yb[bb, h] = (hh * gb_ref[bb, :, sl(h)].astype(F32)).astype(BF16)
    if fuse_out:
        yb_all = jnp.concatenate([jnp.concatenate([yb[bb, h] for h in range(nh)], axis=1)
                                  for bb in range(nb)], axis=0)
        mix = mix_a + jnp.dot(yb_all, wo_ref[D_ATT:, :], preferred_element_type=F32)
        res = _rmsnorm(mix, fg_ref[...])
        y_ref[...] = res.reshape(nb, L, D_MODEL)
    else:
        for bb, h in pairs:
            y_ref[bb, :, sl(h)] = yb[bb, h]
    for bb, h in pairs:
        kt_w = k_ref[bb, :, sl(h)].T.astype(F32) * w_rows[bb][h:h + 1, :]
        upd = jnp.dot(kt_w.astype(BF16), v_one[bb, h], preferred_element_type=F32)
        cn_ref[bb, h] = decays[bb][h:h + 1, :] * cn_old[bb, h] + upd

    @pl.when(c_idx == pl.num_programs(1) - 1)
    def _():
        m_out_ref[...] = m_ref[...]
        for bb in range(nb):
            n_rows = []
            for h in range(nh):
                cn = cn_ref[bb, h]
                c_out_ref[bb, h] = cn[:, :dh]
                n_rows.append(cn[:, dh:].T[0:1, :])
            n_out_ref[bb] = jnp.concatenate(n_rows, axis=0)


def _outproj_kernel(x_ref, ya_ref, yb_ref, w_ref, g_ref, o_ref):
    rows = x_ref.shape[0]
    sub = min(OUTPROJ_SUBTILE, rows)
    for r0 in range(0, rows, sub):
        rs = pl.ds(r0, sub)
        mix = jnp.dot(ya_ref[rs, :], w_ref[0:D_ATT, :], preferred_element_type=F32)
        mix = mix + jnp.dot(yb_ref[rs, :], w_ref[D_ATT:, :], preferred_element_type=F32)
        o_ref[rs, :] = _rmsnorm(x_ref[rs, :] + mix, g_ref[...])


def _rope_tables(pos):
    half = ROT_DIM // 2
    f32 = np.float32
    inv = f32(ROPE_THETA) ** (-np.arange(half, dtype=f32) * f32(2.0) / f32(ROT_DIM))
    ang = np.asarray(pos, dtype=f32)[:, None] * inv[None, :].astype(f32)
    cos, sin = np.cos(ang).astype(f32), np.sin(ang).astype(f32)
    n = ang.shape[0]
    one = np.ones((n, HEAD_DIM_ATT - ROT_DIM), f32)
    zero = np.zeros((n, HEAD_DIM_ATT - ROT_DIM), f32)
    zh = np.zeros((n, half), f32)
    cos_t = np.concatenate([cos, cos, one], axis=1)
    lo_t = np.concatenate([-sin, zh, zero], axis=1)
    hi_t = np.concatenate([zh, sin, zero], axis=1)
    rep = lambda t: np.concatenate([t, t], axis=1)
    return rep(cos_t), rep(lo_t), rep(hi_t)


def _params(sem):
    return pltpu.CompilerParams(dimension_semantics=sem, vmem_limit_bytes=VMEM_LIMIT)


def _const_spec(shape):
    return pl.BlockSpec(shape, lambda *_: (0,) * len(shape))


def _inproj_prompt(x, norm_g, w_main, w_gate, conv_w, conv_b, bif):
    B, S, _ = x.shape
    tm = ROW_TILE
    cos, lo, hi = _rope_tables(np.arange(S))
    tile = lambda width: pl.BlockSpec((1, tm, width), lambda b, j: (b, j, 0))
    tab = pl.BlockSpec((tm, LANES), lambda b, j: (j, 0))
    bf = lambda: jax.ShapeDtypeStruct((B, S, D_ATT), BF16)
    f5 = lambda: jax.ShapeDtypeStruct((B, N_HEADS_ATT, HEAD_DIM_ATT, S), F32)
    tile5 = pl.BlockSpec((1, N_HEADS_ATT, HEAD_DIM_ATT, tm), lambda b, j: (b, 0, 0, j))
    out_shape = (bf(), bf(), bf(), f5(), f5(), bf(), bf(), bf(), bf(), bf(), bf(),
                 jax.ShapeDtypeStruct((B, 2 * SUBLANES, S), F32),
                 jax.ShapeDtypeStruct((B, CONV_WIDTH - 1, 2 * D_MLSTM), F32))
    out_specs = tuple([tile(D_ATT)] * 3 + [tile5] * 2 + [tile(D_ATT)] * 6) + (
        pl.BlockSpec((1, 2 * SUBLANES, tm), lambda b, j: (b, 0, j)),
        pl.BlockSpec((1, CONV_WIDTH - 1, 2 * D_MLSTM), lambda b, j: (b, 0, 0)))
    return pl.pallas_call(
        _inproj_prompt_kernel,
        grid=(B, S // tm),
        in_specs=[tile(D_MODEL), _const_spec((1, D_MODEL)), _const_spec((D_MODEL, OFF_G)),
                  _const_spec(w_gate.shape),
                  _const_spec((CONV_WIDTH, 2 * D_MLSTM)), _const_spec((1, 2 * D_MLSTM)),
                  _const_spec((1, LANES)), tab, tab, tab],
        out_specs=out_specs,
        out_shape=out_shape,
        scratch_shapes=[pltpu.VMEM((tm + 2 * SUBLANES, 2 * D_MLSTM), F32)],
        compiler_params=_params(("arbitrary", "arbitrary")),
        name="inproj_prompt",
    )(x, norm_g, w_main, w_gate, conv_w, conv_b, bif, cos, lo, hi)


def _inproj_sample(x2, norm_g, w_main, w_gate, conv_w, conv_b, bif, hist, t_new):
    rows = x2.shape[0]
    pos = PAST_LEN + np.arange(t_new)
    cos, lo, hi = (np.tile(t, (rows // t_new, 1)) for t in _rope_tables(pos))
    bf = lambda: jax.ShapeDtypeStruct((rows, D_ATT), BF16)
    f3 = lambda: jax.ShapeDtypeStruct((rows, D_ATT), F32)
    f5 = lambda: jax.ShapeDtypeStruct((D_ATT, rows), F32)
    out_shape = (bf(), f3(), f3(), f5(), f5(), bf(),
                 jax.ShapeDtypeStruct((rows, 5 * D_MLSTM), BF16),
                 jax.ShapeDtypeStruct((rows, LANES), F32),
                 jax.ShapeDtypeStruct((rows, 2 * D_MLSTM), F32))
    return pl.pallas_call(
        functools.partial(_inproj_sample_kernel, t_new=t_new),
        out_shape=out_shape,
        compiler_params=pltpu.CompilerParams(vmem_limit_bytes=VMEM_LIMIT),
        name="inproj_sample",
    )(x2, norm_g, w_main, w_gate, conv_w, conv_b, bif, cos, lo, hi, *hist)


def _attn_prompt(q, k, v, gate):
    B, S, _ = q.shape
    spec = pl.BlockSpec((1, S, LANES), lambda b, h: (b, 0, h))
    return pl.pallas_call(
        _attn_prompt_kernel,
        grid=(B, D_ATT // LANES),
        in_specs=[spec, spec, spec, spec],
        out_specs=spec,
        out_shape=jax.ShapeDtypeStruct((B, S, D_ATT), BF16),
        scratch_shapes=[
            pltpu.VMEM((5, S, LANES), BF16), pltpu.VMEM((5, S, LANES), F32),
            pltpu.VMEM((5, 16, S // 16, LANES), BF16),
            pltpu.VMEM((N_BACK, 2 * N_BACK), F32),
            pltpu.VMEM((3, 16, S // 16, LANES), F32),
            pltpu.VMEM((3, S, LANES), F32), pltpu.VMEM((3, S, LANES), F32)],
        compiler_params=_params(("arbitrary", "arbitrary")),
        name="attn_prompt",
    )(q, k, v, gate)


def _attn_sample(q, kn, vn, knt, vnt, gate, ck, cv, t_new):
    B, rows, _ = q.shape
    wb = ck.shape[3]
    small = pl.BlockSpec((1, rows, D_ATT), lambda b: (b, 0, 0))
    big = pl.BlockSpec((1, N_HEADS_ATT, HEAD_DIM_ATT, wb), lambda b: (b, 0, 0, 0))
    win_shape = jax.ShapeDtypeStruct((B, N_HEADS_ATT, HEAD_DIM_ATT, wb), F32)
    return pl.pallas_call(
        functools.partial(_attn_sample_kernel, t_new=t_new),
        grid=(B,),
        in_specs=[small, small, small, _const_spec(knt.shape), _const_spec(vnt.shape), small, big, big],
        out_specs=(small, big, big),
        out_shape=(jax.ShapeDtypeStruct((B, rows, D_ATT), BF16), win_shape, win_shape),
        scratch_shapes=[pltpu.VMEM((len(DILATIONS), rows, wb), F32)],
        compiler_params=_params(("arbitrary",)),
        name="attn_sample",
    )(q, kn, vn, knt, vnt, gate, ck, cv)


def _mlstm(q, k, v, gates_t, sig_o, gate_b, norm_g, state=None, out_proj=None,
           lane_blocks=(0, 0, 0, 0, 0)):
    B, S, _ = q.shape
    nc = S // CHUNK
    nb = MLSTM_BATCH
    nh, dh = N_HEADS_MLSTM, HEAD_DIM_MLSTM
    grows = gates_t.shape[1]
    tile = lambda width, lb=0: pl.BlockSpec((nb, CHUNK, width), lambda b, c: (b, c, lb))
    op = [tile(D_MLSTM, lb) for lb in lane_blocks]
    gspec = pl.BlockSpec((nb, grows, CHUNK), lambda b, c: (b, 0, c))
    c_spec = pl.BlockSpec((nb, nh, dh, dh), lambda b, c: (b, 0, 0, 0))
    n_spec = pl.BlockSpec((nb, nh, dh), lambda b, c: (b, 0, 0))
    m_spec = pl.BlockSpec((nb, SUBLANES, LANES), lambda b, c: (b, 0, 0))
    zero_init = state is None
    fuse_out = out_proj is not None
    in_specs = op[:3] + [gspec] + op[3:] + [_const_spec((1, D_MLSTM))]
    args = [q, k, v, gates_t, sig_o, gate_b, norm_g]
    if not zero_init:
        in_specs += [c_spec, n_spec, m_spec]
        args += list(state)
    if fuse_out:
        in_specs += [tile(D_MODEL), tile(D_ATT), _const_spec((D_ATT + D_MLSTM, D_MODEL)),
                     _const_spec((1, D_MODEL))]
        args += list(out_proj)
        y_spec, y_shape = tile(D_MODEL), jax.ShapeDtypeStruct((B, S, D_MODEL), F32)
    else:
        y_spec, y_shape = tile(D_MLSTM), jax.ShapeDtypeStruct((B, S, D_MLSTM), BF16)
    return pl.pallas_call(
        functools.partial(_mlstm_kernel, zero_init=zero_init, scans_given=grows == 2 * SUBLANES,
                          fuse_out=fuse_out),
        grid=(B // nb, nc),
        in_specs=in_specs,
        out_specs=(y_spec, c_spec, n_spec, m_spec),
        out_shape=(y_shape,
                   jax.ShapeDtypeStruct((B, nh, dh, dh), F32),
                   jax.ShapeDtypeStruct((B, nh, dh), F32),
                   jax.ShapeDtypeStruct((B, SUBLANES, LANES), F32)),
        scratch_shapes=[pltpu.VMEM((nb, nh, dh, 2 * dh), F32),
                        pltpu.VMEM((nb, SUBLANES, LANES), F32)],
        compiler_params=_params(("arbitrary", "arbitrary")),
        name="mlstm_prompt" if zero_init else "mlstm_sample",
    )(*args)


def _outproj(x2, ya, yb, w_out, final_g):
    rows = x2.shape[0]
    tm = min(OUTPROJ_TILE, rows)
    tile = lambda width: pl.BlockSpec((tm, width), lambda i: (i, 0))
    return pl.pallas_call(
        _outproj_kernel,
        grid=(rows // tm,),
        in_specs=[tile(D_MODEL), tile(D_ATT), tile(D_MLSTM),
                  _const_spec((D_ATT + D_MLSTM, D_MODEL)), _const_spec((1, D_MODEL))],
        out_specs=tile(D_MODEL),
        out_shape=jax.ShapeDtypeStruct((rows, D_MODEL), F32),
        compiler_params=_params(("arbitrary",)),
        name="outproj",
    )(x2, ya, yb, w_out, final_g)


def kernel(x_prompt, x_sample, cache_win_k, cache_win_v, state_conv, state_C, state_n, state_m,
           norm_g, w_in, conv_w, conv_b, b_i, b_f, mlstm_norm_g, w_out, final_norm_g):
    assert w_in.shape[0] == 1, "single-layer model"
    B, S, D = x_prompt.shape
    DB, T, _ = x_sample.shape
    HB, DK = N_HEADS_MLSTM, HEAD_DIM_MLSTM
    wb = cache_win_k.shape[2]
    assert S % ROW_TILE == 0 and S == 16 * N_BACK and wb >= 16 * N_BACK and CONV_WIDTH - 1 <= T <= SUBLANES and DB * T == LANES

    w_main = w_in[0].astype(BF16)
    w_gate = jnp.pad(w_main[:, OFF_G:], ((0, 0), (0, LANES - 2 * HB)))
    w_o = w_out[0].astype(BF16)
    g_in = norm_g[0][None, :]
    cw, cb = conv_w[0], conv_b[0][None, :]
    bif = jnp.pad(jnp.concatenate([b_i[0], b_f[0]]), (0, LANES - 2 * HB))[None, :]
    ng = mlstm_norm_g[0][None, :]
    g_fin = final_norm_g[None, :]

    (q_p, k_p, v_p, pk, pv, ga_p, qm_p, km_p, vb_p, so_p, gb_p, gt_p, p_conv) = _inproj_prompt(
        x_prompt, g_in, w_main, w_gate, cw, cb, bif)
    ya_p = _attn_prompt(q_p, k_p, v_p, ga_p)
    y_prompt, c_p, n_p, m_p = _mlstm(qm_p, km_p, vb_p, gt_p, so_p, gb_p, ng,
                                     out_proj=(x_prompt, ya_p, w_o, g_fin))

    sc = state_conv[0]
    zrow = jnp.zeros((DB, 1, 2 * D_MLSTM), F32)
    hist = []
    for sh in (1, 2, 3):
        rows_ = [sc[:, CONV_WIDTH - 1 + t - sh:CONV_WIDTH + t - sh] if t < sh else zrow
                 for t in range(T)]
        hist.append(jnp.concatenate(rows_, axis=1).reshape(DB * T, 2 * D_MLSTM))
    (q_s, kn, vn, knt, vnt, ga_s, mix_s, gates_s, qk_s) = _inproj_sample(
        x_sample.reshape(DB * T, D), g_in, w_main, w_gate, cw, cb, bif, hist, T)
    r3 = lambda a: a.reshape(DB, T, a.shape[-1])
    pad8 = lambda a: jnp.pad(r3(a), ((0, 0), (0, SUBLANES - T), (0, 0)))
    to_hdp = lambda c: jnp.transpose(c[0], (0, 2, 3, 1))
    from_hdp = lambda c: jnp.transpose(c, (0, 3, 1, 2))[None]
    ya_s, s_k, s_v = _attn_sample(pad8(q_s), pad8(kn), pad8(vn), knt, vnt, pad8(ga_s),
                                  to_hdp(cache_win_k), to_hdp(cache_win_v), T)
    ya_s = ya_s[:, :T]

    pad_t = lambda a: jnp.pad(r3(a), ((0, 0), (0, CHUNK - T), (0, 0)))
    g3 = r3(gates_s)[:, :, :SUBLANES]
    null_gate = jnp.concatenate([jnp.full((HB,), NEG_INF, F32), jnp.zeros((HB,), F32)])
    gt_s = jnp.concatenate([g3, jnp.broadcast_to(null_gate, (DB, CHUNK - T, SUBLANES))], axis=1)
    gt_s = gt_s.transpose(0, 2, 1)
    m0 = jnp.broadcast_to(jnp.pad(state_m[0], ((0, 0), (0, SUBLANES - HB)))[..., None],
                          (DB, SUBLANES, LANES))
    mix_p = pad_t(mix_s)
    yb_s, c_s, n_s, m_s = _mlstm(mix_p, mix_p, mix_p, gt_s, mix_p, mix_p, ng,
                                 state=(state_C[0], state_n[0], m0), lane_blocks=(0, 1, 2, 3, 4))
    y_sample = _outproj(x_sample.reshape(DB * T, D), ya_s.reshape(DB * T, D_ATT),
                        yb_s[:, :T].reshape(DB * T, D_MLSTM), w_o, g_fin).reshape(DB, T, D)

    return (y_prompt, y_sample,
            from_hdp(pk), from_hdp(pv), p_conv[None],
            c_p[None], n_p[None], m_p[None, :, :HB, 0],
            from_hdp(s_k), from_hdp(s_v), r3(qk_s)[None, :, T - (CONV_WIDTH - 1):],
            c_s[None], n_s[None], m_s[None, :, :HB, 0])
```

```python
import functools
import math

import jax
import jax.numpy as jnp
import numpy as np
from jax import lax
from jax.experimental import pallas as pl
from jax.experimental.pallas import tpu as pltpu

F32 = jnp.float32
BF16 = jnp.bfloat16

D_MODEL = 1024
D_ATT = 512
N_HEADS_ATT = 8
HEAD_DIM_ATT = 64
D_MLSTM = 512
N_HEADS_MLSTM = 4
HEAD_DIM_MLSTM = 128
ROT_DIM = 16
ROPE_THETA = 500000.0
PAST_LEN = 16384
DILATIONS = (1, 4, 16)
N_BACK = 128
CONV_WIDTH = 4
CHUNK = 128
EPS = 1e-6
NEG_INF = -1e30
LOG2_E = math.log2(math.e)

LANES = 128
SUBLANES = 8
OFF_QA, OFF_KA, OFF_VA, OFF_ZA = (i * D_ATT for i in range(4))
OFF_QB, _, OFF_VB, OFF_OB, OFF_ZB, OFF_G = (4 * D_ATT + i * D_MLSTM for i in range(6))
ROW_TILE = 512
INPROJ_SUBTILE = 128
OUTPROJ_TILE = 2048
OUTPROJ_SUBTILE = 256
MLSTM_BATCH = 4
VMEM_LIMIT = 56 * 1024 * 1024


def _silu(x):
    return x * jax.nn.sigmoid(x)


def _rmsnorm(x, g):
    return x * lax.rsqrt(jnp.mean(x * x, axis=-1, keepdims=True) + EPS) * g


def _rope(u, cos, sin_lo, sin_hi):
    outs = []
    for c in range(u.shape[1] // LANES):
        xs = u[:, c * LANES:(c + 1) * LANES]
        outs.append(xs * cos + pltpu.roll(xs, LANES - ROT_DIM // 2, 1) * sin_lo
                    + pltpu.roll(xs, ROT_DIM // 2, 1) * sin_hi)
    return jnp.concatenate(outs, axis=1)


def _gate_block(ug, bif):
    gz = ug + bif
    lane = lax.broadcasted_iota(jnp.int32, gz.shape, 1)
    logf = jnp.minimum(gz, 0.0) - jnp.log1p(jnp.exp(-jnp.abs(gz)))
    return jnp.where(lane < N_HEADS_MLSTM, gz, logf)


def _segments(hn, w_ref, wg_ref):
    def seg(off, width):
        rhs = wg_ref[...] if off == OFF_G else w_ref[:, off:off + width]
        return jnp.dot(hn, rhs, preferred_element_type=F32)
    return seg


def _inproj_common(seg, cos, sin_lo, sin_hi, q_ref, ga_ref, vb_ref, so_ref, gb_ref):
    q = _rope(seg(OFF_QA, D_ATT), cos, sin_lo, sin_hi) * (HEAD_DIM_ATT ** -0.5 * LOG2_E)
    q_ref[...] = q.astype(BF16).reshape(q_ref.shape)
    k = _rope(seg(OFF_KA, D_ATT), cos, sin_lo, sin_hi)
    v = seg(OFF_VA, D_ATT)
    ga_ref[...] = _silu(seg(OFF_ZA, D_ATT)).astype(BF16).reshape(ga_ref.shape)
    vb_ref[...] = seg(OFF_VB, D_MLSTM).astype(BF16).reshape(vb_ref.shape)
    so_ref[...] = jax.nn.sigmoid(seg(OFF_OB, D_MLSTM)).astype(BF16).reshape(so_ref.shape)
    gb_ref[...] = _silu(seg(OFF_ZB, D_MLSTM)).astype(BF16).reshape(gb_ref.shape)
    return k, v


def _inproj_prompt_kernel(x_ref, g_ref, w_ref, wg_ref, cw_ref, cb_ref, bif_ref, cos_ref, slo_ref,
                          shi_ref,
                          q_ref, kb_ref, vbf_ref, pk_ref, pv_ref, ga_ref, qm_ref, km_ref, vb_ref,
                          so_ref, gb_ref, gt_ref, pconv_ref, xp_ref):
    j = pl.program_id(1)
    tm = x_ref.shape[1]
    sub = INPROJ_SUBTILE
    assert CHUNK == LANES and tm % sub == 0 and sub % CHUNK == 0

    @pl.when(j == 0)
    def _():
        xp_ref[0:SUBLANES, :] = jnp.zeros((SUBLANES, 2 * D_MLSTM), F32)

    @pl.when(j > 0)
    def _():
        xp_ref[0:SUBLANES, :] = xp_ref[tm:tm + SUBLANES, :]

    for r0 in range(0, tm, sub):
        rs = pl.ds(r0, sub)
        part = lambda ref: ref.at[0, rs, :]
        hn = _rmsnorm(x_ref[0, rs, :], g_ref[...]).astype(BF16)
        seg = _segments(hn, w_ref, wg_ref)

        gates = _gate_block(seg(OFF_G, LANES), bif_ref[...])
        for i in range(sub // CHUNK):
            cs = slice(r0 + i * CHUNK, r0 + (i + 1) * CHUNK)
            gt = gates[i * CHUNK:(i + 1) * CHUNK, :].T[0:SUBLANES, :]
            gt_ref[0, 0:SUBLANES, cs] = gt
            gt_ref[0, SUBLANES:2 * SUBLANES, cs] = _gate_scans(gt)

        k, v = _inproj_common(seg, cos_ref[rs, :], slo_ref[rs, :], shi_ref[rs, :],
                              part(q_ref), part(ga_ref), part(vb_ref), part(so_ref), part(gb_ref))
        kb_ref[0, rs, :] = k.astype(BF16)
        vbf_ref[0, rs, :] = v.astype(BF16)
        pk_ref[0, :, :, r0:r0 + sub] = k.T.reshape(N_HEADS_ATT, HEAD_DIM_ATT, sub)
        pv_ref[0, :, :, r0:r0 + sub] = v.T.reshape(N_HEADS_ATT, HEAD_DIM_ATT, sub)

        base = SUBLANES + r0
        xp_ref[base:base + sub, :] = seg(OFF_QB, 2 * D_MLSTM)
        y = cb_ref[...] + xp_ref[base:base + sub, :] * cw_ref[3:4, :]
        for jj in range(CONV_WIDTH - 1):
            sh = CONV_WIDTH - 1 - jj
            y = y + xp_ref[base - sh:base - sh + sub, :] * cw_ref[jj:jj + 1, :]
        y = _silu(y)
        qm_ref[0, rs, :] = y[:, :D_MLSTM].astype(BF16)
        km_ref[0, rs, :] = (y[:, D_MLSTM:] * (HEAD_DIM_MLSTM ** -0.5)).astype(BF16)

    @pl.when(j == pl.num_programs(1) - 1)
    def _():
        pconv_ref[0] = xp_ref[tm + SUBLANES - (CONV_WIDTH - 1):tm + SUBLANES, :]


def _inproj_sample_kernel(x_ref, g_ref, w_ref, wg_ref, cw_ref, cb_ref, bif_ref, cos_ref, slo_ref,
                          shi_ref,
                          h1_ref, h2_ref, h3_ref,
                          q_ref, kn_ref, vn_ref, knt_ref, vnt_ref, ga_ref, mix_ref, gates_ref, qk_ref,
                          *, t_new):
    qm_ref, km_ref, vb_ref, so_ref, gb_ref = (
        mix_ref.at[:, pl.ds(i * D_MLSTM, D_MLSTM)] for i in range(5))
    hn = _rmsnorm(x_ref[...], g_ref[...]).astype(BF16)
    seg = _segments(hn, w_ref, wg_ref)
    k, v = _inproj_common(seg, cos_ref[...], slo_ref[...], shi_ref[...],
                          q_ref, ga_ref, vb_ref, so_ref, gb_ref)
    kn_ref[...] = k
    vn_ref[...] = v
    knt_ref[...] = k.T
    vnt_ref[...] = v.T
    u = seg(OFF_QB, 2 * D_MLSTM)
    qk_ref[...] = u
    t = lax.rem(lax.broadcasted_iota(jnp.int32, u.shape, 0), t_new)
    y = cb_ref[...] + u * cw_ref[3:4, :]
    for sh, h_ref in ((1, h1_ref), (2, h2_ref), (3, h3_ref)):
        prev = jnp.where(t >= sh, pltpu.roll(u, sh, 0), h_ref[...])
        y = y + prev * cw_ref[3 - sh:4 - sh, :]
    y = _silu(y)
    qm_ref[...] = y[:, :D_MLSTM].astype(BF16)
    km_ref[...] = (y[:, D_MLSTM:] * (HEAD_DIM_MLSTM ** -0.5)).astype(BF16)
    gates_ref[...] = _gate_block(seg(OFF_G, LANES), bif_ref[...])


def _attn_prompt_kernel(q_ref, k_ref, v_ref, g_ref, o_ref,
                        src1_ref, src4_ref, src16_ref, bias_ref, st16r_ref, st16_ref, st4_ref):
    seq = q_ref.shape[1]
    group = 8
    head_a_full = lax.broadcasted_iota(jnp.int32, (seq, LANES), 1) < HEAD_DIM_ATT

    def stack(q, k, v, is_a):
        zero, one = jnp.zeros((), q.dtype), jnp.ones((), q.dtype)
        return (jnp.where(is_a, q, zero), jnp.where(is_a, zero, q), k,
                jnp.where(is_a, v, one), jnp.where(is_a, one, v))

    q, k, v = q_ref[0], k_ref[0], v_ref[0]
    for i, x in enumerate(stack(q, k, v, head_a_full)):
        src1_ref[i] = x
    for i, x in enumerate((q, k, v)):
        src4_ref[i] = x.astype(F32)
    blk = 16 * 16
    pa = lax.broadcasted_iota(jnp.int32, (blk, blk), 0)
    pb = lax.broadcasted_iota(jnp.int32, (blk, blk), 1)
    perm = jnp.where(pb == 16 * (pa & 15) + (pa >> 4), 1.0, 0.0).astype(BF16)
    is_a_blk = lax.broadcasted_iota(jnp.int32, (16, 16, LANES), 2) < HEAD_DIM_ATT
    for j in range(seq // blk):
        rows = slice(j * blk, (j + 1) * blk)
        parts = [jnp.dot(perm, x[rows], preferred_element_type=F32).astype(BF16).reshape(16, 16, LANES)
                 for x in (q, k, v)]
        for i, x in enumerate(stack(*parts, is_a_blk)):
            src16_ref[i, :, 16 * j:16 * (j + 1), :] = x
    u = lax.broadcasted_iota(jnp.int32, (N_BACK, 2 * N_BACK), 0)
    w = lax.broadcasted_iota(jnp.int32, (N_BACK, 2 * N_BACK), 1)
    bias_ref[...] = jnp.where((w >= u) & (w <= u + N_BACK), jnp.finfo(F32).max, NEG_INF)
    head_a = lax.broadcasted_iota(jnp.int32, (N_BACK, LANES), 1) < HEAD_DIM_ATT

    def partials(gets):
        staged = []
        for get, has_prev in gets:
            if has_prev:
                both = lambda i, get=get: jnp.concatenate([get(i, True), get(i, False)], axis=0)
                bias = bias_ref[...]
            else:
                both = lambda i, get=get: get(i, False)
                bias = bias_ref[:, N_BACK:]
            kk = both(2)
            heads = []
            for qi in (0, 1):
                s = lax.dot_general(get(qi, False), kk, (((1,), (1,)), ((), ())),
                                    preferred_element_type=F32)
                s = jnp.minimum(s, bias)
                mh = jnp.max(s, axis=-1, keepdims=True)
                heads.append((mh, jnp.exp2(s - mh).astype(BF16)))
            staged.append((both, heads))
        out = []
        for both, ((m_a, p_a), (m_b, p_b)) in staged:
            pv_a = jnp.dot(p_a, both(3), preferred_element_type=F32)
            pv_b = jnp.dot(p_b, both(4), preferred_element_type=F32)
            acc = jnp.where(head_a, pv_a, pv_b)
            den = pltpu.roll(jnp.where(head_a, pv_b, pv_a), HEAD_DIM_ATT, 1)
            out.append((jnp.where(head_a, m_a, m_b), den, acc))
        return out

    def get4(qstart, kprev_start):
        loaded = {}

        def base(j, prev):
            if (j, prev) not in loaded:
                start = kprev_start if prev else qstart
                loaded[j, prev] = src4_ref[j, pl.ds(start, N_BACK, stride=4), :].astype(BF16)
            return loaded[j, prev]

        def get(i, prev):
            if i == 2:
                return base(1, prev)
            x = base(0 if i < 2 else 2, prev)
            fill = jnp.zeros((), BF16) if i < 2 else jnp.ones((), BF16)
            keep_a = i in (0, 3)
            return jnp.where(head_a, x, fill) if keep_a else jnp.where(head_a, fill, x)
        return get, kprev_start is not None

    def get1(qstart, kprev_start):
        def get(i, prev):
            return src1_ref[i, pl.ds(kprev_start if prev else qstart, N_BACK), :]
        return get, kprev_start is not None

    def keep4(blocks):
        res = partials([get4(qs, ks) for qs, ks in blocks])
        for (qs, _), (m, den, acc) in zip(blocks, res):
            rows = pl.ds(qs, N_BACK, stride=4)
            st4_ref[0, rows, :] = m
            st4_ref[1, rows, :] = den
            st4_ref[2, rows, :] = acc

    def finish(blocks):
        res = partials([get1(qs, ks) for qs, ks in blocks])
        for (qs, _), part in zip(blocks, res):
            rows = pl.ds(qs, N_BACK)
            parts = [part] + [tuple(st[i, rows, :] for i in range(3)) for st in (st4_ref, st16_ref)]
            m_all = functools.reduce(jnp.maximum, [pt[0] for pt in parts])
            wts = [jnp.exp2(pt[0] - m_all) for pt in parts]
            den = sum(wt * pt[1] for wt, pt in zip(wts, parts))
            num = sum(wt * pt[2] for wt, pt in zip(wts, parts))
            o_ref[0, rows, :] = (num / den * g_ref[0, rows, :].astype(F32)).astype(BF16)

    for g in range(16 // group):
        rs = [g * group + rr for rr in range(group)]
        res = partials([(lambda i, prev, r=r: src16_ref[i, r], False) for r in rs])
        for r, part in zip(rs, res):
            for i in range(3):
                st16r_ref[i, r] = part[i]
    for i in range(3):
        st16_ref[i] = jnp.swapaxes(st16r_ref[i], 0, 1).reshape(seq, LANES)

    blocks4 = lambda cc: [(r + 4 * N_BACK * cc, r + 4 * N_BACK * (cc - 1)) for r in range(4)]
    keep4([(r, None) for r in range(4)] + blocks4(1))

    def body4(g, c):
        keep4(blocks4(2 * g) + blocks4(2 * g + 1))
        return c
    lax.fori_loop(1, seq // 4 // N_BACK // 2, body4, 0)

    finish([(0, None)] + [(cc * N_BACK, (cc - 1) * N_BACK) for cc in range(1, group)])

    def body1(g, c):
        starts = [pl.multiple_of((g * group + rr) * N_BACK, N_BACK) for rr in range(group)]
        finish([(st, st - N_BACK) for st in starts])
        return c
    lax.fori_loop(1, seq // N_BACK // group, body1, 0)


def _attn_sample_kernel(q_ref, kn_ref, vn_ref, knt_ref, vnt_ref, g_ref, ck_ref, cv_ref,
                        o_ref, sk_ref, sv_ref, clamp_ref, *, t_new):
    b = pl.program_id(0)
    wb = ck_ref.shape[3]
    hd = HEAD_DIM_ATT
    rows = q_ref.shape[1]

    @pl.when(b == 0)
    def _():
        delta = (wb + lax.broadcasted_iota(jnp.int32, (rows, wb), 0)
                 - lax.broadcasted_iota(jnp.int32, (rows, wb), 1))
        for d, dil in enumerate(DILATIONS):
            ok = ((delta & (dil - 1)) == 0) & (delta >= dil) & (delta <= N_BACK * dil)
            clamp_ref[d] = jnp.where(ok, jnp.finfo(F32).max, NEG_INF)

    tq = lax.broadcasted_iota(jnp.int32, (rows, rows), 0)
    tk = lax.broadcasted_iota(jnp.int32, (rows, rows), 1)
    new_ok = [((tk <= tq) if dil == 1 else (tk == tq)) & (tk < t_new) for dil in DILATIONS]

    lane = lax.broadcasted_iota(jnp.int32, (hd, LANES), 1)
    shift_new = (LANES - t_new) - b * t_new
    nt = (((1,), (1,)), ((), ()))
    outs = []
    for h in range(N_HEADS_ATT):
        hs = slice(h * hd, (h + 1) * hd)
        kt = ck_ref[0, h]
        vt = cv_ref[0, h]
        for old, new_ref, out_ref in ((kt, knt_ref, sk_ref), (vt, vnt_ref, sv_ref)):
            moved = pltpu.roll(old, wb - t_new, axis=1)
            new_cols = pltpu.roll(new_ref[hs, :], shift_new, axis=1)
            out_ref[0, h, :, 0:wb - LANES] = moved[:, 0:wb - LANES]
            out_ref[0, h, :, wb - LANES:wb] = jnp.where(lane < LANES - t_new,
                                                        moved[:, wb - LANES:wb], new_cols)
        qh = q_ref[0, :, hs]
        kn_h = kn_ref[0, :, hs].astype(BF16)
        vn_h = vn_ref[0, :, hs].astype(BF16)
        s_old = jnp.dot(qh, kt.astype(BF16), preferred_element_type=F32)
        s_new = lax.dot_general(qh, kn_h, nt, preferred_element_type=F32)
        ps, pes, ms = [], [], []
        for d in range(len(DILATIONS)):
            so = jnp.minimum(s_old, clamp_ref[d])
            sn = jnp.where(new_ok[d], s_new, NEG_INF)
            m = jnp.maximum(jnp.max(so, axis=-1, keepdims=True), jnp.max(sn, axis=-1, keepdims=True))
            ps.append(jnp.exp2(so - m))
            pes.append(jnp.exp2(sn - m))
            ms.append(m)
        acc = lax.dot_general(jnp.concatenate(ps, axis=0).astype(BF16), vt.astype(BF16), nt,
                              preferred_element_type=F32)
        acc = acc + jnp.dot(jnp.concatenate(pes, axis=0).astype(BF16), vn_h,
                            preferred_element_type=F32)
        m_all = functools.reduce(jnp.maximum, ms)
        den = 0.0
        num = 0.0
        for d in range(len(DILATIONS)):
            wgt = jnp.exp2(ms[d] - m_all)
            den = den + wgt * (jnp.sum(ps[d], axis=-1, keepdims=True)
                               + jnp.sum(pes[d], axis=-1, keepdims=True))
            num = num + wgt * acc[d * rows:(d + 1) * rows]
        outs.append(num / den)
    att = jnp.concatenate(outs, axis=1)
    o_ref[0] = (att * g_ref[0].astype(F32)).astype(BF16)


def _scan_lanes(x, op, fill):
    lane = lax.broadcasted_iota(jnp.int32, x.shape, 1)
    d = 1
    while d < x.shape[1]:
        x = op(x, jnp.where(lane >= d, pltpu.roll(x, d, 1), fill))
        d *= 2
    return x


def _gate_scans(gt):
    b = _scan_lanes(pltpu.roll(gt, N_HEADS_MLSTM, 0), jnp.add, 0.0)
    cm = _scan_lanes(gt - b, jnp.maximum, NEG_INF)
    row = lax.broadcasted_iota(jnp.int32, gt.shape, 0)
    return jnp.where(row < N_HEADS_MLSTM, b, pltpu.roll(cm, N_HEADS_MLSTM, 0))


def _mlstm_kernel(*refs, zero_init, scans_given, fuse_out):
    refs = list(refs)
    q_ref, k_ref, v_ref, gt_ref, so_ref, gb_ref, ng_ref = refs[:7]
    del refs[:7]
    if not zero_init:
        c0_ref, n0_ref, m0_ref = refs[:3]
        del refs[:3]
    if fuse_out:
        x_ref, ya_ref, wo_ref, fg_ref = refs[:4]
        del refs[:4]
    y_ref, c_out_ref, n_out_ref, m_out_ref, cn_ref, m_ref = refs
    c_idx = pl.program_id(1)
    nb = q_ref.shape[0]
    L = CHUNK
    dh = HEAD_DIM_MLSTM
    nh = N_HEADS_MLSTM

    @pl.when(c_idx == 0)
    def _():
        if zero_init:
            cn_ref[...] = jnp.zeros(cn_ref.shape, F32)
            m_ref[...] = jnp.zeros(m_ref.shape, F32)
        else:
            m_ref[...] = m0_ref[...]
            for bb in range(nb):
                for h in range(nh):
                    n_rows = jnp.broadcast_to(n0_ref[bb, h:h + 1, :], (dh, dh))
                    cn_ref[bb, h] = jnp.concatenate([c0_ref[bb, h], n_rows.T], axis=1)

    tri_t = lax.broadcasted_iota(jnp.int32, (L, L), 0)
    tri_s = lax.broadcasted_iota(jnp.int32, (L, L), 1)
    causal = tri_t >= tri_s
    ones_blk = jnp.ones((L, dh), BF16)
    nt = (((1,), (1,)), ((), ()))
    pairs = [(bb, h) for bb in range(nb) for h in range(nh)]
    sl = lambda h: slice(h * dh, (h + 1) * dh)

    qk = {(bb, h): lax.dot_general(q_ref[bb, :, sl(h)], k_ref[bb, :, sl(h)], nt,
                                   preferred_element_type=F32) for bb, h in pairs}
    cn_old = {p: cn_ref[p[0], p[1]] for p in pairs}
    if fuse_out:
        mix_a = x_ref[...].reshape(nb * L, D_MODEL) + jnp.dot(
            ya_ref[...].reshape(nb * L, D_ATT), wo_ref[0:D_ATT, :], preferred_element_type=F32)

    rows, cols, decays, w_rows = [], [], [], []
    for bb in range(nb):
        if scans_given:
            i_row = gt_ref[bb, 0:SUBLANES, :]
            sc = gt_ref[bb, SUBLANES:2 * SUBLANES, :]
        else:
            i_row = gt_ref[bb]
            sc = _gate_scans(i_row)
        b = sc
        cm = pltpu.roll(sc, nh, 0)
        m_prev = m_ref[bb]
        m_t = jnp.maximum(m_prev + b, b + cm)
        inter = jnp.exp(m_prev + b - m_t)
        m_last = jnp.broadcast_to(m_t[:, L - 1:L], m_t.shape)
        b_last = jnp.broadcast_to(b[:, L - 1:L], b.shape)
        decay = jnp.exp(m_prev + b_last - m_last)
        w_row = jnp.exp(b_last - b + i_row - m_last)
        m_ref[bb] = m_last
        stack = jnp.concatenate([b - m_t, inter, jnp.exp(-m_t),
                                 jnp.zeros((L - 3 * SUBLANES, L), F32)], axis=0)
        cols.append(stack.T)
        rows.append(i_row - b)
        w_rows.append(w_row)
        decays.append(jnp.concatenate([decay, decay], axis=1))
    col = lambda bb, kind, h: cols[bb][:, kind * SUBLANES + h:kind * SUBLANES + h + 1]

    sqk = {}
    for bb, h in pairs:
        dlog = col(bb, 0, h) + rows[bb][h:h + 1, :]
        sqk[bb, h] = (qk[bb, h] * jnp.exp(jnp.where(causal, dlog, NEG_INF))).astype(BF16)
    v_one = {(bb, h): jnp.concatenate([v_ref[bb, :, sl(h)], ones_blk], axis=1) for bb, h in pairs}
    tots = {}
    for bb, h in pairs:
        q_dec = (col(bb, 1, h) * q_ref[bb, :, sl(h)].astype(F32)).astype(BF16)
        lhs = jnp.concatenate([sqk[bb, h], q_dec], axis=1)
        rhs = jnp.concatenate([v_one[bb, h], cn_old[bb, h].astype(BF16)], axis=0)
        tots[bb, h] = jnp.dot(lhs, rhs, preferred_element_type=F32)
    yb = {}
    for bb, h in pairs:
        tot = tots[bb, h]
        hh = tot[:, :dh] / jnp.maximum(jnp.abs(tot[:, dh:]), col(bb, 2, h))
        hh = so_ref[bb, :, sl(h)].astype(F32) * hh
        hh = hh * lax.rsqrt(jnp.mean(hh * hh, axis=-1, keepdims=True) + EPS)
        hh = hh * ng_ref[:, sl(h)]
        yb[bb, h] = (hh * gb_ref[bb, :, sl(h)].astype(F32)).astype(BF16)
    if fuse_out:
        yb_all = jnp.concatenate([jnp.concatenate([yb[bb, h] for h in range(nh)], axis=1)
                                  for bb in range(nb)], axis=0)
        mix = mix_a + jnp.dot(yb_all, wo_ref[D_ATT:, :], preferred_element_type=F32)
        res = _rmsnorm(mix, fg_ref[...])
        y_ref[...] = res.reshape(nb, L, D_MODEL)
    else:
        for bb, h in pairs:
            y_ref[bb, :, sl(h)] = yb[bb, h]
    for bb, h in pairs:
        kt_w = k_ref[bb, :, sl(h)].T.astype(F32) * w_rows[bb][h:h + 1, :]
        upd = jnp.dot(kt_w.astype(BF16), v_one[bb, h], preferred_element_type=F32)
        cn_ref[bb, h] = decays[bb][h:h + 1, :] * cn_old[bb, h] + upd

    @pl.when(c_idx == pl.num_programs(1) - 1)
    def _():
        m_out_ref[...] = m_ref[...]
        for bb in range(nb):
            n_rows = []
            for h in range(nh):
                cn = cn_ref[bb, h]
                c_out_ref[bb, h] = cn[:, :dh]
                n_rows.append(cn[:, dh:].T[0:1, :])
            n_out_ref[bb] = jnp.concatenate(n_rows, axis=0)


def _outproj_kernel(x_ref, ya_ref, yb_ref, w_ref, g_ref, o_ref):
    rows = x_ref.shape[0]
    sub = min(OUTPROJ_SUBTILE, rows)
    for r0 in range(0, rows, sub):
        rs = pl.ds(r0, sub)
        mix = jnp.dot(ya_ref[rs, :], w_ref[0:D_ATT, :], preferred_element_type=F32)
        mix = mix + jnp.dot(yb_ref[rs, :], w_ref[D_ATT:, :], preferred_element_type=F32)
        o_ref[rs, :] = _rmsnorm(x_ref[rs, :] + mix, g_ref[...])


def _rope_tables(pos):
    half = ROT_DIM // 2
    f32 = np.float32
    inv = f32(ROPE_THETA) ** (-np.arange(half, dtype=f32) * f32(2.0) / f32(ROT_DIM))
    ang = np.asarray(pos, dtype=f32)[:, None] * inv[None, :].astype(f32)
    cos, sin = np.cos(ang).astype(f32), np.sin(ang).astype(f32)
    n = ang.shape[0]
    one = np.ones((n, HEAD_DIM_ATT - ROT_DIM), f32)
    zero = np.zeros((n, HEAD_DIM_ATT - ROT_DIM), f32)
    zh = np.zeros((n, half), f32)
    cos_t = np.concatenate([cos, cos, one], axis=1)
    lo_t = np.concatenate([-sin, zh, zero], axis=1)
    hi_t = np.concatenate([zh, sin, zero], axis=1)
    rep = lambda t: np.concatenate([t, t], axis=1)
    return rep(cos_t), rep(lo_t), rep(hi_t)


def _params(sem):
    return pltpu.CompilerParams(dimension_semantics=sem, vmem_limit_bytes=VMEM_LIMIT)


def _const_spec(shape):
    return pl.BlockSpec(shape, lambda *_: (0,) * len(shape))


def _inproj_prompt(x, norm_g, w_main, w_gate, conv_w, conv_b, bif):
    B, S, _ = x.shape
    tm = ROW_TILE
    cos, lo, hi = _rope_tables(np.arange(S))
    tile = lambda width: pl.BlockSpec((1, tm, width), lambda b, j: (b, j, 0))
    tab = pl.BlockSpec((tm, LANES), lambda b, j: (j, 0))
    bf = lambda: jax.ShapeDtypeStruct((B, S, D_ATT), BF16)
    f5 = lambda: jax.ShapeDtypeStruct((B, N_HEADS_ATT, HEAD_DIM_ATT, S), F32)
    tile5 = pl.BlockSpec((1, N_HEADS_ATT, HEAD_DIM_ATT, tm), lambda b, j: (b, 0, 0, j))
    out_shape = (bf(), bf(), bf(), f5(), f5(), bf(), bf(), bf(), bf(), bf(), bf(),
                 jax.ShapeDtypeStruct((B, 2 * SUBLANES, S), F32),
                 jax.ShapeDtypeStruct((B, CONV_WIDTH - 1, 2 * D_MLSTM), F32))
    out_specs = tuple([tile(D_ATT)] * 3 + [tile5] * 2 + [tile(D_ATT)] * 6) + (
        pl.BlockSpec((1, 2 * SUBLANES, tm), lambda b, j: (b, 0, j)),
        pl.BlockSpec((1, CONV_WIDTH - 1, 2 * D_MLSTM), lambda b, j: (b, 0, 0)))
    return pl.pallas_call(
        _inproj_prompt_kernel,
        grid=(B, S // tm),
        in_specs=[tile(D_MODEL), _const_spec((1, D_MODEL)), _const_spec((D_MODEL, OFF_G)),
                  _const_spec(w_gate.shape),
                  _const_spec((CONV_WIDTH, 2 * D_MLSTM)), _const_spec((1, 2 * D_MLSTM)),
                  _const_spec((1, LANES)), tab, tab, tab],
        out_specs=out_specs,
        out_shape=out_shape,
        scratch_shapes=[pltpu.VMEM((tm + 2 * SUBLANES, 2 * D_MLSTM), F32)],
        compiler_params=_params(("arbitrary", "arbitrary")),
        name="inproj_prompt",
    )(x, norm_g, w_main, w_gate, conv_w, conv_b, bif, cos, lo, hi)


def _inproj_sample(x2, norm_g, w_main, w_gate, conv_w, conv_b, bif, hist, t_new):
    rows = x2.shape[0]
    pos = PAST_LEN + np.arange(t_new)
    cos, lo, hi = (np.tile(t, (rows // t_new, 1)) for t in _rope_tables(pos))
    bf = lambda: jax.ShapeDtypeStruct((rows, D_ATT), BF16)
    f3 = lambda: jax.ShapeDtypeStruct((rows, D_ATT), F32)
    f5 = lambda: jax.ShapeDtypeStruct((D_ATT, rows), F32)
    out_shape = (bf(), f3(), f3(), f5(), f5(), bf(),
                 jax.ShapeDtypeStruct((rows, 5 * D_MLSTM), BF16),
                 jax.ShapeDtypeStruct((rows, LANES), F32),
                 jax.ShapeDtypeStruct((rows, 2 * D_MLSTM), F32))
    return pl.pallas_call(
        functools.partial(_inproj_sample_kernel, t_new=t_new),
        out_shape=out_shape,
        compiler_params=pltpu.CompilerParams(vmem_limit_bytes=VMEM_LIMIT),
        name="inproj_sample",
    )(x2, norm_g, w_main, w_gate, conv_w, conv_b, bif, cos, lo, hi, *hist)


def _attn_prompt(q, k, v, gate):
    B, S, _ = q.shape
    spec = pl.BlockSpec((1, S, LANES), lambda b, h: (b, 0, h))
    return pl.pallas_call(
        _attn_prompt_kernel,
        grid=(B, D_ATT // LANES),
        in_specs=[spec, spec, spec, spec],
        out_specs=spec,
        out_shape=jax.ShapeDtypeStruct((B, S, D_ATT), BF16),
        scratch_shapes=[
            pltpu.VMEM((5, S, LANES), BF16), pltpu.VMEM((3, S, LANES), F32),
            pltpu.VMEM((5, 16, S // 16, LANES), BF16),
            pltpu.VMEM((N_BACK, 2 * N_BACK), F32),
            pltpu.VMEM((3, 16, S // 16, LANES), F32),
            pltpu.VMEM((3, S, LANES), F32), pltpu.VMEM((3, S, LANES), F32)],
        compiler_params=_params(("arbitrary", "arbitrary")),
        name="attn_prompt",
    )(q, k, v, gate)


def _attn_sample(q, kn, vn, knt, vnt, gate, ck, cv, t_new):
    B, rows, _ = q.shape
    wb = ck.shape[3]
    small = pl.BlockSpec((1, rows, D_ATT), lambda b: (b, 0, 0))
    big = pl.BlockSpec((1, N_HEADS_ATT, HEAD_DIM_ATT, wb), lambda b: (b, 0, 0, 0))
    win_shape = jax.ShapeDtypeStruct((B, N_HEADS_ATT, HEAD_DIM_ATT, wb), F32)
    return pl.pallas_call(
        functools.partial(_attn_sample_kernel, t_new=t_new),
        grid=(B,),
        in_specs=[small, small, small, _const_spec(knt.shape), _const_spec(vnt.shape), small, big, big],
        out_specs=(small, big, big),
        out_shape=(jax.ShapeDtypeStruct((B, rows, D_ATT), BF16), win_shape, win_shape),
        scratch_shapes=[pltpu.VMEM((len(DILATIONS), rows, wb), F32)],
        compiler_params=_params(("arbitrary",)),
        name="attn_sample",
    )(q, kn, vn, knt, vnt, gate, ck, cv)


def _mlstm(q, k, v, gates_t, sig_o, gate_b, norm_g, state=None, out_proj=None,
           lane_blocks=(0, 0, 0, 0, 0)):
    B, S, _ = q.shape
    nc = S // CHUNK
    nb = MLSTM_BATCH
    nh, dh = N_HEADS_MLSTM, HEAD_DIM_MLSTM
    grows = gates_t.shape[1]
    tile = lambda width, lb=0: pl.BlockSpec((nb, CHUNK, width), lambda b, c: (b, c, lb))
    op = [tile(D_MLSTM, lb) for lb in lane_blocks]
    gspec = pl.BlockSpec((nb, grows, CHUNK), lambda b, c: (b, 0, c))
    c_spec = pl.BlockSpec((nb, nh, dh, dh), lambda b, c: (b, 0, 0, 0))
    n_spec = pl.BlockSpec((nb, nh, dh), lambda b, c: (b, 0, 0))
    m_spec = pl.BlockSpec((nb, SUBLANES, LANES), lambda b, c: (b, 0, 0))
    zero_init = state is None
    fuse_out = out_proj is not None
    in_specs = op[:3] + [gspec] + op[3:] + [_const_spec((1, D_MLSTM))]
    args = [q, k, v, gates_t, sig_o, gate_b, norm_g]
    if not zero_init:
        in_specs += [c_spec, n_spec, m_spec]
        args += list(state)
    if fuse_out:
        in_specs += [tile(D_MODEL), tile(D_ATT), _const_spec((D_ATT + D_MLSTM, D_MODEL)),
                     _const_spec((1, D_MODEL))]
        args += list(out_proj)
        y_spec, y_shape = tile(D_MODEL), jax.ShapeDtypeStruct((B, S, D_MODEL), F32)
    else:
        y_spec, y_shape = tile(D_MLSTM), jax.ShapeDtypeStruct((B, S, D_MLSTM), BF16)
    return pl.pallas_call(
        functools.partial(_mlstm_kernel, zero_init=zero_init, scans_given=grows == 2 * SUBLANES,
                          fuse_out=fuse_out),
        grid=(B // nb, nc),
        in_specs=in_specs,
        out_specs=(y_spec, c_spec, n_spec, m_spec),
        out_shape=(y_shape,
                   jax.ShapeDtypeStruct((B, nh, dh, dh), F32),
                   jax.ShapeDtypeStruct((B, nh, dh), F32),
                   jax.ShapeDtypeStruct((B, SUBLANES, LANES), F32)),
        scratch_shapes=[pltpu.VMEM((nb, nh, dh, 2 * dh), F32),
                        pltpu.VMEM((nb, SUBLANES, LANES), F32)],
        compiler_params=_params(("arbitrary", "arbitrary")),
        name="mlstm_prompt" if zero_init else "mlstm_sample",
    )(*args)


def _outproj(x2, ya, yb, w_out, final_g):
    rows = x2.shape[0]
    tm = min(OUTPROJ_TILE, rows)
    tile = lambda width: pl.BlockSpec((tm, width), lambda i: (i, 0))
    return pl.pallas_call(
        _outproj_kernel,
        grid=(rows // tm,),
        in_specs=[tile(D_MODEL), tile(D_ATT), tile(D_MLSTM),
                  _const_spec((D_ATT + D_MLSTM, D_MODEL)), _const_spec((1, D_MODEL))],
        out_specs=tile(D_MODEL),
        out_shape=jax.ShapeDtypeStruct((rows, D_MODEL), F32),
        compiler_params=_params(("arbitrary",)),
        name="outproj",
    )(x2, ya, yb, w_out, final_g)


def kernel(x_prompt, x_sample, cache_win_k, cache_win_v, state_conv, state_C, state_n, state_m,
           norm_g, w_in, conv_w, conv_b, b_i, b_f, mlstm_norm_g, w_out, final_norm_g):
    assert w_in.shape[0] == 1, "single-layer model"
    B, S, D = x_prompt.shape
    DB, T, _ = x_sample.shape
    HB, DK = N_HEADS_MLSTM, HEAD_DIM_MLSTM
    wb = cache_win_k.shape[2]
    assert S % ROW_TILE == 0 and S == 16 * N_BACK and wb >= 16 * N_BACK and CONV_WIDTH - 1 <= T <= SUBLANES and DB * T == LANES

    w_main = w_in[0].astype(BF16)
    w_gate = jnp.pad(w_main[:, OFF_G:], ((0, 0), (0, LANES - 2 * HB)))
    w_o = w_out[0].astype(BF16)
    g_in = norm_g[0][None, :]
    cw, cb = conv_w[0], conv_b[0][None, :]
    bif = jnp.pad(jnp.concatenate([b_i[0], b_f[0]]), (0, LANES - 2 * HB))[None, :]
    ng = mlstm_norm_g[0][None, :]
    g_fin = final_norm_g[None, :]

    (q_p, k_p, v_p, pk, pv, ga_p, qm_p, km_p, vb_p, so_p, gb_p, gt_p, p_conv) = _inproj_prompt(
        x_prompt, g_in, w_main, w_gate, cw, cb, bif)
    ya_p = _attn_prompt(q_p, k_p, v_p, ga_p)
    y_prompt, c_p, n_p, m_p = _mlstm(qm_p, km_p, vb_p, gt_p, so_p, gb_p, ng,
                                     out_proj=(x_prompt, ya_p, w_o, g_fin))

    sc = state_conv[0]
    zrow = jnp.zeros((DB, 1, 2 * D_MLSTM), F32)
    hist = []
    for sh in (1, 2, 3):
        rows_ = [sc[:, CONV_WIDTH - 1 + t - sh:CONV_WIDTH + t - sh] if t < sh else zrow
                 for t in range(T)]
        hist.append(jnp.concatenate(rows_, axis=1).reshape(DB * T, 2 * D_MLSTM))
    (q_s, kn, vn, knt, vnt, ga_s, mix_s, gates_s, qk_s) = _inproj_sample(
        x_sample.reshape(DB * T, D), g_in, w_main, w_gate, cw, cb, bif, hist, T)
    r3 = lambda a: a.reshape(DB, T, a.shape[-1])
    pad8 = lambda a: jnp.pad(r3(a), ((0, 0), (0, SUBLANES - T), (0, 0)))
    to_hdp = lambda c: jnp.transpose(c[0], (0, 2, 3, 1))
    from_hdp = lambda c: jnp.transpose(c, (0, 3, 1, 2))[None]
    ya_s, s_k, s_v = _attn_sample(pad8(q_s), pad8(kn), pad8(vn), knt, vnt, pad8(ga_s),
                                  to_hdp(cache_win_k), to_hdp(cache_win_v), T)
    ya_s = ya_s[:, :T]

    pad_t = lambda a: jnp.pad(r3(a), ((0, 0), (0, CHUNK - T), (0, 0)))
    g3 = r3(gates_s)[:, :, :SUBLANES]
    null_gate = jnp.concatenate([jnp.full((HB,), NEG_INF, F32), jnp.zeros((HB,), F32)])
    gt_s = jnp.concatenate([g3, jnp.broadcast_to(null_gate, (DB, CHUNK - T, SUBLANES))], axis=1)
    gt_s = gt_s.transpose(0, 2, 1)
    m0 = jnp.broadcast_to(jnp.pad(state_m[0], ((0, 0), (0, SUBLANES - HB)))[..., None],
                          (DB, SUBLANES, LANES))
    mix_p = pad_t(mix_s)
    yb_s, c_s, n_s, m_s = _mlstm(mix_p, mix_p, mix_p, gt_s, mix_p, mix_p, ng,
                                 state=(state_C[0], state_n[0], m0), lane_blocks=(0, 1, 2, 3, 4))
    y_sample = _outproj(x_sample.reshape(DB * T, D), ya_s.reshape(DB * T, D_ATT),
                        yb_s[:, :T].reshape(DB * T, D_MLSTM), w_o, g_fin).reshape(DB, T, D)

    return (y_prompt, y_sample,
            from_hdp(pk), from_hdp(pv), p_conv[None],
            c_p[None], n_p[None], m_p[None, :, :HB, 0],
            from_hdp(s_k), from_hdp(s_v), r3(qk_s)[None, :, T - (CONV_WIDTH - 1):],
            c_s[None], n_s[None], m_s[None, :, :HB, 0])
```

```python
import functools
import math

import jax
import jax.numpy as jnp
import numpy as np
from jax import lax
from jax.experimental import pallas as pl
from jax.experimental.pallas import tpu as pltpu

F32 = jnp.float32
BF16 = jnp.bfloat16

D_MODEL = 1024
D_ATT = 512
N_HEADS_ATT = 8
HEAD_DIM_ATT = 64
D_MLSTM = 512
N_HEADS_MLSTM = 4
HEAD_DIM_MLSTM = 128
ROT_DIM = 16
ROPE_THETA = 500000.0
PAST_LEN = 16384
DILATIONS = (1, 4, 16)
N_BACK = 128
CONV_WIDTH = 4
CHUNK = 128
EPS = 1e-6
NEG_INF = -1e30
LOG2_E = math.log2(math.e)

LANES = 128
SUBLANES = 8
OFF_QA, OFF_KA, OFF_VA, OFF_ZA = (i * D_ATT for i in range(4))
OFF_QB, _, OFF_VB, OFF_OB, OFF_ZB, OFF_G = (4 * D_ATT + i * D_MLSTM for i in range(6))
ROW_TILE = 512
INPROJ_SUBTILE = 128
OUTPROJ_TILE = 2048
OUTPROJ_SUBTILE = 256
SAMPLE_HEAD_GROUPS = 2
MLSTM_BATCH = 4
VMEM_LIMIT = 56 * 1024 * 1024


def _silu(x):
    return x * jax.nn.sigmoid(x)


def _rmsnorm(x, g):
    return x * lax.rsqrt(jnp.mean(x * x, axis=-1, keepdims=True) + EPS) * g


def _rope(u, cos, sin_lo, sin_hi):
    outs = []
    for c in range(u.shape[1] // LANES):
        xs = u[:, c * LANES:(c + 1) * LANES]
        outs.append(xs * cos + pltpu.roll(xs, LANES - ROT_DIM // 2, 1) * sin_lo
                    + pltpu.roll(xs, ROT_DIM // 2, 1) * sin_hi)
    return jnp.concatenate(outs, axis=1)


def _gate_block(ug, bif):
    gz = ug + bif
    lane = lax.broadcasted_iota(jnp.int32, gz.shape, 1)
    logf = jnp.minimum(gz, 0.0) - jnp.log1p(jnp.exp(-jnp.abs(gz)))
    return jnp.where(lane < N_HEADS_MLSTM, gz, logf)


def _segments(hn, w_ref, wg_ref):
    def seg(off, width):
        rhs = wg_ref[...] if off == OFF_G else w_ref[:, off:off + width]
        return jnp.dot(hn, rhs, preferred_element_type=F32)
    return seg


def _inproj_common(seg, cos, sin_lo, sin_hi, q_ref, ga_ref, vb_ref, so_ref, gb_ref):
    q = _rope(seg(OFF_QA, D_ATT), cos, sin_lo, sin_hi) * (HEAD_DIM_ATT ** -0.5 * LOG2_E)
    q_ref[...] = q.astype(BF16).reshape(q_ref.shape)
    k = _rope(seg(OFF_KA, D_ATT), cos, sin_lo, sin_hi)
    v = seg(OFF_VA, D_ATT)
    ga_ref[...] = _silu(seg(OFF_ZA, D_ATT)).astype(BF16).reshape(ga_ref.shape)
    vb_ref[...] = seg(OFF_VB, D_MLSTM).astype(BF16).reshape(vb_ref.shape)
    so_ref[...] = jax.nn.sigmoid(seg(OFF_OB, D_MLSTM)).astype(BF16).reshape(so_ref.shape)
    gb_ref[...] = _silu(seg(OFF_ZB, D_MLSTM)).astype(BF16).reshape(gb_ref.shape)
    return k, v


def _inproj_prompt_kernel(x_ref, g_ref, w_ref, wg_ref, cw_ref, cb_ref, bif_ref, cos_ref, slo_ref,
                          shi_ref,
                          q_ref, kb_ref, vbf_ref, pk_ref, pv_ref, ga_ref, qm_ref, km_ref, vb_ref,
                          so_ref, gb_ref, gt_ref, pconv_ref, xp_ref):
    j = pl.program_id(1)
    tm = x_ref.shape[1]
    sub = INPROJ_SUBTILE
    assert CHUNK == LANES and tm % sub == 0 and sub % CHUNK == 0

    @pl.when(j == 0)
    def _():
        xp_ref[0:SUBLANES, :] = jnp.zeros((SUBLANES, 2 * D_MLSTM), F32)

    @pl.when(j > 0)
    def _():
        xp_ref[0:SUBLANES, :] = xp_ref[tm:tm + SUBLANES, :]

    for r0 in range(0, tm, sub):
        rs = pl.ds(r0, sub)
        part = lambda ref: ref.at[0, rs, :]
        hn = _rmsnorm(x_ref[0, rs, :], g_ref[...]).astype(BF16)
        seg = _segments(hn, w_ref, wg_ref)

        gates = _gate_block(seg(OFF_G, LANES), bif_ref[...])
        for i in range(sub // CHUNK):
            cs = slice(r0 + i * CHUNK, r0 + (i + 1) * CHUNK)
            gt = gates[i * CHUNK:(i + 1) * CHUNK, :].T[0:SUBLANES, :]
            gt_ref[0, 0:SUBLANES, cs] = gt
            gt_ref[0, SUBLANES:2 * SUBLANES, cs] = _gate_scans(gt)

        k, v = _inproj_common(seg, cos_ref[rs, :], slo_ref[rs, :], shi_ref[rs, :],
                              part(q_ref), part(ga_ref), part(vb_ref), part(so_ref), part(gb_ref))
        kb_ref[0, rs, :] = k.astype(BF16)
        vbf_ref[0, rs, :] = v.astype(BF16)
        pk_ref[0, :, :, r0:r0 + sub] = k.T.reshape(N_HEADS_ATT, HEAD_DIM_ATT, sub)
        pv_ref[0, :, :, r0:r0 + sub] = v.T.reshape(N_HEADS_ATT, HEAD_DIM_ATT, sub)

        base = SUBLANES + r0
        xp_ref[base:base + sub, :] = seg(OFF_QB, 2 * D_MLSTM)
        y = cb_ref[...] + xp_ref[base:base + sub, :] * cw_ref[3:4, :]
        for jj in range(CONV_WIDTH - 1):
            sh = CONV_WIDTH - 1 - jj
            y = y + xp_ref[base - sh:base - sh + sub, :] * cw_ref[jj:jj + 1, :]
        y = _silu(y)
        qm_ref[0, rs, :] = y[:, :D_MLSTM].astype(BF16)
        km_ref[0, rs, :] = (y[:, D_MLSTM:] * (HEAD_DIM_MLSTM ** -0.5)).astype(BF16)

    @pl.when(j == pl.num_programs(1) - 1)
    def _():
        pconv_ref[0] = xp_ref[tm + SUBLANES - (CONV_WIDTH - 1):tm + SUBLANES, :]


def _inproj_sample_kernel(x_ref, g_ref, w_ref, wg_ref, cw_ref, cb_ref, bif_ref, cos_ref, slo_ref,
                          shi_ref,
                          h1_ref, h2_ref, h3_ref,
                          q_ref, kn_ref, vn_ref, knt_ref, vnt_ref, ga_ref, mix_ref, gates_ref, qk_ref,
                          *, t_new):
    qm_ref, km_ref, vb_ref, so_ref, gb_ref = (
        mix_ref.at[:, pl.ds(i * D_MLSTM, D_MLSTM)] for i in range(5))
    hn = _rmsnorm(x_ref[...], g_ref[...]).astype(BF16)
    seg = _segments(hn, w_ref, wg_ref)
    k, v = _inproj_common(seg, cos_ref[...], slo_ref[...], shi_ref[...],
                          q_ref, ga_ref, vb_ref, so_ref, gb_ref)
    kn_ref[...] = k
    vn_ref[...] = v
    knt_ref[...] = k.T
    vnt_ref[...] = v.T
    u = seg(OFF_QB, 2 * D_MLSTM)
    qk_ref[...] = u
    t = lax.rem(lax.broadcasted_iota(jnp.int32, u.shape, 0), t_new)
    y = cb_ref[...] + u * cw_ref[3:4, :]
    for sh, h_ref in ((1, h1_ref), (2, h2_ref), (3, h3_ref)):
        prev = jnp.where(t >= sh, pltpu.roll(u, sh, 0), h_ref[...])
        y = y + prev * cw_ref[3 - sh:4 - sh, :]
    y = _silu(y)
    qm_ref[...] = y[:, :D_MLSTM].astype(BF16)
    km_ref[...] = (y[:, D_MLSTM:] * (HEAD_DIM_MLSTM ** -0.5)).astype(BF16)
    gates_ref[...] = _gate_block(seg(OFF_G, LANES), bif_ref[...])


def _attn_prompt_kernel(q_ref, k_ref, v_ref, g_ref, o_ref,
                        src1_ref, src4_ref, src16_ref, bias_ref, st16r_ref, st16_ref, st4_ref):
    seq = q_ref.shape[1]
    group = 8
    head_a_full = lax.broadcasted_iota(jnp.int32, (seq, LANES), 1) < HEAD_DIM_ATT

    def stack(q, k, v, is_a):
        zero, one = jnp.zeros((), q.dtype), jnp.ones((), q.dtype)
        return (jnp.where(is_a, q, zero), jnp.where(is_a, zero, q), k,
                jnp.where(is_a, v, one), jnp.where(is_a, one, v))

    q, k, v = q_ref[0], k_ref[0], v_ref[0]
    for i, x in enumerate(stack(q, k, v, head_a_full)):
        src1_ref[i] = x
    for i, x in enumerate((q, k, v)):
        src4_ref[i] = x.astype(F32)
    blk = 16 * 16
    pa = lax.broadcasted_iota(jnp.int32, (blk, blk), 0)
    pb = lax.broadcasted_iota(jnp.int32, (blk, blk), 1)
    perm = jnp.where(pb == 16 * (pa & 15) + (pa >> 4), 1.0, 0.0).astype(BF16)
    is_a_blk = lax.broadcasted_iota(jnp.int32, (16, 16, LANES), 2) < HEAD_DIM_ATT
    for j in range(seq // blk):
        rows = slice(j * blk, (j + 1) * blk)
        parts = [jnp.dot(perm, x[rows], preferred_element_type=F32).astype(BF16).reshape(16, 16, LANES)
                 for x in (q, k, v)]
        for i, x in enumerate(stack(*parts, is_a_blk)):
            src16_ref[i, :, 16 * j:16 * (j + 1), :] = x
    u = lax.broadcasted_iota(jnp.int32, (N_BACK, 2 * N_BACK), 0)
    w = lax.broadcasted_iota(jnp.int32, (N_BACK, 2 * N_BACK), 1)
    bias_ref[...] = jnp.where((w >= u) & (w <= u + N_BACK), jnp.finfo(F32).max, NEG_INF)
    head_a = lax.broadcasted_iota(jnp.int32, (N_BACK, LANES), 1) < HEAD_DIM_ATT

    def partials(gets):
        staged = []
        for get, has_prev in gets:
            if has_prev:
                both = lambda i, get=get: jnp.concatenate([get(i, True), get(i, False)], axis=0)
                bias = bias_ref[...]
            else:
                both = lambda i, get=get: get(i, False)
                bias = bias_ref[:, N_BACK:]
            kk = both(2)
            heads = []
            for qi in (0, 1):
                s = lax.dot_general(get(qi, False), kk, (((1,), (1,)), ((), ())),
                                    preferred_element_type=F32)
                s = jnp.minimum(s, bias)
                mh = jnp.max(s, axis=-1, keepdims=True)
                heads.append((mh, jnp.exp2(s - mh).astype(BF16)))
            staged.append((both, heads))
        out = []
        for both, ((m_a, p_a), (m_b, p_b)) in staged:
            pv_a = jnp.dot(p_a, both(3), preferred_element_type=F32)
            pv_b = jnp.dot(p_b, both(4), preferred_element_type=F32)
            acc = jnp.where(head_a, pv_a, pv_b)
            den = pltpu.roll(jnp.where(head_a, pv_b, pv_a), HEAD_DIM_ATT, 1)
            out.append((jnp.where(head_a, m_a, m_b), den, acc))
        return out

    def get4(qstart, kprev_start):
        loaded = {}

        def base(j, prev):
            if (j, prev) not in loaded:
                start = kprev_start if prev else qstart
                loaded[j, prev] = src4_ref[j, pl.ds(start, N_BACK, stride=4), :].astype(BF16)
            return loaded[j, prev]

        def get(i, prev):
            if i == 2:
                return base(1, prev)
            x = base(0 if i < 2 else 2, prev)
            fill = jnp.zeros((), BF16) if i < 2 else jnp.ones((), BF16)
            keep_a = i in (0, 3)
            return jnp.where(head_a, x, fill) if keep_a else jnp.where(head_a, fill, x)
        return get, kprev_start is not None

    def get1(qstart, kprev_start):
        def get(i, prev):
            return src1_ref[i, pl.ds(kprev_start if prev else qstart, N_BACK), :]
        return get, kprev_start is not None

    def keep4(blocks):
        res = partials([get4(qs, ks) for qs, ks in blocks])
        for (qs, _), (m, den, acc) in zip(blocks, res):
            rows = pl.ds(qs, N_BACK, stride=4)
            st4_ref[0, rows, :] = m
            st4_ref[1, rows, :] = den
            st4_ref[2, rows, :] = acc

    def finish(blocks):
        res = partials([get1(qs, ks) for qs, ks in blocks])
        for (qs, _), part in zip(blocks, res):
            rows = pl.ds(qs, N_BACK)
            parts = [part] + [tuple(st[i, rows, :] for i in range(3)) for st in (st4_ref, st16_ref)]
            m_all = functools.reduce(jnp.maximum, [pt[0] for pt in parts])
            wts = [jnp.exp2(pt[0] - m_all) for pt in parts]
            den = sum(wt * pt[1] for wt, pt in zip(wts, parts))
            num = sum(wt * pt[2] for wt, pt in zip(wts, parts))
            o_ref[0, rows, :] = (num / den * g_ref[0, rows, :].astype(F32)).astype(BF16)

    for g in range(16 // group):
        rs = [g * group + rr for rr in range(group)]
        res = partials([(lambda i, prev, r=r: src16_ref[i, r], False) for r in rs])
        for r, part in zip(rs, res):
            for i in range(3):
                st16r_ref[i, r] = part[i]
    for i in range(3):
        st16_ref[i] = jnp.swapaxes(st16r_ref[i], 0, 1).reshape(seq, LANES)

    blocks4 = lambda cc: [(r + 4 * N_BACK * cc, r + 4 * N_BACK * (cc - 1)) for r in range(4)]
    keep4([(r, None) for r in range(4)] + blocks4(1))

    def body4(g, c):
        keep4(blocks4(2 * g) + blocks4(2 * g + 1))
        return c
    lax.fori_loop(1, seq // 4 // N_BACK // 2, body4, 0)

    finish([(0, None)] + [(cc * N_BACK, (cc - 1) * N_BACK) for cc in range(1, group)])

    def body1(g, c):
        starts = [pl.multiple_of((g * group + rr) * N_BACK, N_BACK) for rr in range(group)]
        finish([(st, st - N_BACK) for st in starts])
        return c
    lax.fori_loop(1, seq // N_BACK // group, body1, 0)


def _attn_sample_kernel(q_ref, kn_ref, vn_ref, knt_ref, vnt_ref, g_ref, ck_ref, cv_ref,
                        o_ref, sk_ref, sv_ref, clamp_ref, *, t_new):
    b = pl.program_id(0)
    hg = pl.program_id(1)
    nh = ck_ref.shape[1]
    wb = ck_ref.shape[3]
    hd = HEAD_DIM_ATT
    rows = q_ref.shape[1]

    @pl.when((b == 0) & (hg == 0))
    def _():
        delta = (wb + lax.broadcasted_iota(jnp.int32, (rows, wb), 0)
                 - lax.broadcasted_iota(jnp.int32, (rows, wb), 1))
        for d, dil in enumerate(DILATIONS):
            ok = ((delta & (dil - 1)) == 0) & (delta >= dil) & (delta <= N_BACK * dil)
            clamp_ref[d] = jnp.where(ok, jnp.finfo(F32).max, NEG_INF)

    tq = lax.broadcasted_iota(jnp.int32, (rows, rows), 0)
    tk = lax.broadcasted_iota(jnp.int32, (rows, rows), 1)
    new_ok = [((tk <= tq) if dil == 1 else (tk == tq)) & (tk < t_new) for dil in DILATIONS]

    lane = lax.broadcasted_iota(jnp.int32, (hd, LANES), 1)
    shift_new = (LANES - t_new) - b * t_new
    nt = (((1,), (1,)), ((), ()))
    outs = []
    for h in range(nh):
        hs = slice(h * hd, (h + 1) * hd)
        hs_all = pl.ds(pl.multiple_of((hg * nh + h) * hd, hd), hd)
        kt = ck_ref[0, h]
        vt = cv_ref[0, h]
        for old, new_ref, out_ref in ((kt, knt_ref, sk_ref), (vt, vnt_ref, sv_ref)):
            moved = pltpu.roll(old, wb - t_new, axis=1)
            new_cols = pltpu.roll(new_ref[hs_all, :], shift_new, axis=1)
            out_ref[0, h, :, 0:wb - LANES] = moved[:, 0:wb - LANES]
            out_ref[0, h, :, wb - LANES:wb] = jnp.where(lane < LANES - t_new,
                                                        moved[:, wb - LANES:wb], new_cols)
        qh = q_ref[0, :, hs]
        kn_h = kn_ref[0, :, hs].astype(BF16)
        vn_h = vn_ref[0, :, hs].astype(BF16)
        s_old = jnp.dot(qh, kt.astype(BF16), preferred_element_type=F32)
        s_new = lax.dot_general(qh, kn_h, nt, preferred_element_type=F32)
        ps, pes, ms = [], [], []
        for d in range(len(DILATIONS)):
            so = jnp.minimum(s_old, clamp_ref[d])
            sn = jnp.where(new_ok[d], s_new, NEG_INF)
            m = jnp.maximum(jnp.max(so, axis=-1, keepdims=True), jnp.max(sn, axis=-1, keepdims=True))
            ps.append(jnp.exp2(so - m))
            pes.append(jnp.exp2(sn - m))
            ms.append(m)
        acc = lax.dot_general(jnp.concatenate(ps, axis=0).astype(BF16), vt.astype(BF16), nt,
                              preferred_element_type=F32)
        acc = acc + jnp.dot(jnp.concatenate(pes, axis=0).astype(BF16), vn_h,
                            preferred_element_type=F32)
        m_all = functools.reduce(jnp.maximum, ms)
        den = 0.0
        num = 0.0
        for d in range(len(DILATIONS)):
            wgt = jnp.exp2(ms[d] - m_all)
            den = den + wgt * (jnp.sum(ps[d], axis=-1, keepdims=True)
                               + jnp.sum(pes[d], axis=-1, keepdims=True))
            num = num + wgt * acc[d * rows:(d + 1) * rows]
        outs.append(num / den)
    att = jnp.concatenate(outs, axis=1)
    o_ref[0] = (att * g_ref[0].astype(F32)).astype(BF16)


def _scan_lanes(x, op, fill):
    lane = lax.broadcasted_iota(jnp.int32, x.shape, 1)
    d = 1
    while d < x.shape[1]:
        x = op(x, jnp.where(lane >= d, pltpu.roll(x, d, 1), fill))
        d *= 2
    return x


def _gate_scans(gt):
    b = _scan_lanes(pltpu.roll(gt, N_HEADS_MLSTM, 0), jnp.add, 0.0)
    cm = _scan_lanes(gt - b, jnp.maximum, NEG_INF)
    row = lax.broadcasted_iota(jnp.int32, gt.shape, 0)
    return jnp.where(row < N_HEADS_MLSTM, b, pltpu.roll(cm, N_HEADS_MLSTM, 0))


def _mlstm_kernel(*refs, zero_init, scans_given, fuse_out):
    refs = list(refs)
    q_ref, k_ref, v_ref, gt_ref, so_ref, gb_ref, ng_ref = refs[:7]
    del refs[:7]
    if not zero_init:
        c0_ref, n0_ref, m0_ref = refs[:3]
        del refs[:3]
    if fuse_out:
        x_ref, ya_ref, wo_ref, fg_ref = refs[:4]
        del refs[:4]
    y_ref, c_out_ref, n_out_ref, m_out_ref, cn_ref, m_ref = refs
    c_idx = pl.program_id(1)
    nb = q_ref.shape[0]
    L = CHUNK
    dh = HEAD_DIM_MLSTM
    nh = N_HEADS_MLSTM

    @pl.when(c_idx == 0)
    def _():
        if zero_init:
            cn_ref[...] = jnp.zeros(cn_ref.shape, F32)
            m_ref[...] = jnp.zeros(m_ref.shape, F32)
        else:
            m_ref[...] = m0_ref[...]
            for bb in range(nb):
                for h in range(nh):
                    n_rows = jnp.broadcast_to(n0_ref[bb, h:h + 1, :], (dh, dh))
                    cn_ref[bb, h] = jnp.concatenate([c0_ref[bb, h], n_rows.T], axis=1)

    tri_t = lax.broadcasted_iota(jnp.int32, (L, L), 0)
    tri_s = lax.broadcasted_iota(jnp.int32, (L, L), 1)
    causal = tri_t >= tri_s
    ones_blk = jnp.ones((L, dh), BF16)
    nt = (((1,), (1,)), ((), ()))
    pairs = [(bb, h) for bb in range(nb) for h in range(nh)]
    sl = lambda h: slice(h * dh, (h + 1) * dh)

    qk = {(bb, h): lax.dot_general(q_ref[bb, :, sl(h)], k_ref[bb, :, sl(h)], nt,
                                   preferred_element_type=F32) for bb, h in pairs}
    cn_old = {p: cn_ref[p[0], p[1]] for p in pairs}
    if fuse_out:
        mix_a = x_ref[...].reshape(nb * L, D_MODEL) + jnp.dot(
            ya_ref[...].reshape(nb * L, D_ATT), wo_ref[0:D_ATT, :], preferred_element_type=F32)

    rows, cols, decays, w_rows = [], [], [], []
    for bb in range(nb):
        if scans_given:
            i_row = gt_ref[bb, 0:SUBLANES, :]
            sc = gt_ref[bb, SUBLANES:2 * SUBLANES, :]
        else:
            i_row = gt_ref[bb]
            sc = _gate_scans(i_row)
        b = sc
        cm = pltpu.roll(sc, nh, 0)
        m_prev = m_ref[bb]
        m_t = jnp.maximum(m_prev + b, b + cm)
        inter = jnp.exp(m_prev + b - m_t)
        m_last = jnp.broadcast_to(m_t[:, L - 1:L], m_t.shape)
        b_last = jnp.broadcast_to(b[:, L - 1:L], b.shape)
        decay = jnp.exp(m_prev + b_last - m_last)
        w_row = jnp.exp(b_last - b + i_row - m_last)
        m_ref[bb] = m_last
        stack = jnp.concatenate([b - m_t, inter, jnp.exp(-m_t),
                                 jnp.zeros((L - 3 * SUBLANES, L), F32)], axis=0)
        cols.append(stack.T)
        rows.append(i_row - b)
        w_rows.append(w_row)
        decays.append(jnp.concatenate([decay, decay], axis=1))
    col = lambda bb, kind, h: cols[bb][:, kind * SUBLANES + h:kind * SUBLANES + h + 1]

    sqk = {}
    for bb, h in pairs:
        dlog = col(bb, 0, h) + rows[bb][h:h + 1, :]
        sqk[bb, h] = (qk[bb, h] * jnp.exp(jnp.where(causal, dlog, NEG_INF))).astype(BF16)
    v_one = {(bb, h): jnp.concatenate([v_ref[bb, :, sl(h)], ones_blk], axis=1) for bb, h in pairs}
    tots = {}
    for bb, h in pairs:
        q_dec = (col(bb, 1, h) * q_ref[bb, :, sl(h)].astype(F32)).astype(BF16)
        lhs = jnp.concatenate([sqk[bb, h], q_dec], axis=1)
        rhs = jnp.concatenate([v_one[bb, h], cn_old[bb, h].astype(BF16)], axis=0)
        tots[bb, h] = jnp.dot(lhs, rhs, preferred_element_type=F32)
    yb = {}
    for bb, h in pairs:
        tot = tots[bb, h]
        hh = tot[:, :dh] / jnp.maximum(jnp.abs(tot[:, dh:]), col(bb, 2, h))
        hh = so_ref[bb, :, sl(h)].astype(F32) * hh
        hh = hh * lax.rsqrt(jnp.mean(hh * hh, axis=-1, keepdims=True) + EPS)
        hh = hh * ng_ref[:, sl(h)]
        yb[bb, h] = (hh * gb_ref[bb, :, sl(h)].astype(F32)).astype(BF16)
    if fuse_out:
        yb_all = jnp.concatenate([jnp.concatenate([yb[bb, h] for h in range(nh)], axis=1)
                                  for bb in range(nb)], axis=0)
        mix = mix_a + jnp.dot(yb_all, wo_ref[D_ATT:, :], preferred_element_type=F32)
        res = _rmsnorm(mix, fg_ref[...])
        y_ref[...] = res.reshape(nb, L, D_MODEL)
    else:
        for bb, h in pairs:
            y_ref[bb, :, sl(h)] = yb[bb, h]
    for bb, h in pairs:
        kt_w = k_ref[bb, :, sl(h)].T.astype(F32) * w_rows[bb][h:h + 1, :]
        upd = jnp.dot(kt_w.astype(BF16), v_one[bb, h], preferred_element_type=F32)
        cn_ref[bb, h] = decays[bb][h:h + 1, :] * cn_old[bb, h] + upd

    @pl.when(c_idx == pl.num_programs(1) - 1)
    def _():
        m_out_ref[...] = m_ref[...]
        for bb in range(nb):
            n_rows = []
            for h in range(nh):
                cn = cn_ref[bb, h]
                c_out_ref[bb, h] = cn[:, :dh]
                n_rows.append(cn[:, dh:].T[0:1, :])
            n_out_ref[bb] = jnp.concatenate(n_rows, axis=0)


def _outproj_kernel(x_ref, ya_ref, yb_ref, w_ref, g_ref, o_ref):
    rows = x_ref.shape[0]
    sub = min(OUTPROJ_SUBTILE, rows)
    for r0 in range(0, rows, sub):
        rs = pl.ds(r0, sub)
        mix = jnp.dot(ya_ref[rs, :], w_ref[0:D_ATT, :], preferred_element_type=F32)
        mix = mix + jnp.dot(yb_ref[rs, :], w_ref[D_ATT:, :], preferred_element_type=F32)
        o_ref[rs, :] = _rmsnorm(x_ref[rs, :] + mix, g_ref[...])


def _rope_tables(pos):
    half = ROT_DIM // 2
    f32 = np.float32
    inv = f32(ROPE_THETA) ** (-np.arange(half, dtype=f32) * f32(2.0) / f32(ROT_DIM))
    ang = np.asarray(pos, dtype=f32)[:, None] * inv[None, :].astype(f32)
    cos, sin = np.cos(ang).astype(f32), np.sin(ang).astype(f32)
    n = ang.shape[0]
    one = np.ones((n, HEAD_DIM_ATT - ROT_DIM), f32)
    zero = np.zeros((n, HEAD_DIM_ATT - ROT_DIM), f32)
    zh = np.zeros((n, half), f32)
    cos_t = np.concatenate([cos, cos, one], axis=1)
    lo_t = np.concatenate([-sin, zh, zero], axis=1)
    hi_t = np.concatenate([zh, sin, zero], axis=1)
    rep = lambda t: np.concatenate([t, t], axis=1)
    return rep(cos_t), rep(lo_t), rep(hi_t)


def _params(sem):
    return pltpu.CompilerParams(dimension_semantics=sem, vmem_limit_bytes=VMEM_LIMIT)


def _const_spec(shape):
    return pl.BlockSpec(shape, lambda *_: (0,) * len(shape))


def _inproj_prompt(x, norm_g, w_main, w_gate, conv_w, conv_b, bif):
    B, S, _ = x.shape
    tm = ROW_TILE
    cos, lo, hi = _rope_tables(np.arange(S))
    tile = lambda width: pl.BlockSpec((1, tm, width), lambda b, j: (b, j, 0))
    tab = pl.BlockSpec((tm, LANES), lambda b, j: (j, 0))
    bf = lambda: jax.ShapeDtypeStruct((B, S, D_ATT), BF16)
    f5 = lambda: jax.ShapeDtypeStruct((B, N_HEADS_ATT, HEAD_DIM_ATT, S), F32)
    tile5 = pl.BlockSpec((1, N_HEADS_ATT, HEAD_DIM_ATT, tm), lambda b, j: (b, 0, 0, j))
    out_shape = (bf(), bf(), bf(), f5(), f5(), bf(), bf(), bf(), bf(), bf(), bf(),
                 jax.ShapeDtypeStruct((B, 2 * SUBLANES, S), F32),
                 jax.ShapeDtypeStruct((B, CONV_WIDTH - 1, 2 * D_MLSTM), F32))
    out_specs = tuple([tile(D_ATT)] * 3 + [tile5] * 2 + [tile(D_ATT)] * 6) + (
        pl.BlockSpec((1, 2 * SUBLANES, tm), lambda b, j: (b, 0, j)),
        pl.BlockSpec((1, CONV_WIDTH - 1, 2 * D_MLSTM), lambda b, j: (b, 0, 0)))
    return pl.pallas_call(
        _inproj_prompt_kernel,
        grid=(B, S // tm),
        in_specs=[tile(D_MODEL), _const_spec((1, D_MODEL)), _const_spec((D_MODEL, OFF_G)),
                  _const_spec(w_gate.shape),
                  _const_spec((CONV_WIDTH, 2 * D_MLSTM)), _const_spec((1, 2 * D_MLSTM)),
                  _const_spec((1, LANES)), tab, tab, tab],
        out_specs=out_specs,
        out_shape=out_shape,
        scratch_shapes=[pltpu.VMEM((tm + 2 * SUBLANES, 2 * D_MLSTM), F32)],
        compiler_params=_params(("arbitrary", "arbitrary")),
        name="inproj_prompt",
    )(x, norm_g, w_main, w_gate, conv_w, conv_b, bif, cos, lo, hi)


def _inproj_sample(x2, norm_g, w_main, w_gate, conv_w, conv_b, bif, hist, t_new):
    rows = x2.shape[0]
    pos = PAST_LEN + np.arange(t_new)
    cos, lo, hi = (np.tile(t, (rows // t_new, 1)) for t in _rope_tables(pos))
    bf = lambda: jax.ShapeDtypeStruct((rows, D_ATT), BF16)
    f3 = lambda: jax.ShapeDtypeStruct((rows, D_ATT), F32)
    f5 = lambda: jax.ShapeDtypeStruct((D_ATT, rows), F32)
    out_shape = (bf(), f3(), f3(), f5(), f5(), bf(),
                 jax.ShapeDtypeStruct((rows, 5 * D_MLSTM), BF16),
                 jax.ShapeDtypeStruct((rows, LANES), F32),
                 jax.ShapeDtypeStruct((rows, 2 * D_MLSTM), F32))
    return pl.pallas_call(
        functools.partial(_inproj_sample_kernel, t_new=t_new),
        out_shape=out_shape,
        compiler_params=pltpu.CompilerParams(vmem_limit_bytes=VMEM_LIMIT),
        name="inproj_sample",
    )(x2, norm_g, w_main, w_gate, conv_w, conv_b, bif, cos, lo, hi, *hist)


def _attn_prompt(q, k, v, gate):
    B, S, _ = q.shape
    spec = pl.BlockSpec((1, S, LANES), lambda b, h: (b, 0, h))
    return pl.pallas_call(
        _attn_prompt_kernel,
        grid=(B, D_ATT // LANES),
        in_specs=[spec, spec, spec, spec],
        out_specs=spec,
        out_shape=jax.ShapeDtypeStruct((B, S, D_ATT), BF16),
        scratch_shapes=[
            pltpu.VMEM((5, S, LANES), BF16), pltpu.VMEM((3, S, LANES), F32),
            pltpu.VMEM((5, 16, S // 16, LANES), BF16),
            pltpu.VMEM((N_BACK, 2 * N_BACK), F32),
            pltpu.VMEM((3, 16, S // 16, LANES), F32),
            pltpu.VMEM((3, S, LANES), F32), pltpu.VMEM((3, S, LANES), F32)],
        compiler_params=_params(("arbitrary", "arbitrary")),
        name="attn_prompt",
    )(q, k, v, gate)


def _attn_sample(q, kn, vn, knt, vnt, gate, ck, cv, t_new):
    B, rows, _ = q.shape
    wb = ck.shape[3]
    ng = SAMPLE_HEAD_GROUPS
    nh = N_HEADS_ATT // ng
    small = pl.BlockSpec((1, rows, nh * HEAD_DIM_ATT), lambda b, g: (b, 0, g))
    big = pl.BlockSpec((1, nh, HEAD_DIM_ATT, wb), lambda b, g: (b, g, 0, 0))
    win_shape = jax.ShapeDtypeStruct((B, N_HEADS_ATT, HEAD_DIM_ATT, wb), F32)
    return pl.pallas_call(
        functools.partial(_attn_sample_kernel, t_new=t_new),
        grid=(B, ng),
        in_specs=[small, small, small, _const_spec(knt.shape), _const_spec(vnt.shape), small, big, big],
        out_specs=(small, big, big),
        out_shape=(jax.ShapeDtypeStruct((B, rows, D_ATT), BF16), win_shape, win_shape),
        scratch_shapes=[pltpu.VMEM((len(DILATIONS), rows, wb), F32)],
        compiler_params=_params(("arbitrary", "arbitrary")),
        name="attn_sample",
    )(q, kn, vn, knt, vnt, gate, ck, cv)


def _mlstm(q, k, v, gates_t, sig_o, gate_b, norm_g, state=None, out_proj=None,
           lane_blocks=(0, 0, 0, 0, 0)):
    B, S, _ = q.shape
    nc = S // CHUNK
    nb = MLSTM_BATCH
    nh, dh = N_HEADS_MLSTM, HEAD_DIM_MLSTM
    grows = gates_t.shape[1]
    tile = lambda width, lb=0: pl.BlockSpec((nb, CHUNK, width), lambda b, c: (b, c, lb))
    op = [tile(D_MLSTM, lb) for lb in lane_blocks]
    gspec = pl.BlockSpec((nb, grows, CHUNK), lambda b, c: (b, 0, c))
    c_spec = pl.BlockSpec((nb, nh, dh, dh), lambda b, c: (b, 0, 0, 0))
    n_spec = pl.BlockSpec((nb, nh, dh), lambda b, c: (b, 0, 0))
    m_spec = pl.BlockSpec((nb, SUBLANES, LANES), lambda b, c: (b, 0, 0))
    zero_init = state is None
    fuse_out = out_proj is not None
    in_specs = op[:3] + [gspec] + op[3:] + [_const_spec((1, D_MLSTM))]
    args = [q, k, v, gates_t, sig_o, gate_b, norm_g]
    if not zero_init:
        in_specs += [c_spec, n_spec, m_spec]
        args += list(state)
    if fuse_out:
        in_specs += [tile(D_MODEL), tile(D_ATT), _const_spec((D_ATT + D_MLSTM, D_MODEL)),
                     _const_spec((1, D_MODEL))]
        args += list(out_proj)
        y_spec, y_shape = tile(D_MODEL), jax.ShapeDtypeStruct((B, S, D_MODEL), F32)
    else:
        y_spec, y_shape = tile(D_MLSTM), jax.ShapeDtypeStruct((B, S, D_MLSTM), BF16)
    return pl.pallas_call(
        functools.partial(_mlstm_kernel, zero_init=zero_init, scans_given=grows == 2 * SUBLANES,
                          fuse_out=fuse_out),
        grid=(B // nb, nc),
        in_specs=in_specs,
        out_specs=(y_spec, c_spec, n_spec, m_spec),
        out_shape=(y_shape,
                   jax.ShapeDtypeStruct((B, nh, dh, dh), F32),
                   jax.ShapeDtypeStruct((B, nh, dh), F32),
                   jax.ShapeDtypeStruct((B, SUBLANES, LANES), F32)),
        scratch_shapes=[pltpu.VMEM((nb, nh, dh, 2 * dh), F32),
                        pltpu.VMEM((nb, SUBLANES, LANES), F32)],
        compiler_params=_params(("arbitrary", "arbitrary")),
        name="mlstm_prompt" if zero_init else "mlstm_sample",
    )(*args)


def _outproj(x2, ya, yb, w_out, final_g):
    rows = x2.shape[0]
    tm = min(OUTPROJ_TILE, rows)
    tile = lambda width: pl.BlockSpec((tm, width), lambda i: (i, 0))
    return pl.pallas_call(
        _outproj_kernel,
        grid=(rows // tm,),
        in_specs=[tile(D_MODEL), tile(D_ATT), tile(D_MLSTM),
                  _const_spec((D_ATT + D_MLSTM, D_MODEL)), _const_spec((1, D_MODEL))],
        out_specs=tile(D_MODEL),
        out_shape=jax.ShapeDtypeStruct((rows, D_MODEL), F32),
        compiler_params=_params(("arbitrary",)),
        name="outproj",
    )(x2, ya, yb, w_out, final_g)


def kernel(x_prompt, x_sample, cache_win_k, cache_win_v, state_conv, state_C, state_n, state_m,
           norm_g, w_in, conv_w, conv_b, b_i, b_f, mlstm_norm_g, w_out, final_norm_g):
    assert w_in.shape[0] == 1, "single-layer model"
    B, S, D = x_prompt.shape
    DB, T, _ = x_sample.shape
    HB, DK = N_HEADS_MLSTM, HEAD_DIM_MLSTM
    wb = cache_win_k.shape[2]
    assert S % ROW_TILE == 0 and S == 16 * N_BACK and wb >= 16 * N_BACK and CONV_WIDTH - 1 <= T <= SUBLANES and DB * T == LANES

    w_main = w_in[0].astype(BF16)
    w_gate = jnp.pad(w_main[:, OFF_G:], ((0, 0), (0, LANES - 2 * HB)))
    w_o = w_out[0].astype(BF16)
    g_in = norm_g[0][None, :]
    cw, cb = conv_w[0], conv_b[0][None, :]
    bif = jnp.pad(jnp.concatenate([b_i[0], b_f[0]]), (0, LANES - 2 * HB))[None, :]
    ng = mlstm_norm_g[0][None, :]
    g_fin = final_norm_g[None, :]

    (q_p, k_p, v_p, pk, pv, ga_p, qm_p, km_p, vb_p, so_p, gb_p, gt_p, p_conv) = _inproj_prompt(
        x_prompt, g_in, w_main, w_gate, cw, cb, bif)
    ya_p = _attn_prompt(q_p, k_p, v_p, ga_p)
    y_prompt, c_p, n_p, m_p = _mlstm(qm_p, km_p, vb_p, gt_p, so_p, gb_p, ng,
                                     out_proj=(x_prompt, ya_p, w_o, g_fin))

    sc = state_conv[0]
    zrow = jnp.zeros((DB, 1, 2 * D_MLSTM), F32)
    hist = []
    for sh in (1, 2, 3):
        rows_ = [sc[:, CONV_WIDTH - 1 + t - sh:CONV_WIDTH + t - sh] if t < sh else zrow
                 for t in range(T)]
        hist.append(jnp.concatenate(rows_, axis=1).reshape(DB * T, 2 * D_MLSTM))
    (q_s, kn, vn, knt, vnt, ga_s, mix_s, gates_s, qk_s) = _inproj_sample(
        x_sample.reshape(DB * T, D), g_in, w_main, w_gate, cw, cb, bif, hist, T)
    r3 = lambda a: a.reshape(DB, T, a.shape[-1])
    pad8 = lambda a: jnp.pad(r3(a), ((0, 0), (0, SUBLANES - T), (0, 0)))
    to_hdp = lambda c: jnp.transpose(c[0], (0, 2, 3, 1))
    from_hdp = lambda c: jnp.transpose(c, (0, 3, 1, 2))[None]
    ya_s, s_k, s_v = _attn_sample(pad8(q_s), pad8(kn), pad8(vn), knt, vnt, pad8(ga_s),
                                  to_hdp(cache_win_k), to_hdp(cache_win_v), T)
    ya_s = ya_s[:, :T]

    pad_t = lambda a: jnp.pad(r3(a), ((0, 0), (0, CHUNK - T), (0, 0)))
    g3 = r3(gates_s)[:, :, :SUBLANES]
    null_gate = jnp.concatenate([jnp.full((HB,), NEG_INF, F32), jnp.zeros((HB,), F32)])
    gt_s = jnp.concatenate([g3, jnp.broadcast_to(null_gate, (DB, CHUNK - T, SUBLANES))], axis=1)
    gt_s = gt_s.transpose(0, 2, 1)
    m0 = jnp.broadcast_to(jnp.pad(state_m[0], ((0, 0), (0, SUBLANES - HB)))[..., None],
                          (DB, SUBLANES, LANES))
    mix_p = pad_t(mix_s)
    yb_s, c_s, n_s, m_s = _mlstm(mix_p, mix_p, mix_p, gt_s, mix_p, mix_p, ng,
                                 state=(state_C[0], state_n[0], m0), lane_blocks=(0, 1, 2, 3, 4))
    y_sample = _outproj(x_sample.reshape(DB * T, D), ya_s.reshape(DB * T, D_ATT),
                        yb_s[:, :T].reshape(DB * T, D_MLSTM), w_o, g_fin).reshape(DB, T, D)

    return (y_prompt, y_sample,
            from_hdp(pk), from_hdp(pv), p_conv[None],
            c_p[None], n_p[None], m_p[None, :, :HB, 0],
            from_hdp(s_k), from_hdp(s_v), r3(qk_s)[None, :, T - (CONV_WIDTH - 1):],
            c_s[None], n_s[None], m_s[None, :, :HB, 0])
```
